```python
import jax, jax.numpy as jnp
from jax import lax
import numpy as np

D_MODEL = 2048
BATCH = 8
SEQ = 8192
DEPTH = 1

D_MIX = D_MODEL
GM_WIDTH = D_MIX // 2
LRU_WIDTH = D_MIX - GM_WIDTH
CHUNK = 128
GM_HEADS = 8
GM_HEAD_DIM = GM_WIDTH // GM_HEADS
LRU_HEADS = 8
LRU_BLOCK = LRU_WIDTH // LRU_HEADS
LRU_CONV = 4
LRU_C = 8.0
FFN_MULT = 3
D_FF = FFN_MULT * D_MODEL
FFN_CONV = 3
IN_COLS = 2 * GM_WIDTH + 2 * LRU_WIDTH
RMS_EPS = 1e-6
LN_EPS = 1e-5

kernel_name = "hymba_style_gmlp_rglru_convffn"


def _rmsnorm(x, g):
    xf = x.astype(jnp.float32)
    y = xf * lax.rsqrt(jnp.mean(xf * xf, axis=-1, keepdims=True) + RMS_EPS)
    return (y * g.astype(jnp.float32)).astype(x.dtype)


def _layernorm(x, g, b):
    xf = x.astype(jnp.float32)
    mu = jnp.mean(xf, axis=-1, keepdims=True)
    xc = xf - mu
    y = xc * lax.rsqrt(jnp.mean(xc * xc, axis=-1, keepdims=True) + LN_EPS)
    return (y * g.astype(jnp.float32) + b.astype(jnp.float32)).astype(x.dtype)


def _causal_dwconv(x, w, b):
    k_width = w.shape[0]
    s = x.shape[1]
    xp = jnp.pad(x, ((0, 0), (k_width - 1, 0), (0, 0)))
    y = b
    for k in range(k_width):
        y = y + xp[:, k:k + s] * w[k]
    return y


def _spatial_gating(z, v_g, v_b, ws, bs):
    u, v = jnp.split(z, 2, axis=-1)
    v = _layernorm(v, v_g, v_b)
    bsz, s, _ = v.shape
    vc = v.reshape(bsz, s // CHUNK, CHUNK, GM_HEADS, GM_HEAD_DIM)
    mask = jnp.tril(jnp.ones((CHUNK, CHUNK), dtype=bool))
    w = jnp.where(mask[None], ws, jnp.zeros((), ws.dtype))
    mixed = jnp.einsum('hts,bcshd->bcthd', w, vc) + bs.T[None, None, :, :, None]
    return u * mixed.reshape(bsz, s, GM_WIDTH)


def _lru_combine(left, right):
    a1, b1 = left
    a2, b2 = right
    return a1 * a2, a2 * b1 + b2


def _rg_lru(x, wa, ba, wx, bx, lam):
    bsz, s, w = x.shape
    xh = x.reshape(bsz, s, LRU_HEADS, LRU_BLOCK)
    r = jax.nn.sigmoid(jnp.einsum('bshi,hij->bshj', xh, wa) + ba).reshape(bsz, s, w)
    i = jax.nn.sigmoid(jnp.einsum('bshi,hij->bshj', xh, wx) + bx).reshape(bsz, s, w)
    log_a = -LRU_C * r.astype(jnp.float32) * jax.nn.softplus(-lam.astype(jnp.float32))
    a = jnp.exp(log_a)
    mult = jnp.sqrt(-jnp.expm1(2.0 * log_a))
    b = mult * (i * x).astype(jnp.float32)
    _, h = lax.associative_scan(_lru_combine, (a, b), axis=1)
    return h.astype(x.dtype)


def _fwd_setup_inputs(seed: int = 0) -> dict:
    key = jax.random.key(seed)
    ks = jax.random.split(key, 24)
    f32 = jnp.float32
    L = DEPTH

    def nrm(k, shape, scale):
        return jax.random.normal(k, shape, f32) * scale

    x = jax.random.normal(ks[0], (BATCH, SEQ, D_MODEL), f32)
    norm1_g = 1.0 + nrm(ks[1], (L, D_MODEL), 0.05)
    w_in = nrm(ks[2], (L, D_MODEL, IN_COLS), D_MODEL ** -0.5)
    gm_v_g = 1.0 + nrm(ks[3], (L, GM_WIDTH), 0.05)
    gm_v_b = nrm(ks[4], (L, GM_WIDTH), 0.02)
    gm_ws = nrm(ks[5], (L, GM_HEADS, CHUNK, CHUNK), CHUNK ** -0.5)
    gm_bs = 1.0 + nrm(ks[6], (L, GM_HEADS, CHUNK), 0.1)
    lru_conv_w = nrm(ks[7], (L, LRU_CONV, LRU_WIDTH), LRU_CONV ** -0.5)
    lru_conv_b = nrm(ks[8], (L, LRU_WIDTH), 0.02)
    lru_wa = nrm(ks[9], (L, LRU_HEADS, LRU_BLOCK, LRU_BLOCK), LRU_BLOCK ** -0.5)
    lru_ba = nrm(ks[10], (L, LRU_HEADS, LRU_BLOCK), 0.02)
    lru_wx = nrm(ks[11], (L, LRU_HEADS, LRU_BLOCK, LRU_BLOCK), LRU_BLOCK ** -0.5)
    lru_bx = nrm(ks[12], (L, LRU_HEADS, LRU_BLOCK), 0.02)
    a_c = jax.random.uniform(ks[13], (L, LRU_WIDTH), f32, 0.9, 0.999)
    a_base = a_c ** (1.0 / LRU_C)
    lru_lambda = jnp.log(a_base) - jnp.log1p(-a_base)
    gm_out_g = 1.0 + nrm(ks[14], (L, GM_WIDTH), 0.05)
    lru_out_g = 1.0 + nrm(ks[15], (L, LRU_WIDTH), 0.05)
    w_out = nrm(ks[16], (L, D_MIX, D_MODEL), D_MIX ** -0.5)
    norm2_g = 1.0 + nrm(ks[17], (L, D_MODEL), 0.05)
    ffn_w_up = nrm(ks[18], (L, D_MODEL, 2 * D_FF), D_MODEL ** -0.5)
    ffn_conv_w = nrm(ks[19], (L, FFN_CONV, 2 * D_FF), FFN_CONV ** -0.5)
    ffn_conv_b = nrm(ks[20], (L, 2 * D_FF), 0.02)
    ffn_w_down = nrm(ks[21], (L, D_FF, D_MODEL), D_FF ** -0.5)
    final_g = 1.0 + nrm(ks[22], (D_MODEL,), 0.05)
    return {
        "x": x, "norm1_g": norm1_g, "w_in": w_in,
        "gm_v_g": gm_v_g, "gm_v_b": gm_v_b, "gm_ws": gm_ws, "gm_bs": gm_bs,
        "lru_conv_w": lru_conv_w, "lru_conv_b": lru_conv_b,
        "lru_wa": lru_wa, "lru_ba": lru_ba, "lru_wx": lru_wx, "lru_bx": lru_bx,
        "lru_lambda": lru_lambda, "gm_out_g": gm_out_g, "lru_out_g": lru_out_g,
        "w_out": w_out, "norm2_g": norm2_g, "ffn_w_up": ffn_w_up,
        "ffn_conv_w": ffn_conv_w, "ffn_conv_b": ffn_conv_b, "ffn_w_down": ffn_w_down,
        "final_g": final_g,
    }


def _fwd_reference(x, norm1_g, w_in, gm_v_g, gm_v_b, gm_ws, gm_bs, lru_conv_w, lru_conv_b,
              lru_wa, lru_ba, lru_wx, lru_bx, lru_lambda, gm_out_g, lru_out_g, w_out,
              norm2_g, ffn_w_up, ffn_conv_w, ffn_conv_b, ffn_w_down, final_g):
    for l in range(DEPTH):
        h = _rmsnorm(x, norm1_g[l])
        p = jnp.einsum('bsd,de->bse', h, w_in[l])
        z_gm = p[..., :2 * GM_WIDTH]
        g_lru = p[..., 2 * GM_WIDTH:2 * GM_WIDTH + LRU_WIDTH]
        x_lru = p[..., 2 * GM_WIDTH + LRU_WIDTH:]
        y_gm = _spatial_gating(jax.nn.gelu(z_gm), gm_v_g[l], gm_v_b[l], gm_ws[l], gm_bs[l])
        xr = _causal_dwconv(x_lru, lru_conv_w[l], lru_conv_b[l])
        y_lru = _rg_lru(xr, lru_wa[l], lru_ba[l], lru_wx[l], lru_bx[l], lru_lambda[l])
        y_lru = y_lru * jax.nn.gelu(g_lru)
        y = jnp.concatenate([_rmsnorm(y_gm, gm_out_g[l]), _rmsnorm(y_lru, lru_out_g[l])], axis=-1)
        x = x + jnp.einsum('bse,ed->bsd', y, w_out[l])
        h = _rmsnorm(x, norm2_g[l])
        up = jnp.einsum('bsd,df->bsf', h, ffn_w_up[l])
        up = _causal_dwconv(up, ffn_conv_w[l], ffn_conv_b[l])
        gate, val = jnp.split(up, 2, axis=-1)
        x = x + jnp.einsum('bsf,fd->bsd', jax.nn.gelu(gate) * val, ffn_w_down[l])
    return _rmsnorm(x, final_g)


import jax as _jax
import jax.numpy as _jnp

TWIN_FORMAT = 'train_step'
FWD_PARAMS = ['x', 'norm1_g', 'w_in', 'gm_v_g', 'gm_v_b', 'gm_ws', 'gm_bs', 'lru_conv_w', 'lru_conv_b', 'lru_wa', 'lru_ba', 'lru_wx', 'lru_bx', 'lru_lambda', 'gm_out_g', 'lru_out_g', 'w_out', 'norm2_g', 'ffn_w_up', 'ffn_conv_w', 'ffn_conv_b', 'ffn_w_down', 'final_g']
TWIN_WEIGHTS = ['norm1_g', 'w_in', 'gm_v_g', 'gm_v_b', 'gm_ws', 'gm_bs', 'lru_conv_w', 'lru_conv_b', 'lru_wa', 'lru_ba', 'lru_wx', 'lru_bx', 'lru_lambda', 'gm_out_g', 'lru_out_g', 'w_out', 'norm2_g', 'ffn_w_up', 'ffn_conv_w', 'ffn_conv_b', 'ffn_w_down', 'final_g']
TWIN_DIFF_INPUT = 'x'
TWIN_INPUTS = ['x', 'norm1_g', 'w_in', 'gm_v_g', 'gm_v_b', 'gm_ws', 'gm_bs', 'lru_conv_w', 'lru_conv_b', 'lru_wa', 'lru_ba', 'lru_wx', 'lru_bx', 'lru_lambda', 'gm_out_g', 'lru_out_g', 'w_out', 'norm2_g', 'ffn_w_up', 'ffn_conv_w', 'ffn_conv_b', 'ffn_w_down', 'final_g', 'loss_target', 'm_norm1_g', 'm_w_in', 'm_gm_v_g', 'm_gm_v_b', 'm_gm_ws', 'm_gm_bs', 'm_lru_conv_w', 'm_lru_conv_b', 'm_lru_wa', 'm_lru_ba', 'm_lru_wx', 'm_lru_bx', 'm_lru_lambda', 'm_gm_out_g', 'm_lru_out_g', 'm_w_out', 'm_norm2_g', 'm_ffn_w_up', 'm_ffn_conv_w', 'm_ffn_conv_b', 'm_ffn_w_down', 'm_final_g', 'v_norm1_g', 'v_w_in', 'v_gm_v_g', 'v_gm_v_b', 'v_gm_ws', 'v_gm_bs', 'v_lru_conv_w', 'v_lru_conv_b', 'v_lru_wa', 'v_lru_ba', 'v_lru_wx', 'v_lru_bx', 'v_lru_lambda', 'v_gm_out_g', 'v_lru_out_g', 'v_w_out', 'v_norm2_g', 'v_ffn_w_up', 'v_ffn_conv_w', 'v_ffn_conv_b', 'v_ffn_w_down', 'v_final_g']
TWIN_OUTPUTS = ['loss', 'grad_x', 'grad_norm1_g', 'grad_w_in', 'grad_gm_v_g', 'grad_gm_v_b', 'grad_gm_ws', 'grad_gm_bs', 'grad_lru_conv_w', 'grad_lru_conv_b', 'grad_lru_wa', 'grad_lru_ba', 'grad_lru_wx', 'grad_lru_bx', 'grad_lru_lambda', 'grad_gm_out_g', 'grad_lru_out_g', 'grad_w_out', 'grad_norm2_g', 'grad_ffn_w_up', 'grad_ffn_conv_w', 'grad_ffn_conv_b', 'grad_ffn_w_down', 'grad_final_g', 'delta_norm1_g', 'delta_w_in', 'delta_gm_v_g', 'delta_gm_v_b', 'delta_gm_ws', 'delta_gm_bs', 'delta_lru_conv_w', 'delta_lru_conv_b', 'delta_lru_wa', 'delta_lru_ba', 'delta_lru_wx', 'delta_lru_bx', 'delta_lru_lambda', 'delta_gm_out_g', 'delta_lru_out_g', 'delta_w_out', 'delta_norm2_g', 'delta_ffn_w_up', 'delta_ffn_conv_w', 'delta_ffn_conv_b', 'delta_ffn_w_down', 'delta_final_g', 'new_m_norm1_g', 'new_m_w_in', 'new_m_gm_v_g', 'new_m_gm_v_b', 'new_m_gm_ws', 'new_m_gm_bs', 'new_m_lru_conv_w', 'new_m_lru_conv_b', 'new_m_lru_wa', 'new_m_lru_ba', 'new_m_lru_wx', 'new_m_lru_bx', 'new_m_lru_lambda', 'new_m_gm_out_g', 'new_m_lru_out_g', 'new_m_w_out', 'new_m_norm2_g', 'new_m_ffn_w_up', 'new_m_ffn_conv_w', 'new_m_ffn_conv_b', 'new_m_ffn_w_down', 'new_m_final_g', 'new_v_norm1_g', 'new_v_w_in', 'new_v_gm_v_g', 'new_v_gm_v_b', 'new_v_gm_ws', 'new_v_gm_bs', 'new_v_lru_conv_w', 'new_v_lru_conv_b', 'new_v_lru_wa', 'new_v_lru_ba', 'new_v_lru_wx', 'new_v_lru_bx', 'new_v_lru_lambda', 'new_v_gm_out_g', 'new_v_lru_out_g', 'new_v_w_out', 'new_v_norm2_g', 'new_v_ffn_w_up', 'new_v_ffn_conv_w', 'new_v_ffn_conv_b', 'new_v_ffn_w_down', 'new_v_final_g']
TWIN_LEAF_KINDS = {'loss': 'loss', 'grad_x': 'grad_x', 'grad_norm1_g': 'grad_w', 'grad_w_in': 'grad_w', 'grad_gm_v_g': 'grad_w', 'grad_gm_v_b': 'grad_w', 'grad_gm_ws': 'grad_w', 'grad_gm_bs': 'grad_w', 'grad_lru_conv_w': 'grad_w', 'grad_lru_conv_b': 'grad_w', 'grad_lru_wa': 'grad_w', 'grad_lru_ba': 'grad_w', 'grad_lru_wx': 'grad_w', 'grad_lru_bx': 'grad_w', 'grad_lru_lambda': 'grad_w', 'grad_gm_out_g': 'grad_w', 'grad_lru_out_g': 'grad_w', 'grad_w_out': 'grad_w', 'grad_norm2_g': 'grad_w', 'grad_ffn_w_up': 'grad_w', 'grad_ffn_conv_w': 'grad_w', 'grad_ffn_conv_b': 'grad_w', 'grad_ffn_w_down': 'grad_w', 'grad_final_g': 'grad_w', 'delta_norm1_g': 'delta_w', 'delta_w_in': 'delta_w', 'delta_gm_v_g': 'delta_w', 'delta_gm_v_b': 'delta_w', 'delta_gm_ws': 'delta_w', 'delta_gm_bs': 'delta_w', 'delta_lru_conv_w': 'delta_w', 'delta_lru_conv_b': 'delta_w', 'delta_lru_wa': 'delta_w', 'delta_lru_ba': 'delta_w', 'delta_lru_wx': 'delta_w', 'delta_lru_bx': 'delta_w', 'delta_lru_lambda': 'delta_w', 'delta_gm_out_g': 'delta_w', 'delta_lru_out_g': 'delta_w', 'delta_w_out': 'delta_w', 'delta_norm2_g': 'delta_w', 'delta_ffn_w_up': 'delta_w', 'delta_ffn_conv_w': 'delta_w', 'delta_ffn_conv_b': 'delta_w', 'delta_ffn_w_down': 'delta_w', 'delta_final_g': 'delta_w', 'new_m_norm1_g': 'new_m', 'new_m_w_in': 'new_m', 'new_m_gm_v_g': 'new_m', 'new_m_gm_v_b': 'new_m', 'new_m_gm_ws': 'new_m', 'new_m_gm_bs': 'new_m', 'new_m_lru_conv_w': 'new_m', 'new_m_lru_conv_b': 'new_m', 'new_m_lru_wa': 'new_m', 'new_m_lru_ba': 'new_m', 'new_m_lru_wx': 'new_m', 'new_m_lru_bx': 'new_m', 'new_m_lru_lambda': 'new_m', 'new_m_gm_out_g': 'new_m', 'new_m_lru_out_g': 'new_m', 'new_m_w_out': 'new_m', 'new_m_norm2_g': 'new_m', 'new_m_ffn_w_up': 'new_m', 'new_m_ffn_conv_w': 'new_m', 'new_m_ffn_conv_b': 'new_m', 'new_m_ffn_w_down': 'new_m', 'new_m_final_g': 'new_m', 'new_v_norm1_g': 'new_v', 'new_v_w_in': 'new_v', 'new_v_gm_v_g': 'new_v', 'new_v_gm_v_b': 'new_v', 'new_v_gm_ws': 'new_v', 'new_v_gm_bs': 'new_v', 'new_v_lru_conv_w': 'new_v', 'new_v_lru_conv_b': 'new_v', 'new_v_lru_wa': 'new_v', 'new_v_lru_ba': 'new_v', 'new_v_lru_wx': 'new_v', 'new_v_lru_bx': 'new_v', 'new_v_lru_lambda': 'new_v', 'new_v_gm_out_g': 'new_v', 'new_v_lru_out_g': 'new_v', 'new_v_w_out': 'new_v', 'new_v_norm2_g': 'new_v', 'new_v_ffn_w_up': 'new_v', 'new_v_ffn_conv_w': 'new_v', 'new_v_ffn_conv_b': 'new_v', 'new_v_ffn_w_down': 'new_v', 'new_v_final_g': 'new_v'}


def _forward(args):
    return _fwd_reference(*[args[k] for k in FWD_PARAMS])


def _output_shape():
    def fwd():
        inp = _fwd_setup_inputs(0)
        return _fwd_reference(*[inp[k] for k in FWD_PARAMS])
    out = _jax.eval_shape(fwd)
    return out.shape, out.dtype

N_MICROBATCH = 1
ADAM_LR = 0.001
ADAM_B1 = 0.9
ADAM_B2 = 0.999
ADAM_EPS = 1e-08
ADAM_WD = 0.01
ADAM_STEP = 10
PER_EXAMPLE_BATCH_AXIS = {'x': 0, 'loss_target': 0}
SHARED_INPUTS = []
_WEIGHT_DTYPES = {'norm1_g': _jnp.float32, 'w_in': _jnp.float32, 'gm_v_g': _jnp.float32, 'gm_v_b': _jnp.float32, 'gm_ws': _jnp.float32, 'gm_bs': _jnp.float32, 'lru_conv_w': _jnp.float32, 'lru_conv_b': _jnp.float32, 'lru_wa': _jnp.float32, 'lru_ba': _jnp.float32, 'lru_wx': _jnp.float32, 'lru_bx': _jnp.float32, 'lru_lambda': _jnp.float32, 'gm_out_g': _jnp.float32, 'lru_out_g': _jnp.float32, 'w_out': _jnp.float32, 'norm2_g': _jnp.float32, 'ffn_w_up': _jnp.float32, 'ffn_conv_w': _jnp.float32, 'ffn_conv_b': _jnp.float32, 'ffn_w_down': _jnp.float32, 'final_g': _jnp.float32}
MOMENT_SCALE = {'norm1_g': 1.340750e-01, 'w_in': 9.247504e-02, 'gm_v_g': 5.334843e-02, 'gm_v_b': 5.276284e-02, 'gm_ws': 5.387854e-02, 'gm_bs': 7.857307e-02, 'lru_conv_w': 1.114589e-01, 'lru_conv_b': 1.456802e+00, 'lru_wa': 3.860244e-02, 'lru_ba': 3.140555e-02, 'lru_wx': 6.922620e-02, 'lru_bx': 3.622961e-02, 'lru_lambda': 5.863251e-02, 'gm_out_g': 1.219593e-01, 'lru_out_g': 1.073915e-01, 'w_out': 1.200393e-01, 'norm2_g': 8.226645e-02, 'ffn_w_up': 3.138557e-02, 'ffn_conv_w': 3.217120e-02, 'ffn_conv_b': 4.333283e-02, 'ffn_w_down': 5.528406e-02, 'final_g': 3.203635e+01}


def _to_microbatches(a, axis):
    t = _jnp.moveaxis(a, axis, 0)
    t = t.reshape((N_MICROBATCH, t.shape[0] // N_MICROBATCH) + t.shape[1:])
    return _jnp.moveaxis(t, 1, axis + 1)


def setup_inputs(seed: int = 0) -> dict:
    inp = _fwd_setup_inputs(seed)
    key = _jax.random.fold_in(_jax.random.key(seed), 7919)
    shape, _ = _output_shape()
    out = dict(inp)
    out["loss_target"] = _jax.random.normal(_jax.random.fold_in(key, 0), shape, _jnp.float32)
    for i, name in enumerate(TWIN_WEIGHTS):
        w = inp[name].astype(_jnp.float32)
        if MOMENT_SCALE is None:
            s = _jnp.sqrt(_jnp.mean(_jnp.square(w)) + 1e-30)
        else:
            s = MOMENT_SCALE[name]
        km, kv = _jax.random.split(_jax.random.fold_in(key, i + 1))
        out[name] = w
        out["m_" + name] = s * _jax.random.normal(km, w.shape, _jnp.float32)
        out["v_" + name] = (s * s) * _jax.random.uniform(kv, w.shape, _jnp.float32, 0.5, 1.5)
    if N_MICROBATCH > 1:
        for name, axis in PER_EXAMPLE_BATCH_AXIS.items():
            out[name] = _to_microbatches(out[name], axis)
    return {'x': out['x'], 'norm1_g': out['norm1_g'], 'w_in': out['w_in'], 'gm_v_g': out['gm_v_g'], 'gm_v_b': out['gm_v_b'], 'gm_ws': out['gm_ws'], 'gm_bs': out['gm_bs'], 'lru_conv_w': out['lru_conv_w'], 'lru_conv_b': out['lru_conv_b'], 'lru_wa': out['lru_wa'], 'lru_ba': out['lru_ba'], 'lru_wx': out['lru_wx'], 'lru_bx': out['lru_bx'], 'lru_lambda': out['lru_lambda'], 'gm_out_g': out['gm_out_g'], 'lru_out_g': out['lru_out_g'], 'w_out': out['w_out'], 'norm2_g': out['norm2_g'], 'ffn_w_up': out['ffn_w_up'], 'ffn_conv_w': out['ffn_conv_w'], 'ffn_conv_b': out['ffn_conv_b'], 'ffn_w_down': out['ffn_w_down'], 'final_g': out['final_g'], 'loss_target': out['loss_target'], 'm_norm1_g': out['m_norm1_g'], 'm_w_in': out['m_w_in'], 'm_gm_v_g': out['m_gm_v_g'], 'm_gm_v_b': out['m_gm_v_b'], 'm_gm_ws': out['m_gm_ws'], 'm_gm_bs': out['m_gm_bs'], 'm_lru_conv_w': out['m_lru_conv_w'], 'm_lru_conv_b': out['m_lru_conv_b'], 'm_lru_wa': out['m_lru_wa'], 'm_lru_ba': out['m_lru_ba'], 'm_lru_wx': out['m_lru_wx'], 'm_lru_bx': out['m_lru_bx'], 'm_lru_lambda': out['m_lru_lambda'], 'm_gm_out_g': out['m_gm_out_g'], 'm_lru_out_g': out['m_lru_out_g'], 'm_w_out': out['m_w_out'], 'm_norm2_g': out['m_norm2_g'], 'm_ffn_w_up': out['m_ffn_w_up'], 'm_ffn_conv_w': out['m_ffn_conv_w'], 'm_ffn_conv_b': out['m_ffn_conv_b'], 'm_ffn_w_down': out['m_ffn_w_down'], 'm_final_g': out['m_final_g'], 'v_norm1_g': out['v_norm1_g'], 'v_w_in': out['v_w_in'], 'v_gm_v_g': out['v_gm_v_g'], 'v_gm_v_b': out['v_gm_v_b'], 'v_gm_ws': out['v_gm_ws'], 'v_gm_bs': out['v_gm_bs'], 'v_lru_conv_w': out['v_lru_conv_w'], 'v_lru_conv_b': out['v_lru_conv_b'], 'v_lru_wa': out['v_lru_wa'], 'v_lru_ba': out['v_lru_ba'], 'v_lru_wx': out['v_lru_wx'], 'v_lru_bx': out['v_lru_bx'], 'v_lru_lambda': out['v_lru_lambda'], 'v_gm_out_g': out['v_gm_out_g'], 'v_lru_out_g': out['v_lru_out_g'], 'v_w_out': out['v_w_out'], 'v_norm2_g': out['v_norm2_g'], 'v_ffn_w_up': out['v_ffn_w_up'], 'v_ffn_conv_w': out['v_ffn_conv_w'], 'v_ffn_conv_b': out['v_ffn_conv_b'], 'v_ffn_w_down': out['v_ffn_w_down'], 'v_final_g': out['v_final_g']}


def _loss(weights, diff, rest, loss_target):
    with _jax.named_scope("forward"):
        args = {**rest, TWIN_DIFF_INPUT: diff, **{k: w.astype(_WEIGHT_DTYPES[k]) for k, w in weights.items()}}
        y = _forward(args)
    with _jax.named_scope("loss_head"):
        err = _jnp.square(y.astype(_jnp.float32) - loss_target)
        return 0.5 * _jnp.sum(_jnp.mean(err, axis=-1)) if err.ndim else 0.5 * err


def _adamw(w, g, m, v):
    m = ADAM_B1 * m + (1.0 - ADAM_B1) * g
    v = ADAM_B2 * v + (1.0 - ADAM_B2) * _jnp.square(g)
    m_hat = m / (1.0 - ADAM_B1 ** ADAM_STEP)
    v_hat = v / (1.0 - ADAM_B2 ** ADAM_STEP)
    delta = -ADAM_LR * (m_hat / (_jnp.sqrt(v_hat) + ADAM_EPS) + ADAM_WD * w)
    return delta, m, v


def reference(x, norm1_g, w_in, gm_v_g, gm_v_b, gm_ws, gm_bs, lru_conv_w, lru_conv_b, lru_wa, lru_ba, lru_wx, lru_bx, lru_lambda, gm_out_g, lru_out_g, w_out, norm2_g, ffn_w_up, ffn_conv_w, ffn_conv_b, ffn_w_down, final_g, loss_target, m_norm1_g, m_w_in, m_gm_v_g, m_gm_v_b, m_gm_ws, m_gm_bs, m_lru_conv_w, m_lru_conv_b, m_lru_wa, m_lru_ba, m_lru_wx, m_lru_bx, m_lru_lambda, m_gm_out_g, m_lru_out_g, m_w_out, m_norm2_g, m_ffn_w_up, m_ffn_conv_w, m_ffn_conv_b, m_ffn_w_down, m_final_g, v_norm1_g, v_w_in, v_gm_v_g, v_gm_v_b, v_gm_ws, v_gm_bs, v_lru_conv_w, v_lru_conv_b, v_lru_wa, v_lru_ba, v_lru_wx, v_lru_bx, v_lru_lambda, v_gm_out_g, v_lru_out_g, v_w_out, v_norm2_g, v_ffn_w_up, v_ffn_conv_w, v_ffn_conv_b, v_ffn_w_down, v_final_g):
    given = dict(x=x, norm1_g=norm1_g, w_in=w_in, gm_v_g=gm_v_g, gm_v_b=gm_v_b, gm_ws=gm_ws, gm_bs=gm_bs, lru_conv_w=lru_conv_w, lru_conv_b=lru_conv_b, lru_wa=lru_wa, lru_ba=lru_ba, lru_wx=lru_wx, lru_bx=lru_bx, lru_lambda=lru_lambda, gm_out_g=gm_out_g, lru_out_g=lru_out_g, w_out=w_out, norm2_g=norm2_g, ffn_w_up=ffn_w_up, ffn_conv_w=ffn_conv_w, ffn_conv_b=ffn_conv_b, ffn_w_down=ffn_w_down, final_g=final_g, loss_target=loss_target, m_norm1_g=m_norm1_g, m_w_in=m_w_in, m_gm_v_g=m_gm_v_g, m_gm_v_b=m_gm_v_b, m_gm_ws=m_gm_ws, m_gm_bs=m_gm_bs, m_lru_conv_w=m_lru_conv_w, m_lru_conv_b=m_lru_conv_b, m_lru_wa=m_lru_wa, m_lru_ba=m_lru_ba, m_lru_wx=m_lru_wx, m_lru_bx=m_lru_bx, m_lru_lambda=m_lru_lambda, m_gm_out_g=m_gm_out_g, m_lru_out_g=m_lru_out_g, m_w_out=m_w_out, m_norm2_g=m_norm2_g, m_ffn_w_up=m_ffn_w_up, m_ffn_conv_w=m_ffn_conv_w, m_ffn_conv_b=m_ffn_conv_b, m_ffn_w_down=m_ffn_w_down, m_final_g=m_final_g, v_norm1_g=v_norm1_g, v_w_in=v_w_in, v_gm_v_g=v_gm_v_g, v_gm_v_b=v_gm_v_b, v_gm_ws=v_gm_ws, v_gm_bs=v_gm_bs, v_lru_conv_w=v_lru_conv_w, v_lru_conv_b=v_lru_conv_b, v_lru_wa=v_lru_wa, v_lru_ba=v_lru_ba, v_lru_wx=v_lru_wx, v_lru_bx=v_lru_bx, v_lru_lambda=v_lru_lambda, v_gm_out_g=v_gm_out_g, v_lru_out_g=v_lru_out_g, v_w_out=v_w_out, v_norm2_g=v_norm2_g, v_ffn_w_up=v_ffn_w_up, v_ffn_conv_w=v_ffn_conv_w, v_ffn_conv_b=v_ffn_conv_b, v_ffn_w_down=v_ffn_w_down, v_final_g=v_final_g)
    weights = {n: given[n] for n in TWIN_WEIGHTS}
    shared = {n: given[n] for n in SHARED_INPUTS}
    per_example = {n: given[n] for n in ['x']}
    grad_fn = _jax.value_and_grad(_loss, argnums=(0, 1))

    def one_microbatch(ex, loss_target):
        ex = dict(ex)
        diff = ex.pop(TWIN_DIFF_INPUT)
        return grad_fn(weights, diff, {**shared, **ex}, loss_target)

    if N_MICROBATCH == 1:
        loss, (grad_w, grad_x) = one_microbatch(per_example, given["loss_target"])
    else:
        def body(carry, xs):
            loss_sum, grad_sum = carry
            l_k, (gw_k, gx_k) = one_microbatch(xs[0], xs[1])
            with _jax.named_scope("update"):
                return (loss_sum + l_k, _jax.tree.map(_jnp.add, grad_sum, gw_k)), gx_k

        init = (_jnp.zeros((), _jnp.float32), _jax.tree.map(_jnp.zeros_like, weights))
        (loss, grad_w), grad_x = _jax.lax.scan(body, init, (per_example, given["loss_target"]))
    with _jax.named_scope("update"):
        delta_w, new_m, new_v = {}, {}, {}
        for n in TWIN_WEIGHTS:
            delta_w[n], new_m[n], new_v[n] = _adamw(weights[n], grad_w[n], given["m_" + n], given["v_" + n])
    return (loss, grad_x, *[grad_w[n] for n in TWIN_WEIGHTS], *[delta_w[n] for n in TWIN_WEIGHTS],
            *[new_m[n] for n in TWIN_WEIGHTS], *[new_v[n] for n in TWIN_WEIGHTS])
```

```python
import functools
import math

import jax
import jax.numpy as jnp
from jax import lax
from jax.experimental import pallas as pl
from jax.experimental.pallas import tpu as pltpu

F32 = jnp.float32
BF16 = jnp.bfloat16
MESH = pl.DeviceIdType.MESH
ANY = pl.BlockSpec(memory_space=pltpu.HBM)

GM_W = 1024
LRU_W = 1024
CHUNK = 128
HEADS = 8
HEAD_DIM = 128
LRU_C = 8.0
RMS_EPS = 1e-6
LN_EPS = 1e-5
ADAM_LR = 0.001
ADAM_B1 = 0.9
ADAM_B2 = 0.999
ADAM_EPS = 1e-08
ADAM_WD = 0.01
ADAM_STEP = 10

N_CHIPS = 4
HALO = 8
PACK_COLS = 1024
VMEM_LIMIT = 56 * 1024 * 1024

TM_IN = 512
TS_MIX = 256
TM_OUT = 512
TM_UP = 512
TN_UP = 512
TM_DN = 512
TK_DN = 512
TM_BW = 512
TK_BW = 512
TW_M = 1024
TW_N = 1024
TW_K = 512
T_ELEM = 256

_GELU_K0 = 0.7978845608028654
_GELU_K1 = 0.044715


def _pcall(body, *, name, out_shape, grid=None, in_specs=None, out_specs=None, scratch_shapes=(),
           grid_spec=None, dims=None, aliases=None):
    params = pltpu.CompilerParams(dimension_semantics=dims, vmem_limit_bytes=VMEM_LIMIT)
    kw = dict(name=name, out_shape=out_shape, compiler_params=params)
    if aliases:
        kw["input_output_aliases"] = aliases
    if grid_spec is not None:
        return pl.pallas_call(body, grid_spec=grid_spec, **kw)
    if grid is not None:
        kw["grid"] = grid
    return pl.pallas_call(body, in_specs=in_specs, out_specs=out_specs, scratch_shapes=list(scratch_shapes), **kw)


def _gelu(x):
    t = jnp.tanh(_GELU_K0 * (x + _GELU_K1 * (x * x * x)))
    return 0.5 * x * (1.0 + t)


def _gelu_and_grad(x):
    x2 = x * x
    t = jnp.tanh(_GELU_K0 * (x + _GELU_K1 * (x2 * x)))
    g = 0.5 * x * (1.0 + t)
    dg = 0.5 * (1.0 + t) + 0.5 * x * (1.0 - t * t) * (_GELU_K0 * (1.0 + 3.0 * _GELU_K1 * x2))
    return g, dg


def _sigmoid(x):
    return 1.0 / (1.0 + jnp.exp(-x))


def _neg_expm1(x):
    series = -x * (1.0 + x * (0.5 + x * (1.0 / 6.0 + x * (1.0 / 24.0 + x * (1.0 / 120.0 + x * (1.0 / 720.0))))))
    return jnp.where(x > -0.1, series, 1.0 - jnp.exp(x))


def _softplus(z):
    return jnp.maximum(z, 0.0) + jnp.log(1.0 + jnp.exp(-jnp.abs(z)))


def _rowmean(x):
    return jnp.mean(x, axis=-1, keepdims=True)


def _colsum(x):
    return jnp.sum(x, axis=0, keepdims=True)


def _rms_stats(x):
    r = lax.rsqrt(_rowmean(x * x) + RMS_EPS)
    return r, x * r


def _rms_bwd(dy, n, r, g):
    dn = dy * g
    return r * (dn - n * _rowmean(dn * n)), dy * n


def _shift_prev(x, halo, d):
    cat = jnp.concatenate([halo, x], axis=0)
    return pltpu.roll(cat, d, 0)[HALO:, :]


def _shift_next(x, halo, d):
    n = x.shape[0]
    cat = jnp.concatenate([x, halo], axis=0)
    return pltpu.roll(cat, n + HALO - d, 0)[:n, :]


def _dot(a, b):
    return jnp.dot(a, b, preferred_element_type=F32)


def _dot_nt(a, b):
    return lax.dot_general(a, b, (((1,), (1,)), ((), ())), preferred_element_type=F32)


def _dot_tn(a, b):
    return lax.dot_general(a, b, (((0,), (0,)), ((), ())), preferred_element_type=F32)


def _fwd_in_proj(x, g1, win_g):
    s_len, d = x.shape
    nsh, _, ncol = win_g.shape
    tm = min(TM_IN, s_len)

    def body(x_ref, g_ref, w_ref, p_ref, h_ref):
        @pl.when(pl.program_id(1) == 0)
        def _():
            _, n = _rms_stats(x_ref[...])
            h_ref[...] = (n * g_ref[...]).astype(BF16)

        p_ref[...] = _dot(h_ref[...], w_ref[...])

    return _pcall(
        body, name="fwd_in_proj", grid=(s_len // tm, nsh),
        in_specs=[pl.BlockSpec((tm, d), lambda i, j: (i, 0)),
                  pl.BlockSpec((1, d), lambda i, j: (0, 0)),
                  pl.BlockSpec((None, d, ncol), lambda i, j: (j, 0, 0))],
        out_specs=[pl.BlockSpec((tm, ncol), lambda i, j: (i, j)),
                   pl.BlockSpec((tm, d), lambda i, j: (i, 0))],
        out_shape=[jax.ShapeDtypeStruct((s_len, nsh * ncol), F32), jax.ShapeDtypeStruct((s_len, d), BF16)],
        dims=("arbitrary", "arbitrary"))(x, g1, win_g)


def _gm_forward(z, gv, bv, wt_ref, bst_ref, vl_s, mix_s):
    ts = z.shape[0]
    ge = _gelu(z)
    u = ge[:, :GM_W]
    v = ge[:, GM_W:]
    vc = v - _rowmean(v)
    rs = lax.rsqrt(_rowmean(vc * vc) + LN_EPS)
    vh = vc * rs
    vl_s[...] = (vh * gv + bv).astype(BF16)
    for cc in range(ts // CHUNK):
        rows = slice(cc * CHUNK, (cc + 1) * CHUNK)
        for hh in range(HEADS):
            cols = slice(hh * HEAD_DIM, (hh + 1) * HEAD_DIM)
            mix_s[rows, cols] = _dot(wt_ref[hh], vl_s[rows, cols]) + bst_ref[:, cols]
    mixed = mix_s[...]
    return u, mixed, vh, rs, u * mixed


def _lru_gates(xl, halo, wc_ref, bc_ref, wa_ref, ba_ref, wx_ref, bx_ref, lam_ref, z_s):
    x1 = _shift_prev(xl, halo, 1)
    x2 = _shift_prev(xl, halo, 2)
    x3 = _shift_prev(xl, halo, 3)
    xr = bc_ref[...] + wc_ref[0:1, :] * x3 + wc_ref[1:2, :] * x2 + wc_ref[2:3, :] * x1 + wc_ref[3:4, :] * xl
    xrb = xr.astype(BF16)
    for hh in range(HEADS):
        cols = slice(hh * HEAD_DIM, (hh + 1) * HEAD_DIM)
        z_s[:, cols] = _dot(xrb[:, cols], wa_ref[hh])
        z_s[:, LRU_W + hh * HEAD_DIM:LRU_W + (hh + 1) * HEAD_DIM] = _dot(xrb[:, cols], wx_ref[hh])
    ra = _sigmoid(z_s[:, :LRU_W] + ba_ref[...])
    ri = _sigmoid(z_s[:, LRU_W:] + bx_ref[...])
    sp = _softplus(-lam_ref[...])
    la = (-LRU_C) * ra * sp
    a = jnp.exp(la)
    mult = jnp.sqrt(_neg_expm1(2.0 * la))
    return dict(x1=x1, x2=x2, x3=x3, xr=xr, xrb=xrb, ra=ra, ri=ri, sp=sp, a=a, mult=mult)


def _fwd_mixers(p, gv, bv, wt, bst, wc, bc, wa, ba, wx, bx, lam, ggm, glru):
    s_len = p.shape[0]
    ts = min(TS_MIX, s_len)

    def body(pz_ref, pgl_ref, pxl_ref, gv_ref, bv_ref, wt_ref, bst_ref, wc_ref, bc_ref, wa_ref, ba_ref, wx_ref,
             bx_ref, lam_ref, ggm_ref, glru_ref, y_ref, hs_ref, tail_ref, h_ref, a_s, b_s, vl_s, mix_s, z_s):
        @pl.when(pl.program_id(0) == 0)
        def _():
            tail_ref[...] = jnp.zeros_like(tail_ref)
            h_ref[...] = jnp.zeros_like(h_ref)

        _, _, _, _, ygm = _gm_forward(pz_ref[...], gv_ref[...], bv_ref[...], wt_ref, bst_ref, vl_s, mix_s)
        _, ngm = _rms_stats(ygm)
        y_ref[:, :GM_W] = (ngm * ggm_ref[...]).astype(BF16)

        xl = pxl_ref[...]
        gts = _lru_gates(xl, tail_ref[...], wc_ref, bc_ref, wa_ref, ba_ref, wx_ref, bx_ref, lam_ref, z_s)
        tail_ref[...] = xl[ts - HALO:, :]
        a_s[...] = gts["a"]
        b_s[...] = gts["mult"] * (gts["ri"] * gts["xr"])

        def step(t, h):
            h = a_s[pl.ds(t, 1), :] * h + b_s[pl.ds(t, 1), :]
            hs_ref[pl.ds(t, 1), :] = h
            return h

        h_ref[...] = lax.fori_loop(0, ts, step, h_ref[...], unroll=8)
        yl = hs_ref[...] * _gelu(pgl_ref[...])
        _, nl = _rms_stats(yl)
        y_ref[:, GM_W:] = (nl * glru_ref[...]).astype(BF16)

    full = lambda shape: pl.BlockSpec(shape, lambda i: (0,) * len(shape))
    return _pcall(
        body, name="fwd_mixers", grid=(s_len // ts,),
        in_specs=[pl.BlockSpec((ts, 2 * GM_W), lambda i: (i, 0)),
                  pl.BlockSpec((ts, LRU_W), lambda i: (i, 2)),
                  pl.BlockSpec((ts, LRU_W), lambda i: (i, 3)),
                  full((1, GM_W)), full((1, GM_W)), full((HEADS, CHUNK, CHUNK)), full((CHUNK, GM_W)),
                  full((4, LRU_W)), full((1, LRU_W)), full((HEADS, HEAD_DIM, HEAD_DIM)), full((1, LRU_W)),
                  full((HEADS, HEAD_DIM, HEAD_DIM)), full((1, LRU_W)), full((1, LRU_W)), full((1, GM_W)),
                  full((1, LRU_W))],
        out_specs=[pl.BlockSpec((ts, GM_W + LRU_W), lambda i: (i, 0)), pl.BlockSpec((ts, LRU_W), lambda i: (i, 0))],
        out_shape=[jax.ShapeDtypeStruct((s_len, GM_W + LRU_W), BF16), jax.ShapeDtypeStruct((s_len, LRU_W), F32)],
        scratch_shapes=[pltpu.VMEM((HALO, LRU_W), F32), pltpu.VMEM((1, LRU_W), F32),
                        pltpu.VMEM((ts, LRU_W), F32), pltpu.VMEM((ts, LRU_W), F32),
                        pltpu.VMEM((ts, GM_W), BF16), pltpu.VMEM((ts, GM_W), F32), pltpu.VMEM((ts, 2 * LRU_W), F32)],
        dims=("arbitrary",))(p, p, p, gv, bv, wt, bst, wc, bc, wa, ba, wx, bx, lam, ggm, glru)


def _fwd_out_proj(x, y, wout_g, g2):
    s_len, d = x.shape
    tm = min(TM_OUT, s_len)

    def body(x_ref, y_ref, w_ref, g_ref, x2_ref, h2_ref):
        x2 = x_ref[...] + _dot(y_ref[...], w_ref[...])
        x2_ref[...] = x2
        _, n = _rms_stats(x2)
        h2_ref[...] = (n * g_ref[...]).astype(BF16)

    return _pcall(
        body, name="fwd_out_proj", grid=(s_len // tm,),
        in_specs=[pl.BlockSpec((tm, d), lambda i: (i, 0)), pl.BlockSpec((tm, d), lambda i: (i, 0)),
                  pl.BlockSpec((d, d), lambda i: (0, 0)), pl.BlockSpec((1, d), lambda i: (0, 0))],
        out_specs=[pl.BlockSpec((tm, d), lambda i: (i, 0)), pl.BlockSpec((tm, d), lambda i: (i, 0))],
        out_shape=[jax.ShapeDtypeStruct((s_len, d), F32), jax.ShapeDtypeStruct((s_len, d), BF16)],
        dims=("arbitrary",))(x, y, wout_g, g2)


def _fwd_ffn_up(h2, wup_g, wfc, bfc):
    s_len, d = h2.shape
    nsh, _, ncol = wup_g.shape
    f = nsh * ncol // 2
    tm = min(TM_UP, s_len)
    tn = TN_UP
    nps = ncol // tn
    nj = f // tn

    def body(h_ref, wg_ref, wv_ref, wcg_ref, wcv_ref, bg_ref, bv_ref, up_ref, act_ref, tail_ref):
        i, j = pl.program_id(0), pl.program_id(1)

        @pl.when(jnp.logical_and(i == 0, j == 0))
        def _():
            tail_ref[...] = jnp.zeros_like(tail_ref)

        h = h_ref[...]
        ug = _dot(h, wg_ref[...])
        uv = _dot(h, wv_ref[...])
        up_ref[0] = ug
        up_ref[1] = uv
        tg = tail_ref[0, j]
        tv = tail_ref[1, j]
        tail_ref[0, j] = ug[tm - HALO:, :]
        tail_ref[1, j] = uv[tm - HALO:, :]
        cg = bg_ref[...] + wcg_ref[0:1, :] * _shift_prev(ug, tg, 2) + wcg_ref[1:2, :] * _shift_prev(ug, tg, 1) \
            + wcg_ref[2:3, :] * ug
        cv = bv_ref[...] + wcv_ref[0:1, :] * _shift_prev(uv, tv, 2) + wcv_ref[1:2, :] * _shift_prev(uv, tv, 1) \
            + wcv_ref[2:3, :] * uv
        act_ref[...] = (_gelu(cg) * cv).astype(BF16)

    return _pcall(
        body, name="fwd_ffn_up", grid=(s_len // tm, nj),
        in_specs=[pl.BlockSpec((tm, d), lambda i, j: (i, 0)),
                  pl.BlockSpec((None, d, tn), lambda i, j: (j // nps, 0, j % nps)),
                  pl.BlockSpec((None, d, tn), lambda i, j: (nsh // 2 + j // nps, 0, j % nps)),
                  pl.BlockSpec((3, tn), lambda i, j: (0, j)), pl.BlockSpec((3, tn), lambda i, j: (0, nj + j)),
                  pl.BlockSpec((1, tn), lambda i, j: (0, j)), pl.BlockSpec((1, tn), lambda i, j: (0, nj + j))],
        out_specs=[pl.BlockSpec((2, tm, tn), lambda i, j: (0, i, j)), pl.BlockSpec((tm, tn), lambda i, j: (i, j))],
        out_shape=[jax.ShapeDtypeStruct((2, s_len, f), F32), jax.ShapeDtypeStruct((s_len, f), BF16)],
        scratch_shapes=[pltpu.VMEM((2, nj, HALO, tn), F32)],
        dims=("arbitrary", "arbitrary"))(h2, wup_g, wup_g, wfc, wfc, bfc, bfc)


def _fwd_down_loss(act, wdown_g, x2, gf, target):
    s_len, f = act.shape
    d = x2.shape[1]
    tm = min(TM_DN, s_len)
    tk = TK_DN
    nk = f // tk

    def body(a_ref, w_ref, x2_ref, g_ref, t_ref, dx3_ref, dx3b_ref, loss_ref, dgf_ref, acc_ref):
        i, k = pl.program_id(0), pl.program_id(1)

        @pl.when(jnp.logical_and(i == 0, k == 0))
        def _():
            loss_ref[...] = jnp.zeros_like(loss_ref)
            dgf_ref[...] = jnp.zeros_like(dgf_ref)

        @pl.when(k == 0)
        def _():
            acc_ref[...] = jnp.zeros_like(acc_ref)

        acc_ref[...] += _dot(a_ref[...], w_ref[...])

        @pl.when(k == nk - 1)
        def _():
            x3 = x2_ref[...] + acc_ref[...]
            r, n = _rms_stats(x3)
            g = g_ref[...]
            err = n * g - t_ref[...]
            loss_ref[...] += jnp.sum(err * err) * (0.5 / d)
            dx3, dgn = _rms_bwd(err * (1.0 / d), n, r, g)
            dgf_ref[...] += _colsum(dgn)
            dx3_ref[...] = dx3
            dx3b_ref[...] = dx3.astype(BF16)

    return _pcall(
        body, name="fwd_down_loss", grid=(s_len // tm, nk),
        in_specs=[pl.BlockSpec((tm, tk), lambda i, k: (i, k)), pl.BlockSpec((tk, d), lambda i, k: (k, 0)),
                  pl.BlockSpec((tm, d), lambda i, k: (i, 0)), pl.BlockSpec((1, d), lambda i, k: (0, 0)),
                  pl.BlockSpec((tm, d), lambda i, k: (i, 0))],
        out_specs=[pl.BlockSpec((tm, d), lambda i, k: (i, 0)), pl.BlockSpec((tm, d), lambda i, k: (i, 0)),
                   pl.BlockSpec((8, 128), lambda i, k: (0, 0)), pl.BlockSpec((1, d), lambda i, k: (0, 0))],
        out_shape=[jax.ShapeDtypeStruct((s_len, d), F32), jax.ShapeDtypeStruct((s_len, d), BF16),
                   jax.ShapeDtypeStruct((8, 128), F32), jax.ShapeDtypeStruct((1, d), F32)],
        scratch_shapes=[pltpu.VMEM((tm, d), F32)],
        dims=("arbitrary", "arbitrary"))(act, wdown_g, x2, gf, target)


def _bwd_ffn_act(dx3b, wdown_g, up, wfc, bfc):
    s_len, d = dx3b.shape
    f = up.shape[2]
    tm = min(TM_UP, s_len)
    tn = TN_UP
    nj = f // tn
    nt = s_len // tm
    hb = tm // HALO

    def body(dx_ref, w_ref, up_ref, halo_ref, wcg_ref, wcv_ref, bg_ref, bv_ref, dup_ref, dwf_ref, nxt_ref):
        i, j = pl.program_id(0), pl.program_id(1)
        ti = nt - 1 - i

        @pl.when(jnp.logical_and(i == 0, j == 0))
        def _():
            dwf_ref[...] = jnp.zeros_like(dwf_ref)
            nxt_ref[...] = jnp.zeros_like(nxt_ref)

        dact = _dot_nt(dx_ref[...], w_ref[...])
        outs = []
        for pln, (wc_ref, b_ref) in enumerate(((wcg_ref, bg_ref), (wcv_ref, bv_ref))):
            u = up_ref[pln]
            hal = jnp.where(ti == 0, 0.0, halo_ref[pln])
            u1 = _shift_prev(u, hal, 1)
            u2 = _shift_prev(u, hal, 2)
            c = b_ref[...] + wc_ref[0:1, :] * u2 + wc_ref[1:2, :] * u1 + wc_ref[2:3, :] * u
            outs.append((u, u1, u2, c))
        ge, gd = _gelu_and_grad(outs[0][3])
        dcs = (dact * outs[1][3] * gd, dact * ge)
        for pln, wc_ref in enumerate((wcg_ref, wcv_ref)):
            u, u1, u2, _ = outs[pln]
            dc = dcs[pln]
            dwf_ref[pln, j, 0:1, :] += _colsum(dc * u2)
            dwf_ref[pln, j, 1:2, :] += _colsum(dc * u1)
            dwf_ref[pln, j, 2:3, :] += _colsum(dc * u)
            dwf_ref[pln, j, 3:4, :] += _colsum(dc)
            nxt = nxt_ref[pln, j]
            nxt_ref[pln, j] = dc[:HALO, :]
            dup = wc_ref[2:3, :] * dc + wc_ref[1:2, :] * _shift_next(dc, nxt, 1) \
                + wc_ref[0:1, :] * _shift_next(dc, nxt, 2)
            dup_ref[pln] = dup.astype(BF16)

    return _pcall(
        body, name="bwd_ffn_act", grid=(nt, nj),
        in_specs=[pl.BlockSpec((tm, d), lambda i, j: (nt - 1 - i, 0)),
                  pl.BlockSpec((tn, d), lambda i, j: (j, 0)),
                  pl.BlockSpec((2, tm, tn), lambda i, j: (0, nt - 1 - i, j)),
                  pl.BlockSpec((2, HALO, tn), lambda i, j: (0, jnp.maximum((nt - 1 - i) * hb - 1, 0), j)),
                  pl.BlockSpec((3, tn), lambda i, j: (0, j)), pl.BlockSpec((3, tn), lambda i, j: (0, nj + j)),
                  pl.BlockSpec((1, tn), lambda i, j: (0, j)), pl.BlockSpec((1, tn), lambda i, j: (0, nj + j))],
        out_specs=[pl.BlockSpec((2, tm, tn), lambda i, j: (0, nt - 1 - i, j)),
                   pl.BlockSpec((2, nj, 8, tn), lambda i, j: (0, 0, 0, 0))],
        out_shape=[jax.ShapeDtypeStruct((2, s_len, f), BF16), jax.ShapeDtypeStruct((2, nj, 8, tn), F32)],
        scratch_shapes=[pltpu.VMEM((2, nj, HALO, tn), F32)],
        dims=("arbitrary", "arbitrary"))(dx3b, wdown_g, up, up, wfc, wfc, bfc, bfc)


def _bwd_proj_norm(name, dz, w_g, resid, x_in, g, *, planes):
    s_len, d = x_in.shape
    nsh, _, ncol = w_g.shape
    cols = dz.shape[2]
    tm = min(TM_BW, s_len)
    tk = TK_BW
    npl = cols // tk
    nps = ncol // tk
    nk = planes * npl

    def body(dz_ref, w_ref, r_ref, x_ref, g_ref, dx_ref, dxb_ref, dg_ref, acc_ref):
        i, k = pl.program_id(0), pl.program_id(1)

        @pl.when(jnp.logical_and(i == 0, k == 0))
        def _():
            dg_ref[...] = jnp.zeros_like(dg_ref)

        @pl.when(k == 0)
        def _():
            acc_ref[...] = jnp.zeros_like(acc_ref)

        acc_ref[...] += _dot_nt(dz_ref[...], w_ref[...])

        @pl.when(k == nk - 1)
        def _():
            r, n = _rms_stats(x_ref[...])
            dxn, dgn = _rms_bwd(acc_ref[...], n, r, g_ref[...])
            dg_ref[...] += _colsum(dgn)
            dx = r_ref[...] + dxn
            dx_ref[...] = dx
            dxb_ref[...] = dx.astype(BF16)

    return _pcall(
        body, name=name, grid=(s_len // tm, nk),
        in_specs=[pl.BlockSpec((None, tm, tk), lambda i, k: (k // npl, i, k % npl)),
                  pl.BlockSpec((None, d, tk), lambda i, k: (k // nps, 0, k % nps)),
                  pl.BlockSpec((tm, d), lambda i, k: (i, 0)), pl.BlockSpec((tm, d), lambda i, k: (i, 0)),
                  pl.BlockSpec((1, d), lambda i, k: (0, 0))],
        out_specs=[pl.BlockSpec((tm, d), lambda i, k: (i, 0)), pl.BlockSpec((tm, d), lambda i, k: (i, 0)),
                   pl.BlockSpec((1, d), lambda i, k: (0, 0))],
        out_shape=[jax.ShapeDtypeStruct((s_len, d), F32), jax.ShapeDtypeStruct((s_len, d), BF16),
                   jax.ShapeDtypeStruct((1, d), F32)],
        scratch_shapes=[pltpu.VMEM((tm, d), F32)],
        dims=("arbitrary", "arbitrary"))(dz, w_g, resid, x_in, g)


def _bwd_dy(dx2b, wout_g):
    s_len, d = dx2b.shape
    tm = min(TM_OUT, s_len)

    def body(dx_ref, w_ref, dy_ref):
        dy_ref[...] = _dot_nt(dx_ref[...], w_ref[...])

    return _pcall(
        body, name="bwd_dy", grid=(s_len // tm,),
        in_specs=[pl.BlockSpec((tm, d), lambda i: (i, 0)), pl.BlockSpec((d, d), lambda i: (0, 0))],
        out_specs=pl.BlockSpec((tm, d), lambda i: (i, 0)),
        out_shape=jax.ShapeDtypeStruct((s_len, d), F32), dims=("arbitrary",))(dx2b, wout_g)


def _bwd_mixers(p, dy, hs, gv, bv, wt, wtt, bst, wc, bc, wa, wat, ba, wx, wxt, bx, lam, ggm, glru):
    s_len = p.shape[0]
    ts = min(TS_MIX, s_len)
    nt = s_len // ts
    hb = ts // HALO

    def body(pz_ref, pgl_ref, pxl_ref, xh_ref, dy_ref, hs_ref, hh_ref, gv_ref, bv_ref, wt_ref, wtt_ref, bst_ref,
             wc_ref, bc_ref, wa_ref, wat_ref, ba_ref, wx_ref, wxt_ref, bx_ref, lam_ref, ggm_ref, glru_ref,
             dp_ref, dgv_ref, dbv_ref, dwt_ref, dbst_ref, dwc_ref, dbc_ref, dwa_ref, dba_ref, dwx_ref, dbx_ref,
             dsp_ref, dggm_ref, dglru_ref,
             carry_ref, nxt_ref, a_s, g_s, vl_s, mix_s, z_s, dm_s, dvl_s, dz_s, dxr_s):
        i = pl.program_id(0)
        ti = nt - 1 - i

        @pl.when(i == 0)
        def _():
            for ref in (dgv_ref, dbv_ref, dwt_ref, dbst_ref, dwc_ref, dbc_ref, dwa_ref, dba_ref, dwx_ref, dbx_ref,
                        dsp_ref, dggm_ref, dglru_ref, carry_ref, nxt_ref):
                ref[...] = jnp.zeros_like(ref)

        z = pz_ref[...]
        u, mixed, vh, rs, ygm = _gm_forward(z, gv_ref[...], bv_ref[...], wt_ref, bst_ref, vl_s, mix_s)
        rg, ngm = _rms_stats(ygm)
        dygm, dgn = _rms_bwd(dy_ref[:, :GM_W], ngm, rg, ggm_ref[...])
        dggm_ref[...] += _colsum(dgn)
        du = dygm * mixed
        dmix = dygm * u
        dm_s[...] = dmix.astype(BF16)
        bsum = dmix[0:CHUNK, :]
        for cc in range(1, ts // CHUNK):
            bsum = bsum + dmix[cc * CHUNK:(cc + 1) * CHUNK, :]
        dbst_ref[...] += bsum
        for hh in range(HEADS):
            cols = slice(hh * HEAD_DIM, (hh + 1) * HEAD_DIM)
            dw = jnp.zeros((CHUNK, CHUNK), F32)
            for cc in range(ts // CHUNK):
                rows = slice(cc * CHUNK, (cc + 1) * CHUNK)
                dmb = dm_s[rows, cols]
                dw = dw + _dot_nt(dmb, vl_s[rows, cols])
                dvl_s[rows, cols] = _dot(wtt_ref[hh], dmb)
            dwt_ref[hh] += dw
        dvl = dvl_s[...]
        dgv_ref[...] += _colsum(dvl * vh)
        dbv_ref[...] += _colsum(dvl)
        dvh = dvl * gv_ref[...]
        dv = rs * (dvh - _rowmean(dvh) - vh * _rowmean(dvh * vh))
        _, gd = _gelu_and_grad(z)
        dp_ref[:, :GM_W] = (du * gd[:, :GM_W]).astype(BF16)
        dp_ref[:, GM_W:2 * GM_W] = (dv * gd[:, GM_W:]).astype(BF16)

        xl = pxl_ref[...]
        xhalo = jnp.where(ti == 0, 0.0, xh_ref[...])
        gts = _lru_gates(xl, xhalo, wc_ref, bc_ref, wa_ref, ba_ref, wx_ref, bx_ref, lam_ref, z_s)
        a, mult, ra, ri, xr, sp = gts["a"], gts["mult"], gts["ra"], gts["ri"], gts["xr"], gts["sp"]
        hs = hs_ref[...]
        hprev = _shift_prev(hs, jnp.where(ti == 0, 0.0, hh_ref[...]), 1)
        gl = pgl_ref[...]
        ggl, dggl = _gelu_and_grad(gl)
        yl = hs * ggl
        rl, nl = _rms_stats(yl)
        dyl, dgn = _rms_bwd(dy_ref[:, GM_W:], nl, rl, glru_ref[...])
        dglru_ref[...] += _colsum(dgn)
        dp_ref[:, 2 * GM_W:2 * GM_W + LRU_W] = (dyl * hs * dggl).astype(BF16)
        a_s[...] = a
        g_s[...] = dyl * ggl

        def step(k, carry):
            t = ts - 1 - k
            gt = g_s[pl.ds(t, 1), :] + carry
            g_s[pl.ds(t, 1), :] = gt
            return a_s[pl.ds(t, 1), :] * gt

        carry_ref[...] = lax.fori_loop(0, ts, step, carry_ref[...], unroll=8)
        gsc = g_s[...]
        da = gsc * hprev
        rix = ri * xr
        dmult = gsc * rix
        dri = gsc * mult * xr
        dxr = gsc * mult * ri
        dla = da * a - dmult * (a * a) / mult
        dsp_ref[...] += _colsum(dla * ra) * (-LRU_C)
        dza = (dla * sp) * (-LRU_C) * ra * (1.0 - ra)
        dzi = dri * ri * (1.0 - ri)
        dba_ref[...] += _colsum(dza)
        dbx_ref[...] += _colsum(dzi)
        dzab = dza.astype(BF16)
        dzib = dzi.astype(BF16)
        xrb = gts["xrb"]
        for hh in range(HEADS):
            cols = slice(hh * HEAD_DIM, (hh + 1) * HEAD_DIM)
            dwa_ref[hh] += _dot_tn(xrb[:, cols], dzab[:, cols])
            dwx_ref[hh] += _dot_tn(xrb[:, cols], dzib[:, cols])
            dxr_s[:, cols] = _dot(dzab[:, cols], wat_ref[hh]) + _dot(dzib[:, cols], wxt_ref[hh])
        dxr = dxr + dxr_s[...]
        dbc_ref[...] += _colsum(dxr)
        dwc_ref[0:1, :] += _colsum(dxr * gts["x3"])
        dwc_ref[1:2, :] += _colsum(dxr * gts["x2"])
        dwc_ref[2:3, :] += _colsum(dxr * gts["x1"])
        dwc_ref[3:4, :] += _colsum(dxr * xl)
        nxt = nxt_ref[...]
        nxt_ref[...] = dxr[:HALO, :]
        dxl = wc_ref[3:4, :] * dxr + wc_ref[2:3, :] * _shift_next(dxr, nxt, 1) \
            + wc_ref[1:2, :] * _shift_next(dxr, nxt, 2) + wc_ref[0:1, :] * _shift_next(dxr, nxt, 3)
        dp_ref[:, 2 * GM_W + LRU_W:] = dxl.astype(BF16)

    full = lambda shape: pl.BlockSpec(shape, lambda i: (0,) * len(shape))
    rev = lambda i: nt - 1 - i
    prev_blk = lambda i: jnp.maximum((nt - 1 - i) * hb - 1, 0)
    hhd = (HEADS, HEAD_DIM, HEAD_DIM)
    small_shapes = [(1, GM_W), (1, GM_W), (HEADS, CHUNK, CHUNK), (CHUNK, GM_W), (4, LRU_W), (1, LRU_W), hhd,
                    (1, LRU_W), hhd, (1, LRU_W), (1, LRU_W), (1, GM_W), (1, LRU_W)]
    return _pcall(
        body, name="bwd_mixers", grid=(nt,),
        in_specs=[pl.BlockSpec((ts, 2 * GM_W), lambda i: (rev(i), 0)),
                  pl.BlockSpec((ts, LRU_W), lambda i: (rev(i), 2)),
                  pl.BlockSpec((ts, LRU_W), lambda i: (rev(i), 3)),
                  pl.BlockSpec((HALO, LRU_W), lambda i: (prev_blk(i), 3)),
                  pl.BlockSpec((ts, GM_W + LRU_W), lambda i: (rev(i), 0)),
                  pl.BlockSpec((ts, LRU_W), lambda i: (rev(i), 0)),
                  pl.BlockSpec((HALO, LRU_W), lambda i: (prev_blk(i), 0)),
                  full((1, GM_W)), full((1, GM_W)), full((HEADS, CHUNK, CHUNK)), full((HEADS, CHUNK, CHUNK)),
                  full((CHUNK, GM_W)), full((4, LRU_W)), full((1, LRU_W)), full(hhd), full(hhd), full((1, LRU_W)),
                  full(hhd), full(hhd), full((1, LRU_W)), full((1, LRU_W)), full((1, GM_W)), full((1, LRU_W))],
        out_specs=[pl.BlockSpec((ts, 2 * GM_W + 2 * LRU_W), lambda i: (rev(i), 0))] + [full(s) for s in small_shapes],
        out_shape=[jax.ShapeDtypeStruct((s_len, 2 * GM_W + 2 * LRU_W), BF16)]
        + [jax.ShapeDtypeStruct(s, F32) for s in small_shapes],
        scratch_shapes=[pltpu.VMEM((1, LRU_W), F32), pltpu.VMEM((HALO, LRU_W), F32),
                        pltpu.VMEM((ts, LRU_W), F32), pltpu.VMEM((ts, LRU_W), F32),
                        pltpu.VMEM((ts, GM_W), BF16), pltpu.VMEM((ts, GM_W), F32), pltpu.VMEM((ts, 2 * LRU_W), F32),
                        pltpu.VMEM((ts, GM_W), BF16), pltpu.VMEM((ts, GM_W), F32), pltpu.VMEM((ts, 2 * LRU_W), F32),
                        pltpu.VMEM((ts, LRU_W), F32)],
        dims=("arbitrary",))(p, p, p, p, dy, hs, hs, gv, bv, wt, wtt, bst, wc, bc, wa, wat, ba, wx, wxt, bx, lam,
                             ggm, glru)


def _bwd_weight(name, a, b, *, a_planes, b_planes, shard_rows):
    _, s_len, ma = a.shape
    _, _, nb = b.shape
    m, n = a_planes * ma, b_planes * nb
    tm, tn, tk = TW_M, TW_N, min(TW_K, s_len)
    if shard_rows:
        rows, cols = m // N_CHIPS // 2, n
        tm = min(tm, rows)
        tn = min(tn, cols)
        out_idx = lambda i, j, k: ((i * tm // rows) % 2, i * tm // (2 * rows), (i * tm % rows) // tm, j)
    else:
        rows, cols = m // 2, n // N_CHIPS
        tm = min(tm, rows)
        tn = min(tn, cols)
        out_idx = lambda i, j, k: (i * tm // rows, j * tn // cols, (i * tm % rows) // tm, (j * tn % cols) // tn)
    nk = s_len // tk
    npa, npb = ma // tm, nb // tn

    def body(a_ref, b_ref, o_ref, ob_ref, acc_ref):
        k = pl.program_id(2)

        @pl.when(k == 0)
        def _():
            acc_ref[...] = jnp.zeros_like(acc_ref)

        acc_ref[...] += _dot_tn(a_ref[...], b_ref[...])

        @pl.when(k == nk - 1)
        def _():
            o_ref[...] = acc_ref[...]
            ob_ref[...] = acc_ref[...].astype(BF16)

    shape = (2, N_CHIPS, rows, cols)
    return _pcall(
        body, name=name, grid=(m // tm, n // tn, nk),
        in_specs=[pl.BlockSpec((None, tk, tm), lambda i, j, k: (i // npa, k, i % npa)),
                  pl.BlockSpec((None, tk, tn), lambda i, j, k: (j // npb, k, j % npb))],
        out_specs=[pl.BlockSpec((None, None, tm, tn), out_idx), pl.BlockSpec((None, None, tm, tn), out_idx)],
        out_shape=[jax.ShapeDtypeStruct(shape, F32), jax.ShapeDtypeStruct(shape, BF16)],
        scratch_shapes=[pltpu.VMEM((tm, tn), F32)],
        dims=("arbitrary", "arbitrary", "arbitrary"))(a, b)


def _mesh_pos():
    return lax.axis_index("x"), lax.axis_index("y"), lax.axis_index("c")


def _other_chips(x, y):
    return [(1 - x, y), (x, 1 - y), (1 - x, 1 - y)]


def _all_gather(arrs):
    n = len(arrs)

    def body(*refs):
        ins, outs = refs[:n], refs[n:2 * n]
        send_sems, recv_sems, loc_sems = refs[2 * n:]
        x, y, c = _mesh_pos()
        s = 2 * x + y
        me, sib = (x, y, c), (x, y, 1 - c)
        chips = _other_chips(x, y)

        def rcopy(a, k, src, dst, to):
            return pltpu.make_async_remote_copy(src_ref=src, dst_ref=dst, send_sem=send_sems.at[a * 6 + k],
                                                recv_sem=recv_sems.at[a * 6 + k], device_id=to, device_id_type=MESH)

        locs = [pltpu.make_async_copy(ins[a], outs[a].at[s], loc_sems.at[a]) for a in range(n)]
        for cp in locs:
            cp.start()
        first = [rcopy(a, j, ins[a].at[c], outs[a].at[s, c], (cx, cy, c))
                 for a in range(n) for j, (cx, cy) in enumerate(chips)]
        for cp in first:
            cp.start()
        passed = []
        for a in range(n):
            for j, (cx, cy) in enumerate(chips):
                blk = outs[a].at[2 * cx + cy, c]
                rcopy(a, j, blk, blk, me).wait_recv()
                cp = rcopy(a, 3 + j, blk, blk, sib)
                cp.start()
                passed.append(cp)
        for a in range(n):
            for j, (cx, cy) in enumerate(chips):
                blk = outs[a].at[2 * cx + cy, 1 - c]
                rcopy(a, 3 + j, blk, blk, me).wait_recv()
        for cp in first + passed:
            cp.wait_send()
        for cp in locs:
            cp.wait()

    return _pcall(
        body, name="all_gather_weights",
        in_specs=[ANY] * n, out_specs=[ANY] * n,
        out_shape=[jax.ShapeDtypeStruct((N_CHIPS,) + a.shape, a.dtype) for a in arrs],
        scratch_shapes=[pltpu.SemaphoreType.DMA((6 * n,)), pltpu.SemaphoreType.DMA((6 * n,)),
                        pltpu.SemaphoreType.DMA((n,))])(*arrs)


def _sibling_swap(arrs):
    n = len(arrs)

    def body(*refs):
        ins, outs = refs[:n], refs[n:2 * n]
        send_sems, recv_sems = refs[2 * n:]
        x, y, c = _mesh_pos()
        cps = [pltpu.make_async_remote_copy(src_ref=ins[a].at[1 - c], dst_ref=outs[a], send_sem=send_sems.at[a],
                                            recv_sem=recv_sems.at[a], device_id=(x, y, 1 - c), device_id_type=MESH)
               for a in range(n)]
        for cp in cps:
            cp.start()
        for cp in cps:
            cp.wait()

    return _pcall(
        body, name="pair_swap", in_specs=[ANY] * n, out_specs=[ANY] * n,
        out_shape=[jax.ShapeDtypeStruct(a.shape[1:], a.dtype) for a in arrs],
        scratch_shapes=[pltpu.SemaphoreType.DMA((n,)), pltpu.SemaphoreType.DMA((n,))])(*arrs)


def _chip_exchange(big, small):
    n = len(big)

    def body(*refs):
        ins, outs = refs[:n + 1], refs[n + 1:2 * n + 2]
        send_sems, recv_sems = refs[2 * n + 2:]
        x, y, c = _mesh_pos()
        cps = []
        for a in range(n + 1):
            for j, (cx, cy) in enumerate(_other_chips(x, y)):
                src = ins[a].at[2 * cx + cy] if a < n else ins[a]
                cps.append(pltpu.make_async_remote_copy(src_ref=src, dst_ref=outs[a].at[j], send_sem=send_sems.at[3 * a + j],
                                                        recv_sem=recv_sems.at[3 * a + j], device_id=(cx, cy, c),
                                                        device_id_type=MESH))
        for cp in cps:
            cp.start()
        for cp in cps:
            cp.wait()

    arrs = list(big) + [small]
    shapes = [a.shape[1:] for a in big] + [small.shape]
    return _pcall(
        body, name="chip_exchange", in_specs=[ANY] * (n + 1), out_specs=[ANY] * (n + 1),
        out_shape=[jax.ShapeDtypeStruct((3,) + sh, a.dtype) for sh, a in zip(shapes, arrs)],
        scratch_shapes=[pltpu.SemaphoreType.DMA((3 * (n + 1),)), pltpu.SemaphoreType.DMA((3 * (n + 1),))])(*arrs)


def _halves_swap(arrs):
    n = len(arrs)

    def body(*refs):
        ins, outs = refs[:n], refs[n:2 * n]
        send_sems, recv_sems, loc_sems = refs[2 * n:]
        x, y, c = _mesh_pos()
        cps = []
        for a in range(n):
            cps.append(pltpu.make_async_copy(ins[a], outs[a].at[c], loc_sems.at[a]))
            cps.append(pltpu.make_async_remote_copy(src_ref=ins[a], dst_ref=outs[a].at[c], send_sem=send_sems.at[a],
                                                    recv_sem=recv_sems.at[a], device_id=(x, y, 1 - c),
                                                    device_id_type=MESH))
        for cp in cps:
            cp.start()
        for cp in cps:
            cp.wait()

    return _pcall(
        body, name="halves_swap", in_specs=[ANY] * n, out_specs=[ANY] * n,
        out_shape=[jax.ShapeDtypeStruct((2,) + a.shape, a.dtype) for a in arrs],
        scratch_shapes=[pltpu.SemaphoreType.DMA((n,)), pltpu.SemaphoreType.DMA((n,)),
                        pltpu.SemaphoreType.DMA((n,))])(*arrs)


def _pair_sum(name, g32, recv, cs):
    _, nch, a_rows, b_cols = g32.shape
    ta = min(T_ELEM, a_rows)

    def body(cs_ref, g_ref, r_ref, pb_ref, own_ref):
        k = pl.program_id(1)
        v = g_ref[...] + r_ref[...].astype(F32)
        pb_ref[...] = v.astype(BF16)

        @pl.when(k == cs_ref[1])
        def _():
            own_ref[...] = v

    grid_spec = pltpu.PrefetchScalarGridSpec(
        num_scalar_prefetch=1, grid=(a_rows // ta, nch),
        in_specs=[pl.BlockSpec((None, None, ta, b_cols), lambda r, k, cs_ref: (cs_ref[0], k, r, 0)),
                  pl.BlockSpec((None, ta, b_cols), lambda r, k, cs_ref: (k, r, 0))],
        out_specs=[pl.BlockSpec((None, ta, b_cols), lambda r, k, cs_ref: (k, r, 0)),
                   pl.BlockSpec((ta, b_cols), lambda r, k, cs_ref: (r, 0))])
    return _pcall(
        body, name=name, grid_spec=grid_spec,
        out_shape=[jax.ShapeDtypeStruct((nch, a_rows, b_cols), BF16), jax.ShapeDtypeStruct((a_rows, b_cols), F32)],
        dims=("arbitrary", "arbitrary"))(cs, g32, recv)


def _small_pair_sum(mine, recv, cs):
    _, r, ccols = mine.shape

    def body(cs_ref, a_ref, b_ref, o_ref):
        o_ref[...] = a_ref[...] + b_ref[...]

    grid_spec = pltpu.PrefetchScalarGridSpec(
        num_scalar_prefetch=1, grid=(1,),
        in_specs=[pl.BlockSpec((None, r, ccols), lambda i, cs_ref: (cs_ref[0], 0, 0)),
                  pl.BlockSpec((r, ccols), lambda i, cs_ref: (0, 0))],
        out_specs=pl.BlockSpec((r, ccols), lambda i, cs_ref: (0, 0)))
    return _pcall(body, name="small_pair_sum", grid_spec=grid_spec,
                  out_shape=jax.ShapeDtypeStruct((r, ccols), F32), dims=("arbitrary",))(cs, mine, recv)


def _chip_sum(name, own, recv):
    a_rows, b_cols = own.shape
    ta = min(T_ELEM, a_rows)

    def body(o_ref, r_ref, f_ref):
        f_ref[...] = ((o_ref[...] + r_ref[0].astype(F32)) + r_ref[1].astype(F32)) + r_ref[2].astype(F32)

    return _pcall(
        body, name=name, grid=(a_rows // ta,),
        in_specs=[pl.BlockSpec((ta, b_cols), lambda r: (r, 0)), pl.BlockSpec((3, ta, b_cols), lambda r: (0, r, 0))],
        out_specs=pl.BlockSpec((ta, b_cols), lambda r: (r, 0)),
        out_shape=jax.ShapeDtypeStruct((a_rows, b_cols), F32), dims=("arbitrary",))(own, recv)


def _small_chip_sum(pair, recv, cs):
    r, ccols = pair.shape

    def body(cs_ref, p_ref, r_ref, o_ref):
        s = cs_ref[1]
        own = p_ref[...]
        total = None
        for k in range(N_CHIPS):
            flip = jnp.bitwise_xor(s, k)
            term = jnp.where(flip == 0, own, jnp.where(flip == 2, r_ref[0], jnp.where(flip == 1, r_ref[1], r_ref[2])))
            total = term if total is None else total + term
        o_ref[...] = total

    grid_spec = pltpu.PrefetchScalarGridSpec(
        num_scalar_prefetch=1, grid=(1,),
        in_specs=[pl.BlockSpec((r, ccols), lambda i, cs_ref: (0, 0)),
                  pl.BlockSpec((3, r, ccols), lambda i, cs_ref: (0, 0, 0))],
        out_specs=pl.BlockSpec((r, ccols), lambda i, cs_ref: (0, 0)))
    return _pcall(body, name="small_chip_sum", grid_spec=grid_spec,
                  out_shape=jax.ShapeDtypeStruct((r, ccols), F32), dims=("arbitrary",))(cs, pair, recv)


def _adamw(name, w, g, m, v):
    rows, cols = w.shape
    tr = min(T_ELEM if cols <= PACK_COLS else T_ELEM // 2, rows)
    c1 = 1.0 - ADAM_B1 ** ADAM_STEP
    c2 = 1.0 - ADAM_B2 ** ADAM_STEP

    def body(w_ref, g_ref, m_ref, v_ref, d_ref, mo_ref, vo_ref):
        g_ = g_ref[...]
        m_ = ADAM_B1 * m_ref[...] + (1.0 - ADAM_B1) * g_
        v_ = ADAM_B2 * v_ref[...] + (1.0 - ADAM_B2) * (g_ * g_)
        mo_ref[...] = m_
        vo_ref[...] = v_
        d_ref[...] = (-ADAM_LR) * ((m_ / c1) / (jnp.sqrt(v_ / c2) + ADAM_EPS) + ADAM_WD * w_ref[...])

    spec = pl.BlockSpec((tr, cols), lambda r: (r, 0))
    return _pcall(body, name=name, grid=(rows // tr,), in_specs=[spec] * 4, out_specs=[spec] * 3,
                  out_shape=[jax.ShapeDtypeStruct((rows, cols), F32)] * 3, dims=("arbitrary",))(w, g, m, v)


def _pack(parts, rows):
    flat = jnp.concatenate([a.reshape(-1) for a in parts])
    return jnp.pad(flat, (0, rows * PACK_COLS - flat.shape[0])).reshape(rows, PACK_COLS)


def _unpack(buf, shapes):
    flat = buf.reshape(-1)
    out, off = [], 0
    for sh in shapes:
        size = math.prod(sh)
        out.append(flat[off:off + size].reshape(sh))
        off += size
    return out


def _pack_rows(shapes, multiple):
    total = sum(math.prod(sh) for sh in shapes)
    rows = -(-total // PACK_COLS)
    return -(-rows // multiple) * multiple


SMALL = ["norm1_g", "gm_v_g", "gm_v_b", "gm_ws", "gm_bs", "lru_conv_w", "lru_conv_b", "lru_wa", "lru_ba", "lru_wx",
         "lru_bx", "lru_lambda", "gm_out_g", "lru_out_g", "norm2_g", "ffn_conv_w", "ffn_conv_b", "final_g"]
BIG = ["w_in", "w_out", "ffn_w_up", "ffn_w_down"]
ORDER = ["norm1_g", "w_in", "gm_v_g", "gm_v_b", "gm_ws", "gm_bs", "lru_conv_w", "lru_conv_b", "lru_wa", "lru_ba",
         "lru_wx", "lru_bx", "lru_lambda", "gm_out_g", "lru_out_g", "w_out", "norm2_g", "ffn_w_up", "ffn_conv_w",
         "ffn_conv_b", "ffn_w_down", "final_g"]


def kernel(x, norm1_g, w_in, gm_v_g, gm_v_b, gm_ws, gm_bs, lru_conv_w, lru_conv_b, lru_wa, lru_ba, lru_wx, lru_bx, lru_lambda, gm_out_g, lru_out_g, w_out, norm2_g, ffn_w_up, ffn_conv_w, ffn_conv_b, ffn_w_down, final_g, loss_target, m_norm1_g, m_w_in, m_gm_v_g, m_gm_v_b, m_gm_ws, m_gm_bs, m_lru_conv_w, m_lru_conv_b, m_lru_wa, m_lru_ba, m_lru_wx, m_lru_bx, m_lru_lambda, m_gm_out_g, m_lru_out_g, m_w_out, m_norm2_g, m_ffn_w_up, m_ffn_conv_w, m_ffn_conv_b, m_ffn_w_down, m_final_g, v_norm1_g, v_w_in, v_gm_v_g, v_gm_v_b, v_gm_ws, v_gm_bs, v_lru_conv_w, v_lru_conv_b, v_lru_wa, v_lru_ba, v_lru_wx, v_lru_bx, v_lru_lambda, v_gm_out_g, v_lru_out_g, v_w_out, v_norm2_g, v_ffn_w_up, v_ffn_conv_w, v_ffn_conv_b, v_ffn_w_down, v_final_g):
    w = dict(norm1_g=norm1_g, w_in=w_in, gm_v_g=gm_v_g, gm_v_b=gm_v_b, gm_ws=gm_ws, gm_bs=gm_bs, lru_conv_w=lru_conv_w, lru_conv_b=lru_conv_b, lru_wa=lru_wa, lru_ba=lru_ba, lru_wx=lru_wx, lru_bx=lru_bx, lru_lambda=lru_lambda, gm_out_g=gm_out_g, lru_out_g=lru_out_g, w_out=w_out, norm2_g=norm2_g, ffn_w_up=ffn_w_up, ffn_conv_w=ffn_conv_w, ffn_conv_b=ffn_conv_b, ffn_w_down=ffn_w_down, final_g=final_g)
    m = dict(norm1_g=m_norm1_g, w_in=m_w_in, gm_v_g=m_gm_v_g, gm_v_b=m_gm_v_b, gm_ws=m_gm_ws, gm_bs=m_gm_bs, lru_conv_w=m_lru_conv_w, lru_conv_b=m_lru_conv_b, lru_wa=m_lru_wa, lru_ba=m_lru_ba, lru_wx=m_lru_wx, lru_bx=m_lru_bx, lru_lambda=m_lru_lambda, gm_out_g=m_gm_out_g, lru_out_g=m_lru_out_g, w_out=m_w_out, norm2_g=m_norm2_g, ffn_w_up=m_ffn_w_up, ffn_conv_w=m_ffn_conv_w, ffn_conv_b=m_ffn_conv_b, ffn_w_down=m_ffn_w_down, final_g=m_final_g)
    v = dict(norm1_g=v_norm1_g, w_in=v_w_in, gm_v_g=v_gm_v_g, gm_v_b=v_gm_v_b, gm_ws=v_gm_ws, gm_bs=v_gm_bs, lru_conv_w=v_lru_conv_w, lru_conv_b=v_lru_conv_b, lru_wa=v_lru_wa, lru_ba=v_lru_ba, lru_wx=v_lru_wx, lru_bx=v_lru_bx, lru_lambda=v_lru_lambda, gm_out_g=v_gm_out_g, lru_out_g=v_lru_out_g, w_out=v_w_out, norm2_g=v_norm2_g, ffn_w_up=v_ffn_w_up, ffn_conv_w=v_ffn_conv_w, ffn_conv_b=v_ffn_conv_b, ffn_w_down=v_ffn_w_down, final_g=v_final_g)

    mx, my, mc = _mesh_pos()
    shard = 2 * mx + my
    cs = jnp.stack([mc, shard]).astype(jnp.int32)

    xs = x[0]
    tgt = loss_target[0]
    s_len, d = xs.shape

    halves = lambda a: a.reshape((2, a.shape[0] // 2) + a.shape[1:])
    gathered = _all_gather([halves(w["w_in"][0].astype(BF16)), halves(w["w_out"][0].astype(BF16)),
                            halves(w["ffn_w_up"][0].astype(BF16)), halves(w["ffn_w_down"][0].astype(BF16)),
                            w["lru_conv_w"][0].reshape(2, 4, -1), w["ffn_conv_w"][0].reshape(2, 12, -1)])
    win_g = gathered[0].reshape(N_CHIPS, d, -1)
    wout_g = gathered[1].reshape(-1, d)
    wup_g = gathered[2].reshape(N_CHIPS, d, -1)
    wdown_g = gathered[3].reshape(-1, d)
    wc = gathered[4].reshape(N_CHIPS, 4, -1).transpose(1, 0, 2).reshape(4, -1)
    wfc = gathered[5].reshape(N_CHIPS, 3, -1).transpose(1, 0, 2).reshape(3, -1)
    ff = wfc.shape[1] // 2

    tril = jnp.tril(jnp.ones((CHUNK, CHUNK), bool))
    wt32 = jnp.where(tril[None], w["gm_ws"][0], 0.0)
    wt = wt32.astype(BF16)
    wtt = wt32.transpose(0, 2, 1).astype(BF16)
    bst = jnp.repeat(w["gm_bs"][0].T, HEAD_DIM, axis=1)
    wa = w["lru_wa"][0].astype(BF16)
    wx = w["lru_wx"][0].astype(BF16)
    wat = w["lru_wa"][0].transpose(0, 2, 1).astype(BF16)
    wxt = w["lru_wx"][0].transpose(0, 2, 1).astype(BF16)
    ba = w["lru_ba"][0].reshape(1, -1)
    bx = w["lru_bx"][0].reshape(1, -1)
    gf = w["final_g"].reshape(1, -1)

    p, h1 = _fwd_in_proj(xs, w["norm1_g"], win_g)
    y, hs = _fwd_mixers(p, w["gm_v_g"], w["gm_v_b"], wt, bst, wc, w["lru_conv_b"], wa, ba, wx, bx, w["lru_lambda"],
                        w["gm_out_g"], w["lru_out_g"])
    x2, h2 = _fwd_out_proj(xs, y, wout_g, w["norm2_g"])
    up, act = _fwd_ffn_up(h2, wup_g, wfc, w["ffn_conv_b"])
    dx3, dx3b, loss_tile, dgf = _fwd_down_loss(act, wdown_g, x2, gf, tgt)
    loss = lax.psum(loss_tile[0, 0], ("x", "y", "c"))

    g_down = _bwd_weight("bwd_w_down", act[None], dx3b[None], a_planes=1, b_planes=1, shard_rows=True)
    dup, dwf = _bwd_ffn_act(dx3b, wdown_g, up, wfc, w["ffn_conv_b"])
    g_up = _bwd_weight("bwd_w_up", h2[None], dup, a_planes=1, b_planes=2, shard_rows=False)
    dx2, dx2b, dg2 = _bwd_proj_norm("bwd_ffn_in", dup, wup_g, dx3, x2, w["norm2_g"], planes=2)
    g_out = _bwd_weight("bwd_w_out", y[None], dx2b[None], a_planes=1, b_planes=1, shard_rows=True)
    dy = _bwd_dy(dx2b, wout_g)
    (dp, dgv, dbv, dwt, dbst, dwc, dbc, dwa, dba, dwx, dbx, dsp, dggm, dglru) = _bwd_mixers(
        p, dy, hs, w["gm_v_g"], w["gm_v_b"], wt, wtt, bst, wc, w["lru_conv_b"], wa, wat, ba, wx, wxt, bx,
        w["lru_lambda"], w["gm_out_g"], w["lru_out_g"])
    g_in = _bwd_weight("bwd_w_in", h1[None], dp[None], a_planes=1, b_planes=1, shard_rows=False)
    grad_x, _, dg1 = _bwd_proj_norm("bwd_in", dp[None], win_g, dx2, xs, w["norm1_g"], planes=1)

    dwfc = dwf[:, :, :3].transpose(2, 0, 1, 3).reshape(3, -1)
    dbfc = dwf[:, :, 3].reshape(1, -1)
    dlam = dsp * (-_sigmoid(-w["lru_lambda"]))
    small_grads = dict(
        norm1_g=dg1, gm_v_g=dgv, gm_v_b=dbv, gm_ws=jnp.where(tril[None], dwt, 0.0),
        gm_bs=dbst.reshape(CHUNK, HEADS, HEAD_DIM).sum(-1).T, lru_conv_w=dwc, lru_conv_b=dbc, lru_wa=dwa, lru_ba=dba,
        lru_wx=dwx, lru_bx=dbx, lru_lambda=dlam, gm_out_g=dggm, lru_out_g=dglru, norm2_g=dg2, ffn_conv_w=dwfc,
        ffn_conv_b=dbfc, final_g=dgf)
    full_shapes = [small_grads[k].shape for k in SMALL]
    rows_full = _pack_rows(full_shapes, 16)
    gpack = _pack([small_grads[k] for k in SMALL], rows_full).reshape(2, rows_full // 2, PACK_COLS)

    bigs = [g_in, g_out, g_up, g_down]
    recv = _sibling_swap([gb for _, gb in bigs] + [gpack])
    pair = [_pair_sum("pair_sum_%d" % i, g32, r, cs) for i, ((g32, _), r) in enumerate(zip(bigs, recv[:4]))]
    small_pair = _small_pair_sum(gpack, recv[4], cs)
    got = _chip_exchange([pb for pb, _ in pair], small_pair)
    reduced = [_chip_sum("chip_sum_%d" % i, own, r) for i, ((_, own), r) in enumerate(zip(pair, got[:4]))]
    small_half = _small_chip_sum(small_pair, got[4], cs)
    finals = _halves_swap(reduced + [small_half])
    grads = {}
    for name, gfull in zip(BIG, finals[:4]):
        grads[name] = gfull.reshape(w[name].shape[1:])

    for name, g in zip(SMALL, _unpack(finals[4], full_shapes)):
        blk = w[name].shape[1:] if w[name].ndim > 1 else w[name].shape
        if name in ("lru_conv_w", "ffn_conv_w"):
            g = lax.dynamic_slice_in_dim(g, shard * blk[1], blk[1], axis=1)
        grads[name] = g.reshape(blk)

    delta, new_m, new_v = {}, {}, {}
    for name in BIG:
        shp = w[name].shape
        r2 = lambda a: a.reshape(shp[1:])
        dl, mo, vo = _adamw("adamw_" + name, r2(w[name]), grads[name], r2(m[name]), r2(v[name]))
        delta[name], new_m[name], new_v[name] = dl, mo, vo
    blk_shapes = [grads[k].shape for k in SMALL]
    rows_blk = _pack_rows(blk_shapes, T_ELEM)
    packs = [_pack([src[k] for k in SMALL], rows_blk) for src in (w, grads, m, v)]
    outs = _adamw("adamw_small", *packs)
    for dst, buf in zip((delta, new_m, new_v), outs):
        for name, a in zip(SMALL, _unpack(buf, blk_shapes)):
            dst[name] = a

    def shaped(dct):
        return [dct[k].reshape(w[k].shape) for k in ORDER]

    return (loss, grad_x[None], *shaped(grads), *shaped(delta), *shaped(new_m), *shaped(new_v))
```

```python
import functools
import math

import jax
import jax.numpy as jnp
from jax import lax
from jax.experimental import pallas as pl
from jax.experimental.pallas import tpu as pltpu

F32 = jnp.float32
BF16 = jnp.bfloat16
MESH = pl.DeviceIdType.MESH
ANY = pl.BlockSpec(memory_space=pltpu.HBM)

GM_W = 1024
LRU_W = 1024
CHUNK = 128
HEADS = 8
HEAD_DIM = 128
LRU_C = 8.0
RMS_EPS = 1e-6
LN_EPS = 1e-5
ADAM_LR = 0.001
ADAM_B1 = 0.9
ADAM_B2 = 0.999
ADAM_EPS = 1e-08
ADAM_WD = 0.01
ADAM_STEP = 10

N_CHIPS = 4
HALO = 8
PACK_COLS = 1024
VMEM_LIMIT = 56 * 1024 * 1024

TM_IN = 1024
TS_MIX = 256
TM_OUT = 512
TM_UP = 512
TN_UP = 256
T_EPI = 128
TM_BW = 512
TW_M = 1024
TW_N = 1024
TW_K = 2048
T_ELEM = 256
LANES = 128


def _tile(dim, cap):
    t = min(cap, dim) // LANES * LANES
    while dim % t:
        t -= LANES
    return t

_GELU_K0 = 0.7978845608028654
_GELU_K1 = 0.044715


def _pcall(body, *, name, out_shape, grid=None, in_specs=None, out_specs=None, scratch_shapes=(),
           grid_spec=None, dims=None, aliases=None):
    params = pltpu.CompilerParams(dimension_semantics=dims, vmem_limit_bytes=VMEM_LIMIT)
    kw = dict(name=name, out_shape=out_shape, compiler_params=params)
    if aliases:
        kw["input_output_aliases"] = aliases
    if grid_spec is not None:
        return pl.pallas_call(body, grid_spec=grid_spec, **kw)
    if grid is not None:
        kw["grid"] = grid
    return pl.pallas_call(body, in_specs=in_specs, out_specs=out_specs, scratch_shapes=list(scratch_shapes), **kw)


def _gelu(x):
    t = jnp.tanh(_GELU_K0 * (x + _GELU_K1 * (x * x * x)))
    return 0.5 * x * (1.0 + t)


def _gelu_and_grad(x):
    x2 = x * x
    t = jnp.tanh(_GELU_K0 * (x + _GELU_K1 * (x2 * x)))
    g = 0.5 * x * (1.0 + t)
    dg = 0.5 * (1.0 + t) + 0.5 * x * (1.0 - t * t) * (_GELU_K0 * (1.0 + 3.0 * _GELU_K1 * x2))
    return g, dg


def _sigmoid(x):
    return 1.0 / (1.0 + jnp.exp(-x))


def _neg_expm1(x):
    series = -x * (1.0 + x * (0.5 + x * (1.0 / 6.0 + x * (1.0 / 24.0 + x * (1.0 / 120.0 + x * (1.0 / 720.0))))))
    return jnp.where(x > -0.1, series, 1.0 - jnp.exp(x))


def _softplus(z):
    return jnp.maximum(z, 0.0) + jnp.log(1.0 + jnp.exp(-jnp.abs(z)))


def _rowmean(x):
    return jnp.mean(x, axis=-1, keepdims=True)


def _colsum(x):
    return jnp.sum(x, axis=0, keepdims=True)


def _rms_stats(x):
    r = lax.rsqrt(_rowmean(x * x) + RMS_EPS)
    return r, x * r


def _rms_bwd(dy, n, r, g):
    dn = dy * g
    return r * (dn - n * _rowmean(dn * n)), dy * n


def _shift_prev(x, halo, d):
    cat = jnp.concatenate([halo, x], axis=0)
    return pltpu.roll(cat, d, 0)[HALO:, :]


def _shift_next(x, halo, d):
    n = x.shape[0]
    cat = jnp.concatenate([x, halo], axis=0)
    return pltpu.roll(cat, n + HALO - d, 0)[:n, :]


def _dot(a, b):
    return jnp.dot(a, b, preferred_element_type=F32)


def _dot_nt(a, b):
    return lax.dot_general(a, b, (((1,), (1,)), ((), ())), preferred_element_type=F32)


def _dot_tn(a, b):
    return lax.dot_general(a, b, (((0,), (0,)), ((), ())), preferred_element_type=F32)


def _fwd_in_proj(x, g1, win_g):
    s_len, d = x.shape
    nsh, _, ncol = win_g.shape
    tm = min(TM_IN, s_len)

    def body(x_ref, g_ref, w_ref, p_ref, h_ref):
        @pl.when(pl.program_id(1) == 0)
        def _():
            _, n = _rms_stats(x_ref[...])
            h_ref[...] = (n * g_ref[...]).astype(BF16)

        p_ref[...] = _dot(h_ref[...], w_ref[...])

    return _pcall(
        body, name="fwd_in_proj", grid=(s_len // tm, nsh),
        in_specs=[pl.BlockSpec((tm, d), lambda i, j: (i, 0)),
                  pl.BlockSpec((1, d), lambda i, j: (0, 0)),
                  pl.BlockSpec((None, d, ncol), lambda i, j: (j, 0, 0))],
        out_specs=[pl.BlockSpec((tm, ncol), lambda i, j: (i, j)),
                   pl.BlockSpec((tm, d), lambda i, j: (i, 0))],
        out_shape=[jax.ShapeDtypeStruct((s_len, nsh * ncol), F32), jax.ShapeDtypeStruct((s_len, d), BF16)],
        dims=("arbitrary", "arbitrary"))(x, g1, win_g)


def _gm_forward(z, gv, bv, wt_ref, bst_ref, vl_s, mix_s):
    ts = z.shape[0]
    ge = _gelu(z)
    u = ge[:, :GM_W]
    v = ge[:, GM_W:]
    vc = v - _rowmean(v)
    rs = lax.rsqrt(_rowmean(vc * vc) + LN_EPS)
    vh = vc * rs
    vl_s[...] = (vh * gv + bv).astype(BF16)
    for cc in range(ts // CHUNK):
        rows = slice(cc * CHUNK, (cc + 1) * CHUNK)
        for hh in range(HEADS):
            cols = slice(hh * HEAD_DIM, (hh + 1) * HEAD_DIM)
            mix_s[rows, cols] = _dot(wt_ref[hh], vl_s[rows, cols]) + bst_ref[:, cols]
    mixed = mix_s[...]
    return u, mixed, vh, rs, u * mixed


def _lru_gates(xl, halo, wc_ref, bc_ref, wa_ref, ba_ref, wx_ref, bx_ref, lam_ref, z_s):
    x1 = _shift_prev(xl, halo, 1)
    x2 = _shift_prev(xl, halo, 2)
    x3 = _shift_prev(xl, halo, 3)
    xr = bc_ref[...] + wc_ref[0:1, :] * x3 + wc_ref[1:2, :] * x2 + wc_ref[2:3, :] * x1 + wc_ref[3:4, :] * xl
    xrb = xr.astype(BF16)
    for hh in range(HEADS):
        cols = slice(hh * HEAD_DIM, (hh + 1) * HEAD_DIM)
        z_s[:, cols] = _dot(xrb[:, cols], wa_ref[hh])
        z_s[:, LRU_W + hh * HEAD_DIM:LRU_W + (hh + 1) * HEAD_DIM] = _dot(xrb[:, cols], wx_ref[hh])
    ra = _sigmoid(z_s[:, :LRU_W] + ba_ref[...])
    ri = _sigmoid(z_s[:, LRU_W:] + bx_ref[...])
    sp = _softplus(-lam_ref[...])
    la = (-LRU_C) * ra * sp
    a = jnp.exp(la)
    mult = jnp.sqrt(_neg_expm1(2.0 * la))
    return dict(x1=x1, x2=x2, x3=x3, xr=xr, xrb=xrb, ra=ra, ri=ri, sp=sp, a=a, mult=mult)


def _fwd_mixers(p, gv, bv, wt, bst, wc, bc, wa, ba, wx, bx, lam, ggm, glru):
    s_len = p.shape[0]
    ts = min(TS_MIX, s_len)

    def body(pz_ref, pgl_ref, pxl_ref, gv_ref, bv_ref, wt_ref, bst_ref, wc_ref, bc_ref, wa_ref, ba_ref, wx_ref,
             bx_ref, lam_ref, ggm_ref, glru_ref, y_ref, hs_ref, tail_ref, h_ref, a_s, b_s, vl_s, mix_s, z_s):
        @pl.when(pl.program_id(0) == 0)
        def _():
            tail_ref[...] = jnp.zeros_like(tail_ref)
            h_ref[...] = jnp.zeros_like(h_ref)

        _, _, _, _, ygm = _gm_forward(pz_ref[...], gv_ref[...], bv_ref[...], wt_ref, bst_ref, vl_s, mix_s)
        _, ngm = _rms_stats(ygm)
        y_ref[:, :GM_W] = (ngm * ggm_ref[...]).astype(BF16)

        xl = pxl_ref[...]
        gts = _lru_gates(xl, tail_ref[...], wc_ref, bc_ref, wa_ref, ba_ref, wx_ref, bx_ref, lam_ref, z_s)
        tail_ref[...] = xl[ts - HALO:, :]
        a_s[...] = gts["a"]
        b_s[...] = gts["mult"] * (gts["ri"] * gts["xr"])

        def step(t, h):
            h = a_s[pl.ds(t, 1), :] * h + b_s[pl.ds(t, 1), :]
            hs_ref[pl.ds(t, 1), :] = h
            return h

        h_ref[...] = lax.fori_loop(0, ts, step, h_ref[...], unroll=8)
        yl = hs_ref[...] * _gelu(pgl_ref[...])
        _, nl = _rms_stats(yl)
        y_ref[:, GM_W:] = (nl * glru_ref[...]).astype(BF16)

    full = lambda shape: pl.BlockSpec(shape, lambda i: (0,) * len(shape))
    return _pcall(
        body, name="fwd_mixers", grid=(s_len // ts,),
        in_specs=[pl.BlockSpec((ts, 2 * GM_W), lambda i: (i, 0)),
                  pl.BlockSpec((ts, LRU_W), lambda i: (i, 2)),
                  pl.BlockSpec((ts, LRU_W), lambda i: (i, 3)),
                  full((1, GM_W)), full((1, GM_W)), full((HEADS, CHUNK, CHUNK)), full((CHUNK, GM_W)),
                  full((4, LRU_W)), full((1, LRU_W)), full((HEADS, HEAD_DIM, HEAD_DIM)), full((1, LRU_W)),
                  full((HEADS, HEAD_DIM, HEAD_DIM)), full((1, LRU_W)), full((1, LRU_W)), full((1, GM_W)),
                  full((1, LRU_W))],
        out_specs=[pl.BlockSpec((ts, GM_W + LRU_W), lambda i: (i, 0)), pl.BlockSpec((ts, LRU_W), lambda i: (i, 0))],
        out_shape=[jax.ShapeDtypeStruct((s_len, GM_W + LRU_W), BF16), jax.ShapeDtypeStruct((s_len, LRU_W), F32)],
        scratch_shapes=[pltpu.VMEM((HALO, LRU_W), F32), pltpu.VMEM((1, LRU_W), F32),
                        pltpu.VMEM((ts, LRU_W), F32), pltpu.VMEM((ts, LRU_W), F32),
                        pltpu.VMEM((ts, GM_W), BF16), pltpu.VMEM((ts, GM_W), F32), pltpu.VMEM((ts, 2 * LRU_W), F32)],
        dims=("arbitrary",))(p, p, p, gv, bv, wt, bst, wc, bc, wa, ba, wx, bx, lam, ggm, glru)


def _fwd_out_proj(x, y, wout_g, g2):
    s_len, d = x.shape
    tm = min(TM_OUT, s_len)

    def body(x_ref, y_ref, w_ref, g_ref, x2_ref, h2_ref):
        x2 = x_ref[...] + _dot(y_ref[...], w_ref[...])
        x2_ref[...] = x2
        _, n = _rms_stats(x2)
        h2_ref[...] = (n * g_ref[...]).astype(BF16)

    return _pcall(
        body, name="fwd_out_proj", grid=(s_len // tm,),
        in_specs=[pl.BlockSpec((tm, d), lambda i: (i, 0)), pl.BlockSpec((tm, d), lambda i: (i, 0)),
                  pl.BlockSpec((d, d), lambda i: (0, 0)), pl.BlockSpec((1, d), lambda i: (0, 0))],
        out_specs=[pl.BlockSpec((tm, d), lambda i: (i, 0)), pl.BlockSpec((tm, d), lambda i: (i, 0))],
        out_shape=[jax.ShapeDtypeStruct((s_len, d), F32), jax.ShapeDtypeStruct((s_len, d), BF16)],
        dims=("arbitrary",))(x, y, wout_g, g2)


def _row_fetch(hbm_ref, buf_ref, sem, row0, rows):
    return pltpu.make_async_copy(hbm_ref.at[pl.ds(row0, rows), :], buf_ref, sem)


def _fwd_ffn(h2, wup_g, wfc, bfc, wdown_g, x2, gf, target):
    s_len, d = h2.shape
    nsh, _, ncol = wup_g.shape
    f = nsh * ncol // 2
    tm = min(TM_UP, s_len)
    te = min(T_EPI, tm)
    tn = TN_UP
    nps = ncol // tn
    nj = f // tn

    def body(h_ref, wg_ref, wv_ref, wcg_ref, wcv_ref, bg_ref, bv_ref, wd_ref, g_ref, x2_hbm, t_hbm,
             up_ref, act_ref, dx3_ref, dx3b_ref, loss_ref, dgf_ref, tail_ref, acc_ref, x2_buf, t_buf, sems):
        i, j = pl.program_id(0), pl.program_id(1)
        row0 = pl.multiple_of(i * tm, tm)
        fetches = (_row_fetch(x2_hbm, x2_buf, sems.at[0], row0, tm), _row_fetch(t_hbm, t_buf, sems.at[1], row0, tm))

        @pl.when(jnp.logical_and(i == 0, j == 0))
        def _():
            tail_ref[...] = jnp.zeros_like(tail_ref)
            loss_ref[...] = jnp.zeros_like(loss_ref)
            dgf_ref[...] = jnp.zeros_like(dgf_ref)

        @pl.when(j == 0)
        def _():
            acc_ref[...] = jnp.zeros_like(acc_ref)
            for cp in fetches:
                cp.start()

        h = h_ref[...]
        ug = _dot(h, wg_ref[...])
        uv = _dot(h, wv_ref[...])
        up_ref[0] = ug
        up_ref[1] = uv
        tg = tail_ref[0, j]
        tv = tail_ref[1, j]
        tail_ref[0, j] = ug[tm - HALO:, :]
        tail_ref[1, j] = uv[tm - HALO:, :]
        cg = bg_ref[...] + wcg_ref[0:1, :] * _shift_prev(ug, tg, 2) + wcg_ref[1:2, :] * _shift_prev(ug, tg, 1) \
            + wcg_ref[2:3, :] * ug
        cv = bv_ref[...] + wcv_ref[0:1, :] * _shift_prev(uv, tv, 2) + wcv_ref[1:2, :] * _shift_prev(uv, tv, 1) \
            + wcv_ref[2:3, :] * uv
        act = (_gelu(cg) * cv).astype(BF16)
        act_ref[...] = act
        acc_ref[...] += _dot(act, wd_ref[...])

        @pl.when(j == nj - 1)
        def _():
            for cp in fetches:
                cp.wait()
            g = g_ref[...]

            def chunk(k, carry):
                rows = pl.ds(pl.multiple_of(k * te, te), te)
                x3 = x2_buf[rows, :] + acc_ref[rows, :]
                r, n = _rms_stats(x3)
                err = n * g - t_buf[rows, :]
                loss_ref[...] += jnp.sum(err * err) * (0.5 / d)
                dx3, dgn = _rms_bwd(err * (1.0 / d), n, r, g)
                dgf_ref[...] += _colsum(dgn)
                dx3_ref[rows, :] = dx3
                dx3b_ref[rows, :] = dx3.astype(BF16)
                return carry

            lax.fori_loop(0, tm // te, chunk, 0)

    hbm = pl.BlockSpec(memory_space=pl.ANY)
    return _pcall(
        body, name="fwd_ffn", grid=(s_len // tm, nj),
        in_specs=[pl.BlockSpec((tm, d), lambda i, j: (i, 0)),
                  pl.BlockSpec((None, d, tn), lambda i, j: (j // nps, 0, j % nps)),
                  pl.BlockSpec((None, d, tn), lambda i, j: (nsh // 2 + j // nps, 0, j % nps)),
                  pl.BlockSpec((3, tn), lambda i, j: (0, j)), pl.BlockSpec((3, tn), lambda i, j: (0, nj + j)),
                  pl.BlockSpec((1, tn), lambda i, j: (0, j)), pl.BlockSpec((1, tn), lambda i, j: (0, nj + j)),
                  pl.BlockSpec((tn, d), lambda i, j: (j, 0)), pl.BlockSpec((1, d), lambda i, j: (0, 0)), hbm, hbm],
        out_specs=[pl.BlockSpec((2, tm, tn), lambda i, j: (0, i, j)), pl.BlockSpec((tm, tn), lambda i, j: (i, j)),
                   pl.BlockSpec((tm, d), lambda i, j: (i, 0)), pl.BlockSpec((tm, d), lambda i, j: (i, 0)),
                   pl.BlockSpec((8, 128), lambda i, j: (0, 0)), pl.BlockSpec((1, d), lambda i, j: (0, 0))],
        out_shape=[jax.ShapeDtypeStruct((2, s_len, f), F32), jax.ShapeDtypeStruct((s_len, f), BF16),
                   jax.ShapeDtypeStruct((s_len, d), F32), jax.ShapeDtypeStruct((s_len, d), BF16),
                   jax.ShapeDtypeStruct((8, 128), F32), jax.ShapeDtypeStruct((1, d), F32)],
        scratch_shapes=[pltpu.VMEM((2, nj, HALO, tn), F32), pltpu.VMEM((tm, d), F32), pltpu.VMEM((tm, d), F32),
                        pltpu.VMEM((tm, d), F32), pltpu.SemaphoreType.DMA((2,))],
        dims=("arbitrary", "arbitrary"))(h2, wup_g, wup_g, wfc, wfc, bfc, bfc, wdown_g, gf, x2, target)


def _bwd_ffn(dx3b, wdown_g, up, wfc, bfc, wup_g, dx3, x2, g2):
    s_len, d = dx3b.shape
    f = up.shape[2]
    nsh, _, ncol = wup_g.shape
    tm = min(TM_UP, s_len)
    te = min(T_EPI, tm)
    tn = TN_UP
    nj = f // tn
    nps = ncol // tn
    nt = s_len // tm
    hb = tm // HALO

    def body(dx_ref, w_ref, up_ref, halo_ref, wcg_ref, wcv_ref, bg_ref, bv_ref, wug_ref, wuv_ref, g_ref, r_hbm, x_hbm,
             dup_ref, dwf_ref, dx2_ref, dx2b_ref, dg_ref, nxt_ref, acc_ref, r_buf, x_buf, sems):
        i, j = pl.program_id(0), pl.program_id(1)
        ti = nt - 1 - i
        row0 = pl.multiple_of(ti * tm, tm)
        fetches = (_row_fetch(r_hbm, r_buf, sems.at[0], row0, tm), _row_fetch(x_hbm, x_buf, sems.at[1], row0, tm))

        @pl.when(jnp.logical_and(i == 0, j == 0))
        def _():
            dwf_ref[...] = jnp.zeros_like(dwf_ref)
            nxt_ref[...] = jnp.zeros_like(nxt_ref)
            dg_ref[...] = jnp.zeros_like(dg_ref)

        @pl.when(j == 0)
        def _():
            acc_ref[...] = jnp.zeros_like(acc_ref)
            for cp in fetches:
                cp.start()

        dact = _dot_nt(dx_ref[...], w_ref[...])
        outs = []
        for pln, (wc_ref, b_ref) in enumerate(((wcg_ref, bg_ref), (wcv_ref, bv_ref))):
            u = up_ref[pln]
            hal = jnp.where(ti == 0, 0.0, halo_ref[pln])
            u1 = _shift_prev(u, hal, 1)
            u2 = _shift_prev(u, hal, 2)
            c = b_ref[...] + wc_ref[0:1, :] * u2 + wc_ref[1:2, :] * u1 + wc_ref[2:3, :] * u
            outs.append((u, u1, u2, c))
        ge, gd = _gelu_and_grad(outs[0][3])
        dcs = (dact * outs[1][3] * gd, dact * ge)
        for pln, wc_ref in enumerate((wcg_ref, wcv_ref)):
            u, u1, u2, _ = outs[pln]
            dc = dcs[pln]
            dwf_ref[pln, j, 0:1, :] += _colsum(dc * u2)
            dwf_ref[pln, j, 1:2, :] += _colsum(dc * u1)
            dwf_ref[pln, j, 2:3, :] += _colsum(dc * u)
            dwf_ref[pln, j, 3:4, :] += _colsum(dc)
            nxt = nxt_ref[pln, j]
            nxt_ref[pln, j] = dc[:HALO, :]
            dup = wc_ref[2:3, :] * dc + wc_ref[1:2, :] * _shift_next(dc, nxt, 1) \
                + wc_ref[0:1, :] * _shift_next(dc, nxt, 2)
            dupb = dup.astype(BF16)
            dup_ref[pln] = dupb
            acc_ref[...] += _dot_nt(dupb, (wug_ref, wuv_ref)[pln][...])

        @pl.when(j == nj - 1)
        def _():
            for cp in fetches:
                cp.wait()
            g = g_ref[...]

            def chunk(k, carry):
                rows = pl.ds(pl.multiple_of(k * te, te), te)
                r, n = _rms_stats(x_buf[rows, :])
                dxn, dgn = _rms_bwd(acc_ref[rows, :], n, r, g)
                dg_ref[...] += _colsum(dgn)
                dx = r_buf[rows, :] + dxn
                dx2_ref[rows, :] = dx
                dx2b_ref[rows, :] = dx.astype(BF16)
                return carry

            lax.fori_loop(0, tm // te, chunk, 0)

    hbm = pl.BlockSpec(memory_space=pl.ANY)
    rev = lambda i: nt - 1 - i
    return _pcall(
        body, name="bwd_ffn", grid=(nt, nj),
        in_specs=[pl.BlockSpec((tm, d), lambda i, j: (rev(i), 0)),
                  pl.BlockSpec((tn, d), lambda i, j: (j, 0)),
                  pl.BlockSpec((2, tm, tn), lambda i, j: (0, rev(i), j)),
                  pl.BlockSpec((2, HALO, tn), lambda i, j: (0, jnp.maximum(rev(i) * hb - 1, 0), j)),
                  pl.BlockSpec((3, tn), lambda i, j: (0, j)), pl.BlockSpec((3, tn), lambda i, j: (0, nj + j)),
                  pl.BlockSpec((1, tn), lambda i, j: (0, j)), pl.BlockSpec((1, tn), lambda i, j: (0, nj + j)),
                  pl.BlockSpec((None, d, tn), lambda i, j: (j // nps, 0, j % nps)),
                  pl.BlockSpec((None, d, tn), lambda i, j: (nsh // 2 + j // nps, 0, j % nps)),
                  pl.BlockSpec((1, d), lambda i, j: (0, 0)), hbm, hbm],
        out_specs=[pl.BlockSpec((2, tm, tn), lambda i, j: (0, rev(i), j)),
                   pl.BlockSpec((2, nj, 8, tn), lambda i, j: (0, 0, 0, 0)),
                   pl.BlockSpec((tm, d), lambda i, j: (rev(i), 0)), pl.BlockSpec((tm, d), lambda i, j: (rev(i), 0)),
                   pl.BlockSpec((1, d), lambda i, j: (0, 0))],
        out_shape=[jax.ShapeDtypeStruct((2, s_len, f), BF16), jax.ShapeDtypeStruct((2, nj, 8, tn), F32),
                   jax.ShapeDtypeStruct((s_len, d), F32), jax.ShapeDtypeStruct((s_len, d), BF16),
                   jax.ShapeDtypeStruct((1, d), F32)],
        scratch_shapes=[pltpu.VMEM((2, nj, HALO, tn), F32), pltpu.VMEM((tm, d), F32), pltpu.VMEM((tm, d), F32),
                        pltpu.VMEM((tm, d), F32), pltpu.SemaphoreType.DMA((2,))],
        dims=("arbitrary", "arbitrary"))(dx3b, wdown_g, up, up, wfc, wfc, bfc, bfc, wup_g, wup_g, g2, dx3, x2)


def _bwd_in(dp, win_g, resid, x_in, g):
    s_len, d = x_in.shape
    nsh, _, ncol = win_g.shape
    tm = min(TM_BW, s_len)

    def body(dz_ref, w_ref, r_ref, x_ref, g_ref, dx_ref, dg_ref, acc_ref):
        i, k = pl.program_id(0), pl.program_id(1)

        @pl.when(jnp.logical_and(i == 0, k == 0))
        def _():
            dg_ref[...] = jnp.zeros_like(dg_ref)

        @pl.when(k == 0)
        def _():
            acc_ref[...] = jnp.zeros_like(acc_ref)

        acc_ref[...] += _dot_nt(dz_ref[...], w_ref[...])

        @pl.when(k == nsh - 1)
        def _():
            r, n = _rms_stats(x_ref[...])
            dxn, dgn = _rms_bwd(acc_ref[...], n, r, g_ref[...])
            dg_ref[...] += _colsum(dgn)
            dx_ref[...] = r_ref[...] + dxn

    return _pcall(
        body, name="bwd_in", grid=(s_len // tm, nsh),
        in_specs=[pl.BlockSpec((tm, ncol), lambda i, k: (i, k)),
                  pl.BlockSpec((None, d, ncol), lambda i, k: (k, 0, 0)),
                  pl.BlockSpec((tm, d), lambda i, k: (i, 0)), pl.BlockSpec((tm, d), lambda i, k: (i, 0)),
                  pl.BlockSpec((1, d), lambda i, k: (0, 0))],
        out_specs=[pl.BlockSpec((tm, d), lambda i, k: (i, 0)), pl.BlockSpec((1, d), lambda i, k: (0, 0))],
        out_shape=[jax.ShapeDtypeStruct((s_len, d), F32), jax.ShapeDtypeStruct((1, d), F32)],
        scratch_shapes=[pltpu.VMEM((tm, d), F32)],
        dims=("arbitrary", "arbitrary"))(dp, win_g, resid, x_in, g)


def _bwd_dy(dx2b, wout_g):
    s_len, d = dx2b.shape
    tm = min(TM_OUT, s_len)

    def body(dx_ref, w_ref, dy_ref):
        dy_ref[...] = _dot_nt(dx_ref[...], w_ref[...])

    return _pcall(
        body, name="bwd_dy", grid=(s_len // tm,),
        in_specs=[pl.BlockSpec((tm, d), lambda i: (i, 0)), pl.BlockSpec((d, d), lambda i: (0, 0))],
        out_specs=pl.BlockSpec((tm, d), lambda i: (i, 0)),
        out_shape=jax.ShapeDtypeStruct((s_len, d), F32), dims=("arbitrary",))(dx2b, wout_g)


def _bwd_mixers(p, dy, hs, gv, bv, wt, wtt, bst, wc, bc, wa, wat, ba, wx, wxt, bx, lam, ggm, glru):
    s_len = p.shape[0]
    ts = min(TS_MIX, s_len)
    nt = s_len // ts
    hb = ts // HALO

    def body(pz_ref, pgl_ref, pxl_ref, xh_ref, dy_ref, hs_ref, hh_ref, gv_ref, bv_ref, wt_ref, wtt_ref, bst_ref,
             wc_ref, bc_ref, wa_ref, wat_ref, ba_ref, wx_ref, wxt_ref, bx_ref, lam_ref, ggm_ref, glru_ref,
             dp_ref, dgv_ref, dbv_ref, dwt_ref, dbst_ref, dwc_ref, dbc_ref, dwa_ref, dba_ref, dwx_ref, dbx_ref,
             dsp_ref, dggm_ref, dglru_ref,
             carry_ref, nxt_ref, a_s, g_s, vl_s, mix_s, z_s, dm_s, dvl_s, dz_s, dxr_s):
        i = pl.program_id(0)
        ti = nt - 1 - i

        @pl.when(i == 0)
        def _():
            for ref in (dgv_ref, dbv_ref, dwt_ref, dbst_ref, dwc_ref, dbc_ref, dwa_ref, dba_ref, dwx_ref, dbx_ref,
                        dsp_ref, dggm_ref, dglru_ref, carry_ref, nxt_ref):
                ref[...] = jnp.zeros_like(ref)

        z = pz_ref[...]
        u, mixed, vh, rs, ygm = _gm_forward(z, gv_ref[...], bv_ref[...], wt_ref, bst_ref, vl_s, mix_s)
        rg, ngm = _rms_stats(ygm)
        dygm, dgn = _rms_bwd(dy_ref[:, :GM_W], ngm, rg, ggm_ref[...])
        dggm_ref[...] += _colsum(dgn)
        du = dygm * mixed
        dmix = dygm * u
        dm_s[...] = dmix.astype(BF16)
        bsum = dmix[0:CHUNK, :]
        for cc in range(1, ts // CHUNK):
            bsum = bsum + dmix[cc * CHUNK:(cc + 1) * CHUNK, :]
        dbst_ref[...] += bsum
        for hh in range(HEADS):
            cols = slice(hh * HEAD_DIM, (hh + 1) * HEAD_DIM)
            dw = jnp.zeros((CHUNK, CHUNK), F32)
            for cc in range(ts // CHUNK):
                rows = slice(cc * CHUNK, (cc + 1) * CHUNK)
                dmb = dm_s[rows, cols]
                dw = dw + _dot_nt(dmb, vl_s[rows, cols])
                dvl_s[rows, cols] = _dot(wtt_ref[hh], dmb)
            dwt_ref[hh] += dw
        dvl = dvl_s[...]
        dgv_ref[...] += _colsum(dvl * vh)
        dbv_ref[...] += _colsum(dvl)
        dvh = dvl * gv_ref[...]
        dv = rs * (dvh - _rowmean(dvh) - vh * _rowmean(dvh * vh))
        _, gd = _gelu_and_grad(z)
        dp_ref[:, :GM_W] = (du * gd[:, :GM_W]).astype(BF16)
        dp_ref[:, GM_W:2 * GM_W] = (dv * gd[:, GM_W:]).astype(BF16)

        xl = pxl_ref[...]
        xhalo = jnp.where(ti == 0, 0.0, xh_ref[...])
        gts = _lru_gates(xl, xhalo, wc_ref, bc_ref, wa_ref, ba_ref, wx_ref, bx_ref, lam_ref, z_s)
        a, mult, ra, ri, xr, sp = gts["a"], gts["mult"], gts["ra"], gts["ri"], gts["xr"], gts["sp"]
        hs = hs_ref[...]
        hprev = _shift_prev(hs, jnp.where(ti == 0, 0.0, hh_ref[...]), 1)
        gl = pgl_ref[...]
        ggl, dggl = _gelu_and_grad(gl)
        yl = hs * ggl
        rl, nl = _rms_stats(yl)
        dyl, dgn = _rms_bwd(dy_ref[:, GM_W:], nl, rl, glru_ref[...])
        dglru_ref[...] += _colsum(dgn)
        dp_ref[:, 2 * GM_W:2 * GM_W + LRU_W] = (dyl * hs * dggl).astype(BF16)
        a_s[...] = a
        g_s[...] = dyl * ggl

        def step(k, carry):
            t = ts - 1 - k
            gt = g_s[pl.ds(t, 1), :] + carry
            g_s[pl.ds(t, 1), :] = gt
            return a_s[pl.ds(t, 1), :] * gt

        carry_ref[...] = lax.fori_loop(0, ts, step, carry_ref[...], unroll=8)
        gsc = g_s[...]
        da = gsc * hprev
        rix = ri * xr
        dmult = gsc * rix
        dri = gsc * mult * xr
        dxr = gsc * mult * ri
        dla = da * a - dmult * (a * a) / mult
        dsp_ref[...] += _colsum(dla * ra) * (-LRU_C)
        dza = (dla * sp) * (-LRU_C) * ra * (1.0 - ra)
        dzi = dri * ri * (1.0 - ri)
        dba_ref[...] += _colsum(dza)
        dbx_ref[...] += _colsum(dzi)
        dzab = dza.astype(BF16)
        dzib = dzi.astype(BF16)
        xrb = gts["xrb"]
        for hh in range(HEADS):
            cols = slice(hh * HEAD_DIM, (hh + 1) * HEAD_DIM)
            dwa_ref[hh] += _dot_tn(xrb[:, cols], dzab[:, cols])
            dwx_ref[hh] += _dot_tn(xrb[:, cols], dzib[:, cols])
            dxr_s[:, cols] = _dot(dzab[:, cols], wat_ref[hh]) + _dot(dzib[:, cols], wxt_ref[hh])
        dxr = dxr + dxr_s[...]
        dbc_ref[...] += _colsum(dxr)
        dwc_ref[0:1, :] += _colsum(dxr * gts["x3"])
        dwc_ref[1:2, :] += _colsum(dxr * gts["x2"])
        dwc_ref[2:3, :] += _colsum(dxr * gts["x1"])
        dwc_ref[3:4, :] += _colsum(dxr * xl)
        nxt = nxt_ref[...]
        nxt_ref[...] = dxr[:HALO, :]
        dxl = wc_ref[3:4, :] * dxr + wc_ref[2:3, :] * _shift_next(dxr, nxt, 1) \
            + wc_ref[1:2, :] * _shift_next(dxr, nxt, 2) + wc_ref[0:1, :] * _shift_next(dxr, nxt, 3)
        dp_ref[:, 2 * GM_W + LRU_W:] = dxl.astype(BF16)

    full = lambda shape: pl.BlockSpec(shape, lambda i: (0,) * len(shape))
    rev = lambda i: nt - 1 - i
    prev_blk = lambda i: jnp.maximum((nt - 1 - i) * hb - 1, 0)
    hhd = (HEADS, HEAD_DIM, HEAD_DIM)
    small_shapes = [(1, GM_W), (1, GM_W), (HEADS, CHUNK, CHUNK), (CHUNK, GM_W), (4, LRU_W), (1, LRU_W), hhd,
                    (1, LRU_W), hhd, (1, LRU_W), (1, LRU_W), (1, GM_W), (1, LRU_W)]
    return _pcall(
        body, name="bwd_mixers", grid=(nt,),
        in_specs=[pl.BlockSpec((ts, 2 * GM_W), lambda i: (rev(i), 0)),
                  pl.BlockSpec((ts, LRU_W), lambda i: (rev(i), 2)),
                  pl.BlockSpec((ts, LRU_W), lambda i: (rev(i), 3)),
                  pl.BlockSpec((HALO, LRU_W), lambda i: (prev_blk(i), 3)),
                  pl.BlockSpec((ts, GM_W + LRU_W), lambda i: (rev(i), 0)),
                  pl.BlockSpec((ts, LRU_W), lambda i: (rev(i), 0)),
                  pl.BlockSpec((HALO, LRU_W), lambda i: (prev_blk(i), 0)),
                  full((1, GM_W)), full((1, GM_W)), full((HEADS, CHUNK, CHUNK)), full((HEADS, CHUNK, CHUNK)),
                  full((CHUNK, GM_W)), full((4, LRU_W)), full((1, LRU_W)), full(hhd), full(hhd), full((1, LRU_W)),
                  full(hhd), full(hhd), full((1, LRU_W)), full((1, LRU_W)), full((1, GM_W)), full((1, LRU_W))],
        out_specs=[pl.BlockSpec((ts, 2 * GM_W + 2 * LRU_W), lambda i: (rev(i), 0))] + [full(s) for s in small_shapes],
        out_shape=[jax.ShapeDtypeStruct((s_len, 2 * GM_W + 2 * LRU_W), BF16)]
        + [jax.ShapeDtypeStruct(s, F32) for s in small_shapes],
        scratch_shapes=[pltpu.VMEM((1, LRU_W), F32), pltpu.VMEM((HALO, LRU_W), F32),
                        pltpu.VMEM((ts, LRU_W), F32), pltpu.VMEM((ts, LRU_W), F32),
                        pltpu.VMEM((ts, GM_W), BF16), pltpu.VMEM((ts, GM_W), F32), pltpu.VMEM((ts, 2 * LRU_W), F32),
                        pltpu.VMEM((ts, GM_W), BF16), pltpu.VMEM((ts, GM_W), F32), pltpu.VMEM((ts, 2 * LRU_W), F32),
                        pltpu.VMEM((ts, LRU_W), F32)],
        dims=("arbitrary",))(p, p, p, p, dy, hs, hs, gv, bv, wt, wtt, bst, wc, bc, wa, wat, ba, wx, wxt, bx, lam,
                             ggm, glru)


def _bwd_weight(name, a, b, *, a_planes, b_planes, shard_rows):
    _, s_len, ma = a.shape
    _, _, nb = b.shape
    m, n = a_planes * ma, b_planes * nb
    tk = min(TW_K, s_len)
    if shard_rows:
        rows, cols = m // N_CHIPS, n // 2
        tm, tn = _tile(rows, TW_M), _tile(cols, TW_N)
        out_idx = lambda i, j, k: (j * tn // cols, i * tm // rows, (i * tm % rows) // tm, (j * tn % cols) // tn)
    else:
        rows, cols = m // 2, n // N_CHIPS
        tm, tn = _tile(rows, TW_M), _tile(cols, TW_N)
        out_idx = lambda i, j, k: (i * tm // rows, j * tn // cols, (i * tm % rows) // tm, (j * tn % cols) // tn)
    nk = s_len // tk
    npa, npb = ma // tm, nb // tn

    def body(a_ref, b_ref, o_ref, ob_ref, acc_ref):
        k = pl.program_id(2)

        @pl.when(k == 0)
        def _():
            acc_ref[...] = jnp.zeros_like(acc_ref)

        acc_ref[...] += _dot_tn(a_ref[...], b_ref[...])

        @pl.when(k == nk - 1)
        def _():
            o_ref[...] = acc_ref[...]
            ob_ref[...] = acc_ref[...].astype(BF16)

    shape = (2, N_CHIPS, rows, cols)
    return _pcall(
        body, name=name, grid=(m // tm, n // tn, nk),
        in_specs=[pl.BlockSpec((None, tk, tm), lambda i, j, k: (i // npa, k, i % npa)),
                  pl.BlockSpec((None, tk, tn), lambda i, j, k: (j // npb, k, j % npb))],
        out_specs=[pl.BlockSpec((None, None, tm, tn), out_idx), pl.BlockSpec((None, None, tm, tn), out_idx)],
        out_shape=[jax.ShapeDtypeStruct(shape, F32), jax.ShapeDtypeStruct(shape, BF16)],
        scratch_shapes=[pltpu.VMEM((tm, tn), F32)],
        dims=("arbitrary", "arbitrary", "arbitrary"))(a, b)


def _mesh_pos():
    return lax.axis_index("x"), lax.axis_index("y"), lax.axis_index("c")


def _other_chips(x, y):
    return [(1 - x, y), (x, 1 - y), (1 - x, 1 - y)]


def _to_slot(name, a, cs, dtype):
    _, a_rows, b_cols = a.shape
    ta = min(T_ELEM, a_rows)

    def body(cs_ref, a_ref, o_ref):
        o_ref[...] = a_ref[...].astype(dtype)

    grid_spec = pltpu.PrefetchScalarGridSpec(
        num_scalar_prefetch=1, grid=(2, a_rows // ta),
        in_specs=[pl.BlockSpec((None, ta, b_cols), lambda h, r, cs_ref: (h, r, 0))],
        out_specs=pl.BlockSpec((None, None, ta, b_cols), lambda h, r, cs_ref: (cs_ref[1], h, r, 0)))
    return _pcall(body, name=name, grid_spec=grid_spec,
                  out_shape=jax.ShapeDtypeStruct((N_CHIPS,) + a.shape, dtype), dims=("arbitrary", "arbitrary"))(cs, a)


def _all_gather(bufs):
    n = len(bufs)

    def body(*refs):
        outs = refs[n:2 * n]
        send_sems, recv_sems = refs[2 * n:]
        x, y, c = _mesh_pos()
        s = 2 * x + y
        me, sib = (x, y, c), (x, y, 1 - c)
        chips = _other_chips(x, y)

        def rcopy(a, k, blk, to):
            return pltpu.make_async_remote_copy(src_ref=blk, dst_ref=blk, send_sem=send_sems.at[a * 6 + k],
                                                recv_sem=recv_sems.at[a * 6 + k], device_id=to, device_id_type=MESH)

        first = [rcopy(a, j, outs[a].at[s, c], (cx, cy, c)) for a in range(n) for j, (cx, cy) in enumerate(chips)]
        for cp in first:
            cp.start()
        passed = []
        for a in range(n):
            for j, (cx, cy) in enumerate(chips):
                blk = outs[a].at[2 * cx + cy, c]
                rcopy(a, j, blk, me).wait_recv()
                cp = rcopy(a, 3 + j, blk, sib)
                cp.start()
                passed.append(cp)
        for a in range(n):
            for j, (cx, cy) in enumerate(chips):
                rcopy(a, 3 + j, outs[a].at[2 * cx + cy, 1 - c], me).wait_recv()
        for cp in first + passed:
            cp.wait_send()

    return _pcall(
        body, name="all_gather_weights",
        in_specs=[ANY] * n, out_specs=[ANY] * n,
        out_shape=[jax.ShapeDtypeStruct(a.shape, a.dtype) for a in bufs],
        scratch_shapes=[pltpu.SemaphoreType.DMA((6 * n,)), pltpu.SemaphoreType.DMA((6 * n,))],
        aliases={a: a for a in range(n)})(*bufs)


def _sibling_swap(arrs):
    n = len(arrs)

    def body(*refs):
        ins, outs = refs[:n], refs[n:2 * n]
        send_sems, recv_sems = refs[2 * n:]
        x, y, c = _mesh_pos()
        cps = [pltpu.make_async_remote_copy(src_ref=ins[a].at[1 - c], dst_ref=outs[a], send_sem=send_sems.at[a],
                                            recv_sem=recv_sems.at[a], device_id=(x, y, 1 - c), device_id_type=MESH)
               for a in range(n)]
        for cp in cps:
            cp.start()
        for cp in cps:
            cp.wait()

    return _pcall(
        body, name="pair_swap", in_specs=[ANY] * n, out_specs=[ANY] * n,
        out_shape=[jax.ShapeDtypeStruct(a.shape[1:], a.dtype) for a in arrs],
        scratch_shapes=[pltpu.SemaphoreType.DMA((n,)), pltpu.SemaphoreType.DMA((n,))])(*arrs)


def _chip_exchange(big, small):
    n = len(big)

    def body(*refs):
        ins, outs = refs[:n + 1], refs[n + 1:2 * n + 2]
        send_sems, recv_sems = refs[2 * n + 2:]
        x, y, c = _mesh_pos()
        cps = []
        for a in range(n + 1):
            for j, (cx, cy) in enumerate(_other_chips(x, y)):
                src = ins[a].at[2 * cx + cy] if a < n else ins[a]
                cps.append(pltpu.make_async_remote_copy(src_ref=src, dst_ref=outs[a].at[j], send_sem=send_sems.at[3 * a + j],
                                                        recv_sem=recv_sems.at[3 * a + j], device_id=(cx, cy, c),
                                                        device_id_type=MESH))
        for cp in cps:
            cp.start()
        for cp in cps:
            cp.wait()

    arrs = list(big) + [small]
    shapes = [a.shape[1:] for a in big] + [small.shape]
    return _pcall(
        body, name="chip_exchange", in_specs=[ANY] * (n + 1), out_specs=[ANY] * (n + 1),
        out_shape=[jax.ShapeDtypeStruct((3,) + sh, a.dtype) for sh, a in zip(shapes, arrs)],
        scratch_shapes=[pltpu.SemaphoreType.DMA((3 * (n + 1),)), pltpu.SemaphoreType.DMA((3 * (n + 1),))])(*arrs)


def _halves_swap(bufs):
    n = len(bufs)

    def body(*refs):
        outs = refs[n:2 * n]
        send_sems, recv_sems = refs[2 * n:]
        x, y, c = _mesh_pos()
        cps = [pltpu.make_async_remote_copy(src_ref=outs[a].at[c], dst_ref=outs[a].at[c], send_sem=send_sems.at[a],
                                            recv_sem=recv_sems.at[a], device_id=(x, y, 1 - c), device_id_type=MESH)
               for a in range(n)]
        for cp in cps:
            cp.start()
        for cp in cps:
            cp.wait()

    return _pcall(
        body, name="halves_swap", in_specs=[ANY] * n, out_specs=[ANY] * n,
        out_shape=[jax.ShapeDtypeStruct(a.shape, a.dtype) for a in bufs],
        scratch_shapes=[pltpu.SemaphoreType.DMA((n,)), pltpu.SemaphoreType.DMA((n,))],
        aliases={a: a for a in range(n)})(*bufs)


def _pair_sum(name, g32, recv, cs):
    _, nch, a_rows, b_cols = g32.shape
    ta = min(T_ELEM, a_rows)

    def body(cs_ref, g_ref, r_ref, pb_ref, own_ref):
        k = pl.program_id(1)
        v = g_ref[...] + r_ref[...].astype(F32)
        pb_ref[...] = v.astype(BF16)

        @pl.when(k == cs_ref[1])
        def _():
            own_ref[...] = v

    grid_spec = pltpu.PrefetchScalarGridSpec(
        num_scalar_prefetch=1, grid=(a_rows // ta, nch),
        in_specs=[pl.BlockSpec((None, None, ta, b_cols), lambda r, k, cs_ref: (cs_ref[0], k, r, 0)),
                  pl.BlockSpec((None, ta, b_cols), lambda r, k, cs_ref: (k, r, 0))],
        out_specs=[pl.BlockSpec((None, ta, b_cols), lambda r, k, cs_ref: (k, r, 0)),
                   pl.BlockSpec((ta, b_cols), lambda r, k, cs_ref: (r, 0))])
    return _pcall(
        body, name=name, grid_spec=grid_spec,
        out_shape=[jax.ShapeDtypeStruct((nch, a_rows, b_cols), BF16), jax.ShapeDtypeStruct((a_rows, b_cols), F32)],
        dims=("arbitrary", "arbitrary"))(cs, g32, recv)


def _small_pair_sum(mine, recv, cs):
    _, r, ccols = mine.shape

    def body(cs_ref, a_ref, b_ref, o_ref):
        o_ref[...] = a_ref[...] + b_ref[...]

    grid_spec = pltpu.PrefetchScalarGridSpec(
        num_scalar_prefetch=1, grid=(1,),
        in_specs=[pl.BlockSpec((None, r, ccols), lambda i, cs_ref: (cs_ref[0], 0, 0)),
                  pl.BlockSpec((r, ccols), lambda i, cs_ref: (0, 0))],
        out_specs=pl.BlockSpec((r, ccols), lambda i, cs_ref: (0, 0)))
    return _pcall(body, name="small_pair_sum", grid_spec=grid_spec,
                  out_shape=jax.ShapeDtypeStruct((r, ccols), F32), dims=("arbitrary",))(cs, mine, recv)


def _chip_sum(name, own, recv, cs):
    a_rows, b_cols = own.shape
    ta = min(T_ELEM, a_rows)

    def body(cs_ref, o_ref, r_ref, f_ref):
        f_ref[...] = ((o_ref[...] + r_ref[0].astype(F32)) + r_ref[1].astype(F32)) + r_ref[2].astype(F32)

    grid_spec = pltpu.PrefetchScalarGridSpec(
        num_scalar_prefetch=1, grid=(a_rows // ta,),
        in_specs=[pl.BlockSpec((ta, b_cols), lambda r, cs_ref: (r, 0)),
                  pl.BlockSpec((3, ta, b_cols), lambda r, cs_ref: (0, r, 0))],
        out_specs=pl.BlockSpec((None, ta, b_cols), lambda r, cs_ref: (cs_ref[0], r, 0)))
    return _pcall(body, name=name, grid_spec=grid_spec,
                  out_shape=jax.ShapeDtypeStruct((2, a_rows, b_cols), F32), dims=("arbitrary",))(cs, own, recv)


def _small_chip_sum(pair, recv, cs):
    r, ccols = pair.shape

    def body(cs_ref, p_ref, r_ref, o_ref):
        s = cs_ref[1]
        own = p_ref[...]
        total = None
        for k in range(N_CHIPS):
            flip = jnp.bitwise_xor(s, k)
            term = jnp.where(flip == 0, own, jnp.where(flip == 2, r_ref[0], jnp.where(flip == 1, r_ref[1], r_ref[2])))
            total = term if total is None else total + term
        o_ref[...] = total

    grid_spec = pltpu.PrefetchScalarGridSpec(
        num_scalar_prefetch=1, grid=(1,),
        in_specs=[pl.BlockSpec((r, ccols), lambda i, cs_ref: (0, 0)),
                  pl.BlockSpec((3, r, ccols), lambda i, cs_ref: (0, 0, 0))],
        out_specs=pl.BlockSpec((None, r, ccols), lambda i, cs_ref: (cs_ref[0], 0, 0)))
    return _pcall(body, name="small_chip_sum", grid_spec=grid_spec,
                  out_shape=jax.ShapeDtypeStruct((2, r, ccols), F32), dims=("arbitrary",))(cs, pair, recv)


def _adamw(name, w, g, m, v, *, halves):
    rows, cols = w.shape
    tr, tc = min(T_ELEM, rows), PACK_COLS
    c1 = 1.0 - ADAM_B1 ** ADAM_STEP
    c2 = 1.0 - ADAM_B2 ** ADAM_STEP

    def body(w_ref, g_ref, m_ref, v_ref, d_ref, mo_ref, vo_ref, go_ref):
        g_ = g_ref[...]
        m_ = ADAM_B1 * m_ref[...] + (1.0 - ADAM_B1) * g_
        v_ = ADAM_B2 * v_ref[...] + (1.0 - ADAM_B2) * (g_ * g_)
        mo_ref[...] = m_
        vo_ref[...] = v_
        go_ref[...] = g_
        d_ref[...] = (-ADAM_LR) * ((m_ / c1) / (jnp.sqrt(v_ / c2) + ADAM_EPS) + ADAM_WD * w_ref[...])

    spec = pl.BlockSpec((tr, tc), lambda r, j: (r, j))
    if halves == "rows":
        nrh = rows // 2 // tr
        g_spec = pl.BlockSpec((None, tr, tc), lambda r, j: (r // nrh, r % nrh, j))
    elif halves == "cols":
        nch = cols // 2 // tc
        g_spec = pl.BlockSpec((None, tr, tc), lambda r, j: (j // nch, r, j % nch))
    else:
        g_spec = spec
    return _pcall(body, name=name, grid=(rows // tr, cols // tc), in_specs=[spec, g_spec, spec, spec],
                  out_specs=[spec] * 4, out_shape=[jax.ShapeDtypeStruct((rows, cols), F32)] * 4,
                  dims=("arbitrary", "arbitrary"))(w, g, m, v)


def _pack(parts, rows):
    flat = jnp.concatenate([a.reshape(-1) for a in parts])
    return jnp.pad(flat, (0, rows * PACK_COLS - flat.shape[0])).reshape(rows, PACK_COLS)


def _unpack(buf, shapes):
    flat = buf.reshape(-1)
    out, off = [], 0
    for sh in shapes:
        size = math.prod(sh)
        out.append(flat[off:off + size].reshape(sh))
        off += size
    return out


def _pack_rows(shapes, multiple):
    total = sum(math.prod(sh) for sh in shapes)
    rows = -(-total // PACK_COLS)
    return -(-rows // multiple) * multiple


SMALL = ["norm1_g", "gm_v_g", "gm_v_b", "gm_ws", "gm_bs", "lru_conv_w", "lru_conv_b", "lru_wa", "lru_ba", "lru_wx",
         "lru_bx", "lru_lambda", "gm_out_g", "lru_out_g", "norm2_g", "ffn_conv_w", "ffn_conv_b", "final_g"]
BIG = ["w_in", "w_out", "ffn_w_up", "ffn_w_down"]
ORDER = ["norm1_g", "w_in", "gm_v_g", "gm_v_b", "gm_ws", "gm_bs", "lru_conv_w", "lru_conv_b", "lru_wa", "lru_ba",
         "lru_wx", "lru_bx", "lru_lambda", "gm_out_g", "lru_out_g", "w_out", "norm2_g", "ffn_w_up", "ffn_conv_w",
         "ffn_conv_b", "ffn_w_down", "final_g"]


def kernel(x, norm1_g, w_in, gm_v_g, gm_v_b, gm_ws, gm_bs, lru_conv_w, lru_conv_b, lru_wa, lru_ba, lru_wx, lru_bx, lru_lambda, gm_out_g, lru_out_g, w_out, norm2_g, ffn_w_up, ffn_conv_w, ffn_conv_b, ffn_w_down, final_g, loss_target, m_norm1_g, m_w_in, m_gm_v_g, m_gm_v_b, m_gm_ws, m_gm_bs, m_lru_conv_w, m_lru_conv_b, m_lru_wa, m_lru_ba, m_lru_wx, m_lru_bx, m_lru_lambda, m_gm_out_g, m_lru_out_g, m_w_out, m_norm2_g, m_ffn_w_up, m_ffn_conv_w, m_ffn_conv_b, m_ffn_w_down, m_final_g, v_norm1_g, v_w_in, v_gm_v_g, v_gm_v_b, v_gm_ws, v_gm_bs, v_lru_conv_w, v_lru_conv_b, v_lru_wa, v_lru_ba, v_lru_wx, v_lru_bx, v_lru_lambda, v_gm_out_g, v_lru_out_g, v_w_out, v_norm2_g, v_ffn_w_up, v_ffn_conv_w, v_ffn_conv_b, v_ffn_w_down, v_final_g):
    w = dict(norm1_g=norm1_g, w_in=w_in, gm_v_g=gm_v_g, gm_v_b=gm_v_b, gm_ws=gm_ws, gm_bs=gm_bs, lru_conv_w=lru_conv_w, lru_conv_b=lru_conv_b, lru_wa=lru_wa, lru_ba=lru_ba, lru_wx=lru_wx, lru_bx=lru_bx, lru_lambda=lru_lambda, gm_out_g=gm_out_g, lru_out_g=lru_out_g, w_out=w_out, norm2_g=norm2_g, ffn_w_up=ffn_w_up, ffn_conv_w=ffn_conv_w, ffn_conv_b=ffn_conv_b, ffn_w_down=ffn_w_down, final_g=final_g)
    m = dict(norm1_g=m_norm1_g, w_in=m_w_in, gm_v_g=m_gm_v_g, gm_v_b=m_gm_v_b, gm_ws=m_gm_ws, gm_bs=m_gm_bs, lru_conv_w=m_lru_conv_w, lru_conv_b=m_lru_conv_b, lru_wa=m_lru_wa, lru_ba=m_lru_ba, lru_wx=m_lru_wx, lru_bx=m_lru_bx, lru_lambda=m_lru_lambda, gm_out_g=m_gm_out_g, lru_out_g=m_lru_out_g, w_out=m_w_out, norm2_g=m_norm2_g, ffn_w_up=m_ffn_w_up, ffn_conv_w=m_ffn_conv_w, ffn_conv_b=m_ffn_conv_b, ffn_w_down=m_ffn_w_down, final_g=m_final_g)
    v = dict(norm1_g=v_norm1_g, w_in=v_w_in, gm_v_g=v_gm_v_g, gm_v_b=v_gm_v_b, gm_ws=v_gm_ws, gm_bs=v_gm_bs, lru_conv_w=v_lru_conv_w, lru_conv_b=v_lru_conv_b, lru_wa=v_lru_wa, lru_ba=v_lru_ba, lru_wx=v_lru_wx, lru_bx=v_lru_bx, lru_lambda=v_lru_lambda, gm_out_g=v_gm_out_g, lru_out_g=v_lru_out_g, w_out=v_w_out, norm2_g=v_norm2_g, ffn_w_up=v_ffn_w_up, ffn_conv_w=v_ffn_conv_w, ffn_conv_b=v_ffn_conv_b, ffn_w_down=v_ffn_w_down, final_g=v_final_g)

    mx, my, mc = _mesh_pos()
    shard = 2 * mx + my
    cs = jnp.stack([mc, shard]).astype(jnp.int32)

    xs = x[0]
    tgt = loss_target[0]
    s_len, d = xs.shape

    halves = lambda a: a.reshape((2, a.shape[0] // 2) + a.shape[1:])
    gathered = _all_gather([_to_slot("slot_" + k, halves(w[k][0]), cs, BF16) for k in BIG]
                           + [_to_slot("slot_lru_conv_w", w["lru_conv_w"][0].reshape(2, 4, -1), cs, F32),
                              _to_slot("slot_ffn_conv_w", w["ffn_conv_w"][0].reshape(2, 12, -1), cs, F32)])
    win_g = gathered[0].reshape(N_CHIPS, d, -1)
    wout_g = gathered[1].reshape(-1, d)
    wup_g = gathered[2].reshape(N_CHIPS, d, -1)
    wdown_g = gathered[3].reshape(-1, d)
    wc = gathered[4].reshape(N_CHIPS, 4, -1).transpose(1, 0, 2).reshape(4, -1)
    wfc = gathered[5].reshape(N_CHIPS, 3, -1).transpose(1, 0, 2).reshape(3, -1)
    ff = wfc.shape[1] // 2

    tril = jnp.tril(jnp.ones((CHUNK, CHUNK), bool))
    wt32 = jnp.where(tril[None], w["gm_ws"][0], 0.0)
    wt = wt32.astype(BF16)
    wtt = wt32.transpose(0, 2, 1).astype(BF16)
    bst = jnp.repeat(w["gm_bs"][0].T, HEAD_DIM, axis=1)
    wa = w["lru_wa"][0].astype(BF16)
    wx = w["lru_wx"][0].astype(BF16)
    wat = w["lru_wa"][0].transpose(0, 2, 1).astype(BF16)
    wxt = w["lru_wx"][0].transpose(0, 2, 1).astype(BF16)
    ba = w["lru_ba"][0].reshape(1, -1)
    bx = w["lru_bx"][0].reshape(1, -1)
    gf = w["final_g"].reshape(1, -1)

    p, h1 = _fwd_in_proj(xs, w["norm1_g"], win_g)
    y, hs = _fwd_mixers(p, w["gm_v_g"], w["gm_v_b"], wt, bst, wc, w["lru_conv_b"], wa, ba, wx, bx, w["lru_lambda"],
                        w["gm_out_g"], w["lru_out_g"])
    x2, h2 = _fwd_out_proj(xs, y, wout_g, w["norm2_g"])
    up, act, dx3, dx3b, loss_tile, dgf = _fwd_ffn(h2, wup_g, wfc, w["ffn_conv_b"], wdown_g, x2, gf, tgt)
    loss = lax.psum(loss_tile[0, 0], ("x", "y", "c"))

    g_down = _bwd_weight("bwd_w_down", act[None], dx3b[None], a_planes=1, b_planes=1, shard_rows=True)
    dup, dwf, dx2, dx2b, dg2 = _bwd_ffn(dx3b, wdown_g, up, wfc, w["ffn_conv_b"], wup_g, dx3, x2, w["norm2_g"])
    g_up = _bwd_weight("bwd_w_up", h2[None], dup, a_planes=1, b_planes=2, shard_rows=False)
    g_out = _bwd_weight("bwd_w_out", y[None], dx2b[None], a_planes=1, b_planes=1, shard_rows=True)
    dy = _bwd_dy(dx2b, wout_g)
    (dp, dgv, dbv, dwt, dbst, dwc, dbc, dwa, dba, dwx, dbx, dsp, dggm, dglru) = _bwd_mixers(
        p, dy, hs, w["gm_v_g"], w["gm_v_b"], wt, wtt, bst, wc, w["lru_conv_b"], wa, wat, ba, wx, wxt, bx,
        w["lru_lambda"], w["gm_out_g"], w["lru_out_g"])
    g_in = _bwd_weight("bwd_w_in", h1[None], dp[None], a_planes=1, b_planes=1, shard_rows=False)
    grad_x, dg1 = _bwd_in(dp, win_g, dx2, xs, w["norm1_g"])

    dwfc = dwf[:, :, :3].transpose(2, 0, 1, 3).reshape(3, -1)
    dbfc = dwf[:, :, 3].reshape(1, -1)
    dlam = dsp * (-_sigmoid(-w["lru_lambda"]))
    small_grads = dict(
        norm1_g=dg1, gm_v_g=dgv, gm_v_b=dbv, gm_ws=jnp.where(tril[None], dwt, 0.0),
        gm_bs=dbst.reshape(CHUNK, HEADS, HEAD_DIM).sum(-1).T, lru_conv_w=dwc, lru_conv_b=dbc, lru_wa=dwa, lru_ba=dba,
        lru_wx=dwx, lru_bx=dbx, lru_lambda=dlam, gm_out_g=dggm, lru_out_g=dglru, norm2_g=dg2, ffn_conv_w=dwfc,
        ffn_conv_b=dbfc, final_g=dgf)
    full_shapes = [small_grads[k].shape for k in SMALL]
    rows_full = _pack_rows(full_shapes, 16)
    gpack = _pack([small_grads[k] for k in SMALL], rows_full).reshape(2, rows_full // 2, PACK_COLS)

    bigs = [g_in, g_out, g_up, g_down]
    recv = _sibling_swap([gb for _, gb in bigs] + [gpack])
    pair = [_pair_sum("pair_sum_%d" % i, g32, r, cs) for i, ((g32, _), r) in enumerate(zip(bigs, recv[:4]))]
    small_pair = _small_pair_sum(gpack, recv[4], cs)
    got = _chip_exchange([pb for pb, _ in pair], small_pair)
    reduced = [_chip_sum("chip_sum_%d" % i, own, r, cs) for i, ((_, own), r) in enumerate(zip(pair, got[:4]))]
    small_half = _small_chip_sum(small_pair, got[4], cs)
    finals = _halves_swap(reduced + [small_half])

    grads = {}
    for name, g in zip(SMALL, _unpack(finals[4], full_shapes)):
        blk = w[name].shape[1:] if w[name].ndim > 1 else w[name].shape
        if name in ("lru_conv_w", "ffn_conv_w"):
            g = lax.dynamic_slice_in_dim(g, shard * blk[1], blk[1], axis=1)
        grads[name] = g.reshape(blk)

    delta, new_m, new_v = {}, {}, {}
    for name, gfull, hv in zip(BIG, finals[:4], ("rows", "cols", "rows", "cols")):
        shp = w[name].shape
        r2 = lambda a: a.reshape(shp[1:])
        delta[name], new_m[name], new_v[name], grads[name] = _adamw(
            "adamw_" + name, r2(w[name]), gfull, r2(m[name]), r2(v[name]), halves=hv)
    blk_shapes = [grads[k].shape for k in SMALL]
    rows_blk = _pack_rows(blk_shapes, T_ELEM)
    packs = [_pack([src[k] for k in SMALL], rows_blk) for src in (w, grads, m, v)]
    outs = _adamw("adamw_small", *packs, halves=None)
    for dst, buf in zip((delta, new_m, new_v), outs):
        for name, a in zip(SMALL, _unpack(buf, blk_shapes)):
            dst[name] = a

    def shaped(dct):
        return [dct[k].reshape(w[k].shape) for k in ORDER]

    return (loss, grad_x[None], *shaped(grads), *shaped(delta), *shaped(new_m), *shaped(new_v))
```

```python
import functools
import math

import jax
import jax.numpy as jnp
from jax import lax
from jax.experimental import pallas as pl
from jax.experimental.pallas import tpu as pltpu

F32 = jnp.float32
BF16 = jnp.bfloat16
MESH = pl.DeviceIdType.MESH
ANY = pl.BlockSpec(memory_space=pltpu.HBM)

GM_W = 1024
LRU_W = 1024
CHUNK = 128
HEADS = 8
HEAD_DIM = 128
LRU_C = 8.0
RMS_EPS = 1e-6
LN_EPS = 1e-5
ADAM_LR = 0.001
ADAM_B1 = 0.9
ADAM_B2 = 0.999
ADAM_EPS = 1e-08
ADAM_WD = 0.01
ADAM_STEP = 10

N_CHIPS = 4
HALO = 8
PACK_COLS = 1024
VMEM_LIMIT = 56 * 1024 * 1024

TM_IN = 1024
TS_MIX = 256
TM_OUT = 512
TM_UP = 512
TN_UP = 512
T_EPI = 128
T_CHUNK = 16
TM_BW = 512
TW_M = 1024
TW_N = 1024
TW_K = 2048
T_ELEM = 256
LANES = 128


def _tile(dim, cap):
    t = min(cap, dim) // LANES * LANES
    while dim % t:
        t -= LANES
    return t

_GELU_K0 = 0.7978845608028654
_GELU_K1 = 0.044715


def _pcall(body, *, name, out_shape, grid=None, in_specs=None, out_specs=None, scratch_shapes=(),
           grid_spec=None, dims=None, aliases=None):
    params = pltpu.CompilerParams(dimension_semantics=dims, vmem_limit_bytes=VMEM_LIMIT)
    kw = dict(name=name, out_shape=out_shape, compiler_params=params)
    if aliases:
        kw["input_output_aliases"] = aliases
    if grid_spec is not None:
        return pl.pallas_call(body, grid_spec=grid_spec, **kw)
    if grid is not None:
        kw["grid"] = grid
    return pl.pallas_call(body, in_specs=in_specs, out_specs=out_specs, scratch_shapes=list(scratch_shapes), **kw)


def _gelu(x):
    t = jnp.tanh(_GELU_K0 * (x + _GELU_K1 * (x * x * x)))
    return 0.5 * x * (1.0 + t)


def _gelu_and_grad(x):
    x2 = x * x
    t = jnp.tanh(_GELU_K0 * (x + _GELU_K1 * (x2 * x)))
    g = 0.5 * x * (1.0 + t)
    dg = 0.5 * (1.0 + t) + 0.5 * x * (1.0 - t * t) * (_GELU_K0 * (1.0 + 3.0 * _GELU_K1 * x2))
    return g, dg


def _sigmoid(x):
    return 1.0 / (1.0 + jnp.exp(-x))


def _neg_expm1(x):
    series = -x * (1.0 + x * (0.5 + x * (1.0 / 6.0 + x * (1.0 / 24.0 + x * (1.0 / 120.0 + x * (1.0 / 720.0))))))
    return jnp.where(x > -0.1, series, 1.0 - jnp.exp(x))


def _softplus(z):
    return jnp.maximum(z, 0.0) + jnp.log(1.0 + jnp.exp(-jnp.abs(z)))


def _rowmean(x):
    return jnp.mean(x, axis=-1, keepdims=True)


def _colsum(x):
    return jnp.sum(x, axis=0, keepdims=True)


def _rms_stats(x):
    r = lax.rsqrt(_rowmean(x * x) + RMS_EPS)
    return r, x * r


def _rms_bwd(dy, n, r, g):
    dn = dy * g
    return r * (dn - n * _rowmean(dn * n)), dy * n


def _shift_prev(x, halo, d):
    cat = jnp.concatenate([halo, x], axis=0)
    return pltpu.roll(cat, d, 0)[HALO:, :]


def _prev_rows(cat):
    return cat[HALO:, :], pltpu.roll(cat, 1, 0)[HALO:, :], pltpu.roll(cat, 2, 0)[HALO:, :]


def _next_rows(cat):
    n = cat.shape[0]
    return cat[:n - HALO, :], pltpu.roll(cat, n - 1, 0)[:n - HALO, :], pltpu.roll(cat, n - 2, 0)[:n - HALO, :]


def _shift_next(x, halo, d):
    n = x.shape[0]
    cat = jnp.concatenate([x, halo], axis=0)
    return pltpu.roll(cat, n + HALO - d, 0)[:n, :]


def _dot(a, b):
    return jnp.dot(a, b, preferred_element_type=F32)


def _dot_nt(a, b):
    return lax.dot_general(a, b, (((1,), (1,)), ((), ())), preferred_element_type=F32)


def _dot_tn(a, b):
    return lax.dot_general(a, b, (((0,), (0,)), ((), ())), preferred_element_type=F32)


def _fwd_in_proj(x, g1, win_g):
    s_len, d = x.shape
    nsh, _, ncol = win_g.shape
    tm = min(TM_IN, s_len)

    def body(x_ref, g_ref, w_ref, p_ref, h_ref):
        @pl.when(pl.program_id(1) == 0)
        def _():
            _, n = _rms_stats(x_ref[...])
            h_ref[...] = (n * g_ref[...]).astype(BF16)

        p_ref[...] = _dot(h_ref[...], w_ref[...])

    return _pcall(
        body, name="fwd_in_proj", grid=(s_len // tm, nsh),
        in_specs=[pl.BlockSpec((tm, d), lambda i, j: (i, 0)),
                  pl.BlockSpec((1, d), lambda i, j: (0, 0)),
                  pl.BlockSpec((None, d, ncol), lambda i, j: (j, 0, 0))],
        out_specs=[pl.BlockSpec((tm, ncol), lambda i, j: (i, j)),
                   pl.BlockSpec((tm, d), lambda i, j: (i, 0))],
        out_shape=[jax.ShapeDtypeStruct((s_len, nsh * ncol), F32), jax.ShapeDtypeStruct((s_len, d), BF16)],
        dims=("arbitrary", "arbitrary"))(x, g1, win_g)


def _gm_forward(z, gv, bv, wt_ref, bst_ref, vl_s, mix_s):
    ts = z.shape[0]
    ge = _gelu(z)
    u = ge[:, :GM_W]
    v = ge[:, GM_W:]
    vc = v - _rowmean(v)
    rs = lax.rsqrt(_rowmean(vc * vc) + LN_EPS)
    vh = vc * rs
    vl_s[...] = (vh * gv + bv).astype(BF16)
    for cc in range(ts // CHUNK):
        rows = slice(cc * CHUNK, (cc + 1) * CHUNK)
        for hh in range(HEADS):
            cols = slice(hh * HEAD_DIM, (hh + 1) * HEAD_DIM)
            mix_s[rows, cols] = _dot(wt_ref[hh], vl_s[rows, cols]) + bst_ref[:, cols]
    mixed = mix_s[...]
    return u, mixed, vh, rs, u * mixed


def _lru_gates(xl, halo, wc_ref, bc_ref, wa_ref, ba_ref, wx_ref, bx_ref, lam_ref, z_s):
    x1 = _shift_prev(xl, halo, 1)
    x2 = _shift_prev(xl, halo, 2)
    x3 = _shift_prev(xl, halo, 3)
    xr = bc_ref[...] + wc_ref[0:1, :] * x3 + wc_ref[1:2, :] * x2 + wc_ref[2:3, :] * x1 + wc_ref[3:4, :] * xl
    xrb = xr.astype(BF16)
    for hh in range(HEADS):
        cols = slice(hh * HEAD_DIM, (hh + 1) * HEAD_DIM)
        z_s[:, cols] = _dot(xrb[:, cols], wa_ref[hh])
        z_s[:, LRU_W + hh * HEAD_DIM:LRU_W + (hh + 1) * HEAD_DIM] = _dot(xrb[:, cols], wx_ref[hh])
    ra = _sigmoid(z_s[:, :LRU_W] + ba_ref[...])
    ri = _sigmoid(z_s[:, LRU_W:] + bx_ref[...])
    sp = _softplus(-lam_ref[...])
    la = (-LRU_C) * ra * sp
    a = jnp.exp(la)
    mult = jnp.sqrt(_neg_expm1(2.0 * la))
    return dict(x1=x1, x2=x2, x3=x3, xr=xr, xrb=xrb, ra=ra, ri=ri, sp=sp, a=a, mult=mult)


def _fwd_mixers(p, gv, bv, wt, bst, wc, bc, wa, ba, wx, bx, lam, ggm, glru):
    s_len = p.shape[0]
    ts = min(TS_MIX, s_len)

    def body(pz_ref, pgl_ref, pxl_ref, gv_ref, bv_ref, wt_ref, bst_ref, wc_ref, bc_ref, wa_ref, ba_ref, wx_ref,
             bx_ref, lam_ref, ggm_ref, glru_ref, y_ref, hs_ref, tail_ref, h_ref, a_s, b_s, vl_s, mix_s, z_s):
        @pl.when(pl.program_id(0) == 0)
        def _():
            tail_ref[...] = jnp.zeros_like(tail_ref)
            h_ref[...] = jnp.zeros_like(h_ref)

        _, _, _, _, ygm = _gm_forward(pz_ref[...], gv_ref[...], bv_ref[...], wt_ref, bst_ref, vl_s, mix_s)
        _, ngm = _rms_stats(ygm)
        y_ref[:, :GM_W] = (ngm * ggm_ref[...]).astype(BF16)

        xl = pxl_ref[...]
        gts = _lru_gates(xl, tail_ref[...], wc_ref, bc_ref, wa_ref, ba_ref, wx_ref, bx_ref, lam_ref, z_s)
        tail_ref[...] = xl[ts - HALO:, :]
        a_s[...] = gts["a"]
        b_s[...] = gts["mult"] * (gts["ri"] * gts["xr"])

        def step(t, h):
            h = a_s[pl.ds(t, 1), :] * h + b_s[pl.ds(t, 1), :]
            hs_ref[pl.ds(t, 1), :] = h
            return h

        h_ref[...] = lax.fori_loop(0, ts, step, h_ref[...], unroll=8)
        yl = hs_ref[...] * _gelu(pgl_ref[...])
        _, nl = _rms_stats(yl)
        y_ref[:, GM_W:] = (nl * glru_ref[...]).astype(BF16)

    full = lambda shape: pl.BlockSpec(shape, lambda i: (0,) * len(shape))
    return _pcall(
        body, name="fwd_mixers", grid=(s_len // ts,),
        in_specs=[pl.BlockSpec((ts, 2 * GM_W), lambda i: (i, 0)),
                  pl.BlockSpec((ts, LRU_W), lambda i: (i, 2)),
                  pl.BlockSpec((ts, LRU_W), lambda i: (i, 3)),
                  full((1, GM_W)), full((1, GM_W)), full((HEADS, CHUNK, CHUNK)), full((CHUNK, GM_W)),
                  full((4, LRU_W)), full((1, LRU_W)), full((HEADS, HEAD_DIM, HEAD_DIM)), full((1, LRU_W)),
                  full((HEADS, HEAD_DIM, HEAD_DIM)), full((1, LRU_W)), full((1, LRU_W)), full((1, GM_W)),
                  full((1, LRU_W))],
        out_specs=[pl.BlockSpec((ts, GM_W + LRU_W), lambda i: (i, 0)), pl.BlockSpec((ts, LRU_W), lambda i: (i, 0))],
        out_shape=[jax.ShapeDtypeStruct((s_len, GM_W + LRU_W), BF16), jax.ShapeDtypeStruct((s_len, LRU_W), F32)],
        scratch_shapes=[pltpu.VMEM((HALO, LRU_W), F32), pltpu.VMEM((1, LRU_W), F32),
                        pltpu.VMEM((ts, LRU_W), F32), pltpu.VMEM((ts, LRU_W), F32),
                        pltpu.VMEM((ts, GM_W), BF16), pltpu.VMEM((ts, GM_W), F32), pltpu.VMEM((ts, 2 * LRU_W), F32)],
        dims=("arbitrary",))(p, p, p, gv, bv, wt, bst, wc, bc, wa, ba, wx, bx, lam, ggm, glru)


def _fwd_out_proj(x, y, wout_g, g2):
    s_len, d = x.shape
    tm = min(TM_OUT, s_len)

    def body(x_ref, y_ref, w_ref, g_ref, x2_ref, h2_ref):
        x2 = x_ref[...] + _dot(y_ref[...], w_ref[...])
        x2_ref[...] = x2
        _, n = _rms_stats(x2)
        h2_ref[...] = (n * g_ref[...]).astype(BF16)

    return _pcall(
        body, name="fwd_out_proj", grid=(s_len // tm,),
        in_specs=[pl.BlockSpec((tm, d), lambda i: (i, 0)), pl.BlockSpec((tm, d), lambda i: (i, 0)),
                  pl.BlockSpec((d, d), lambda i: (0, 0)), pl.BlockSpec((1, d), lambda i: (0, 0))],
        out_specs=[pl.BlockSpec((tm, d), lambda i: (i, 0)), pl.BlockSpec((tm, d), lambda i: (i, 0))],
        out_shape=[jax.ShapeDtypeStruct((s_len, d), F32), jax.ShapeDtypeStruct((s_len, d), BF16)],
        dims=("arbitrary",))(x, y, wout_g, g2)


def _row_fetch(hbm_ref, buf_ref, sem, row0, rows):
    return pltpu.make_async_copy(hbm_ref.at[pl.ds(row0, rows), :], buf_ref, sem)


def _fwd_ffn(h2, wup_g, wfc, bfc, wdown_g, x2, gf, target):
    s_len, d = h2.shape
    nsh, _, ncol = wup_g.shape
    f = nsh * ncol // 2
    tm = min(TM_UP, s_len)
    te = min(T_EPI, tm)
    tr = T_CHUNK
    tn = TN_UP
    nps = ncol // tn
    nj = f // tn

    def body(h_ref, wg_ref, wv_ref, wcg_ref, wcv_ref, bg_ref, bv_ref, wd_ref, g_ref, x2_hbm, t_hbm,
             up_ref, act_ref, dx3_ref, dx3b_ref, loss_ref, dgf_ref, tail_ref, acc_ref, x2_buf, t_buf, sems):
        i, j = pl.program_id(0), pl.program_id(1)
        row0 = pl.multiple_of(i * tm, tm)
        fetches = (_row_fetch(x2_hbm, x2_buf, sems.at[0], row0, tm), _row_fetch(t_hbm, t_buf, sems.at[1], row0, tm))

        @pl.when(jnp.logical_and(i == 0, j == 0))
        def _():
            tail_ref[...] = jnp.zeros_like(tail_ref)
            loss_ref[...] = jnp.zeros_like(loss_ref)
            dgf_ref[...] = jnp.zeros_like(dgf_ref)

        @pl.when(j == 0)
        def _():
            acc_ref[...] = jnp.zeros_like(acc_ref)
            for cp in fetches:
                cp.start()

        h = h_ref[...]
        up_ref[0] = _dot(h, wg_ref[...])
        up_ref[1] = _dot(h, wv_ref[...])

        planes = ((wcg_ref, bg_ref), (wcv_ref, bv_ref))
        for c in range(tn // LANES):
            cols = slice(c * LANES, (c + 1) * LANES)

            def conv(pln, u, u1, u2):
                wc_ref, b_ref = planes[pln]
                return b_ref[:, cols] + wc_ref[0:1, cols] * u2 + wc_ref[1:2, cols] * u1 + wc_ref[2:3, cols] * u

            def emit(rows, us):
                act_ref[rows, cols] = (_gelu(conv(0, *us[0])) * conv(1, *us[1])).astype(BF16)

            first = [_prev_rows(jnp.concatenate([tail_ref[pln, j, :, cols], up_ref[pln, 0:tr, cols]], axis=0))
                     for pln in range(2)]
            emit(slice(0, tr), first)

            for k in range(1, tm // tr):
                r0 = k * tr
                us = [_prev_rows(up_ref[pln, r0 - HALO:r0 + tr, cols]) for pln in range(2)]
                emit(slice(r0, r0 + tr), us)
        tail_ref[0, j] = up_ref[0, tm - HALO:tm, :]
        tail_ref[1, j] = up_ref[1, tm - HALO:tm, :]
        acc_ref[...] += _dot(act_ref[...], wd_ref[...])

        @pl.when(j == nj - 1)
        def _():
            for cp in fetches:
                cp.wait()
            g = g_ref[...]

            def chunk(k, carry):
                rows = pl.ds(pl.multiple_of(k * te, te), te)
                x3 = x2_buf[rows, :] + acc_ref[rows, :]
                r, n = _rms_stats(x3)
                err = n * g - t_buf[rows, :]
                loss_ref[...] += jnp.sum(err * err) * (0.5 / d)
                dx3, dgn = _rms_bwd(err * (1.0 / d), n, r, g)
                dgf_ref[...] += _colsum(dgn)
                dx3_ref[rows, :] = dx3
                dx3b_ref[rows, :] = dx3.astype(BF16)
                return carry

            lax.fori_loop(0, tm // te, chunk, 0)

    hbm = pl.BlockSpec(memory_space=pl.ANY)
    return _pcall(
        body, name="fwd_ffn", grid=(s_len // tm, nj),
        in_specs=[pl.BlockSpec((tm, d), lambda i, j: (i, 0)),
                  pl.BlockSpec((None, d, tn), lambda i, j: (j // nps, 0, j % nps)),
                  pl.BlockSpec((None, d, tn), lambda i, j: (nsh // 2 + j // nps, 0, j % nps)),
                  pl.BlockSpec((3, tn), lambda i, j: (0, j)), pl.BlockSpec((3, tn), lambda i, j: (0, nj + j)),
                  pl.BlockSpec((1, tn), lambda i, j: (0, j)), pl.BlockSpec((1, tn), lambda i, j: (0, nj + j)),
                  pl.BlockSpec((tn, d), lambda i, j: (j, 0)), pl.BlockSpec((1, d), lambda i, j: (0, 0)), hbm, hbm],
        out_specs=[pl.BlockSpec((2, tm, tn), lambda i, j: (0, i, j)), pl.BlockSpec((tm, tn), lambda i, j: (i, j)),
                   pl.BlockSpec((tm, d), lambda i, j: (i, 0)), pl.BlockSpec((tm, d), lambda i, j: (i, 0)),
                   pl.BlockSpec((8, 128), lambda i, j: (0, 0)), pl.BlockSpec((1, d), lambda i, j: (0, 0))],
        out_shape=[jax.ShapeDtypeStruct((2, s_len, f), F32), jax.ShapeDtypeStruct((s_len, f), BF16),
                   jax.ShapeDtypeStruct((s_len, d), F32), jax.ShapeDtypeStruct((s_len, d), BF16),
                   jax.ShapeDtypeStruct((8, 128), F32), jax.ShapeDtypeStruct((1, d), F32)],
        scratch_shapes=[pltpu.VMEM((2, nj, HALO, tn), F32), pltpu.VMEM((tm, d), F32), pltpu.VMEM((tm, d), F32),
                        pltpu.VMEM((tm, d), F32), pltpu.SemaphoreType.DMA((2,))],
        dims=("arbitrary", "arbitrary"))(h2, wup_g, wup_g, wfc, wfc, bfc, bfc, wdown_g, gf, x2, target)


def _bwd_ffn(dx3b, wdown_g, up, wfc, bfc, wup_g, dx3, x2, g2):
    s_len, d = dx3b.shape
    f = up.shape[2]
    nsh, _, ncol = wup_g.shape
    tm = min(TM_UP, s_len)
    te = min(T_EPI, tm)
    tn = TN_UP
    nj = f // tn
    nps = ncol // tn
    nt = s_len // tm
    hb = tm // HALO

    tr = T_CHUNK

    def body(dx_ref, w_ref, up_ref, halo_ref, wcg_ref, wcv_ref, bg_ref, bv_ref, wug_ref, wuv_ref, g_ref, r_hbm, x_hbm,
             dup_ref, dwf_ref, dx2_ref, dx2b_ref, dg_ref, nxt_ref, acc_ref, r_buf, x_buf, da_s, dc_s, sems):
        i, j = pl.program_id(0), pl.program_id(1)
        ti = nt - 1 - i
        row0 = pl.multiple_of(ti * tm, tm)
        fetches = (_row_fetch(r_hbm, r_buf, sems.at[0], row0, tm), _row_fetch(x_hbm, x_buf, sems.at[1], row0, tm))

        @pl.when(jnp.logical_and(i == 0, j == 0))
        def _():
            dwf_ref[...] = jnp.zeros_like(dwf_ref)
            nxt_ref[...] = jnp.zeros_like(nxt_ref)
            dg_ref[...] = jnp.zeros_like(dg_ref)

        @pl.when(j == 0)
        def _():
            acc_ref[...] = jnp.zeros_like(acc_ref)
            for cp in fetches:
                cp.start()

        da_s[...] = _dot_nt(dx_ref[...], w_ref[...])
        planes = ((wcg_ref, bg_ref), (wcv_ref, bv_ref))
        for pln in range(2):
            dc_s[pln, tm:tm + HALO, :] = nxt_ref[pln, j]

        for c in range(tn // LANES):
            cols = slice(c * LANES, (c + 1) * LANES)

            def conv(pln, u, u1, u2):
                wc_ref, b_ref = planes[pln]
                return b_ref[:, cols] + wc_ref[0:1, cols] * u2 + wc_ref[1:2, cols] * u1 + wc_ref[2:3, cols] * u

            def fold(v):
                out = v[0:8, :]
                for q in range(1, tr // 8):
                    out = out + v[8 * q:8 * q + 8, :]
                return out

            def pass1(rows, us, sums):
                dact = da_s[rows, cols]
                ge, gd = _gelu_and_grad(conv(0, *us[0]))
                dcs = (dact * conv(1, *us[1]) * gd, dact * ge)
                new = []
                for pln in range(2):
                    u, u1, u2 = us[pln]
                    dc_s[pln, rows, cols] = dcs[pln]
                    new += [fold(dcs[pln] * u2), fold(dcs[pln] * u1), fold(dcs[pln] * u), fold(dcs[pln])]
                return tuple(a + b for a, b in zip(sums, new))

            first = [_prev_rows(jnp.concatenate([jnp.where(ti == 0, 0.0, halo_ref[pln, :, cols]),
                                                 up_ref[pln, 0:tr, cols]], axis=0)) for pln in range(2)]
            sums = pass1(slice(0, tr), first, (jnp.zeros((8, LANES), F32),) * 8)

            for k in range(1, tm // tr):
                r0 = k * tr
                us = [_prev_rows(up_ref[pln, r0 - HALO:r0 + tr, cols]) for pln in range(2)]
                sums = pass1(slice(r0, r0 + tr), us, sums)
            for pln in range(2):
                for term in range(4):
                    dwf_ref[pln, j, term:term + 1, cols] += _colsum(sums[4 * pln + term])

        for pln in range(2):
            nxt_ref[pln, j] = dc_s[pln, 0:HALO, :]
        for c in range(tn // LANES):
            cols = slice(c * LANES, (c + 1) * LANES)

            for k in range(tm // tr):
                r0 = k * tr
                for pln in range(2):
                    wc_ref = planes[pln][0]
                    dc, dc1, dc2 = _next_rows(dc_s[pln, r0:r0 + tr + HALO, cols])
                    dup = wc_ref[2:3, cols] * dc + wc_ref[1:2, cols] * dc1 + wc_ref[0:1, cols] * dc2
                    dup_ref[pln, r0:r0 + tr, cols] = dup.astype(BF16)
        acc_ref[...] += _dot_nt(dup_ref[0], wug_ref[...]) + _dot_nt(dup_ref[1], wuv_ref[...])

        @pl.when(j == nj - 1)
        def _():
            for cp in fetches:
                cp.wait()
            g = g_ref[...]

            def chunk(k, carry):
                rows = pl.ds(pl.multiple_of(k * te, te), te)
                r, n = _rms_stats(x_buf[rows, :])
                dxn, dgn = _rms_bwd(acc_ref[rows, :], n, r, g)
                dg_ref[...] += _colsum(dgn)
                dx = r_buf[rows, :] + dxn
                dx2_ref[rows, :] = dx
                dx2b_ref[rows, :] = dx.astype(BF16)
                return carry

            lax.fori_loop(0, tm // te, chunk, 0)

    hbm = pl.BlockSpec(memory_space=pl.ANY)
    rev = lambda i: nt - 1 - i
    return _pcall(
        body, name="bwd_ffn", grid=(nt, nj),
        in_specs=[pl.BlockSpec((tm, d), lambda i, j: (rev(i), 0)),
                  pl.BlockSpec((tn, d), lambda i, j: (j, 0)),
                  pl.BlockSpec((2, tm, tn), lambda i, j: (0, rev(i), j)),
                  pl.BlockSpec((2, HALO, tn), lambda i, j: (0, jnp.maximum(rev(i) * hb - 1, 0), j)),
                  pl.BlockSpec((3, tn), lambda i, j: (0, j)), pl.BlockSpec((3, tn), lambda i, j: (0, nj + j)),
                  pl.BlockSpec((1, tn), lambda i, j: (0, j)), pl.BlockSpec((1, tn), lambda i, j: (0, nj + j)),
                  pl.BlockSpec((None, d, tn), lambda i, j: (j // nps, 0, j % nps)),
                  pl.BlockSpec((None, d, tn), lambda i, j: (nsh // 2 + j // nps, 0, j % nps)),
                  pl.BlockSpec((1, d), lambda i, j: (0, 0)), hbm, hbm],
        out_specs=[pl.BlockSpec((2, tm, tn), lambda i, j: (0, rev(i), j)),
                   pl.BlockSpec((2, nj, 8, tn), lambda i, j: (0, 0, 0, 0)),
                   pl.BlockSpec((tm, d), lambda i, j: (rev(i), 0)), pl.BlockSpec((tm, d), lambda i, j: (rev(i), 0)),
                   pl.BlockSpec((1, d), lambda i, j: (0, 0))],
        out_shape=[jax.ShapeDtypeStruct((2, s_len, f), BF16), jax.ShapeDtypeStruct((2, nj, 8, tn), F32),
                   jax.ShapeDtypeStruct((s_len, d), F32), jax.ShapeDtypeStruct((s_len, d), BF16),
                   jax.ShapeDtypeStruct((1, d), F32)],
        scratch_shapes=[pltpu.VMEM((2, nj, HALO, tn), F32), pltpu.VMEM((tm, d), F32), pltpu.VMEM((tm, d), F32),
                        pltpu.VMEM((tm, d), F32), pltpu.VMEM((tm, tn), F32), pltpu.VMEM((2, tm + HALO, tn), F32),
                        pltpu.SemaphoreType.DMA((2,))],
        dims=("arbitrary", "arbitrary"))(dx3b, wdown_g, up, up, wfc, wfc, bfc, bfc, wup_g, wup_g, g2, dx3, x2)


def _bwd_in(dp, win_g, resid, x_in, g):
    s_len, d = x_in.shape
    nsh, _, ncol = win_g.shape
    tm = min(TM_BW, s_len)

    def body(dz_ref, w_ref, r_ref, x_ref, g_ref, dx_ref, dg_ref, acc_ref):
        i, k = pl.program_id(0), pl.program_id(1)

        @pl.when(jnp.logical_and(i == 0, k == 0))
        def _():
            dg_ref[...] = jnp.zeros_like(dg_ref)

        @pl.when(k == 0)
        def _():
            acc_ref[...] = jnp.zeros_like(acc_ref)

        acc_ref[...] += _dot_nt(dz_ref[...], w_ref[...])

        @pl.when(k == nsh - 1)
        def _():
            r, n = _rms_stats(x_ref[...])
            dxn, dgn = _rms_bwd(acc_ref[...], n, r, g_ref[...])
            dg_ref[...] += _colsum(dgn)
            dx_ref[...] = r_ref[...] + dxn

    return _pcall(
        body, name="bwd_in", grid=(s_len // tm, nsh),
        in_specs=[pl.BlockSpec((tm, ncol), lambda i, k: (i, k)),
                  pl.BlockSpec((None, d, ncol), lambda i, k: (k, 0, 0)),
                  pl.BlockSpec((tm, d), lambda i, k: (i, 0)), pl.BlockSpec((tm, d), lambda i, k: (i, 0)),
                  pl.BlockSpec((1, d), lambda i, k: (0, 0))],
        out_specs=[pl.BlockSpec((tm, d), lambda i, k: (i, 0)), pl.BlockSpec((1, d), lambda i, k: (0, 0))],
        out_shape=[jax.ShapeDtypeStruct((s_len, d), F32), jax.ShapeDtypeStruct((1, d), F32)],
        scratch_shapes=[pltpu.VMEM((tm, d), F32)],
        dims=("arbitrary", "arbitrary"))(dp, win_g, resid, x_in, g)


def _bwd_dy(dx2b, wout_g):
    s_len, d = dx2b.shape
    tm = min(TM_OUT, s_len)

    def body(dx_ref, w_ref, dy_ref):
        dy_ref[...] = _dot_nt(dx_ref[...], w_ref[...])

    return _pcall(
        body, name="bwd_dy", grid=(s_len // tm,),
        in_specs=[pl.BlockSpec((tm, d), lambda i: (i, 0)), pl.BlockSpec((d, d), lambda i: (0, 0))],
        out_specs=pl.BlockSpec((tm, d), lambda i: (i, 0)),
        out_shape=jax.ShapeDtypeStruct((s_len, d), F32), dims=("arbitrary",))(dx2b, wout_g)


def _bwd_mixers(p, dy, hs, gv, bv, wt, wtt, bst, wc, bc, wa, wat, ba, wx, wxt, bx, lam, ggm, glru):
    s_len = p.shape[0]
    ts = min(TS_MIX, s_len)
    nt = s_len // ts
    hb = ts // HALO

    def body(pz_ref, pgl_ref, pxl_ref, xh_ref, dy_ref, hs_ref, hh_ref, gv_ref, bv_ref, wt_ref, wtt_ref, bst_ref,
             wc_ref, bc_ref, wa_ref, wat_ref, ba_ref, wx_ref, wxt_ref, bx_ref, lam_ref, ggm_ref, glru_ref,
             dp_ref, dgv_ref, dbv_ref, dwt_ref, dbst_ref, dwc_ref, dbc_ref, dwa_ref, dba_ref, dwx_ref, dbx_ref,
             dsp_ref, dggm_ref, dglru_ref,
             carry_ref, nxt_ref, a_s, g_s, vl_s, mix_s, z_s, dm_s, dvl_s, dz_s, dxr_s):
        i = pl.program_id(0)
        ti = nt - 1 - i

        @pl.when(i == 0)
        def _():
            for ref in (dgv_ref, dbv_ref, dwt_ref, dbst_ref, dwc_ref, dbc_ref, dwa_ref, dba_ref, dwx_ref, dbx_ref,
                        dsp_ref, dggm_ref, dglru_ref, carry_ref, nxt_ref):
                ref[...] = jnp.zeros_like(ref)

        z = pz_ref[...]
        u, mixed, vh, rs, ygm = _gm_forward(z, gv_ref[...], bv_ref[...], wt_ref, bst_ref, vl_s, mix_s)
        rg, ngm = _rms_stats(ygm)
        dygm, dgn = _rms_bwd(dy_ref[:, :GM_W], ngm, rg, ggm_ref[...])
        dggm_ref[...] += _colsum(dgn)
        du = dygm * mixed
        dmix = dygm * u
        dm_s[...] = dmix.astype(BF16)
        bsum = dmix[0:CHUNK, :]
        for cc in range(1, ts // CHUNK):
            bsum = bsum + dmix[cc * CHUNK:(cc + 1) * CHUNK, :]
        dbst_ref[...] += bsum
        for hh in range(HEADS):
            cols = slice(hh * HEAD_DIM, (hh + 1) * HEAD_DIM)
            dw = jnp.zeros((CHUNK, CHUNK), F32)
            for cc in range(ts // CHUNK):
                rows = slice(cc * CHUNK, (cc + 1) * CHUNK)
                dmb = dm_s[rows, cols]
                dw = dw + _dot_nt(dmb, vl_s[rows, cols])
                dvl_s[rows, cols] = _dot(wtt_ref[hh], dmb)
            dwt_ref[hh] += dw
        dvl = dvl_s[...]
        dgv_ref[...] += _colsum(dvl * vh)
        dbv_ref[...] += _colsum(dvl)
        dvh = dvl * gv_ref[...]
        dv = rs * (dvh - _rowmean(dvh) - vh * _rowmean(dvh * vh))
        _, gd = _gelu_and_grad(z)
        dp_ref[:, :GM_W] = (du * gd[:, :GM_W]).astype(BF16)
        dp_ref[:, GM_W:2 * GM_W] = (dv * gd[:, GM_W:]).astype(BF16)

        xl = pxl_ref[...]
        xhalo = jnp.where(ti == 0, 0.0, xh_ref[...])
        gts = _lru_gates(xl, xhalo, wc_ref, bc_ref, wa_ref, ba_ref, wx_ref, bx_ref, lam_ref, z_s)
        a, mult, ra, ri, xr, sp = gts["a"], gts["mult"], gts["ra"], gts["ri"], gts["xr"], gts["sp"]
        hs = hs_ref[...]
        hprev = _shift_prev(hs, jnp.where(ti == 0, 0.0, hh_ref[...]), 1)
        gl = pgl_ref[...]
        ggl, dggl = _gelu_and_grad(gl)
        yl = hs * ggl
        rl, nl = _rms_stats(yl)
        dyl, dgn = _rms_bwd(dy_ref[:, GM_W:], nl, rl, glru_ref[...])
        dglru_ref[...] += _colsum(dgn)
        dp_ref[:, 2 * GM_W:2 * GM_W + LRU_W] = (dyl * hs * dggl).astype(BF16)
        a_s[...] = a
        g_s[...] = dyl * ggl

        def step(k, carry):
            t = ts - 1 - k
            gt = g_s[pl.ds(t, 1), :] + carry
            g_s[pl.ds(t, 1), :] = gt
            return a_s[pl.ds(t, 1), :] * gt

        carry_ref[...] = lax.fori_loop(0, ts, step, carry_ref[...], unroll=8)
        gsc = g_s[...]
        da = gsc * hprev
        rix = ri * xr
        dmult = gsc * rix
        dri = gsc * mult * xr
        dxr = gsc * mult * ri
        dla = da * a - dmult * (a * a) / mult
        dsp_ref[...] += _colsum(dla * ra) * (-LRU_C)
        dza = (dla * sp) * (-LRU_C) * ra * (1.0 - ra)
        dzi = dri * ri * (1.0 - ri)
        dba_ref[...] += _colsum(dza)
        dbx_ref[...] += _colsum(dzi)
        dzab = dza.astype(BF16)
        dzib = dzi.astype(BF16)
        xrb = gts["xrb"]
        for hh in range(HEADS):
            cols = slice(hh * HEAD_DIM, (hh + 1) * HEAD_DIM)
            dwa_ref[hh] += _dot_tn(xrb[:, cols], dzab[:, cols])
            dwx_ref[hh] += _dot_tn(xrb[:, cols], dzib[:, cols])
            dxr_s[:, cols] = _dot(dzab[:, cols], wat_ref[hh]) + _dot(dzib[:, cols], wxt_ref[hh])
        dxr = dxr + dxr_s[...]
        dbc_ref[...] += _colsum(dxr)
        dwc_ref[0:1, :] += _colsum(dxr * gts["x3"])
        dwc_ref[1:2, :] += _colsum(dxr * gts["x2"])
        dwc_ref[2:3, :] += _colsum(dxr * gts["x1"])
        dwc_ref[3:4, :] += _colsum(dxr * xl)
        nxt = nxt_ref[...]
        nxt_ref[...] = dxr[:HALO, :]
        dxl = wc_ref[3:4, :] * dxr + wc_ref[2:3, :] * _shift_next(dxr, nxt, 1) \
            + wc_ref[1:2, :] * _shift_next(dxr, nxt, 2) + wc_ref[0:1, :] * _shift_next(dxr, nxt, 3)
        dp_ref[:, 2 * GM_W + LRU_W:] = dxl.astype(BF16)

    full = lambda shape: pl.BlockSpec(shape, lambda i: (0,) * len(shape))
    rev = lambda i: nt - 1 - i
    prev_blk = lambda i: jnp.maximum((nt - 1 - i) * hb - 1, 0)
    hhd = (HEADS, HEAD_DIM, HEAD_DIM)
    small_shapes = [(1, GM_W), (1, GM_W), (HEADS, CHUNK, CHUNK), (CHUNK, GM_W), (4, LRU_W), (1, LRU_W), hhd,
                    (1, LRU_W), hhd, (1, LRU_W), (1, LRU_W), (1, GM_W), (1, LRU_W)]
    return _pcall(
        body, name="bwd_mixers", grid=(nt,),
        in_specs=[pl.BlockSpec((ts, 2 * GM_W), lambda i: (rev(i), 0)),
                  pl.BlockSpec((ts, LRU_W), lambda i: (rev(i), 2)),
                  pl.BlockSpec((ts, LRU_W), lambda i: (rev(i), 3)),
                  pl.BlockSpec((HALO, LRU_W), lambda i: (prev_blk(i), 3)),
                  pl.BlockSpec((ts, GM_W + LRU_W), lambda i: (rev(i), 0)),
                  pl.BlockSpec((ts, LRU_W), lambda i: (rev(i), 0)),
                  pl.BlockSpec((HALO, LRU_W), lambda i: (prev_blk(i), 0)),
                  full((1, GM_W)), full((1, GM_W)), full((HEADS, CHUNK, CHUNK)), full((HEADS, CHUNK, CHUNK)),
                  full((CHUNK, GM_W)), full((4, LRU_W)), full((1, LRU_W)), full(hhd), full(hhd), full((1, LRU_W)),
                  full(hhd), full(hhd), full((1, LRU_W)), full((1, LRU_W)), full((1, GM_W)), full((1, LRU_W))],
        out_specs=[pl.BlockSpec((ts, 2 * GM_W + 2 * LRU_W), lambda i: (rev(i), 0))] + [full(s) for s in small_shapes],
        out_shape=[jax.ShapeDtypeStruct((s_len, 2 * GM_W + 2 * LRU_W), BF16)]
        + [jax.ShapeDtypeStruct(s, F32) for s in small_shapes],
        scratch_shapes=[pltpu.VMEM((1, LRU_W), F32), pltpu.VMEM((HALO, LRU_W), F32),
                        pltpu.VMEM((ts, LRU_W), F32), pltpu.VMEM((ts, LRU_W), F32),
                        pltpu.VMEM((ts, GM_W), BF16), pltpu.VMEM((ts, GM_W), F32), pltpu.VMEM((ts, 2 * LRU_W), F32),
                        pltpu.VMEM((ts, GM_W), BF16), pltpu.VMEM((ts, GM_W), F32), pltpu.VMEM((ts, 2 * LRU_W), F32),
                        pltpu.VMEM((ts, LRU_W), F32)],
        dims=("arbitrary",))(p, p, p, p, dy, hs, hs, gv, bv, wt, wtt, bst, wc, bc, wa, wat, ba, wx, wxt, bx, lam,
                             ggm, glru)


def _bwd_weight(name, a, b, *, a_planes, b_planes, shard_rows):
    _, s_len, ma = a.shape
    _, _, nb = b.shape
    m, n = a_planes * ma, b_planes * nb
    tk = min(TW_K, s_len)
    if shard_rows:
        rows, cols = m // N_CHIPS, n // 2
        tm, tn = _tile(rows, TW_M), _tile(cols, TW_N)
        out_idx = lambda i, j, k: (j * tn // cols, i * tm // rows, (i * tm % rows) // tm, (j * tn % cols) // tn)
    else:
        rows, cols = m // 2, n // N_CHIPS
        tm, tn = _tile(rows, TW_M), _tile(cols, TW_N)
        out_idx = lambda i, j, k: (i * tm // rows, j * tn // cols, (i * tm % rows) // tm, (j * tn % cols) // tn)
    nk = s_len // tk
    npa, npb = ma // tm, nb // tn

    def body(a_ref, b_ref, o_ref, ob_ref, acc_ref):
        k = pl.program_id(2)

        @pl.when(k == 0)
        def _():
            acc_ref[...] = jnp.zeros_like(acc_ref)

        acc_ref[...] += _dot_tn(a_ref[...], b_ref[...])

        @pl.when(k == nk - 1)
        def _():
            o_ref[...] = acc_ref[...]
            ob_ref[...] = acc_ref[...].astype(BF16)

    shape = (2, N_CHIPS, rows, cols)
    return _pcall(
        body, name=name, grid=(m // tm, n // tn, nk),
        in_specs=[pl.BlockSpec((None, tk, tm), lambda i, j, k: (i // npa, k, i % npa)),
                  pl.BlockSpec((None, tk, tn), lambda i, j, k: (j // npb, k, j % npb))],
        out_specs=[pl.BlockSpec((None, None, tm, tn), out_idx), pl.BlockSpec((None, None, tm, tn), out_idx)],
        out_shape=[jax.ShapeDtypeStruct(shape, F32), jax.ShapeDtypeStruct(shape, BF16)],
        scratch_shapes=[pltpu.VMEM((tm, tn), F32)],
        dims=("arbitrary", "arbitrary", "arbitrary"))(a, b)


def _mesh_pos():
    return lax.axis_index("x"), lax.axis_index("y"), lax.axis_index("c")


def _other_chips(x, y):
    return [(1 - x, y), (x, 1 - y), (1 - x, 1 - y)]


def _to_slot(name, a, cs, dtype):
    _, a_rows, b_cols = a.shape
    ta = min(T_ELEM, a_rows)

    def body(cs_ref, a_ref, o_ref):
        o_ref[...] = a_ref[...].astype(dtype)

    grid_spec = pltpu.PrefetchScalarGridSpec(
        num_scalar_prefetch=1, grid=(2, a_rows // ta),
        in_specs=[pl.BlockSpec((None, ta, b_cols), lambda h, r, cs_ref: (h, r, 0))],
        out_specs=pl.BlockSpec((None, None, ta, b_cols), lambda h, r, cs_ref: (cs_ref[1], h, r, 0)))
    return _pcall(body, name=name, grid_spec=grid_spec,
                  out_shape=jax.ShapeDtypeStruct((N_CHIPS,) + a.shape, dtype), dims=("arbitrary", "arbitrary"))(cs, a)


def _all_gather(bufs):
    n = len(bufs)

    def body(*refs):
        outs = refs[n:2 * n]
        send_sems, recv_sems = refs[2 * n:]
        x, y, c = _mesh_pos()
        s = 2 * x + y
        me, sib = (x, y, c), (x, y, 1 - c)
        chips = _other_chips(x, y)

        def rcopy(a, k, blk, to):
            return pltpu.make_async_remote_copy(src_ref=blk, dst_ref=blk, send_sem=send_sems.at[a * 6 + k],
                                                recv_sem=recv_sems.at[a * 6 + k], device_id=to, device_id_type=MESH)

        first = [rcopy(a, j, outs[a].at[s, c], (cx, cy, c)) for a in range(n) for j, (cx, cy) in enumerate(chips)]
        for cp in first:
            cp.start()
        passed = []
        for a in range(n):
            for j, (cx, cy) in enumerate(chips):
                blk = outs[a].at[2 * cx + cy, c]
                rcopy(a, j, blk, me).wait_recv()
                cp = rcopy(a, 3 + j, blk, sib)
                cp.start()
                passed.append(cp)
        for a in range(n):
            for j, (cx, cy) in enumerate(chips):
                rcopy(a, 3 + j, outs[a].at[2 * cx + cy, 1 - c], me).wait_recv()
        for cp in first + passed:
            cp.wait_send()

    return _pcall(
        body, name="all_gather_weights",
        in_specs=[ANY] * n, out_specs=[ANY] * n,
        out_shape=[jax.ShapeDtypeStruct(a.shape, a.dtype) for a in bufs],
        scratch_shapes=[pltpu.SemaphoreType.DMA((6 * n,)), pltpu.SemaphoreType.DMA((6 * n,))],
        aliases={a: a for a in range(n)})(*bufs)


def _sibling_swap(arrs):
    n = len(arrs)

    def body(*refs):
        ins, outs = refs[:n], refs[n:2 * n]
        send_sems, recv_sems = refs[2 * n:]
        x, y, c = _mesh_pos()
        cps = [pltpu.make_async_remote_copy(src_ref=ins[a].at[1 - c], dst_ref=outs[a], send_sem=send_sems.at[a],
                                            recv_sem=recv_sems.at[a], device_id=(x, y, 1 - c), device_id_type=MESH)
               for a in range(n)]
        for cp in cps:
            cp.start()
        for cp in cps:
            cp.wait()

    return _pcall(
        body, name="pair_swap", in_specs=[ANY] * n, out_specs=[ANY] * n,
        out_shape=[jax.ShapeDtypeStruct(a.shape[1:], a.dtype) for a in arrs],
        scratch_shapes=[pltpu.SemaphoreType.DMA((n,)), pltpu.SemaphoreType.DMA((n,))])(*arrs)


def _chip_exchange(big, small):
    n = len(big)

    def body(*refs):
        ins, outs = refs[:n + 1], refs[n + 1:2 * n + 2]
        send_sems, recv_sems = refs[2 * n + 2:]
        x, y, c = _mesh_pos()
        cps = []
        for a in range(n + 1):
            for j, (cx, cy) in enumerate(_other_chips(x, y)):
                src = ins[a].at[2 * cx + cy] if a < n else ins[a]
                cps.append(pltpu.make_async_remote_copy(src_ref=src, dst_ref=outs[a].at[j], send_sem=send_sems.at[3 * a + j],
                                                        recv_sem=recv_sems.at[3 * a + j], device_id=(cx, cy, c),
                                                        device_id_type=MESH))
        for cp in cps:
            cp.start()
        for cp in cps:
            cp.wait()

    arrs = list(big) + [small]
    shapes = [a.shape[1:] for a in big] + [small.shape]
    return _pcall(
        body, name="chip_exchange", in_specs=[ANY] * (n + 1), out_specs=[ANY] * (n + 1),
        out_shape=[jax.ShapeDtypeStruct((3,) + sh, a.dtype) for sh, a in zip(shapes, arrs)],
        scratch_shapes=[pltpu.SemaphoreType.DMA((3 * (n + 1),)), pltpu.SemaphoreType.DMA((3 * (n + 1),))])(*arrs)


def _halves_swap(bufs):
    n = len(bufs)

    def body(*refs):
        outs = refs[n:2 * n]
        send_sems, recv_sems = refs[2 * n:]
        x, y, c = _mesh_pos()
        cps = [pltpu.make_async_remote_copy(src_ref=outs[a].at[c], dst_ref=outs[a].at[c], send_sem=send_sems.at[a],
                                            recv_sem=recv_sems.at[a], device_id=(x, y, 1 - c), device_id_type=MESH)
               for a in range(n)]
        for cp in cps:
            cp.start()
        for cp in cps:
            cp.wait()

    return _pcall(
        body, name="halves_swap", in_specs=[ANY] * n, out_specs=[ANY] * n,
        out_shape=[jax.ShapeDtypeStruct(a.shape, a.dtype) for a in bufs],
        scratch_shapes=[pltpu.SemaphoreType.DMA((n,)), pltpu.SemaphoreType.DMA((n,))],
        aliases={a: a for a in range(n)})(*bufs)


def _pair_sum(name, g32, recv, cs):
    _, nch, a_rows, b_cols = g32.shape
    ta = min(T_ELEM, a_rows)

    def body(cs_ref, g_ref, r_ref, pb_ref, own_ref):
        k = pl.program_id(1)
        v = g_ref[...] + r_ref[...].astype(F32)
        pb_ref[...] = v.astype(BF16)

        @pl.when(k == cs_ref[1])
        def _():
            own_ref[...] = v

    grid_spec = pltpu.PrefetchScalarGridSpec(
        num_scalar_prefetch=1, grid=(a_rows // ta, nch),
        in_specs=[pl.BlockSpec((None, None, ta, b_cols), lambda r, k, cs_ref: (cs_ref[0], k, r, 0)),
                  pl.BlockSpec((None, ta, b_cols), lambda r, k, cs_ref: (k, r, 0))],
        out_specs=[pl.BlockSpec((None, ta, b_cols), lambda r, k, cs_ref: (k, r, 0)),
                   pl.BlockSpec((ta, b_cols), lambda r, k, cs_ref: (r, 0))])
    return _pcall(
        body, name=name, grid_spec=grid_spec,
        out_shape=[jax.ShapeDtypeStruct((nch, a_rows, b_cols), BF16), jax.ShapeDtypeStruct((a_rows, b_cols), F32)],
        dims=("arbitrary", "arbitrary"))(cs, g32, recv)


def _small_pair_sum(mine, recv, cs):
    _, r, ccols = mine.shape

    def body(cs_ref, a_ref, b_ref, o_ref):
        o_ref[...] = a_ref[...] + b_ref[...]

    grid_spec = pltpu.PrefetchScalarGridSpec(
        num_scalar_prefetch=1, grid=(1,),
        in_specs=[pl.BlockSpec((None, r, ccols), lambda i, cs_ref: (cs_ref[0], 0, 0)),
                  pl.BlockSpec((r, ccols), lambda i, cs_ref: (0, 0))],
        out_specs=pl.BlockSpec((r, ccols), lambda i, cs_ref: (0, 0)))
    return _pcall(body, name="small_pair_sum", grid_spec=grid_spec,
                  out_shape=jax.ShapeDtypeStruct((r, ccols), F32), dims=("arbitrary",))(cs, mine, recv)


def _chip_sum(name, own, recv, cs):
    a_rows, b_cols = own.shape
    ta = min(T_ELEM, a_rows)

    def body(cs_ref, o_ref, r_ref, f_ref):
        f_ref[...] = ((o_ref[...] + r_ref[0].astype(F32)) + r_ref[1].astype(F32)) + r_ref[2].astype(F32)

    grid_spec = pltpu.PrefetchScalarGridSpec(
        num_scalar_prefetch=1, grid=(a_rows // ta,),
        in_specs=[pl.BlockSpec((ta, b_cols), lambda r, cs_ref: (r, 0)),
                  pl.BlockSpec((3, ta, b_cols), lambda r, cs_ref: (0, r, 0))],
        out_specs=pl.BlockSpec((None, ta, b_cols), lambda r, cs_ref: (cs_ref[0], r, 0)))
    return _pcall(body, name=name, grid_spec=grid_spec,
                  out_shape=jax.ShapeDtypeStruct((2, a_rows, b_cols), F32), dims=("arbitrary",))(cs, own, recv)


def _small_chip_sum(pair, recv, cs):
    r, ccols = pair.shape

    def body(cs_ref, p_ref, r_ref, o_ref):
        s = cs_ref[1]
        own = p_ref[...]
        total = None
        for k in range(N_CHIPS):
            flip = jnp.bitwise_xor(s, k)
            term = jnp.where(flip == 0, own, jnp.where(flip == 2, r_ref[0], jnp.where(flip == 1, r_ref[1], r_ref[2])))
            total = term if total is None else total + term
        o_ref[...] = total

    grid_spec = pltpu.PrefetchScalarGridSpec(
        num_scalar_prefetch=1, grid=(1,),
        in_specs=[pl.BlockSpec((r, ccols), lambda i, cs_ref: (0, 0)),
                  pl.BlockSpec((3, r, ccols), lambda i, cs_ref: (0, 0, 0))],
        out_specs=pl.BlockSpec((None, r, ccols), lambda i, cs_ref: (cs_ref[0], 0, 0)))
    return _pcall(body, name="small_chip_sum", grid_spec=grid_spec,
                  out_shape=jax.ShapeDtypeStruct((2, r, ccols), F32), dims=("arbitrary",))(cs, pair, recv)


def _adamw(name, w, g, m, v, *, halves):
    rows, cols = w.shape
    tr, tc = min(T_ELEM, rows), PACK_COLS
    c1 = 1.0 - ADAM_B1 ** ADAM_STEP
    c2 = 1.0 - ADAM_B2 ** ADAM_STEP

    def body(w_ref, g_ref, m_ref, v_ref, d_ref, mo_ref, vo_ref, go_ref):
        g_ = g_ref[...]
        m_ = ADAM_B1 * m_ref[...] + (1.0 - ADAM_B1) * g_
        v_ = ADAM_B2 * v_ref[...] + (1.0 - ADAM_B2) * (g_ * g_)
        mo_ref[...] = m_
        vo_ref[...] = v_
        go_ref[...] = g_
        d_ref[...] = (-ADAM_LR) * ((m_ / c1) / (jnp.sqrt(v_ / c2) + ADAM_EPS) + ADAM_WD * w_ref[...])

    spec = pl.BlockSpec((tr, tc), lambda r, j: (r, j))
    if halves == "rows":
        nrh = rows // 2 // tr
        g_spec = pl.BlockSpec((None, tr, tc), lambda r, j: (r // nrh, r % nrh, j))
    elif halves == "cols":
        nch = cols // 2 // tc
        g_spec = pl.BlockSpec((None, tr, tc), lambda r, j: (j // nch, r, j % nch))
    else:
        g_spec = spec
    return _pcall(body, name=name, grid=(rows // tr, cols // tc), in_specs=[spec, g_spec, spec, spec],
                  out_specs=[spec] * 4, out_shape=[jax.ShapeDtypeStruct((rows, cols), F32)] * 4,
                  dims=("arbitrary", "arbitrary"))(w, g, m, v)


def _pack(parts, rows):
    flat = jnp.concatenate([a.reshape(-1) for a in parts])
    return jnp.pad(flat, (0, rows * PACK_COLS - flat.shape[0])).reshape(rows, PACK_COLS)


def _unpack(buf, shapes):
    flat = buf.reshape(-1)
    out, off = [], 0
    for sh in shapes:
        size = math.prod(sh)
        out.append(flat[off:off + size].reshape(sh))
        off += size
    return out


def _pack_rows(shapes, multiple):
    total = sum(math.prod(sh) for sh in shapes)
    rows = -(-total // PACK_COLS)
    return -(-rows // multiple) * multiple


SMALL = ["norm1_g", "gm_v_g", "gm_v_b", "gm_ws", "gm_bs", "lru_conv_w", "lru_conv_b", "lru_wa", "lru_ba", "lru_wx",
         "lru_bx", "lru_lambda", "gm_out_g", "lru_out_g", "norm2_g", "ffn_conv_w", "ffn_conv_b", "final_g"]
BIG = ["w_in", "w_out", "ffn_w_up", "ffn_w_down"]
ORDER = ["norm1_g", "w_in", "gm_v_g", "gm_v_b", "gm_ws", "gm_bs", "lru_conv_w", "lru_conv_b", "lru_wa", "lru_ba",
         "lru_wx", "lru_bx", "lru_lambda", "gm_out_g", "lru_out_g", "w_out", "norm2_g", "ffn_w_up", "ffn_conv_w",
         "ffn_conv_b", "ffn_w_down", "final_g"]


def kernel(x, norm1_g, w_in, gm_v_g, gm_v_b, gm_ws, gm_bs, lru_conv_w, lru_conv_b, lru_wa, lru_ba, lru_wx, lru_bx, lru_lambda, gm_out_g, lru_out_g, w_out, norm2_g, ffn_w_up, ffn_conv_w, ffn_conv_b, ffn_w_down, final_g, loss_target, m_norm1_g, m_w_in, m_gm_v_g, m_gm_v_b, m_gm_ws, m_gm_bs, m_lru_conv_w, m_lru_conv_b, m_lru_wa, m_lru_ba, m_lru_wx, m_lru_bx, m_lru_lambda, m_gm_out_g, m_lru_out_g, m_w_out, m_norm2_g, m_ffn_w_up, m_ffn_conv_w, m_ffn_conv_b, m_ffn_w_down, m_final_g, v_norm1_g, v_w_in, v_gm_v_g, v_gm_v_b, v_gm_ws, v_gm_bs, v_lru_conv_w, v_lru_conv_b, v_lru_wa, v_lru_ba, v_lru_wx, v_lru_bx, v_lru_lambda, v_gm_out_g, v_lru_out_g, v_w_out, v_norm2_g, v_ffn_w_up, v_ffn_conv_w, v_ffn_conv_b, v_ffn_w_down, v_final_g):
    w = dict(norm1_g=norm1_g, w_in=w_in, gm_v_g=gm_v_g, gm_v_b=gm_v_b, gm_ws=gm_ws, gm_bs=gm_bs, lru_conv_w=lru_conv_w, lru_conv_b=lru_conv_b, lru_wa=lru_wa, lru_ba=lru_ba, lru_wx=lru_wx, lru_bx=lru_bx, lru_lambda=lru_lambda, gm_out_g=gm_out_g, lru_out_g=lru_out_g, w_out=w_out, norm2_g=norm2_g, ffn_w_up=ffn_w_up, ffn_conv_w=ffn_conv_w, ffn_conv_b=ffn_conv_b, ffn_w_down=ffn_w_down, final_g=final_g)
    m = dict(norm1_g=m_norm1_g, w_in=m_w_in, gm_v_g=m_gm_v_g, gm_v_b=m_gm_v_b, gm_ws=m_gm_ws, gm_bs=m_gm_bs, lru_conv_w=m_lru_conv_w, lru_conv_b=m_lru_conv_b, lru_wa=m_lru_wa, lru_ba=m_lru_ba, lru_wx=m_lru_wx, lru_bx=m_lru_bx, lru_lambda=m_lru_lambda, gm_out_g=m_gm_out_g, lru_out_g=m_lru_out_g, w_out=m_w_out, norm2_g=m_norm2_g, ffn_w_up=m_ffn_w_up, ffn_conv_w=m_ffn_conv_w, ffn_conv_b=m_ffn_conv_b, ffn_w_down=m_ffn_w_down, final_g=m_final_g)
    v = dict(norm1_g=v_norm1_g, w_in=v_w_in, gm_v_g=v_gm_v_g, gm_v_b=v_gm_v_b, gm_ws=v_gm_ws, gm_bs=v_gm_bs, lru_conv_w=v_lru_conv_w, lru_conv_b=v_lru_conv_b, lru_wa=v_lru_wa, lru_ba=v_lru_ba, lru_wx=v_lru_wx, lru_bx=v_lru_bx, lru_lambda=v_lru_lambda, gm_out_g=v_gm_out_g, lru_out_g=v_lru_out_g, w_out=v_w_out, norm2_g=v_norm2_g, ffn_w_up=v_ffn_w_up, ffn_conv_w=v_ffn_conv_w, ffn_conv_b=v_ffn_conv_b, ffn_w_down=v_ffn_w_down, final_g=v_final_g)

    mx, my, mc = _mesh_pos()
    shard = 2 * mx + my
    cs = jnp.stack([mc, shard]).astype(jnp.int32)

    xs = x[0]
    tgt = loss_target[0]
    s_len, d = xs.shape

    halves = lambda a: a.reshape((2, a.shape[0] // 2) + a.shape[1:])
    gathered = _all_gather([_to_slot("slot_" + k, halves(w[k][0]), cs, BF16) for k in BIG]
                           + [_to_slot("slot_lru_conv_w", w["lru_conv_w"][0].reshape(2, 4, -1), cs, F32),
                              _to_slot("slot_ffn_conv_w", w["ffn_conv_w"][0].reshape(2, 12, -1), cs, F32)])
    win_g = gathered[0].reshape(N_CHIPS, d, -1)
    wout_g = gathered[1].reshape(-1, d)
    wup_g = gathered[2].reshape(N_CHIPS, d, -1)
    wdown_g = gathered[3].reshape(-1, d)
    wc = gathered[4].reshape(N_CHIPS, 4, -1).transpose(1, 0, 2).reshape(4, -1)
    wfc = gathered[5].reshape(N_CHIPS, 3, -1).transpose(1, 0, 2).reshape(3, -1)
    ff = wfc.shape[1] // 2

    tril = jnp.tril(jnp.ones((CHUNK, CHUNK), bool))
    wt32 = jnp.where(tril[None], w["gm_ws"][0], 0.0)
    wt = wt32.astype(BF16)
    wtt = wt32.transpose(0, 2, 1).astype(BF16)
    bst = jnp.repeat(w["gm_bs"][0].T, HEAD_DIM, axis=1)
    wa = w["lru_wa"][0].astype(BF16)
    wx = w["lru_wx"][0].astype(BF16)
    wat = w["lru_wa"][0].transpose(0, 2, 1).astype(BF16)
    wxt = w["lru_wx"][0].transpose(0, 2, 1).astype(BF16)
    ba = w["lru_ba"][0].reshape(1, -1)
    bx = w["lru_bx"][0].reshape(1, -1)
    gf = w["final_g"].reshape(1, -1)

    p, h1 = _fwd_in_proj(xs, w["norm1_g"], win_g)
    y, hs = _fwd_mixers(p, w["gm_v_g"], w["gm_v_b"], wt, bst, wc, w["lru_conv_b"], wa, ba, wx, bx, w["lru_lambda"],
                        w["gm_out_g"], w["lru_out_g"])
    x2, h2 = _fwd_out_proj(xs, y, wout_g, w["norm2_g"])
    up, act, dx3, dx3b, loss_tile, dgf = _fwd_ffn(h2, wup_g, wfc, w["ffn_conv_b"], wdown_g, x2, gf, tgt)
    loss = lax.psum(loss_tile[0, 0], ("x", "y", "c"))

    g_down = _bwd_weight("bwd_w_down", act[None], dx3b[None], a_planes=1, b_planes=1, shard_rows=True)
    dup, dwf, dx2, dx2b, dg2 = _bwd_ffn(dx3b, wdown_g, up, wfc, w["ffn_conv_b"], wup_g, dx3, x2, w["norm2_g"])
    g_up = _bwd_weight("bwd_w_up", h2[None], dup, a_planes=1, b_planes=2, shard_rows=False)
    g_out = _bwd_weight("bwd_w_out", y[None], dx2b[None], a_planes=1, b_planes=1, shard_rows=True)
    dy = _bwd_dy(dx2b, wout_g)
    (dp, dgv, dbv, dwt, dbst, dwc, dbc, dwa, dba, dwx, dbx, dsp, dggm, dglru) = _bwd_mixers(
        p, dy, hs, w["gm_v_g"], w["gm_v_b"], wt, wtt, bst, wc, w["lru_conv_b"], wa, wat, ba, wx, wxt, bx,
        w["lru_lambda"], w["gm_out_g"], w["lru_out_g"])
    g_in = _bwd_weight("bwd_w_in", h1[None], dp[None], a_planes=1, b_planes=1, shard_rows=False)
    grad_x, dg1 = _bwd_in(dp, win_g, dx2, xs, w["norm1_g"])

    dwfc = dwf[:, :, :3].transpose(2, 0, 1, 3).reshape(3, -1)
    dbfc = dwf[:, :, 3].reshape(1, -1)
    dlam = dsp * (-_sigmoid(-w["lru_lambda"]))
    small_grads = dict(
        norm1_g=dg1, gm_v_g=dgv, gm_v_b=dbv, gm_ws=jnp.where(tril[None], dwt, 0.0),
        gm_bs=dbst.reshape(CHUNK, HEADS, HEAD_DIM).sum(-1).T, lru_conv_w=dwc, lru_conv_b=dbc, lru_wa=dwa, lru_ba=dba,
        lru_wx=dwx, lru_bx=dbx, lru_lambda=dlam, gm_out_g=dggm, lru_out_g=dglru, norm2_g=dg2, ffn_conv_w=dwfc,
        ffn_conv_b=dbfc, final_g=dgf)
    full_shapes = [small_grads[k].shape for k in SMALL]
    rows_full = _pack_rows(full_shapes, 16)
    gpack = _pack([small_grads[k] for k in SMALL], rows_full).reshape(2, rows_full // 2, PACK_COLS)

    bigs = [g_in, g_out, g_up, g_down]
    recv = _sibling_swap([gb for _, gb in bigs] + [gpack])
    pair = [_pair_sum("pair_sum_%d" % i, g32, r, cs) for i, ((g32, _), r) in enumerate(zip(bigs, recv[:4]))]
    small_pair = _small_pair_sum(gpack, recv[4], cs)
    got = _chip_exchange([pb for pb, _ in pair], small_pair)
    reduced = [_chip_sum("chip_sum_%d" % i, own, r, cs) for i, ((_, own), r) in enumerate(zip(pair, got[:4]))]
    small_half = _small_chip_sum(small_pair, got[4], cs)
    finals = _halves_swap(reduced + [small_half])

    grads = {}
    for name, g in zip(SMALL, _unpack(finals[4], full_shapes)):
        blk = w[name].shape[1:] if w[name].ndim > 1 else w[name].shape
        if name in ("lru_conv_w", "ffn_conv_w"):
            g = lax.dynamic_slice_in_dim(g, shard * blk[1], blk[1], axis=1)
        grads[name] = g.reshape(blk)

    delta, new_m, new_v = {}, {}, {}
    for name, gfull, hv in zip(BIG, finals[:4], ("rows", "cols", "rows", "cols")):
        shp = w[name].shape
        r2 = lambda a: a.reshape(shp[1:])
        delta[name], new_m[name], new_v[name], grads[name] = _adamw(
            "adamw_" + name, r2(w[name]), gfull, r2(m[name]), r2(v[name]), halves=hv)
    blk_shapes = [grads[k].shape for k in SMALL]
    rows_blk = _pack_rows(blk_shapes, T_ELEM)
    packs = [_pack([src[k] for k in SMALL], rows_blk) for src in (w, grads, m, v)]
    outs = _adamw("adamw_small", *packs, halves=None)
    for dst, buf in zip((delta, new_m, new_v), outs):
        for name, a in zip(SMALL, _unpack(buf, blk_shapes)):
            dst[name] = a

    def shaped(dct):
        return [dct[k].reshape(w[k].shape) for k in ORDER]

    return (loss, grad_x[None], *shaped(grads), *shaped(delta), *shaped(new_m), *shaped(new_v))
```

```python
import functools
import math

import jax
import jax.numpy as jnp
from jax import lax
from jax.experimental import pallas as pl
from jax.experimental.pallas import tpu as pltpu

F32 = jnp.float32
BF16 = jnp.bfloat16
MESH = pl.DeviceIdType.MESH
ANY = pl.BlockSpec(memory_space=pltpu.HBM)

GM_W = 1024
LRU_W = 1024
CHUNK = 128
HEADS = 8
HEAD_DIM = 128
LRU_C = 8.0
RMS_EPS = 1e-6
LN_EPS = 1e-5
ADAM_LR = 0.001
ADAM_B1 = 0.9
ADAM_B2 = 0.999
ADAM_EPS = 1e-08
ADAM_WD = 0.01
ADAM_STEP = 10

N_CHIPS = 4
HALO = 8
PACK_COLS = 1024
VMEM_LIMIT = 56 * 1024 * 1024

TM_IN = 1024
TS_MIX = 256
TM_OUT = 512
TM_UP = 512
TN_UP = 512
T_EPI = 128
T_CHUNK = 16
TM_BW = 512
TW_M = 1024
TW_N = 1024
TW_K = 2048
T_ELEM = 256
LANES = 128


def _tile(dim, cap):
    t = min(cap, dim) // LANES * LANES
    while dim % t:
        t -= LANES
    return t

_GELU_K0 = 0.7978845608028654
_GELU_K1 = 0.044715


class _Job:
    def __init__(self, ins, out_shapes, aliases, n, copies):
        self.ins, self.out_shapes, self.aliases, self.n, self.copies = list(ins), list(out_shapes), dict(aliases), n, copies

    def make(self, cin, cout, send_sems, recv_sems):
        x, y, c = _mesh_pos()
        return [pltpu.make_async_remote_copy(src_ref=src, dst_ref=dst, send_sem=send_sems.at[k], recv_sem=recv_sems.at[k],
                                             device_id=dev, device_id_type=MESH)
                for k, (src, dst, dev) in enumerate(self.copies(cin, cout, x, y, c))]


def _merge_jobs(jobs):
    ins, outs, aliases, spans = [], [], {}, []
    for jb in jobs:
        spans.append((len(ins), len(jb.ins), len(outs), len(jb.out_shapes)))
        aliases.update({len(ins) + a: len(outs) + b for a, b in jb.aliases.items()})
        ins += jb.ins
        outs += jb.out_shapes

    def copies(cin, cout, x, y, c):
        out = []
        for jb, (i0, ni, o0, no) in zip(jobs, spans):
            out += jb.copies(cin[i0:i0 + ni], cout[o0:o0 + no], x, y, c)
        return out

    return _Job(ins, outs, aliases, sum(jb.n for jb in jobs), copies)


def _pcall(body, *, name, out_shape, grid=None, in_specs=None, out_specs=None, scratch_shapes=(),
           grid_spec=None, dims=None, aliases=None, comm=None):
    params = pltpu.CompilerParams(dimension_semantics=dims, vmem_limit_bytes=VMEM_LIMIT)
    kw = dict(name=name, compiler_params=params)
    if grid_spec is not None:
        if aliases:
            kw["input_output_aliases"] = aliases
        return pl.pallas_call(body, grid_spec=grid_spec, out_shape=out_shape, **kw)
    scratch_shapes = list(scratch_shapes)
    if comm is not None:
        out_shape = list(out_shape) if isinstance(out_shape, (list, tuple)) else [out_shape]
        out_specs = list(out_specs) if isinstance(out_specs, (list, tuple)) else [out_specs]
        n_in, n_out, n_scr = len(in_specs), len(out_shape), len(scratch_shapes)
        n_ci, n_co = len(comm.ins), len(comm.out_shapes)
        inner, steps = body, tuple(grid)

        def body(*refs):
            o0 = n_in + n_ci
            s0 = o0 + n_out + n_co
            cps = comm.make(refs[n_in:o0], refs[o0 + n_out:s0], refs[s0 + n_scr], refs[s0 + n_scr + 1])
            ids = [pl.program_id(k) for k in range(len(steps))]

            @pl.when(functools.reduce(jnp.logical_and, [i == 0 for i in ids]))
            def _():
                for cp in cps:
                    cp.start()

            inner(*refs[:n_in], *refs[o0:o0 + n_out], *refs[s0:s0 + n_scr])

            @pl.when(functools.reduce(jnp.logical_and, [i == n - 1 for i, n in zip(ids, steps)]))
            def _():
                for cp in cps:
                    cp.wait()

        in_specs = list(in_specs) + [ANY] * n_ci
        out_specs = out_specs + [ANY] * n_co
        out_shape = out_shape + comm.out_shapes
        scratch_shapes = scratch_shapes + [pltpu.SemaphoreType.DMA((comm.n,)), pltpu.SemaphoreType.DMA((comm.n,))]
        aliases = {**(aliases or {}), **{n_in + a: n_out + b for a, b in comm.aliases.items()}}
    if aliases:
        kw["input_output_aliases"] = aliases
    if grid is not None:
        kw["grid"] = grid
    call = pl.pallas_call(body, in_specs=in_specs, out_specs=out_specs, scratch_shapes=scratch_shapes,
                          out_shape=out_shape, **kw)
    if comm is None:
        return call
    return lambda *args: call(*args, *comm.ins)


def _comm_call(name, job):
    n_ci, n_co = len(job.ins), len(job.out_shapes)

    def body(*refs):
        cps = job.make(refs[:n_ci], refs[n_ci:n_ci + n_co], refs[n_ci + n_co], refs[n_ci + n_co + 1])
        for cp in cps:
            cp.start()
        for cp in cps:
            cp.wait()

    return pl.pallas_call(
        body, name=name, in_specs=[ANY] * n_ci, out_specs=[ANY] * n_co, out_shape=job.out_shapes,
        scratch_shapes=[pltpu.SemaphoreType.DMA((job.n,)), pltpu.SemaphoreType.DMA((job.n,))],
        input_output_aliases=job.aliases)(*job.ins)


def _gelu(x):
    t = jnp.tanh(_GELU_K0 * (x + _GELU_K1 * (x * x * x)))
    return 0.5 * x * (1.0 + t)


def _gelu_and_grad(x):
    x2 = x * x
    t = jnp.tanh(_GELU_K0 * (x + _GELU_K1 * (x2 * x)))
    g = 0.5 * x * (1.0 + t)
    dg = 0.5 * (1.0 + t) + 0.5 * x * (1.0 - t * t) * (_GELU_K0 * (1.0 + 3.0 * _GELU_K1 * x2))
    return g, dg


def _sigmoid(x):
    return 1.0 / (1.0 + jnp.exp(-x))


def _neg_expm1(x):
    series = -x * (1.0 + x * (0.5 + x * (1.0 / 6.0 + x * (1.0 / 24.0 + x * (1.0 / 120.0 + x * (1.0 / 720.0))))))
    return jnp.where(x > -0.1, series, 1.0 - jnp.exp(x))


def _softplus(z):
    return jnp.maximum(z, 0.0) + jnp.log(1.0 + jnp.exp(-jnp.abs(z)))


def _rowmean(x):
    return jnp.mean(x, axis=-1, keepdims=True)


def _colsum(x):
    return jnp.sum(x, axis=0, keepdims=True)


def _rms_stats(x):
    r = lax.rsqrt(_rowmean(x * x) + RMS_EPS)
    return r, x * r


def _rms_bwd(dy, n, r, g):
    dn = dy * g
    return r * (dn - n * _rowmean(dn * n)), dy * n


def _shift_prev(x, halo, d):
    cat = jnp.concatenate([halo, x], axis=0)
    return pltpu.roll(cat, d, 0)[HALO:, :]


def _prev_rows(cat):
    return cat[HALO:, :], pltpu.roll(cat, 1, 0)[HALO:, :], pltpu.roll(cat, 2, 0)[HALO:, :]


def _next_rows(cat):
    n = cat.shape[0]
    return cat[:n - HALO, :], pltpu.roll(cat, n - 1, 0)[:n - HALO, :], pltpu.roll(cat, n - 2, 0)[:n - HALO, :]


def _shift_next(x, halo, d):
    n = x.shape[0]
    cat = jnp.concatenate([x, halo], axis=0)
    return pltpu.roll(cat, n + HALO - d, 0)[:n, :]


def _dot(a, b):
    return jnp.dot(a, b, preferred_element_type=F32)


def _dot_nt(a, b):
    return lax.dot_general(a, b, (((1,), (1,)), ((), ())), preferred_element_type=F32)


def _dot_tn(a, b):
    return lax.dot_general(a, b, (((0,), (0,)), ((), ())), preferred_element_type=F32)


def _fwd_in_proj(x, g1, win_g, comm=None):
    s_len, d = x.shape
    nsh, _, ncol = win_g.shape
    tm = min(TM_IN, s_len)

    def body(x_ref, g_ref, w_ref, p_ref, h_ref):
        @pl.when(pl.program_id(1) == 0)
        def _():
            _, n = _rms_stats(x_ref[...])
            h_ref[...] = (n * g_ref[...]).astype(BF16)

        p_ref[...] = _dot(h_ref[...], w_ref[...])

    return _pcall(
        body, name="fwd_in_proj", grid=(s_len // tm, nsh),
        in_specs=[pl.BlockSpec((tm, d), lambda i, j: (i, 0)),
                  pl.BlockSpec((1, d), lambda i, j: (0, 0)),
                  pl.BlockSpec((None, d, ncol), lambda i, j: (j, 0, 0))],
        out_specs=[pl.BlockSpec((tm, ncol), lambda i, j: (i, j)),
                   pl.BlockSpec((tm, d), lambda i, j: (i, 0))],
        out_shape=[jax.ShapeDtypeStruct((s_len, nsh * ncol), F32), jax.ShapeDtypeStruct((s_len, d), BF16)],
        dims=("arbitrary", "arbitrary"), comm=comm)(x, g1, win_g)


def _gm_forward(z, gv, bv, wt_ref, bst_ref, vl_s, mix_s):
    ts = z.shape[0]
    ge = _gelu(z)
    u = ge[:, :GM_W]
    v = ge[:, GM_W:]
    vc = v - _rowmean(v)
    rs = lax.rsqrt(_rowmean(vc * vc) + LN_EPS)
    vh = vc * rs
    vl_s[...] = (vh * gv + bv).astype(BF16)
    for cc in range(ts // CHUNK):
        rows = slice(cc * CHUNK, (cc + 1) * CHUNK)
        for hh in range(HEADS):
            cols = slice(hh * HEAD_DIM, (hh + 1) * HEAD_DIM)
            mix_s[rows, cols] = _dot(wt_ref[hh], vl_s[rows, cols]) + bst_ref[:, cols]
    mixed = mix_s[...]
    return u, mixed, vh, rs, u * mixed


def _lru_gates(xl, halo, wc_ref, bc_ref, wa_ref, ba_ref, wx_ref, bx_ref, lam_ref, z_s):
    x1 = _shift_prev(xl, halo, 1)
    x2 = _shift_prev(xl, halo, 2)
    x3 = _shift_prev(xl, halo, 3)
    xr = bc_ref[...] + wc_ref[0:1, :] * x3 + wc_ref[1:2, :] * x2 + wc_ref[2:3, :] * x1 + wc_ref[3:4, :] * xl
    xrb = xr.astype(BF16)
    for hh in range(HEADS):
        cols = slice(hh * HEAD_DIM, (hh + 1) * HEAD_DIM)
        z_s[:, cols] = _dot(xrb[:, cols], wa_ref[hh])
        z_s[:, LRU_W + hh * HEAD_DIM:LRU_W + (hh + 1) * HEAD_DIM] = _dot(xrb[:, cols], wx_ref[hh])
    ra = _sigmoid(z_s[:, :LRU_W] + ba_ref[...])
    ri = _sigmoid(z_s[:, LRU_W:] + bx_ref[...])
    sp = _softplus(-lam_ref[...])
    la = (-LRU_C) * ra * sp
    a = jnp.exp(la)
    mult = jnp.sqrt(_neg_expm1(2.0 * la))
    return dict(x1=x1, x2=x2, x3=x3, xr=xr, xrb=xrb, ra=ra, ri=ri, sp=sp, a=a, mult=mult)


def _fwd_mixers(p, gv, bv, wt, bst, wc, bc, wa, ba, wx, bx, lam, ggm, glru, comm=None):
    s_len = p.shape[0]
    ts = min(TS_MIX, s_len)

    def body(pz_ref, pgl_ref, pxl_ref, gv_ref, bv_ref, wt_ref, bst_ref, wc_ref, bc_ref, wa_ref, ba_ref, wx_ref,
             bx_ref, lam_ref, ggm_ref, glru_ref, y_ref, hs_ref, tail_ref, h_ref, a_s, b_s, vl_s, mix_s, z_s):
        @pl.when(pl.program_id(0) == 0)
        def _():
            tail_ref[...] = jnp.zeros_like(tail_ref)
            h_ref[...] = jnp.zeros_like(h_ref)

        _, _, _, _, ygm = _gm_forward(pz_ref[...], gv_ref[...], bv_ref[...], wt_ref, bst_ref, vl_s, mix_s)
        _, ngm = _rms_stats(ygm)
        y_ref[:, :GM_W] = (ngm * ggm_ref[...]).astype(BF16)

        xl = pxl_ref[...]
        gts = _lru_gates(xl, tail_ref[...], wc_ref, bc_ref, wa_ref, ba_ref, wx_ref, bx_ref, lam_ref, z_s)
        tail_ref[...] = xl[ts - HALO:, :]
        a_s[...] = gts["a"]
        b_s[...] = gts["mult"] * (gts["ri"] * gts["xr"])

        def step(t, h):
            h = a_s[pl.ds(t, 1), :] * h + b_s[pl.ds(t, 1), :]
            hs_ref[pl.ds(t, 1), :] = h
            return h

        h_ref[...] = lax.fori_loop(0, ts, step, h_ref[...], unroll=8)
        yl = hs_ref[...] * _gelu(pgl_ref[...])
        _, nl = _rms_stats(yl)
        y_ref[:, GM_W:] = (nl * glru_ref[...]).astype(BF16)

    full = lambda shape: pl.BlockSpec(shape, lambda i: (0,) * len(shape))
    return _pcall(
        body, name="fwd_mixers", grid=(s_len // ts,),
        in_specs=[pl.BlockSpec((ts, 2 * GM_W), lambda i: (i, 0)),
                  pl.BlockSpec((ts, LRU_W), lambda i: (i, 2)),
                  pl.BlockSpec((ts, LRU_W), lambda i: (i, 3)),
                  full((1, GM_W)), full((1, GM_W)), full((HEADS, CHUNK, CHUNK)), full((CHUNK, GM_W)),
                  full((4, LRU_W)), full((1, LRU_W)), full((HEADS, HEAD_DIM, HEAD_DIM)), full((1, LRU_W)),
                  full((HEADS, HEAD_DIM, HEAD_DIM)), full((1, LRU_W)), full((1, LRU_W)), full((1, GM_W)),
                  full((1, LRU_W))],
        out_specs=[pl.BlockSpec((ts, GM_W + LRU_W), lambda i: (i, 0)), pl.BlockSpec((ts, LRU_W), lambda i: (i, 0))],
        out_shape=[jax.ShapeDtypeStruct((s_len, GM_W + LRU_W), BF16), jax.ShapeDtypeStruct((s_len, LRU_W), F32)],
        scratch_shapes=[pltpu.VMEM((HALO, LRU_W), F32), pltpu.VMEM((1, LRU_W), F32),
                        pltpu.VMEM((ts, LRU_W), F32), pltpu.VMEM((ts, LRU_W), F32),
                        pltpu.VMEM((ts, GM_W), BF16), pltpu.VMEM((ts, GM_W), F32), pltpu.VMEM((ts, 2 * LRU_W), F32)],
        dims=("arbitrary",), comm=comm)(p, p, p, gv, bv, wt, bst, wc, bc, wa, ba, wx, bx, lam, ggm, glru)


def _fwd_out_proj(x, y, wout_g, g2, comm=None):
    s_len, d = x.shape
    tm = min(TM_OUT, s_len)

    def body(x_ref, y_ref, w_ref, g_ref, x2_ref, h2_ref):
        x2 = x_ref[...] + _dot(y_ref[...], w_ref[...])
        x2_ref[...] = x2
        _, n = _rms_stats(x2)
        h2_ref[...] = (n * g_ref[...]).astype(BF16)

    return _pcall(
        body, name="fwd_out_proj", grid=(s_len // tm,),
        in_specs=[pl.BlockSpec((tm, d), lambda i: (i, 0)), pl.BlockSpec((tm, d), lambda i: (i, 0)),
                  pl.BlockSpec((d, d), lambda i: (0, 0)), pl.BlockSpec((1, d), lambda i: (0, 0))],
        out_specs=[pl.BlockSpec((tm, d), lambda i: (i, 0)), pl.BlockSpec((tm, d), lambda i: (i, 0))],
        out_shape=[jax.ShapeDtypeStruct((s_len, d), F32), jax.ShapeDtypeStruct((s_len, d), BF16)],
        dims=("arbitrary",), comm=comm)(x, y, wout_g, g2)


def _row_fetch(hbm_ref, buf_ref, sem, row0, rows):
    return pltpu.make_async_copy(hbm_ref.at[pl.ds(row0, rows), :], buf_ref, sem)


def _fwd_ffn(h2, wup_g, wfc, bfc, wdown_g, x2, gf, target):
    s_len, d = h2.shape
    nsh, _, ncol = wup_g.shape
    f = nsh * ncol // 2
    tm = min(TM_UP, s_len)
    te = min(T_EPI, tm)
    tr = T_CHUNK
    tn = TN_UP
    nps = ncol // tn
    nj = f // tn

    def body(h_ref, wg_ref, wv_ref, wcg_ref, wcv_ref, bg_ref, bv_ref, wd_ref, g_ref, x2_hbm, t_hbm,
             up_ref, act_ref, dx3_ref, dx3b_ref, loss_ref, dgf_ref, tail_ref, acc_ref, x2_buf, t_buf, sems):
        i, j = pl.program_id(0), pl.program_id(1)
        row0 = pl.multiple_of(i * tm, tm)
        fetches = (_row_fetch(x2_hbm, x2_buf, sems.at[0], row0, tm), _row_fetch(t_hbm, t_buf, sems.at[1], row0, tm))

        @pl.when(jnp.logical_and(i == 0, j == 0))
        def _():
            tail_ref[...] = jnp.zeros_like(tail_ref)
            loss_ref[...] = jnp.zeros_like(loss_ref)
            dgf_ref[...] = jnp.zeros_like(dgf_ref)

        @pl.when(j == 0)
        def _():
            acc_ref[...] = jnp.zeros_like(acc_ref)
            for cp in fetches:
                cp.start()

        h = h_ref[...]
        up_ref[0] = _dot(h, wg_ref[...])
        up_ref[1] = _dot(h, wv_ref[...])

        planes = ((wcg_ref, bg_ref), (wcv_ref, bv_ref))
        for c in range(tn // LANES):
            cols = slice(c * LANES, (c + 1) * LANES)

            def conv(pln, u, u1, u2):
                wc_ref, b_ref = planes[pln]
                return b_ref[:, cols] + wc_ref[0:1, cols] * u2 + wc_ref[1:2, cols] * u1 + wc_ref[2:3, cols] * u

            def emit(rows, us):
                act_ref[rows, cols] = (_gelu(conv(0, *us[0])) * conv(1, *us[1])).astype(BF16)

            first = [_prev_rows(jnp.concatenate([tail_ref[pln, j, :, cols], up_ref[pln, 0:tr, cols]], axis=0))
                     for pln in range(2)]
            emit(slice(0, tr), first)

            for k in range(1, tm // tr):
                r0 = k * tr
                us = [_prev_rows(up_ref[pln, r0 - HALO:r0 + tr, cols]) for pln in range(2)]
                emit(slice(r0, r0 + tr), us)
        tail_ref[0, j] = up_ref[0, tm - HALO:tm, :]
        tail_ref[1, j] = up_ref[1, tm - HALO:tm, :]
        acc_ref[...] += _dot(act_ref[...], wd_ref[...])

        @pl.when(j == nj - 1)
        def _():
            for cp in fetches:
                cp.wait()
            g = g_ref[...]

            def chunk(k, carry):
                rows = pl.ds(pl.multiple_of(k * te, te), te)
                x3 = x2_buf[rows, :] + acc_ref[rows, :]
                r, n = _rms_stats(x3)
                err = n * g - t_buf[rows, :]
                loss_ref[...] += jnp.sum(err * err) * (0.5 / d)
                dx3, dgn = _rms_bwd(err * (1.0 / d), n, r, g)
                dgf_ref[...] += _colsum(dgn)
                dx3_ref[rows, :] = dx3
                dx3b_ref[rows, :] = dx3.astype(BF16)
                return carry

            lax.fori_loop(0, tm // te, chunk, 0)

    hbm = pl.BlockSpec(memory_space=pl.ANY)
    return _pcall(
        body, name="fwd_ffn", grid=(s_len // tm, nj),
        in_specs=[pl.BlockSpec((tm, d), lambda i, j: (i, 0)),
                  pl.BlockSpec((None, d, tn), lambda i, j: (j // nps, 0, j % nps)),
                  pl.BlockSpec((None, d, tn), lambda i, j: (nsh // 2 + j // nps, 0, j % nps)),
                  pl.BlockSpec((3, tn), lambda i, j: (0, j)), pl.BlockSpec((3, tn), lambda i, j: (0, nj + j)),
                  pl.BlockSpec((1, tn), lambda i, j: (0, j)), pl.BlockSpec((1, tn), lambda i, j: (0, nj + j)),
                  pl.BlockSpec((tn, d), lambda i, j: (j, 0)), pl.BlockSpec((1, d), lambda i, j: (0, 0)), hbm, hbm],
        out_specs=[pl.BlockSpec((2, tm, tn), lambda i, j: (0, i, j)), pl.BlockSpec((tm, tn), lambda i, j: (i, j)),
                   pl.BlockSpec((tm, d), lambda i, j: (i, 0)), pl.BlockSpec((tm, d), lambda i, j: (i, 0)),
                   pl.BlockSpec((8, 128), lambda i, j: (0, 0)), pl.BlockSpec((1, d), lambda i, j: (0, 0))],
        out_shape=[jax.ShapeDtypeStruct((2, s_len, f), F32), jax.ShapeDtypeStruct((s_len, f), BF16),
                   jax.ShapeDtypeStruct((s_len, d), F32), jax.ShapeDtypeStruct((s_len, d), BF16),
                   jax.ShapeDtypeStruct((8, 128), F32), jax.ShapeDtypeStruct((1, d), F32)],
        scratch_shapes=[pltpu.VMEM((2, nj, HALO, tn), F32), pltpu.VMEM((tm, d), F32), pltpu.VMEM((tm, d), F32),
                        pltpu.VMEM((tm, d), F32), pltpu.SemaphoreType.DMA((2,))],
        dims=("arbitrary", "arbitrary"))(h2, wup_g, wup_g, wfc, wfc, bfc, bfc, wdown_g, gf, x2, target)


def _bwd_ffn(dx3b, wdown_g, up, wfc, bfc, wup_g, dx3, x2, g2, comm=None):
    s_len, d = dx3b.shape
    f = up.shape[2]
    nsh, _, ncol = wup_g.shape
    tm = min(TM_UP, s_len)
    te = min(T_EPI, tm)
    tn = TN_UP
    nj = f // tn
    nps = ncol // tn
    nt = s_len // tm
    hb = tm // HALO

    tr = T_CHUNK

    def body(dx_ref, w_ref, up_ref, halo_ref, wcg_ref, wcv_ref, bg_ref, bv_ref, wug_ref, wuv_ref, g_ref, r_hbm, x_hbm,
             dup_ref, dwf_ref, dx2_ref, dx2b_ref, dg_ref, nxt_ref, acc_ref, r_buf, x_buf, da_s, dc_s, sems):
        i, j = pl.program_id(0), pl.program_id(1)
        ti = nt - 1 - i
        row0 = pl.multiple_of(ti * tm, tm)
        fetches = (_row_fetch(r_hbm, r_buf, sems.at[0], row0, tm), _row_fetch(x_hbm, x_buf, sems.at[1], row0, tm))

        @pl.when(jnp.logical_and(i == 0, j == 0))
        def _():
            dwf_ref[...] = jnp.zeros_like(dwf_ref)
            nxt_ref[...] = jnp.zeros_like(nxt_ref)
            dg_ref[...] = jnp.zeros_like(dg_ref)

        @pl.when(j == 0)
        def _():
            acc_ref[...] = jnp.zeros_like(acc_ref)
            for cp in fetches:
                cp.start()

        da_s[...] = _dot_nt(dx_ref[...], w_ref[...])
        planes = ((wcg_ref, bg_ref), (wcv_ref, bv_ref))
        for pln in range(2):
            dc_s[pln, tm:tm + HALO, :] = nxt_ref[pln, j]

        for c in range(tn // LANES):
            cols = slice(c * LANES, (c + 1) * LANES)

            def conv(pln, u, u1, u2):
                wc_ref, b_ref = planes[pln]
                return b_ref[:, cols] + wc_ref[0:1, cols] * u2 + wc_ref[1:2, cols] * u1 + wc_ref[2:3, cols] * u

            def fold(v):
                out = v[0:8, :]
                for q in range(1, tr // 8):
                    out = out + v[8 * q:8 * q + 8, :]
                return out

            def pass1(rows, us, sums):
                dact = da_s[rows, cols]
                ge, gd = _gelu_and_grad(conv(0, *us[0]))
                dcs = (dact * conv(1, *us[1]) * gd, dact * ge)
                new = []
                for pln in range(2):
                    u, u1, u2 = us[pln]
                    dc_s[pln, rows, cols] = dcs[pln]
                    new += [fold(dcs[pln] * u2), fold(dcs[pln] * u1), fold(dcs[pln] * u), fold(dcs[pln])]
                return tuple(a + b for a, b in zip(sums, new))

            first = [_prev_rows(jnp.concatenate([jnp.where(ti == 0, 0.0, halo_ref[pln, :, cols]),
                                                 up_ref[pln, 0:tr, cols]], axis=0)) for pln in range(2)]
            sums = pass1(slice(0, tr), first, (jnp.zeros((8, LANES), F32),) * 8)

            for k in range(1, tm // tr):
                r0 = k * tr
                us = [_prev_rows(up_ref[pln, r0 - HALO:r0 + tr, cols]) for pln in range(2)]
                sums = pass1(slice(r0, r0 + tr), us, sums)
            for pln in range(2):
                for term in range(4):
                    dwf_ref[pln, j, term:term + 1, cols] += _colsum(sums[4 * pln + term])

        for pln in range(2):
            nxt_ref[pln, j] = dc_s[pln, 0:HALO, :]
        for c in range(tn // LANES):
            cols = slice(c * LANES, (c + 1) * LANES)

            for k in range(tm // tr):
                r0 = k * tr
                for pln in range(2):
                    wc_ref = planes[pln][0]
                    dc, dc1, dc2 = _next_rows(dc_s[pln, r0:r0 + tr + HALO, cols])
                    dup = wc_ref[2:3, cols] * dc + wc_ref[1:2, cols] * dc1 + wc_ref[0:1, cols] * dc2
                    dup_ref[pln, r0:r0 + tr, cols] = dup.astype(BF16)
        acc_ref[...] += _dot_nt(dup_ref[0], wug_ref[...]) + _dot_nt(dup_ref[1], wuv_ref[...])

        @pl.when(j == nj - 1)
        def _():
            for cp in fetches:
                cp.wait()
            g = g_ref[...]

            def chunk(k, carry):
                rows = pl.ds(pl.multiple_of(k * te, te), te)
                r, n = _rms_stats(x_buf[rows, :])
                dxn, dgn = _rms_bwd(acc_ref[rows, :], n, r, g)
                dg_ref[...] += _colsum(dgn)
                dx = r_buf[rows, :] + dxn
                dx2_ref[rows, :] = dx
                dx2b_ref[rows, :] = dx.astype(BF16)
                return carry

            lax.fori_loop(0, tm // te, chunk, 0)

    hbm = pl.BlockSpec(memory_space=pl.ANY)
    rev = lambda i: nt - 1 - i
    return _pcall(
        body, name="bwd_ffn", grid=(nt, nj),
        in_specs=[pl.BlockSpec((tm, d), lambda i, j: (rev(i), 0)),
                  pl.BlockSpec((tn, d), lambda i, j: (j, 0)),
                  pl.BlockSpec((2, tm, tn), lambda i, j: (0, rev(i), j)),
                  pl.BlockSpec((2, HALO, tn), lambda i, j: (0, jnp.maximum(rev(i) * hb - 1, 0), j)),
                  pl.BlockSpec((3, tn), lambda i, j: (0, j)), pl.BlockSpec((3, tn), lambda i, j: (0, nj + j)),
                  pl.BlockSpec((1, tn), lambda i, j: (0, j)), pl.BlockSpec((1, tn), lambda i, j: (0, nj + j)),
                  pl.BlockSpec((None, d, tn), lambda i, j: (j // nps, 0, j % nps)),
                  pl.BlockSpec((None, d, tn), lambda i, j: (nsh // 2 + j // nps, 0, j % nps)),
                  pl.BlockSpec((1, d), lambda i, j: (0, 0)), hbm, hbm],
        out_specs=[pl.BlockSpec((2, tm, tn), lambda i, j: (0, rev(i), j)),
                   pl.BlockSpec((2, nj, 8, tn), lambda i, j: (0, 0, 0, 0)),
                   pl.BlockSpec((tm, d), lambda i, j: (rev(i), 0)), pl.BlockSpec((tm, d), lambda i, j: (rev(i), 0)),
                   pl.BlockSpec((1, d), lambda i, j: (0, 0))],
        out_shape=[jax.ShapeDtypeStruct((2, s_len, f), BF16), jax.ShapeDtypeStruct((2, nj, 8, tn), F32),
                   jax.ShapeDtypeStruct((s_len, d), F32), jax.ShapeDtypeStruct((s_len, d), BF16),
                   jax.ShapeDtypeStruct((1, d), F32)],
        scratch_shapes=[pltpu.VMEM((2, nj, HALO, tn), F32), pltpu.VMEM((tm, d), F32), pltpu.VMEM((tm, d), F32),
                        pltpu.VMEM((tm, d), F32), pltpu.VMEM((tm, tn), F32), pltpu.VMEM((2, tm + HALO, tn), F32),
                        pltpu.SemaphoreType.DMA((2,))],
        dims=("arbitrary", "arbitrary"), comm=comm)(dx3b, wdown_g, up, up, wfc, wfc, bfc, bfc, wup_g, wup_g, g2, dx3, x2)


def _bwd_in(dp, win_g, resid, x_in, g, comm=None):
    s_len, d = x_in.shape
    nsh, _, ncol = win_g.shape
    tm = min(TM_BW, s_len)

    def body(dz_ref, w_ref, r_ref, x_ref, g_ref, dx_ref, dg_ref, acc_ref):
        i, k = pl.program_id(0), pl.program_id(1)

        @pl.when(jnp.logical_and(i == 0, k == 0))
        def _():
            dg_ref[...] = jnp.zeros_like(dg_ref)

        @pl.when(k == 0)
        def _():
            acc_ref[...] = jnp.zeros_like(acc_ref)

        acc_ref[...] += _dot_nt(dz_ref[...], w_ref[...])

        @pl.when(k == nsh - 1)
        def _():
            r, n = _rms_stats(x_ref[...])
            dxn, dgn = _rms_bwd(acc_ref[...], n, r, g_ref[...])
            dg_ref[...] += _colsum(dgn)
            dx_ref[...] = r_ref[...] + dxn

    return _pcall(
        body, name="bwd_in", grid=(s_len // tm, nsh),
        in_specs=[pl.BlockSpec((tm, ncol), lambda i, k: (i, k)),
                  pl.BlockSpec((None, d, ncol), lambda i, k: (k, 0, 0)),
                  pl.BlockSpec((tm, d), lambda i, k: (i, 0)), pl.BlockSpec((tm, d), lambda i, k: (i, 0)),
                  pl.BlockSpec((1, d), lambda i, k: (0, 0))],
        out_specs=[pl.BlockSpec((tm, d), lambda i, k: (i, 0)), pl.BlockSpec((1, d), lambda i, k: (0, 0))],
        out_shape=[jax.ShapeDtypeStruct((s_len, d), F32), jax.ShapeDtypeStruct((1, d), F32)],
        scratch_shapes=[pltpu.VMEM((tm, d), F32)],
        dims=("arbitrary", "arbitrary"), comm=comm)(dp, win_g, resid, x_in, g)


def _bwd_dy(dx2b, wout_g, comm=None):
    s_len, d = dx2b.shape
    tm = min(TM_OUT, s_len)

    def body(dx_ref, w_ref, dy_ref):
        dy_ref[...] = _dot_nt(dx_ref[...], w_ref[...])

    return _pcall(
        body, name="bwd_dy", grid=(s_len // tm,),
        in_specs=[pl.BlockSpec((tm, d), lambda i: (i, 0)), pl.BlockSpec((d, d), lambda i: (0, 0))],
        out_specs=pl.BlockSpec((tm, d), lambda i: (i, 0)),
        out_shape=jax.ShapeDtypeStruct((s_len, d), F32), dims=("arbitrary",), comm=comm)(dx2b, wout_g)


def _bwd_mixers(p, dy, hs, gv, bv, wt, wtt, bst, wc, bc, wa, wat, ba, wx, wxt, bx, lam, ggm, glru, comm=None):
    s_len = p.shape[0]
    ts = min(TS_MIX, s_len)
    nt = s_len // ts
    hb = ts // HALO

    def body(pz_ref, pgl_ref, pxl_ref, xh_ref, dy_ref, hs_ref, hh_ref, gv_ref, bv_ref, wt_ref, wtt_ref, bst_ref,
             wc_ref, bc_ref, wa_ref, wat_ref, ba_ref, wx_ref, wxt_ref, bx_ref, lam_ref, ggm_ref, glru_ref,
             dp_ref, dgv_ref, dbv_ref, dwt_ref, dbst_ref, dwc_ref, dbc_ref, dwa_ref, dba_ref, dwx_ref, dbx_ref,
             dsp_ref, dggm_ref, dglru_ref,
             carry_ref, nxt_ref, a_s, g_s, vl_s, mix_s, z_s, dm_s, dvl_s, dz_s, dxr_s):
        i = pl.program_id(0)
        ti = nt - 1 - i

        @pl.when(i == 0)
        def _():
            for ref in (dgv_ref, dbv_ref, dwt_ref, dbst_ref, dwc_ref, dbc_ref, dwa_ref, dba_ref, dwx_ref, dbx_ref,
                        dsp_ref, dggm_ref, dglru_ref, carry_ref, nxt_ref):
                ref[...] = jnp.zeros_like(ref)

        z = pz_ref[...]
        u, mixed, vh, rs, ygm = _gm_forward(z, gv_ref[...], bv_ref[...], wt_ref, bst_ref, vl_s, mix_s)
        rg, ngm = _rms_stats(ygm)
        dygm, dgn = _rms_bwd(dy_ref[:, :GM_W], ngm, rg, ggm_ref[...])
        dggm_ref[...] += _colsum(dgn)
        du = dygm * mixed
        dmix = dygm * u
        dm_s[...] = dmix.astype(BF16)
        bsum = dmix[0:CHUNK, :]
        for cc in range(1, ts // CHUNK):
            bsum = bsum + dmix[cc * CHUNK:(cc + 1) * CHUNK, :]
        dbst_ref[...] += bsum
        for hh in range(HEADS):
            cols = slice(hh * HEAD_DIM, (hh + 1) * HEAD_DIM)
            dw = jnp.zeros((CHUNK, CHUNK), F32)
            for cc in range(ts // CHUNK):
                rows = slice(cc * CHUNK, (cc + 1) * CHUNK)
                dmb = dm_s[rows, cols]
                dw = dw + _dot_nt(dmb, vl_s[rows, cols])
                dvl_s[rows, cols] = _dot(wtt_ref[hh], dmb)
            dwt_ref[hh] += dw
        dvl = dvl_s[...]
        dgv_ref[...] += _colsum(dvl * vh)
        dbv_ref[...] += _colsum(dvl)
        dvh = dvl * gv_ref[...]
        dv = rs * (dvh - _rowmean(dvh) - vh * _rowmean(dvh * vh))
        _, gd = _gelu_and_grad(z)
        dp_ref[:, :GM_W] = (du * gd[:, :GM_W]).astype(BF16)
        dp_ref[:, GM_W:2 * GM_W] = (dv * gd[:, GM_W:]).astype(BF16)

        xl = pxl_ref[...]
        xhalo = jnp.where(ti == 0, 0.0, xh_ref[...])
        gts = _lru_gates(xl, xhalo, wc_ref, bc_ref, wa_ref, ba_ref, wx_ref, bx_ref, lam_ref, z_s)
        a, mult, ra, ri, xr, sp = gts["a"], gts["mult"], gts["ra"], gts["ri"], gts["xr"], gts["sp"]
        hs = hs_ref[...]
        hprev = _shift_prev(hs, jnp.where(ti == 0, 0.0, hh_ref[...]), 1)
        gl = pgl_ref[...]
        ggl, dggl = _gelu_and_grad(gl)
        yl = hs * ggl
        rl, nl = _rms_stats(yl)
        dyl, dgn = _rms_bwd(dy_ref[:, GM_W:], nl, rl, glru_ref[...])
        dglru_ref[...] += _colsum(dgn)
        dp_ref[:, 2 * GM_W:2 * GM_W + LRU_W] = (dyl * hs * dggl).astype(BF16)
        a_s[...] = a
        g_s[...] = dyl * ggl

        def step(k, carry):
            t = ts - 1 - k
            gt = g_s[pl.ds(t, 1), :] + carry
            g_s[pl.ds(t, 1), :] = gt
            return a_s[pl.ds(t, 1), :] * gt

        carry_ref[...] = lax.fori_loop(0, ts, step, carry_ref[...], unroll=8)
        gsc = g_s[...]
        da = gsc * hprev
        rix = ri * xr
        dmult = gsc * rix
        dri = gsc * mult * xr
        dxr = gsc * mult * ri
        dla = da * a - dmult * (a * a) / mult
        dsp_ref[...] += _colsum(dla * ra) * (-LRU_C)
        dza = (dla * sp) * (-LRU_C) * ra * (1.0 - ra)
        dzi = dri * ri * (1.0 - ri)
        dba_ref[...] += _colsum(dza)
        dbx_ref[...] += _colsum(dzi)
        dzab = dza.astype(BF16)
        dzib = dzi.astype(BF16)
        xrb = gts["xrb"]
        for hh in range(HEADS):
            cols = slice(hh * HEAD_DIM, (hh + 1) * HEAD_DIM)
            dwa_ref[hh] += _dot_tn(xrb[:, cols], dzab[:, cols])
            dwx_ref[hh] += _dot_tn(xrb[:, cols], dzib[:, cols])
            dxr_s[:, cols] = _dot(dzab[:, cols], wat_ref[hh]) + _dot(dzib[:, cols], wxt_ref[hh])
        dxr = dxr + dxr_s[...]
        dbc_ref[...] += _colsum(dxr)
        dwc_ref[0:1, :] += _colsum(dxr * gts["x3"])
        dwc_ref[1:2, :] += _colsum(dxr * gts["x2"])
        dwc_ref[2:3, :] += _colsum(dxr * gts["x1"])
        dwc_ref[3:4, :] += _colsum(dxr * xl)
        nxt = nxt_ref[...]
        nxt_ref[...] = dxr[:HALO, :]
        dxl = wc_ref[3:4, :] * dxr + wc_ref[2:3, :] * _shift_next(dxr, nxt, 1) \
            + wc_ref[1:2, :] * _shift_next(dxr, nxt, 2) + wc_ref[0:1, :] * _shift_next(dxr, nxt, 3)
        dp_ref[:, 2 * GM_W + LRU_W:] = dxl.astype(BF16)

    full = lambda shape: pl.BlockSpec(shape, lambda i: (0,) * len(shape))
    rev = lambda i: nt - 1 - i
    prev_blk = lambda i: jnp.maximum((nt - 1 - i) * hb - 1, 0)
    hhd = (HEADS, HEAD_DIM, HEAD_DIM)
    small_shapes = [(1, GM_W), (1, GM_W), (HEADS, CHUNK, CHUNK), (CHUNK, GM_W), (4, LRU_W), (1, LRU_W), hhd,
                    (1, LRU_W), hhd, (1, LRU_W), (1, LRU_W), (1, GM_W), (1, LRU_W)]
    return _pcall(
        body, name="bwd_mixers", grid=(nt,),
        in_specs=[pl.BlockSpec((ts, 2 * GM_W), lambda i: (rev(i), 0)),
                  pl.BlockSpec((ts, LRU_W), lambda i: (rev(i), 2)),
                  pl.BlockSpec((ts, LRU_W), lambda i: (rev(i), 3)),
                  pl.BlockSpec((HALO, LRU_W), lambda i: (prev_blk(i), 3)),
                  pl.BlockSpec((ts, GM_W + LRU_W), lambda i: (rev(i), 0)),
                  pl.BlockSpec((ts, LRU_W), lambda i: (rev(i), 0)),
                  pl.BlockSpec((HALO, LRU_W), lambda i: (prev_blk(i), 0)),
                  full((1, GM_W)), full((1, GM_W)), full((HEADS, CHUNK, CHUNK)), full((HEADS, CHUNK, CHUNK)),
                  full((CHUNK, GM_W)), full((4, LRU_W)), full((1, LRU_W)), full(hhd), full(hhd), full((1, LRU_W)),
                  full(hhd), full(hhd), full((1, LRU_W)), full((1, LRU_W)), full((1, GM_W)), full((1, LRU_W))],
        out_specs=[pl.BlockSpec((ts, 2 * GM_W + 2 * LRU_W), lambda i: (rev(i), 0))] + [full(s) for s in small_shapes],
        out_shape=[jax.ShapeDtypeStruct((s_len, 2 * GM_W + 2 * LRU_W), BF16)]
        + [jax.ShapeDtypeStruct(s, F32) for s in small_shapes],
        scratch_shapes=[pltpu.VMEM((1, LRU_W), F32), pltpu.VMEM((HALO, LRU_W), F32),
                        pltpu.VMEM((ts, LRU_W), F32), pltpu.VMEM((ts, LRU_W), F32),
                        pltpu.VMEM((ts, GM_W), BF16), pltpu.VMEM((ts, GM_W), F32), pltpu.VMEM((ts, 2 * LRU_W), F32),
                        pltpu.VMEM((ts, GM_W), BF16), pltpu.VMEM((ts, GM_W), F32), pltpu.VMEM((ts, 2 * LRU_W), F32),
                        pltpu.VMEM((ts, LRU_W), F32)],
        dims=("arbitrary",), comm=comm)(p, p, p, p, dy, hs, hs, gv, bv, wt, wtt, bst, wc, bc, wa, wat, ba, wx, wxt, bx, lam,
                             ggm, glru)


def _bwd_weight(name, a, b, *, a_planes, b_planes, shard_rows, comm=None):
    _, s_len, ma = a.shape
    _, _, nb = b.shape
    m, n = a_planes * ma, b_planes * nb
    tk = min(TW_K, s_len)
    if shard_rows:
        rows, cols = m // N_CHIPS, n // 2
        tm, tn = _tile(rows, TW_M), _tile(cols, TW_N)
        out_idx = lambda i, j, k: (j * tn // cols, i * tm // rows, (i * tm % rows) // tm, (j * tn % cols) // tn)
    else:
        rows, cols = m // 2, n // N_CHIPS
        tm, tn = _tile(rows, TW_M), _tile(cols, TW_N)
        out_idx = lambda i, j, k: (i * tm // rows, j * tn // cols, (i * tm % rows) // tm, (j * tn % cols) // tn)
    nk = s_len // tk
    npa, npb = ma // tm, nb // tn

    def body(a_ref, b_ref, o_ref, ob_ref, acc_ref):
        k = pl.program_id(2)

        @pl.when(k == 0)
        def _():
            acc_ref[...] = jnp.zeros_like(acc_ref)

        acc_ref[...] += _dot_tn(a_ref[...], b_ref[...])

        @pl.when(k == nk - 1)
        def _():
            o_ref[...] = acc_ref[...]
            ob_ref[...] = acc_ref[...].astype(BF16)

    shape = (2, N_CHIPS, rows, cols)
    return _pcall(
        body, name=name, grid=(m // tm, n // tn, nk),
        in_specs=[pl.BlockSpec((None, tk, tm), lambda i, j, k: (i // npa, k, i % npa)),
                  pl.BlockSpec((None, tk, tn), lambda i, j, k: (j // npb, k, j % npb))],
        out_specs=[pl.BlockSpec((None, None, tm, tn), out_idx), pl.BlockSpec((None, None, tm, tn), out_idx)],
        out_shape=[jax.ShapeDtypeStruct(shape, F32), jax.ShapeDtypeStruct(shape, BF16)],
        scratch_shapes=[pltpu.VMEM((tm, tn), F32)],
        dims=("arbitrary", "arbitrary", "arbitrary"), comm=comm)(a, b)


def _mesh_pos():
    return lax.axis_index("x"), lax.axis_index("y"), lax.axis_index("c")


def _other_chips(x, y):
    return [(1 - x, y), (x, 1 - y), (1 - x, 1 - y)]


def _to_slot(name, a, cs, dtype):
    _, a_rows, b_cols = a.shape
    ta = min(T_ELEM, a_rows)

    def body(cs_ref, a_ref, o_ref):
        o_ref[...] = a_ref[...].astype(dtype)

    grid_spec = pltpu.PrefetchScalarGridSpec(
        num_scalar_prefetch=1, grid=(2, a_rows // ta),
        in_specs=[pl.BlockSpec((None, ta, b_cols), lambda h, r, cs_ref: (h, r, 0))],
        out_specs=pl.BlockSpec((None, None, ta, b_cols), lambda h, r, cs_ref: (cs_ref[1], h, r, 0)))
    return _pcall(body, name=name, grid_spec=grid_spec,
                  out_shape=jax.ShapeDtypeStruct((N_CHIPS,) + a.shape, dtype), dims=("arbitrary", "arbitrary"))(cs, a)


def _all_gather(bufs):
    n = len(bufs)

    def body(*refs):
        outs = refs[n:2 * n]
        send_sems, recv_sems = refs[2 * n:]
        x, y, c = _mesh_pos()
        s = 2 * x + y
        me, sib = (x, y, c), (x, y, 1 - c)
        chips = _other_chips(x, y)

        def rcopy(a, k, blk, to):
            return pltpu.make_async_remote_copy(src_ref=blk, dst_ref=blk, send_sem=send_sems.at[a * 6 + k],
                                                recv_sem=recv_sems.at[a * 6 + k], device_id=to, device_id_type=MESH)

        first = [rcopy(a, j, outs[a].at[s, c], (cx, cy, c)) for a in range(n) for j, (cx, cy) in enumerate(chips)]
        for cp in first:
            cp.start()
        passed = []
        for a in range(n):
            for j, (cx, cy) in enumerate(chips):
                blk = outs[a].at[2 * cx + cy, c]
                rcopy(a, j, blk, me).wait_recv()
                cp = rcopy(a, 3 + j, blk, sib)
                cp.start()
                passed.append(cp)
        for a in range(n):
            for j, (cx, cy) in enumerate(chips):
                rcopy(a, 3 + j, outs[a].at[2 * cx + cy, 1 - c], me).wait_recv()
        for cp in first + passed:
            cp.wait_send()

    return _pcall(
        body, name="all_gather_weights",
        in_specs=[ANY] * n, out_specs=[ANY] * n,
        out_shape=[jax.ShapeDtypeStruct(a.shape, a.dtype) for a in bufs],
        scratch_shapes=[pltpu.SemaphoreType.DMA((6 * n,)), pltpu.SemaphoreType.DMA((6 * n,))],
        aliases={a: a for a in range(n)})(*bufs)


def _same(a):
    return jax.ShapeDtypeStruct(a.shape, a.dtype)


def _job_gather(bufs, ici_parts, relay_parts):
    def copies(cin, cout, x, y, c):
        out = []
        for a, lo, n in ici_parts:
            blk = cout[a].at[2 * x + y, c, pl.ds(lo, n)]
            out += [(blk, blk, (cx, cy, c)) for cx, cy in _other_chips(x, y)]
        for a, lo, n in relay_parts:
            for cx, cy in _other_chips(x, y):
                blk = cout[a].at[2 * cx + cy, c, pl.ds(lo, n)]
                out.append((blk, blk, (x, y, 1 - c)))
        return out

    return _Job(bufs, [_same(b) for b in bufs], {a: a for a in range(len(bufs))},
                3 * (len(ici_parts) + len(relay_parts)), copies)


def _job_pair_swap(arrs):
    def copies(cin, cout, x, y, c):
        return [(cin[a].at[1 - c], cout[a], (x, y, 1 - c)) for a in range(len(arrs))]

    return _Job(arrs, [jax.ShapeDtypeStruct(a.shape[1:], a.dtype) for a in arrs], {}, len(arrs), copies)


def _job_chip_exchange(big, small=()):
    nb = len(big)
    arrs = list(big) + list(small)

    def copies(cin, cout, x, y, c):
        out = []
        for a in range(len(arrs)):
            for j, (cx, cy) in enumerate(_other_chips(x, y)):
                out.append((cin[a].at[2 * cx + cy] if a < nb else cin[a], cout[a].at[j], (cx, cy, c)))
        return out

    shapes = [a.shape[1:] for a in big] + [a.shape for a in small]
    return _Job(arrs, [jax.ShapeDtypeStruct((3,) + sh, a.dtype) for sh, a in zip(shapes, arrs)], {}, 3 * len(arrs), copies)


def _job_halves_swap(bufs):
    def copies(cin, cout, x, y, c):
        return [(cout[a].at[c], cout[a].at[c], (x, y, 1 - c)) for a in range(len(bufs))]

    return _Job(bufs, [_same(b) for b in bufs], {a: a for a in range(len(bufs))}, len(bufs), copies)


def _pair_sum(name, g32, recv, cs):
    _, nch, a_rows, b_cols = g32.shape
    ta = min(T_ELEM, a_rows)

    def body(cs_ref, g_ref, r_ref, pb_ref, own_ref):
        k = pl.program_id(1)
        v = g_ref[...] + r_ref[...].astype(F32)
        pb_ref[...] = v.astype(BF16)

        @pl.when(k == cs_ref[1])
        def _():
            own_ref[...] = v

    grid_spec = pltpu.PrefetchScalarGridSpec(
        num_scalar_prefetch=1, grid=(a_rows // ta, nch),
        in_specs=[pl.BlockSpec((None, None, ta, b_cols), lambda r, k, cs_ref: (cs_ref[0], k, r, 0)),
                  pl.BlockSpec((None, ta, b_cols), lambda r, k, cs_ref: (k, r, 0))],
        out_specs=[pl.BlockSpec((None, ta, b_cols), lambda r, k, cs_ref: (k, r, 0)),
                   pl.BlockSpec((ta, b_cols), lambda r, k, cs_ref: (r, 0))])
    return _pcall(
        body, name=name, grid_spec=grid_spec,
        out_shape=[jax.ShapeDtypeStruct((nch, a_rows, b_cols), BF16), jax.ShapeDtypeStruct((a_rows, b_cols), F32)],
        dims=("arbitrary", "arbitrary"))(cs, g32, recv)


def _small_pair_sum(mine, recv, cs):
    _, r, ccols = mine.shape

    def body(cs_ref, a_ref, b_ref, o_ref):
        o_ref[...] = a_ref[...] + b_ref[...]

    grid_spec = pltpu.PrefetchScalarGridSpec(
        num_scalar_prefetch=1, grid=(1,),
        in_specs=[pl.BlockSpec((None, r, ccols), lambda i, cs_ref: (cs_ref[0], 0, 0)),
                  pl.BlockSpec((r, ccols), lambda i, cs_ref: (0, 0))],
        out_specs=pl.BlockSpec((r, ccols), lambda i, cs_ref: (0, 0)))
    return _pcall(body, name="small_pair_sum", grid_spec=grid_spec,
                  out_shape=jax.ShapeDtypeStruct((r, ccols), F32), dims=("arbitrary",))(cs, mine, recv)


def _chip_sum(name, own, recv, cs):
    a_rows, b_cols = own.shape
    ta = min(T_ELEM, a_rows)

    def body(cs_ref, o_ref, r_ref, f_ref):
        f_ref[...] = ((o_ref[...] + r_ref[0].astype(F32)) + r_ref[1].astype(F32)) + r_ref[2].astype(F32)

    grid_spec = pltpu.PrefetchScalarGridSpec(
        num_scalar_prefetch=1, grid=(a_rows // ta,),
        in_specs=[pl.BlockSpec((ta, b_cols), lambda r, cs_ref: (r, 0)),
                  pl.BlockSpec((3, ta, b_cols), lambda r, cs_ref: (0, r, 0))],
        out_specs=pl.BlockSpec((None, ta, b_cols), lambda r, cs_ref: (cs_ref[0], r, 0)))
    return _pcall(body, name=name, grid_spec=grid_spec,
                  out_shape=jax.ShapeDtypeStruct((2, a_rows, b_cols), F32), dims=("arbitrary",))(cs, own, recv)


def _small_chip_sum(pair, recv, cs):
    r, ccols = pair.shape

    def body(cs_ref, p_ref, r_ref, o_ref):
        s = cs_ref[1]
        own = p_ref[...]
        total = None
        for k in range(N_CHIPS):
            flip = jnp.bitwise_xor(s, k)
            term = jnp.where(flip == 0, own, jnp.where(flip == 2, r_ref[0], jnp.where(flip == 1, r_ref[1], r_ref[2])))
            total = term if total is None else total + term
        o_ref[...] = total

    grid_spec = pltpu.PrefetchScalarGridSpec(
        num_scalar_prefetch=1, grid=(1,),
        in_specs=[pl.BlockSpec((r, ccols), lambda i, cs_ref: (0, 0)),
                  pl.BlockSpec((3, r, ccols), lambda i, cs_ref: (0, 0, 0))],
        out_specs=pl.BlockSpec((None, r, ccols), lambda i, cs_ref: (cs_ref[0], 0, 0)))
    return _pcall(body, name="small_chip_sum", grid_spec=grid_spec,
                  out_shape=jax.ShapeDtypeStruct((2, r, ccols), F32), dims=("arbitrary",))(cs, pair, recv)


def _adamw(name, w, g, m, v, *, halves, comm=None):
    rows, cols = w.shape
    tr, tc = min(T_ELEM, rows), PACK_COLS
    c1 = 1.0 - ADAM_B1 ** ADAM_STEP
    c2 = 1.0 - ADAM_B2 ** ADAM_STEP

    def body(w_ref, g_ref, m_ref, v_ref, d_ref, mo_ref, vo_ref, go_ref):
        g_ = g_ref[...]
        m_ = ADAM_B1 * m_ref[...] + (1.0 - ADAM_B1) * g_
        v_ = ADAM_B2 * v_ref[...] + (1.0 - ADAM_B2) * (g_ * g_)
        mo_ref[...] = m_
        vo_ref[...] = v_
        go_ref[...] = g_
        d_ref[...] = (-ADAM_LR) * ((m_ / c1) / (jnp.sqrt(v_ / c2) + ADAM_EPS) + ADAM_WD * w_ref[...])

    spec = pl.BlockSpec((tr, tc), lambda r, j: (r, j))
    if halves == "rows":
        nrh = rows // 2 // tr
        g_spec = pl.BlockSpec((None, tr, tc), lambda r, j: (r // nrh, r % nrh, j))
    elif halves == "cols":
        nch = cols // 2 // tc
        g_spec = pl.BlockSpec((None, tr, tc), lambda r, j: (j // nch, r, j % nch))
    else:
        g_spec = spec
    return _pcall(body, name=name, grid=(rows // tr, cols // tc), in_specs=[spec, g_spec, spec, spec],
                  out_specs=[spec] * 4, out_shape=[jax.ShapeDtypeStruct((rows, cols), F32)] * 4,
                  dims=("arbitrary", "arbitrary"), comm=comm)(w, g, m, v)


def _pack(parts, rows):
    flat = jnp.concatenate([a.reshape(-1) for a in parts])
    return jnp.pad(flat, (0, rows * PACK_COLS - flat.shape[0])).reshape(rows, PACK_COLS)


def _unpack(buf, shapes):
    flat = buf.reshape(-1)
    out, off = [], 0
    for sh in shapes:
        size = math.prod(sh)
        out.append(flat[off:off + size].reshape(sh))
        off += size
    return out


def _pack_rows(shapes, multiple):
    total = sum(math.prod(sh) for sh in shapes)
    rows = -(-total // PACK_COLS)
    return -(-rows // multiple) * multiple


SMALL = ["norm1_g", "gm_v_g", "gm_v_b", "gm_ws", "gm_bs", "lru_conv_w", "lru_conv_b", "lru_wa", "lru_ba", "lru_wx",
         "lru_bx", "lru_lambda", "gm_out_g", "lru_out_g", "norm2_g", "ffn_conv_w", "ffn_conv_b", "final_g"]
BIG = ["w_in", "w_out", "ffn_w_up", "ffn_w_down"]
ORDER = ["norm1_g", "w_in", "gm_v_g", "gm_v_b", "gm_ws", "gm_bs", "lru_conv_w", "lru_conv_b", "lru_wa", "lru_ba",
         "lru_wx", "lru_bx", "lru_lambda", "gm_out_g", "lru_out_g", "w_out", "norm2_g", "ffn_w_up", "ffn_conv_w",
         "ffn_conv_b", "ffn_w_down", "final_g"]


def kernel(x, norm1_g, w_in, gm_v_g, gm_v_b, gm_ws, gm_bs, lru_conv_w, lru_conv_b, lru_wa, lru_ba, lru_wx, lru_bx, lru_lambda, gm_out_g, lru_out_g, w_out, norm2_g, ffn_w_up, ffn_conv_w, ffn_conv_b, ffn_w_down, final_g, loss_target, m_norm1_g, m_w_in, m_gm_v_g, m_gm_v_b, m_gm_ws, m_gm_bs, m_lru_conv_w, m_lru_conv_b, m_lru_wa, m_lru_ba, m_lru_wx, m_lru_bx, m_lru_lambda, m_gm_out_g, m_lru_out_g, m_w_out, m_norm2_g, m_ffn_w_up, m_ffn_conv_w, m_ffn_conv_b, m_ffn_w_down, m_final_g, v_norm1_g, v_w_in, v_gm_v_g, v_gm_v_b, v_gm_ws, v_gm_bs, v_lru_conv_w, v_lru_conv_b, v_lru_wa, v_lru_ba, v_lru_wx, v_lru_bx, v_lru_lambda, v_gm_out_g, v_lru_out_g, v_w_out, v_norm2_g, v_ffn_w_up, v_ffn_conv_w, v_ffn_conv_b, v_ffn_w_down, v_final_g):
    w = dict(norm1_g=norm1_g, w_in=w_in, gm_v_g=gm_v_g, gm_v_b=gm_v_b, gm_ws=gm_ws, gm_bs=gm_bs, lru_conv_w=lru_conv_w, lru_conv_b=lru_conv_b, lru_wa=lru_wa, lru_ba=lru_ba, lru_wx=lru_wx, lru_bx=lru_bx, lru_lambda=lru_lambda, gm_out_g=gm_out_g, lru_out_g=lru_out_g, w_out=w_out, norm2_g=norm2_g, ffn_w_up=ffn_w_up, ffn_conv_w=ffn_conv_w, ffn_conv_b=ffn_conv_b, ffn_w_down=ffn_w_down, final_g=final_g)
    m = dict(norm1_g=m_norm1_g, w_in=m_w_in, gm_v_g=m_gm_v_g, gm_v_b=m_gm_v_b, gm_ws=m_gm_ws, gm_bs=m_gm_bs, lru_conv_w=m_lru_conv_w, lru_conv_b=m_lru_conv_b, lru_wa=m_lru_wa, lru_ba=m_lru_ba, lru_wx=m_lru_wx, lru_bx=m_lru_bx, lru_lambda=m_lru_lambda, gm_out_g=m_gm_out_g, lru_out_g=m_lru_out_g, w_out=m_w_out, norm2_g=m_norm2_g, ffn_w_up=m_ffn_w_up, ffn_conv_w=m_ffn_conv_w, ffn_conv_b=m_ffn_conv_b, ffn_w_down=m_ffn_w_down, final_g=m_final_g)
    v = dict(norm1_g=v_norm1_g, w_in=v_w_in, gm_v_g=v_gm_v_g, gm_v_b=v_gm_v_b, gm_ws=v_gm_ws, gm_bs=v_gm_bs, lru_conv_w=v_lru_conv_w, lru_conv_b=v_lru_conv_b, lru_wa=v_lru_wa, lru_ba=v_lru_ba, lru_wx=v_lru_wx, lru_bx=v_lru_bx, lru_lambda=v_lru_lambda, gm_out_g=v_gm_out_g, lru_out_g=v_lru_out_g, w_out=v_w_out, norm2_g=v_norm2_g, ffn_w_up=v_ffn_w_up, ffn_conv_w=v_ffn_conv_w, ffn_conv_b=v_ffn_conv_b, ffn_w_down=v_ffn_w_down, final_g=v_final_g)

    mx, my, mc = _mesh_pos()
    shard = 2 * mx + my
    cs = jnp.stack([mc, shard]).astype(jnp.int32)

    xs = x[0]
    tgt = loss_target[0]
    s_len, d = xs.shape

    halves = lambda a: a.reshape((2, a.shape[0] // 2) + a.shape[1:])
    slot = {k: _to_slot("slot_" + k, halves(w[k][0]), cs, BF16) for k in BIG}
    win_b, wc_b, wfc_b = _all_gather([slot["w_in"],
                                      _to_slot("slot_lru_conv_w", w["lru_conv_w"][0].reshape(2, 4, -1), cs, F32),
                                      _to_slot("slot_ffn_conv_w", w["ffn_conv_w"][0].reshape(2, 12, -1), cs, F32)])
    win_g = win_b.reshape(N_CHIPS, d, -1)
    wc = wc_b.reshape(N_CHIPS, 4, -1).transpose(1, 0, 2).reshape(4, -1)
    wfc = wfc_b.reshape(N_CHIPS, 3, -1).transpose(1, 0, 2).reshape(3, -1)
    wout_b, wup_b, wdown_b = slot["w_out"], slot["ffn_w_up"], slot["ffn_w_down"]
    r_out, r_up, r_dn = wout_b.shape[2], wup_b.shape[2] // 2, wdown_b.shape[2] // 2

    tril = jnp.tril(jnp.ones((CHUNK, CHUNK), bool))
    wt32 = jnp.where(tril[None], w["gm_ws"][0], 0.0)
    wt = wt32.astype(BF16)
    wtt = wt32.transpose(0, 2, 1).astype(BF16)
    bst = jnp.repeat(w["gm_bs"][0].T, HEAD_DIM, axis=1)
    wa = w["lru_wa"][0].astype(BF16)
    wx = w["lru_wx"][0].astype(BF16)
    wat = w["lru_wa"][0].transpose(0, 2, 1).astype(BF16)
    wxt = w["lru_wx"][0].transpose(0, 2, 1).astype(BF16)
    ba = w["lru_ba"][0].reshape(1, -1)
    bx = w["lru_bx"][0].reshape(1, -1)
    gf = w["final_g"].reshape(1, -1)

    p, h1, wout_b, wup_b = _fwd_in_proj(
        xs, w["norm1_g"], win_g, comm=_job_gather([wout_b, wup_b], [(0, 0, r_out), (1, 0, r_up)], []))
    y, hs, wout_b, wup_b, wdown_b = _fwd_mixers(
        p, w["gm_v_g"], w["gm_v_b"], wt, bst, wc, w["lru_conv_b"], wa, ba, wx, bx, w["lru_lambda"],
        w["gm_out_g"], w["lru_out_g"],
        comm=_job_gather([wout_b, wup_b, wdown_b], [(1, r_up, r_up), (2, 0, r_dn)], [(0, 0, r_out), (1, 0, r_up)]))
    wout_g = wout_b.reshape(-1, d)
    x2, h2, wup_b, wdown_b = _fwd_out_proj(
        xs, y, wout_g, w["norm2_g"], comm=_job_gather([wup_b, wdown_b], [(1, r_dn, r_dn)], [(0, r_up, r_up), (1, 0, r_dn)]))
    wdown_b, = _comm_call("gather_tail", _job_gather([wdown_b], [], [(0, r_dn, r_dn)]))
    wup_g = wup_b.reshape(N_CHIPS, d, -1)
    wdown_g = wdown_b.reshape(-1, d)
    up, act, dx3, dx3b, loss_tile, dgf = _fwd_ffn(h2, wup_g, wfc, w["ffn_conv_b"], wdown_g, x2, gf, tgt)
    loss = lax.psum(loss_tile[0, 0], ("x", "y", "c"))

    adam = {}

    def adamw_big(name, gfull, hv, comm=None):
        r2 = lambda a: a.reshape(w[name].shape[1:])
        res = _adamw("adamw_" + name, r2(w[name]), gfull, r2(m[name]), r2(v[name]), halves=hv, comm=comm)
        adam[name] = res[:4]
        return res[4:]

    gd32, gdb = _bwd_weight("bwd_w_down", act[None], dx3b[None], a_planes=1, b_planes=1, shard_rows=True)
    dup, dwf, dx2, dx2b, dg2, rcv = _bwd_ffn(dx3b, wdown_g, up, wfc, w["ffn_conv_b"], wup_g, dx3, x2, w["norm2_g"],
                                             comm=_job_pair_swap([gdb]))
    pb_dn, own_dn = _pair_sum("pair_sum_3", gd32, rcv, cs)
    gu32, gub, got = _bwd_weight("bwd_w_up", h2[None], dup, a_planes=1, b_planes=2, shard_rows=False,
                                 comm=_job_chip_exchange([pb_dn]))
    red_dn = _chip_sum("chip_sum_3", own_dn, got, cs)
    go32, gob, rcv, red_dn = _bwd_weight("bwd_w_out", y[None], dx2b[None], a_planes=1, b_planes=1, shard_rows=True,
                                         comm=_merge_jobs([_job_pair_swap([gub]), _job_halves_swap([red_dn])]))
    pb_up, own_up = _pair_sum("pair_sum_2", gu32, rcv, cs)
    adamw_big("ffn_w_down", red_dn, "cols")
    dy, rcv = _bwd_dy(dx2b, wout_g, comm=_job_pair_swap([gob]))
    pb_out, own_out = _pair_sum("pair_sum_1", go32, rcv, cs)
    (dp, dgv, dbv, dwt, dbst, dwc, dbc, dwa, dba, dwx, dbx, dsp, dggm, dglru, got_up, got_out) = _bwd_mixers(
        p, dy, hs, w["gm_v_g"], w["gm_v_b"], wt, wtt, bst, wc, w["lru_conv_b"], wa, wat, ba, wx, wxt, bx,
        w["lru_lambda"], w["gm_out_g"], w["lru_out_g"], comm=_job_chip_exchange([pb_up, pb_out]))
    red_up = _chip_sum("chip_sum_2", own_up, got_up, cs)
    red_out = _chip_sum("chip_sum_1", own_out, got_out, cs)
    gi32, gib, red_up, red_out = _bwd_weight("bwd_w_in", h1[None], dp[None], a_planes=1, b_planes=1, shard_rows=False,
                                             comm=_job_halves_swap([red_up, red_out]))
    grad_x, dg1, rcv = _bwd_in(dp, win_g, dx2, xs, w["norm1_g"], comm=_job_pair_swap([gib]))
    pb_in, own_in = _pair_sum("pair_sum_0", gi32, rcv, cs)

    dwfc = dwf[:, :, :3].transpose(2, 0, 1, 3).reshape(3, -1)
    dbfc = dwf[:, :, 3].reshape(1, -1)
    dlam = dsp * (-_sigmoid(-w["lru_lambda"]))
    small_grads = dict(
        norm1_g=dg1, gm_v_g=dgv, gm_v_b=dbv, gm_ws=jnp.where(tril[None], dwt, 0.0),
        gm_bs=dbst.reshape(CHUNK, HEADS, HEAD_DIM).sum(-1).T, lru_conv_w=dwc, lru_conv_b=dbc, lru_wa=dwa, lru_ba=dba,
        lru_wx=dwx, lru_bx=dbx, lru_lambda=dlam, gm_out_g=dggm, lru_out_g=dglru, norm2_g=dg2, ffn_conv_w=dwfc,
        ffn_conv_b=dbfc, final_g=dgf)
    full_shapes = [small_grads[k].shape for k in SMALL]
    rows_full = _pack_rows(full_shapes, 16)
    gpack = _pack([small_grads[k] for k in SMALL], rows_full).reshape(2, rows_full // 2, PACK_COLS)

    rcv, = _comm_call("small_swap", _job_pair_swap([gpack]))
    small_pair = _small_pair_sum(gpack, rcv, cs)
    got_in, got_small = adamw_big("ffn_w_up", red_up, "rows", comm=_job_chip_exchange([pb_in], [small_pair]))
    red_in = _chip_sum("chip_sum_0", own_in, got_in, cs)
    small_half = _small_chip_sum(small_pair, got_small, cs)
    red_in, small_full = adamw_big("w_out", red_out, "cols", comm=_job_halves_swap([red_in, small_half]))
    adamw_big("w_in", red_in, "rows")

    grads = {}
    for name, g in zip(SMALL, _unpack(small_full, full_shapes)):
        blk = w[name].shape[1:] if w[name].ndim > 1 else w[name].shape
        if name in ("lru_conv_w", "ffn_conv_w"):
            g = lax.dynamic_slice_in_dim(g, shard * blk[1], blk[1], axis=1)
        grads[name] = g.reshape(blk)

    delta, new_m, new_v = {}, {}, {}
    for name in BIG:
        delta[name], new_m[name], new_v[name], grads[name] = adam[name]
    blk_shapes = [grads[k].shape for k in SMALL]
    rows_blk = _pack_rows(blk_shapes, T_ELEM)
    packs = [_pack([src[k] for k in SMALL], rows_blk) for src in (w, grads, m, v)]
    outs = _adamw("adamw_small", *packs, halves=None)
    for dst, buf in zip((delta, new_m, new_v), outs):
        for name, a in zip(SMALL, _unpack(buf, blk_shapes)):
            dst[name] = a

    def shaped(dct):
        return [dct[k].reshape(w[k].shape) for k in ORDER]

    return (loss, grad_x[None], *shaped(grads), *shaped(delta), *shaped(new_m), *shaped(new_v))
```

```python
import functools
import math

import jax
import jax.numpy as jnp
from jax import lax
from jax.experimental import pallas as pl
from jax.experimental.pallas import tpu as pltpu

F32 = jnp.float32
BF16 = jnp.bfloat16
MESH = pl.DeviceIdType.MESH
ANY = pl.BlockSpec(memory_space=pltpu.HBM)

GM_W = 1024
LRU_W = 1024
CHUNK = 128
HEADS = 8
HEAD_DIM = 128
LRU_C = 8.0
RMS_EPS = 1e-6
LN_EPS = 1e-5
ADAM_LR = 0.001
ADAM_B1 = 0.9
ADAM_B2 = 0.999
ADAM_EPS = 1e-08
ADAM_WD = 0.01
ADAM_STEP = 10

N_CHIPS = 4
HALO = 8
PACK_COLS = 1024
VMEM_LIMIT = 56 * 1024 * 1024

TM_IN = 1024
TS_MIX = 256
TM_OUT = 512
TM_UP = 512
TN_UP = 512
T_EPI = 128
T_CHUNK = 16
TM_BW = 512
TW_M = 1024
TW_N = 1024
TW_K = 2048
T_ELEM = 256
LANES = 128


def _tile(dim, cap):
    t = min(cap, dim) // LANES * LANES
    while dim % t:
        t -= LANES
    return t

_GELU_K0 = 0.7978845608028654
_GELU_K1 = 0.044715


class _Job:
    def __init__(self, ins, out_shapes, aliases, n, copies):
        self.ins, self.out_shapes, self.aliases, self.n, self.copies = list(ins), list(out_shapes), dict(aliases), n, copies

    def make(self, cin, cout, send_sems, recv_sems):
        x, y, c = _mesh_pos()
        return [pltpu.make_async_remote_copy(src_ref=src, dst_ref=dst, send_sem=send_sems.at[k], recv_sem=recv_sems.at[k],
                                             device_id=dev, device_id_type=MESH)
                for k, (src, dst, dev) in enumerate(self.copies(cin, cout, x, y, c))]


def _merge_jobs(jobs):
    ins, outs, aliases, spans = [], [], {}, []
    for jb in jobs:
        spans.append((len(ins), len(jb.ins), len(outs), len(jb.out_shapes)))
        aliases.update({len(ins) + a: len(outs) + b for a, b in jb.aliases.items()})
        ins += jb.ins
        outs += jb.out_shapes

    def copies(cin, cout, x, y, c):
        out = []
        for jb, (i0, ni, o0, no) in zip(jobs, spans):
            out += jb.copies(cin[i0:i0 + ni], cout[o0:o0 + no], x, y, c)
        return out

    return _Job(ins, outs, aliases, sum(jb.n for jb in jobs), copies)


def _pcall(body, *, name, out_shape, grid=None, in_specs=None, out_specs=None, scratch_shapes=(),
           grid_spec=None, dims=None, aliases=None, comm=None):
    params = pltpu.CompilerParams(dimension_semantics=dims, vmem_limit_bytes=VMEM_LIMIT)
    kw = dict(name=name, compiler_params=params)
    if grid_spec is not None:
        if aliases:
            kw["input_output_aliases"] = aliases
        return pl.pallas_call(body, grid_spec=grid_spec, out_shape=out_shape, **kw)
    scratch_shapes = list(scratch_shapes)
    if comm is not None:
        out_shape = list(out_shape) if isinstance(out_shape, (list, tuple)) else [out_shape]
        out_specs = list(out_specs) if isinstance(out_specs, (list, tuple)) else [out_specs]
        n_in, n_out, n_scr = len(in_specs), len(out_shape), len(scratch_shapes)
        n_ci, n_co = len(comm.ins), len(comm.out_shapes)
        inner, steps = body, tuple(grid)

        def body(*refs):
            o0 = n_in + n_ci
            s0 = o0 + n_out + n_co
            cps = comm.make(refs[n_in:o0], refs[o0 + n_out:s0], refs[s0 + n_scr], refs[s0 + n_scr + 1])
            ids = [pl.program_id(k) for k in range(len(steps))]

            @pl.when(functools.reduce(jnp.logical_and, [i == 0 for i in ids]))
            def _():
                for cp in cps:
                    cp.start()

            inner(*refs[:n_in], *refs[o0:o0 + n_out], *refs[s0:s0 + n_scr])

            @pl.when(functools.reduce(jnp.logical_and, [i == n - 1 for i, n in zip(ids, steps)]))
            def _():
                for cp in cps:
                    cp.wait()

        in_specs = list(in_specs) + [ANY] * n_ci
        out_specs = out_specs + [ANY] * n_co
        out_shape = out_shape + comm.out_shapes
        scratch_shapes = scratch_shapes + [pltpu.SemaphoreType.DMA((comm.n,)), pltpu.SemaphoreType.DMA((comm.n,))]
        aliases = {**(aliases or {}), **{n_in + a: n_out + b for a, b in comm.aliases.items()}}
    if aliases:
        kw["input_output_aliases"] = aliases
    if grid is not None:
        kw["grid"] = grid
    call = pl.pallas_call(body, in_specs=in_specs, out_specs=out_specs, scratch_shapes=scratch_shapes,
                          out_shape=out_shape, **kw)
    if comm is None:
        return call
    return lambda *args: call(*args, *comm.ins)


def _comm_call(name, job):
    n_ci, n_co = len(job.ins), len(job.out_shapes)

    def body(*refs):
        cps = job.make(refs[:n_ci], refs[n_ci:n_ci + n_co], refs[n_ci + n_co], refs[n_ci + n_co + 1])
        for cp in cps:
            cp.start()
        for cp in cps:
            cp.wait()

    return pl.pallas_call(
        body, name=name, in_specs=[ANY] * n_ci, out_specs=[ANY] * n_co, out_shape=job.out_shapes,
        scratch_shapes=[pltpu.SemaphoreType.DMA((job.n,)), pltpu.SemaphoreType.DMA((job.n,))],
        input_output_aliases=job.aliases)(*job.ins)


def _gelu(x):
    t = jnp.tanh(_GELU_K0 * (x + _GELU_K1 * (x * x * x)))
    return 0.5 * x * (1.0 + t)


def _gelu_and_grad(x):
    x2 = x * x
    t = jnp.tanh(_GELU_K0 * (x + _GELU_K1 * (x2 * x)))
    g = 0.5 * x * (1.0 + t)
    dg = 0.5 * (1.0 + t) + 0.5 * x * (1.0 - t * t) * (_GELU_K0 * (1.0 + 3.0 * _GELU_K1 * x2))
    return g, dg


def _sigmoid(x):
    return 1.0 / (1.0 + jnp.exp(-x))


def _neg_expm1(x):
    series = -x * (1.0 + x * (0.5 + x * (1.0 / 6.0 + x * (1.0 / 24.0 + x * (1.0 / 120.0 + x * (1.0 / 720.0))))))
    return jnp.where(x > -0.1, series, 1.0 - jnp.exp(x))


def _softplus(z):
    return jnp.maximum(z, 0.0) + jnp.log(1.0 + jnp.exp(-jnp.abs(z)))


def _rowmean(x):
    return jnp.mean(x, axis=-1, keepdims=True)


def _colsum(x):
    return jnp.sum(x, axis=0, keepdims=True)


def _rms_stats(x):
    r = lax.rsqrt(_rowmean(x * x) + RMS_EPS)
    return r, x * r


def _rms_bwd(dy, n, r, g):
    dn = dy * g
    return r * (dn - n * _rowmean(dn * n)), dy * n


def _shift_prev(x, halo, d):
    cat = jnp.concatenate([halo, x], axis=0)
    return pltpu.roll(cat, d, 0)[HALO:, :]


def _prev_rows(cat):
    return cat[HALO:, :], pltpu.roll(cat, 1, 0)[HALO:, :], pltpu.roll(cat, 2, 0)[HALO:, :]


def _next_rows(cat):
    n = cat.shape[0]
    return cat[:n - HALO, :], pltpu.roll(cat, n - 1, 0)[:n - HALO, :], pltpu.roll(cat, n - 2, 0)[:n - HALO, :]


def _shift_next(x, halo, d):
    n = x.shape[0]
    cat = jnp.concatenate([x, halo], axis=0)
    return pltpu.roll(cat, n + HALO - d, 0)[:n, :]


def _dot(a, b):
    return jnp.dot(a, b, preferred_element_type=F32)


def _dot_nt(a, b):
    return lax.dot_general(a, b, (((1,), (1,)), ((), ())), preferred_element_type=F32)


def _dot_tn(a, b):
    return lax.dot_general(a, b, (((0,), (0,)), ((), ())), preferred_element_type=F32)


def _fwd_in_proj(x, g1, win_g, comm=None):
    s_len, d = x.shape
    nsh, _, ncol = win_g.shape
    tm = min(TM_IN, s_len)
    te = min(T_EPI, tm)

    def body(x_ref, g_ref, w_ref, p_ref, h_ref):
        @pl.when(pl.program_id(1) == 0)
        def _():
            g = g_ref[...]

            def chunk(q, carry):
                rows = pl.ds(pl.multiple_of(q * te, te), te)
                _, n = _rms_stats(x_ref[rows, :])
                h_ref[rows, :] = (n * g).astype(BF16)
                return carry

            lax.fori_loop(0, tm // te, chunk, 0)

        p_ref[...] = _dot(h_ref[...], w_ref[...])

    return _pcall(
        body, name="fwd_in_proj", grid=(s_len // tm, nsh),
        in_specs=[pl.BlockSpec((tm, d), lambda i, j: (i, 0)),
                  pl.BlockSpec((1, d), lambda i, j: (0, 0)),
                  pl.BlockSpec((None, d, ncol), lambda i, j: (j, 0, 0))],
        out_specs=[pl.BlockSpec((tm, ncol), lambda i, j: (i, j)),
                   pl.BlockSpec((tm, d), lambda i, j: (i, 0))],
        out_shape=[jax.ShapeDtypeStruct((s_len, nsh * ncol), F32), jax.ShapeDtypeStruct((s_len, d), BF16)],
        dims=("arbitrary", "arbitrary"), comm=comm)(x, g1, win_g)


def _gm_forward(z, gv, bv, wt_ref, bst_ref, vl_s, mix_s):
    ts = z.shape[0]
    ge = _gelu(z)
    u = ge[:, :GM_W]
    v = ge[:, GM_W:]
    vc = v - _rowmean(v)
    rs = lax.rsqrt(_rowmean(vc * vc) + LN_EPS)
    vh = vc * rs
    vl_s[...] = (vh * gv + bv).astype(BF16)
    for cc in range(ts // CHUNK):
        rows = slice(cc * CHUNK, (cc + 1) * CHUNK)
        for hh in range(HEADS):
            cols = slice(hh * HEAD_DIM, (hh + 1) * HEAD_DIM)
            mix_s[rows, cols] = _dot(wt_ref[hh], vl_s[rows, cols]) + bst_ref[:, cols]
    mixed = mix_s[...]
    return u, mixed, vh, rs, u * mixed


def _lru_gates(xl, halo, wc_ref, bc_ref, wa_ref, ba_ref, wx_ref, bx_ref, lam_ref, z_s):
    x1 = _shift_prev(xl, halo, 1)
    x2 = _shift_prev(xl, halo, 2)
    x3 = _shift_prev(xl, halo, 3)
    xr = bc_ref[...] + wc_ref[0:1, :] * x3 + wc_ref[1:2, :] * x2 + wc_ref[2:3, :] * x1 + wc_ref[3:4, :] * xl
    xrb = xr.astype(BF16)
    for hh in range(HEADS):
        cols = slice(hh * HEAD_DIM, (hh + 1) * HEAD_DIM)
        z_s[:, cols] = _dot(xrb[:, cols], wa_ref[hh])
        z_s[:, LRU_W + hh * HEAD_DIM:LRU_W + (hh + 1) * HEAD_DIM] = _dot(xrb[:, cols], wx_ref[hh])
    ra = _sigmoid(z_s[:, :LRU_W] + ba_ref[...])
    ri = _sigmoid(z_s[:, LRU_W:] + bx_ref[...])
    sp = _softplus(-lam_ref[...])
    la = (-LRU_C) * ra * sp
    a = jnp.exp(la)
    mult = jnp.sqrt(_neg_expm1(2.0 * la))
    return dict(x1=x1, x2=x2, x3=x3, xr=xr, xrb=xrb, ra=ra, ri=ri, sp=sp, a=a, mult=mult)


def _fwd_mixers(p, gv, bv, wt, bst, wc, bc, wa, ba, wx, bx, lam, ggm, glru, comm=None):
    s_len = p.shape[0]
    ts = min(TS_MIX, s_len)

    def body(pz_ref, pgl_ref, pxl_ref, gv_ref, bv_ref, wt_ref, bst_ref, wc_ref, bc_ref, wa_ref, ba_ref, wx_ref,
             bx_ref, lam_ref, ggm_ref, glru_ref, y_ref, hs_ref, tail_ref, h_ref, a_s, b_s, vl_s, mix_s, z_s):
        @pl.when(pl.program_id(0) == 0)
        def _():
            tail_ref[...] = jnp.zeros_like(tail_ref)
            h_ref[...] = jnp.zeros_like(h_ref)

        _, _, _, _, ygm = _gm_forward(pz_ref[...], gv_ref[...], bv_ref[...], wt_ref, bst_ref, vl_s, mix_s)
        _, ngm = _rms_stats(ygm)
        y_ref[:, :GM_W] = (ngm * ggm_ref[...]).astype(BF16)

        xl = pxl_ref[...]
        gts = _lru_gates(xl, tail_ref[...], wc_ref, bc_ref, wa_ref, ba_ref, wx_ref, bx_ref, lam_ref, z_s)
        tail_ref[...] = xl[ts - HALO:, :]
        a_s[...] = gts["a"]
        b_s[...] = gts["mult"] * (gts["ri"] * gts["xr"])

        def step(t, h):
            h = a_s[pl.ds(t, 1), :] * h + b_s[pl.ds(t, 1), :]
            hs_ref[pl.ds(t, 1), :] = h
            return h

        h_ref[...] = lax.fori_loop(0, ts, step, h_ref[...], unroll=8)
        yl = hs_ref[...] * _gelu(pgl_ref[...])
        _, nl = _rms_stats(yl)
        y_ref[:, GM_W:] = (nl * glru_ref[...]).astype(BF16)

    full = lambda shape: pl.BlockSpec(shape, lambda i: (0,) * len(shape))
    return _pcall(
        body, name="fwd_mixers", grid=(s_len // ts,),
        in_specs=[pl.BlockSpec((ts, 2 * GM_W), lambda i: (i, 0)),
                  pl.BlockSpec((ts, LRU_W), lambda i: (i, 2)),
                  pl.BlockSpec((ts, LRU_W), lambda i: (i, 3)),
                  full((1, GM_W)), full((1, GM_W)), full((HEADS, CHUNK, CHUNK)), full((CHUNK, GM_W)),
                  full((4, LRU_W)), full((1, LRU_W)), full((HEADS, HEAD_DIM, HEAD_DIM)), full((1, LRU_W)),
                  full((HEADS, HEAD_DIM, HEAD_DIM)), full((1, LRU_W)), full((1, LRU_W)), full((1, GM_W)),
                  full((1, LRU_W))],
        out_specs=[pl.BlockSpec((ts, GM_W + LRU_W), lambda i: (i, 0)), pl.BlockSpec((ts, LRU_W), lambda i: (i, 0))],
        out_shape=[jax.ShapeDtypeStruct((s_len, GM_W + LRU_W), BF16), jax.ShapeDtypeStruct((s_len, LRU_W), F32)],
        scratch_shapes=[pltpu.VMEM((HALO, LRU_W), F32), pltpu.VMEM((1, LRU_W), F32),
                        pltpu.VMEM((ts, LRU_W), F32), pltpu.VMEM((ts, LRU_W), F32),
                        pltpu.VMEM((ts, GM_W), BF16), pltpu.VMEM((ts, GM_W), F32), pltpu.VMEM((ts, 2 * LRU_W), F32)],
        dims=("arbitrary",), comm=comm)(p, p, p, gv, bv, wt, bst, wc, bc, wa, ba, wx, bx, lam, ggm, glru)


def _fwd_out_proj(x, y, wout_g, g2, comm=None):
    s_len, d = x.shape
    tm = min(TM_OUT, s_len)

    te = min(T_EPI, tm)

    def body(x_ref, y_ref, w_ref, g_ref, x2_ref, h2_ref):
        x2_ref[...] = _dot(y_ref[...], w_ref[...])
        g = g_ref[...]

        def chunk(q, carry):
            rows = pl.ds(pl.multiple_of(q * te, te), te)
            x2 = x_ref[rows, :] + x2_ref[rows, :]
            x2_ref[rows, :] = x2
            _, n = _rms_stats(x2)
            h2_ref[rows, :] = (n * g).astype(BF16)
            return carry

        lax.fori_loop(0, tm // te, chunk, 0)

    return _pcall(
        body, name="fwd_out_proj", grid=(s_len // tm,),
        in_specs=[pl.BlockSpec((tm, d), lambda i: (i, 0)), pl.BlockSpec((tm, d), lambda i: (i, 0)),
                  pl.BlockSpec((d, d), lambda i: (0, 0)), pl.BlockSpec((1, d), lambda i: (0, 0))],
        out_specs=[pl.BlockSpec((tm, d), lambda i: (i, 0)), pl.BlockSpec((tm, d), lambda i: (i, 0))],
        out_shape=[jax.ShapeDtypeStruct((s_len, d), F32), jax.ShapeDtypeStruct((s_len, d), BF16)],
        dims=("arbitrary",), comm=comm)(x, y, wout_g, g2)


def _row_fetch(hbm_ref, buf_ref, sem, row0, rows):
    return pltpu.make_async_copy(hbm_ref.at[pl.ds(row0, rows), :], buf_ref, sem)


def _fwd_ffn(h2, wup_i, wfc, bfc, wdown_g, x2, gf, target):
    s_len, d = h2.shape
    nj, _, tn2 = wup_i.shape
    tn = tn2 // 2
    f = nj * tn
    tm = min(TM_UP, s_len)
    te = min(T_EPI, tm)
    tr = T_CHUNK

    def body(h_ref, w_ref, wcg_ref, wcv_ref, bg_ref, bv_ref, wd_ref, g_ref, x2_hbm, t_hbm,
             upb_ref, cb_ref, act_ref, dx3_ref, dx3b_ref, loss_ref, dgf_ref, tail_ref, acc_ref, x2_buf, t_buf, up_ref,
             sems):
        i, j = pl.program_id(0), pl.program_id(1)
        row0 = pl.multiple_of(i * tm, tm)
        fetches = (_row_fetch(x2_hbm, x2_buf, sems.at[0], row0, tm), _row_fetch(t_hbm, t_buf, sems.at[1], row0, tm))

        @pl.when(jnp.logical_and(i == 0, j == 0))
        def _():
            tail_ref[...] = jnp.zeros_like(tail_ref)
            loss_ref[...] = jnp.zeros_like(loss_ref)
            dgf_ref[...] = jnp.zeros_like(dgf_ref)

        @pl.when(j == 0)
        def _():
            acc_ref[...] = jnp.zeros_like(acc_ref)
            for cp in fetches:
                cp.start()

        up_ref[...] = _dot(h_ref[...], w_ref[...])

        planes = ((wcg_ref, bg_ref), (wcv_ref, bv_ref))
        for c in range(tn // LANES):
            cols = slice(c * LANES, (c + 1) * LANES)
            pcols = [slice(pln * tn + c * LANES, pln * tn + (c + 1) * LANES) for pln in range(2)]

            def conv(pln, u, u1, u2):
                wc_ref, b_ref = planes[pln]
                return b_ref[:, cols] + wc_ref[0:1, cols] * u2 + wc_ref[1:2, cols] * u1 + wc_ref[2:3, cols] * u

            def emit(rows, us):
                cs_ = [conv(pln, *us[pln]) for pln in range(2)]
                for pln in range(2):
                    upb_ref[rows, pcols[pln]] = us[pln][0].astype(BF16)
                    cb_ref[rows, pcols[pln]] = cs_[pln].astype(BF16)
                act_ref[rows, cols] = (_gelu(cs_[0]) * cs_[1]).astype(BF16)

            first = [_prev_rows(jnp.concatenate([tail_ref[j, :, pc], up_ref[0:tr, pc]], axis=0)) for pc in pcols]
            emit(slice(0, tr), first)

            for k in range(1, tm // tr):
                r0 = k * tr
                us = [_prev_rows(up_ref[r0 - HALO:r0 + tr, pc]) for pc in pcols]
                emit(slice(r0, r0 + tr), us)
        tail_ref[j] = up_ref[tm - HALO:tm, :]
        acc_ref[...] += _dot(act_ref[...], wd_ref[...])

        @pl.when(j == nj - 1)
        def _():
            for cp in fetches:
                cp.wait()
            g = g_ref[...]

            def chunk(k, carry):
                rows = pl.ds(pl.multiple_of(k * te, te), te)
                x3 = x2_buf[rows, :] + acc_ref[rows, :]
                r, n = _rms_stats(x3)
                err = n * g - t_buf[rows, :]
                loss_ref[...] += jnp.sum(err * err) * (0.5 / d)
                dx3, dgn = _rms_bwd(err * (1.0 / d), n, r, g)
                dgf_ref[...] += _colsum(dgn)
                dx3_ref[rows, :] = dx3
                dx3b_ref[rows, :] = dx3.astype(BF16)
                return carry

            lax.fori_loop(0, tm // te, chunk, 0)

    hbm = pl.BlockSpec(memory_space=pl.ANY)
    return _pcall(
        body, name="fwd_ffn", grid=(s_len // tm, nj),
        in_specs=[pl.BlockSpec((tm, d), lambda i, j: (i, 0)),
                  pl.BlockSpec((None, d, tn2), lambda i, j: (j, 0, 0)),
                  pl.BlockSpec((3, tn), lambda i, j: (0, j)), pl.BlockSpec((3, tn), lambda i, j: (0, nj + j)),
                  pl.BlockSpec((1, tn), lambda i, j: (0, j)), pl.BlockSpec((1, tn), lambda i, j: (0, nj + j)),
                  pl.BlockSpec((tn, d), lambda i, j: (j, 0)), pl.BlockSpec((1, d), lambda i, j: (0, 0)), hbm, hbm],
        out_specs=[pl.BlockSpec((tm, tn2), lambda i, j: (i, j)), pl.BlockSpec((tm, tn2), lambda i, j: (i, j)),
                   pl.BlockSpec((tm, tn), lambda i, j: (i, j)),
                   pl.BlockSpec((tm, d), lambda i, j: (i, 0)), pl.BlockSpec((tm, d), lambda i, j: (i, 0)),
                   pl.BlockSpec((8, 128), lambda i, j: (0, 0)), pl.BlockSpec((1, d), lambda i, j: (0, 0))],
        out_shape=[jax.ShapeDtypeStruct((s_len, 2 * f), BF16), jax.ShapeDtypeStruct((s_len, 2 * f), BF16),
                   jax.ShapeDtypeStruct((s_len, f), BF16),
                   jax.ShapeDtypeStruct((s_len, d), F32), jax.ShapeDtypeStruct((s_len, d), BF16),
                   jax.ShapeDtypeStruct((8, 128), F32), jax.ShapeDtypeStruct((1, d), F32)],
        scratch_shapes=[pltpu.VMEM((nj, HALO, tn2), F32), pltpu.VMEM((tm, d), F32), pltpu.VMEM((tm, d), F32),
                        pltpu.VMEM((tm, d), F32), pltpu.VMEM((tm, tn2), F32), pltpu.SemaphoreType.DMA((2,))],
        dims=("arbitrary", "arbitrary"))(h2, wup_i, wfc, wfc, bfc, bfc, wdown_g, gf, x2, target)


def _bwd_ffn(dx3b, wdown_g, upb, cb, wfc, wup_i, dx3, x2, g2, comm=None):
    s_len, d = dx3b.shape
    nj, _, tn2 = wup_i.shape
    tn = tn2 // 2
    tm = min(TM_UP, s_len)
    te = min(T_EPI, tm)
    nt = s_len // tm
    hb = tm // HALO
    tr = T_CHUNK

    def body(dx_ref, w_ref, upb_ref, cb_ref, wcg_ref, wcv_ref, wu_ref, g_ref, r_hbm, x_hbm,
             dup_ref, dwf_ref, dx2_ref, dx2b_ref, dg_ref, nxt_ref, acc_ref, r_buf, x_buf, da_s, dc_s, sems):
        i, j = pl.program_id(0), pl.program_id(1)
        ti = nt - 1 - i
        row0 = pl.multiple_of(ti * tm, tm)
        fetches = (_row_fetch(r_hbm, r_buf, sems.at[0], row0, tm), _row_fetch(x_hbm, x_buf, sems.at[1], row0, tm))

        @pl.when(jnp.logical_and(i == 0, j == 0))
        def _():
            dwf_ref[...] = jnp.zeros_like(dwf_ref)
            nxt_ref[...] = jnp.zeros_like(nxt_ref)
            dg_ref[...] = jnp.zeros_like(dg_ref)

        @pl.when(j == 0)
        def _():
            acc_ref[...] = jnp.zeros_like(acc_ref)
            for cp in fetches:
                cp.start()

        da_s[...] = _dot_nt(dx_ref[...], w_ref[...])
        planes = ((wcg_ref,), (wcv_ref,))
        for pln in range(2):
            dc_s[pln, tm:tm + HALO, :] = nxt_ref[pln, j]

        for c in range(tn // LANES):
            cols = slice(c * LANES, (c + 1) * LANES)
            pcols = [slice(pln * tn + c * LANES, pln * tn + (c + 1) * LANES) for pln in range(2)]
            for k in range(tm // tr):
                rows = slice(k * tr, (k + 1) * tr)
                dact = da_s[rows, cols]
                ge, gd = _gelu_and_grad(cb_ref[rows, pcols[0]].astype(F32))
                dc_s[0, rows, cols] = dact * cb_ref[rows, pcols[1]].astype(F32) * gd
                dc_s[1, rows, cols] = dact * ge

        for pln in range(2):
            nxt_ref[pln, j] = dc_s[pln, 0:HALO, :]

        def fold(v):
            out = v[0:8, :]
            for q in range(1, tr // 8):
                out = out + v[8 * q:8 * q + 8, :]
            return out

        for c in range(tn // LANES):
            cols = slice(c * LANES, (c + 1) * LANES)
            for pln in range(2):
                wc_ref = planes[pln][0]
                pc = slice(pln * tn + c * LANES, pln * tn + (c + 1) * LANES)
                sums = (jnp.zeros((8, LANES), F32),) * 4
                for k in range(tm // tr):
                    r0 = k * tr
                    dc, dc1, dc2 = _next_rows(dc_s[pln, r0:r0 + tr + HALO, cols])
                    dup = wc_ref[2:3, cols] * dc + wc_ref[1:2, cols] * dc1 + wc_ref[0:1, cols] * dc2
                    dup_ref[r0:r0 + tr, pc] = dup.astype(BF16)
                    u = upb_ref[r0:r0 + tr, pc].astype(F32)
                    new = (fold(dc2 * u), fold(dc1 * u), fold(dc * u), fold(dc))
                    sums = tuple(a + b for a, b in zip(sums, new))
                for term in range(4):
                    dwf_ref[pln, j, term:term + 1, cols] += _colsum(sums[term])
        acc_ref[...] += _dot_nt(dup_ref[...], wu_ref[...])

        @pl.when(j == nj - 1)
        def _():
            for cp in fetches:
                cp.wait()
            g = g_ref[...]

            def chunk(k, carry):
                rows = pl.ds(pl.multiple_of(k * te, te), te)
                r, n = _rms_stats(x_buf[rows, :])
                dxn, dgn = _rms_bwd(acc_ref[rows, :], n, r, g)
                dg_ref[...] += _colsum(dgn)
                dx = r_buf[rows, :] + dxn
                dx2_ref[rows, :] = dx
                dx2b_ref[rows, :] = dx.astype(BF16)
                return carry

            lax.fori_loop(0, tm // te, chunk, 0)

    hbm = pl.BlockSpec(memory_space=pl.ANY)
    rev = lambda i: nt - 1 - i
    return _pcall(
        body, name="bwd_ffn", grid=(nt, nj),
        in_specs=[pl.BlockSpec((tm, d), lambda i, j: (rev(i), 0)),
                  pl.BlockSpec((tn, d), lambda i, j: (j, 0)),
                  pl.BlockSpec((tm, tn2), lambda i, j: (rev(i), j)),
                  pl.BlockSpec((tm, tn2), lambda i, j: (rev(i), j)),
                  pl.BlockSpec((3, tn), lambda i, j: (0, j)), pl.BlockSpec((3, tn), lambda i, j: (0, nj + j)),
                  pl.BlockSpec((None, d, tn2), lambda i, j: (j, 0, 0)),
                  pl.BlockSpec((1, d), lambda i, j: (0, 0)), hbm, hbm],
        out_specs=[pl.BlockSpec((tm, tn2), lambda i, j: (rev(i), j)),
                   pl.BlockSpec((2, nj, 8, tn), lambda i, j: (0, 0, 0, 0)),
                   pl.BlockSpec((tm, d), lambda i, j: (rev(i), 0)), pl.BlockSpec((tm, d), lambda i, j: (rev(i), 0)),
                   pl.BlockSpec((1, d), lambda i, j: (0, 0))],
        out_shape=[jax.ShapeDtypeStruct((s_len, nj * tn2), BF16), jax.ShapeDtypeStruct((2, nj, 8, tn), F32),
                   jax.ShapeDtypeStruct((s_len, d), F32), jax.ShapeDtypeStruct((s_len, d), BF16),
                   jax.ShapeDtypeStruct((1, d), F32)],
        scratch_shapes=[pltpu.VMEM((2, nj, HALO, tn), F32), pltpu.VMEM((tm, d), F32), pltpu.VMEM((tm, d), F32),
                        pltpu.VMEM((tm, d), F32), pltpu.VMEM((tm, tn), F32), pltpu.VMEM((2, tm + HALO, tn), F32),
                        pltpu.SemaphoreType.DMA((2,))],
        dims=("arbitrary", "arbitrary"), comm=comm)(dx3b, wdown_g, upb, cb, wfc, wfc, wup_i, g2, dx3, x2)


def _bwd_in(dp, win_g, resid, x_in, g, comm=None):
    s_len, d = x_in.shape
    nsh, _, ncol = win_g.shape
    tm = min(TM_BW, s_len)
    te = min(T_EPI, tm)

    def body(dz_ref, w_ref, r_ref, x_ref, g_ref, dx_ref, dg_ref, acc_ref):
        i, k = pl.program_id(0), pl.program_id(1)

        @pl.when(jnp.logical_and(i == 0, k == 0))
        def _():
            dg_ref[...] = jnp.zeros_like(dg_ref)

        @pl.when(k == 0)
        def _():
            acc_ref[...] = jnp.zeros_like(acc_ref)

        acc_ref[...] += _dot_nt(dz_ref[...], w_ref[...])

        @pl.when(k == nsh - 1)
        def _():
            g = g_ref[...]

            def chunk(q, carry):
                rows = pl.ds(pl.multiple_of(q * te, te), te)
                r, n = _rms_stats(x_ref[rows, :])
                dxn, dgn = _rms_bwd(acc_ref[rows, :], n, r, g)
                dg_ref[...] += _colsum(dgn)
                dx_ref[rows, :] = r_ref[rows, :] + dxn
                return carry

            lax.fori_loop(0, tm // te, chunk, 0)

    return _pcall(
        body, name="bwd_in", grid=(s_len // tm, nsh),
        in_specs=[pl.BlockSpec((tm, ncol), lambda i, k: (i, k)),
                  pl.BlockSpec((None, d, ncol), lambda i, k: (k, 0, 0)),
                  pl.BlockSpec((tm, d), lambda i, k: (i, 0)), pl.BlockSpec((tm, d), lambda i, k: (i, 0)),
                  pl.BlockSpec((1, d), lambda i, k: (0, 0))],
        out_specs=[pl.BlockSpec((tm, d), lambda i, k: (i, 0)), pl.BlockSpec((1, d), lambda i, k: (0, 0))],
        out_shape=[jax.ShapeDtypeStruct((s_len, d), F32), jax.ShapeDtypeStruct((1, d), F32)],
        scratch_shapes=[pltpu.VMEM((tm, d), F32)],
        dims=("arbitrary", "arbitrary"), comm=comm)(dp, win_g, resid, x_in, g)


def _bwd_dy(dx2b, wout_g, comm=None):
    s_len, d = dx2b.shape
    tm = min(TM_OUT, s_len)

    def body(dx_ref, w_ref, dy_ref):
        dy_ref[...] = _dot_nt(dx_ref[...], w_ref[...])

    return _pcall(
        body, name="bwd_dy", grid=(s_len // tm,),
        in_specs=[pl.BlockSpec((tm, d), lambda i: (i, 0)), pl.BlockSpec((d, d), lambda i: (0, 0))],
        out_specs=pl.BlockSpec((tm, d), lambda i: (i, 0)),
        out_shape=jax.ShapeDtypeStruct((s_len, d), F32), dims=("arbitrary",), comm=comm)(dx2b, wout_g)


def _bwd_mixers(p, dy, hs, gv, bv, wt, wtt, bst, wc, bc, wa, wat, ba, wx, wxt, bx, lam, ggm, glru, comm=None):
    s_len = p.shape[0]
    ts = min(TS_MIX, s_len)
    nt = s_len // ts
    hb = ts // HALO

    def body(pz_ref, pgl_ref, pxl_ref, xh_ref, dy_ref, hs_ref, hh_ref, gv_ref, bv_ref, wt_ref, wtt_ref, bst_ref,
             wc_ref, bc_ref, wa_ref, wat_ref, ba_ref, wx_ref, wxt_ref, bx_ref, lam_ref, ggm_ref, glru_ref,
             dp_ref, dgv_ref, dbv_ref, dwt_ref, dbst_ref, dwc_ref, dbc_ref, dwa_ref, dba_ref, dwx_ref, dbx_ref,
             dsp_ref, dggm_ref, dglru_ref,
             carry_ref, nxt_ref, a_s, g_s, vl_s, mix_s, z_s, dm_s, dvl_s, dz_s, dxr_s):
        i = pl.program_id(0)
        ti = nt - 1 - i

        @pl.when(i == 0)
        def _():
            for ref in (dgv_ref, dbv_ref, dwt_ref, dbst_ref, dwc_ref, dbc_ref, dwa_ref, dba_ref, dwx_ref, dbx_ref,
                        dsp_ref, dggm_ref, dglru_ref, carry_ref, nxt_ref):
                ref[...] = jnp.zeros_like(ref)

        z = pz_ref[...]
        u, mixed, vh, rs, ygm = _gm_forward(z, gv_ref[...], bv_ref[...], wt_ref, bst_ref, vl_s, mix_s)
        rg, ngm = _rms_stats(ygm)
        dygm, dgn = _rms_bwd(dy_ref[:, :GM_W], ngm, rg, ggm_ref[...])
        dggm_ref[...] += _colsum(dgn)
        du = dygm * mixed
        dmix = dygm * u
        dm_s[...] = dmix.astype(BF16)
        bsum = dmix[0:CHUNK, :]
        for cc in range(1, ts // CHUNK):
            bsum = bsum + dmix[cc * CHUNK:(cc + 1) * CHUNK, :]
        dbst_ref[...] += bsum
        for hh in range(HEADS):
            cols = slice(hh * HEAD_DIM, (hh + 1) * HEAD_DIM)
            dw = jnp.zeros((CHUNK, CHUNK), F32)
            for cc in range(ts // CHUNK):
                rows = slice(cc * CHUNK, (cc + 1) * CHUNK)
                dmb = dm_s[rows, cols]
                dw = dw + _dot_nt(dmb, vl_s[rows, cols])
                dvl_s[rows, cols] = _dot(wtt_ref[hh], dmb)
            dwt_ref[hh] += dw
        dvl = dvl_s[...]
        dgv_ref[...] += _colsum(dvl * vh)
        dbv_ref[...] += _colsum(dvl)
        dvh = dvl * gv_ref[...]
        dv = rs * (dvh - _rowmean(dvh) - vh * _rowmean(dvh * vh))
        _, gd = _gelu_and_grad(z)
        dp_ref[:, :GM_W] = (du * gd[:, :GM_W]).astype(BF16)
        dp_ref[:, GM_W:2 * GM_W] = (dv * gd[:, GM_W:]).astype(BF16)

        xl = pxl_ref[...]
        xhalo = jnp.where(ti == 0, 0.0, xh_ref[...])
        gts = _lru_gates(xl, xhalo, wc_ref, bc_ref, wa_ref, ba_ref, wx_ref, bx_ref, lam_ref, z_s)
        a, mult, ra, ri, xr, sp = gts["a"], gts["mult"], gts["ra"], gts["ri"], gts["xr"], gts["sp"]
        hs = hs_ref[...]
        hprev = _shift_prev(hs, jnp.where(ti == 0, 0.0, hh_ref[...]), 1)
        gl = pgl_ref[...]
        ggl, dggl = _gelu_and_grad(gl)
        yl = hs * ggl
        rl, nl = _rms_stats(yl)
        dyl, dgn = _rms_bwd(dy_ref[:, GM_W:], nl, rl, glru_ref[...])
        dglru_ref[...] += _colsum(dgn)
        dp_ref[:, 2 * GM_W:2 * GM_W + LRU_W] = (dyl * hs * dggl).astype(BF16)
        a_s[...] = a
        g_s[...] = dyl * ggl

        def step(k, carry):
            t = ts - 1 - k
            gt = g_s[pl.ds(t, 1), :] + carry
            g_s[pl.ds(t, 1), :] = gt
            return a_s[pl.ds(t, 1), :] * gt

        carry_ref[...] = lax.fori_loop(0, ts, step, carry_ref[...], unroll=8)
        gsc = g_s[...]
        da = gsc * hprev
        rix = ri * xr
        dmult = gsc * rix
        dri = gsc * mult * xr
        dxr = gsc * mult * ri
        dla = da * a - dmult * (a * a) / mult
        dsp_ref[...] += _colsum(dla * ra) * (-LRU_C)
        dza = (dla * sp) * (-LRU_C) * ra * (1.0 - ra)
        dzi = dri * ri * (1.0 - ri)
        dba_ref[...] += _colsum(dza)
        dbx_ref[...] += _colsum(dzi)
        dzab = dza.astype(BF16)
        dzib = dzi.astype(BF16)
        xrb = gts["xrb"]
        for hh in range(HEADS):
            cols = slice(hh * HEAD_DIM, (hh + 1) * HEAD_DIM)
            dwa_ref[hh] += _dot_tn(xrb[:, cols], dzab[:, cols])
            dwx_ref[hh] += _dot_tn(xrb[:, cols], dzib[:, cols])
            dxr_s[:, cols] = _dot(dzab[:, cols], wat_ref[hh]) + _dot(dzib[:, cols], wxt_ref[hh])
        dxr = dxr + dxr_s[...]
        dbc_ref[...] += _colsum(dxr)
        dwc_ref[0:1, :] += _colsum(dxr * gts["x3"])
        dwc_ref[1:2, :] += _colsum(dxr * gts["x2"])
        dwc_ref[2:3, :] += _colsum(dxr * gts["x1"])
        dwc_ref[3:4, :] += _colsum(dxr * xl)
        nxt = nxt_ref[...]
        nxt_ref[...] = dxr[:HALO, :]
        dxl = wc_ref[3:4, :] * dxr + wc_ref[2:3, :] * _shift_next(dxr, nxt, 1) \
            + wc_ref[1:2, :] * _shift_next(dxr, nxt, 2) + wc_ref[0:1, :] * _shift_next(dxr, nxt, 3)
        dp_ref[:, 2 * GM_W + LRU_W:] = dxl.astype(BF16)

    full = lambda shape: pl.BlockSpec(shape, lambda i: (0,) * len(shape))
    rev = lambda i: nt - 1 - i
    prev_blk = lambda i: jnp.maximum((nt - 1 - i) * hb - 1, 0)
    hhd = (HEADS, HEAD_DIM, HEAD_DIM)
    small_shapes = [(1, GM_W), (1, GM_W), (HEADS, CHUNK, CHUNK), (CHUNK, GM_W), (4, LRU_W), (1, LRU_W), hhd,
                    (1, LRU_W), hhd, (1, LRU_W), (1, LRU_W), (1, GM_W), (1, LRU_W)]
    return _pcall(
        body, name="bwd_mixers", grid=(nt,),
        in_specs=[pl.BlockSpec((ts, 2 * GM_W), lambda i: (rev(i), 0)),
                  pl.BlockSpec((ts, LRU_W), lambda i: (rev(i), 2)),
                  pl.BlockSpec((ts, LRU_W), lambda i: (rev(i), 3)),
                  pl.BlockSpec((HALO, LRU_W), lambda i: (prev_blk(i), 3)),
                  pl.BlockSpec((ts, GM_W + LRU_W), lambda i: (rev(i), 0)),
                  pl.BlockSpec((ts, LRU_W), lambda i: (rev(i), 0)),
                  pl.BlockSpec((HALO, LRU_W), lambda i: (prev_blk(i), 0)),
                  full((1, GM_W)), full((1, GM_W)), full((HEADS, CHUNK, CHUNK)), full((HEADS, CHUNK, CHUNK)),
                  full((CHUNK, GM_W)), full((4, LRU_W)), full((1, LRU_W)), full(hhd), full(hhd), full((1, LRU_W)),
                  full(hhd), full(hhd), full((1, LRU_W)), full((1, LRU_W)), full((1, GM_W)), full((1, LRU_W))],
        out_specs=[pl.BlockSpec((ts, 2 * GM_W + 2 * LRU_W), lambda i: (rev(i), 0))] + [full(s) for s in small_shapes],
        out_shape=[jax.ShapeDtypeStruct((s_len, 2 * GM_W + 2 * LRU_W), BF16)]
        + [jax.ShapeDtypeStruct(s, F32) for s in small_shapes],
        scratch_shapes=[pltpu.VMEM((1, LRU_W), F32), pltpu.VMEM((HALO, LRU_W), F32),
                        pltpu.VMEM((ts, LRU_W), F32), pltpu.VMEM((ts, LRU_W), F32),
                        pltpu.VMEM((ts, GM_W), BF16), pltpu.VMEM((ts, GM_W), F32), pltpu.VMEM((ts, 2 * LRU_W), F32),
                        pltpu.VMEM((ts, GM_W), BF16), pltpu.VMEM((ts, GM_W), F32), pltpu.VMEM((ts, 2 * LRU_W), F32),
                        pltpu.VMEM((ts, LRU_W), F32)],
        dims=("arbitrary",), comm=comm)(p, p, p, p, dy, hs, hs, gv, bv, wt, wtt, bst, wc, bc, wa, wat, ba, wx, wxt, bx, lam,
                             ggm, glru)


def _bwd_weight(name, a, b, *, a_planes, b_planes, shard_rows, interleaved=0, comm=None):
    _, s_len, ma = a.shape
    _, _, nb = b.shape
    m, n = a_planes * ma, b_planes * nb
    tk = min(TW_K, s_len)
    if shard_rows:
        rows, cols = m // N_CHIPS, n // 2
        tm, tn = _tile(rows, TW_M), _tile(cols, TW_N)
        out_idx = lambda i, j, k: (j * tn // cols, i * tm // rows, (i * tm % rows) // tm, (j * tn % cols) // tn)
    else:
        rows, cols = m // 2, n // N_CHIPS
        tm, tn = _tile(rows, TW_M), (interleaved or _tile(cols, TW_N))
        col0 = (lambda j: (j % 2) * (n // 2) + (j // 2) * tn) if interleaved else (lambda j: j * tn)
        out_idx = lambda i, j, k: (i * tm // rows, col0(j) // cols, (i * tm % rows) // tm, (col0(j) % cols) // tn)
    nk = s_len // tk
    npa, npb = ma // tm, nb // tn

    def body(a_ref, b_ref, o_ref, ob_ref, acc_ref):
        k = pl.program_id(2)

        @pl.when(k == 0)
        def _():
            acc_ref[...] = jnp.zeros_like(acc_ref)

        acc_ref[...] += _dot_tn(a_ref[...], b_ref[...])

        @pl.when(k == nk - 1)
        def _():
            o_ref[...] = acc_ref[...]
            ob_ref[...] = acc_ref[...].astype(BF16)

    shape = (2, N_CHIPS, rows, cols)
    return _pcall(
        body, name=name, grid=(m // tm, n // tn, nk),
        in_specs=[pl.BlockSpec((None, tk, tm), lambda i, j, k: (i // npa, k, i % npa)),
                  pl.BlockSpec((None, tk, tn), lambda i, j, k: (j // npb, k, j % npb))],
        out_specs=[pl.BlockSpec((None, None, tm, tn), out_idx), pl.BlockSpec((None, None, tm, tn), out_idx)],
        out_shape=[jax.ShapeDtypeStruct(shape, F32), jax.ShapeDtypeStruct(shape, BF16)],
        scratch_shapes=[pltpu.VMEM((tm, tn), F32)],
        dims=("arbitrary", "arbitrary", "arbitrary"), comm=comm)(a, b)


def _mesh_pos():
    return lax.axis_index("x"), lax.axis_index("y"), lax.axis_index("c")


def _other_chips(x, y):
    return [(1 - x, y), (x, 1 - y), (1 - x, 1 - y)]


def _to_slot(name, a, cs, dtype):
    _, a_rows, b_cols = a.shape
    ta = min(T_ELEM, a_rows)

    def body(cs_ref, a_ref, o_ref):
        o_ref[...] = a_ref[...].astype(dtype)

    grid_spec = pltpu.PrefetchScalarGridSpec(
        num_scalar_prefetch=1, grid=(2, a_rows // ta),
        in_specs=[pl.BlockSpec((None, ta, b_cols), lambda h, r, cs_ref: (h, r, 0))],
        out_specs=pl.BlockSpec((None, None, ta, b_cols), lambda h, r, cs_ref: (cs_ref[1], h, r, 0)))
    return _pcall(body, name=name, grid_spec=grid_spec,
                  out_shape=jax.ShapeDtypeStruct((N_CHIPS,) + a.shape, dtype), dims=("arbitrary", "arbitrary"))(cs, a)


def _all_gather(bufs):
    n = len(bufs)

    def body(*refs):
        outs = refs[n:2 * n]
        send_sems, recv_sems = refs[2 * n:]
        x, y, c = _mesh_pos()
        s = 2 * x + y
        me, sib = (x, y, c), (x, y, 1 - c)
        chips = _other_chips(x, y)

        def rcopy(a, k, blk, to):
            return pltpu.make_async_remote_copy(src_ref=blk, dst_ref=blk, send_sem=send_sems.at[a * 6 + k],
                                                recv_sem=recv_sems.at[a * 6 + k], device_id=to, device_id_type=MESH)

        first = [rcopy(a, j, outs[a].at[s, c], (cx, cy, c)) for a in range(n) for j, (cx, cy) in enumerate(chips)]
        for cp in first:
            cp.start()
        passed = []
        for a in range(n):
            for j, (cx, cy) in enumerate(chips):
                blk = outs[a].at[2 * cx + cy, c]
                rcopy(a, j, blk, me).wait_recv()
                cp = rcopy(a, 3 + j, blk, sib)
                cp.start()
                passed.append(cp)
        for a in range(n):
            for j, (cx, cy) in enumerate(chips):
                rcopy(a, 3 + j, outs[a].at[2 * cx + cy, 1 - c], me).wait_recv()
        for cp in first + passed:
            cp.wait_send()

    return _pcall(
        body, name="all_gather_weights",
        in_specs=[ANY] * n, out_specs=[ANY] * n,
        out_shape=[jax.ShapeDtypeStruct(a.shape, a.dtype) for a in bufs],
        scratch_shapes=[pltpu.SemaphoreType.DMA((6 * n,)), pltpu.SemaphoreType.DMA((6 * n,))],
        aliases={a: a for a in range(n)})(*bufs)


def _same(a):
    return jax.ShapeDtypeStruct(a.shape, a.dtype)


def _job_gather(bufs, ici_parts, relay_parts):
    def copies(cin, cout, x, y, c):
        out = []
        for a, lo, n in ici_parts:
            blk = cout[a].at[2 * x + y, c, pl.ds(lo, n)]
            out += [(blk, blk, (cx, cy, c)) for cx, cy in _other_chips(x, y)]
        for a, lo, n in relay_parts:
            for cx, cy in _other_chips(x, y):
                blk = cout[a].at[2 * cx + cy, c, pl.ds(lo, n)]
                out.append((blk, blk, (x, y, 1 - c)))
        return out

    return _Job(bufs, [_same(b) for b in bufs], {a: a for a in range(len(bufs))},
                3 * (len(ici_parts) + len(relay_parts)), copies)


def _job_pair_swap(arrs):
    def copies(cin, cout, x, y, c):
        return [(cin[a].at[1 - c], cout[a], (x, y, 1 - c)) for a in range(len(arrs))]

    return _Job(arrs, [jax.ShapeDtypeStruct(a.shape[1:], a.dtype) for a in arrs], {}, len(arrs), copies)


def _job_chip_exchange(big, small=()):
    nb = len(big)
    arrs = list(big) + list(small)

    def copies(cin, cout, x, y, c):
        out = []
        for a in range(len(arrs)):
            for j, (cx, cy) in enumerate(_other_chips(x, y)):
                out.append((cin[a].at[2 * cx + cy] if a < nb else cin[a], cout[a].at[j], (cx, cy, c)))
        return out

    shapes = [a.shape[1:] for a in big] + [a.shape for a in small]
    return _Job(arrs, [jax.ShapeDtypeStruct((3,) + sh, a.dtype) for sh, a in zip(shapes, arrs)], {}, 3 * len(arrs), copies)


def _job_halves_swap(bufs):
    def copies(cin, cout, x, y, c):
        return [(cout[a].at[c], cout[a].at[c], (x, y, 1 - c)) for a in range(len(bufs))]

    return _Job(bufs, [_same(b) for b in bufs], {a: a for a in range(len(bufs))}, len(bufs), copies)


def _pair_sum(name, g32, recv, cs):
    _, nch, a_rows, b_cols = g32.shape
    ta = min(T_ELEM, a_rows)

    def body(cs_ref, g_ref, r_ref, pb_ref, own_ref):
        k = pl.program_id(1)
        v = g_ref[...] + r_ref[...].astype(F32)
        pb_ref[...] = v.astype(BF16)

        @pl.when(k == cs_ref[1])
        def _():
            own_ref[...] = v

    grid_spec = pltpu.PrefetchScalarGridSpec(
        num_scalar_prefetch=1, grid=(a_rows // ta, nch),
        in_specs=[pl.BlockSpec((None, None, ta, b_cols), lambda r, k, cs_ref: (cs_ref[0], k, r, 0)),
                  pl.BlockSpec((None, ta, b_cols), lambda r, k, cs_ref: (k, r, 0))],
        out_specs=[pl.BlockSpec((None, ta, b_cols), lambda r, k, cs_ref: (k, r, 0)),
                   pl.BlockSpec((ta, b_cols), lambda r, k, cs_ref: (r, 0))])
    return _pcall(
        body, name=name, grid_spec=grid_spec,
        out_shape=[jax.ShapeDtypeStruct((nch, a_rows, b_cols), BF16), jax.ShapeDtypeStruct((a_rows, b_cols), F32)],
        dims=("arbitrary", "arbitrary"))(cs, g32, recv)


def _small_pair_sum(mine, recv, cs):
    _, r, ccols = mine.shape

    def body(cs_ref, a_ref, b_ref, o_ref):
        o_ref[...] = a_ref[...] + b_ref[...]

    grid_spec = pltpu.PrefetchScalarGridSpec(
        num_scalar_prefetch=1, grid=(1,),
        in_specs=[pl.BlockSpec((None, r, ccols), lambda i, cs_ref: (cs_ref[0], 0, 0)),
                  pl.BlockSpec((r, ccols), lambda i, cs_ref: (0, 0))],
        out_specs=pl.BlockSpec((r, ccols), lambda i, cs_ref: (0, 0)))
    return _pcall(body, name="small_pair_sum", grid_spec=grid_spec,
                  out_shape=jax.ShapeDtypeStruct((r, ccols), F32), dims=("arbitrary",))(cs, mine, recv)


def _chip_sum(name, own, recv, cs):
    a_rows, b_cols = own.shape
    ta = min(T_ELEM, a_rows)

    def body(cs_ref, o_ref, r_ref, f_ref):
        f_ref[...] = ((o_ref[...] + r_ref[0].astype(F32)) + r_ref[1].astype(F32)) + r_ref[2].astype(F32)

    grid_spec = pltpu.PrefetchScalarGridSpec(
        num_scalar_prefetch=1, grid=(a_rows // ta,),
        in_specs=[pl.BlockSpec((ta, b_cols), lambda r, cs_ref: (r, 0)),
                  pl.BlockSpec((3, ta, b_cols), lambda r, cs_ref: (0, r, 0))],
        out_specs=pl.BlockSpec((None, ta, b_cols), lambda r, cs_ref: (cs_ref[0], r, 0)))
    return _pcall(body, name=name, grid_spec=grid_spec,
                  out_shape=jax.ShapeDtypeStruct((2, a_rows, b_cols), F32), dims=("arbitrary",))(cs, own, recv)


def _small_chip_sum(pair, recv, cs):
    r, ccols = pair.shape

    def body(cs_ref, p_ref, r_ref, o_ref):
        s = cs_ref[1]
        own = p_ref[...]
        total = None
        for k in range(N_CHIPS):
            flip = jnp.bitwise_xor(s, k)
            term = jnp.where(flip == 0, own, jnp.where(flip == 2, r_ref[0], jnp.where(flip == 1, r_ref[1], r_ref[2])))
            total = term if total is None else total + term
        o_ref[...] = total

    grid_spec = pltpu.PrefetchScalarGridSpec(
        num_scalar_prefetch=1, grid=(1,),
        in_specs=[pl.BlockSpec((r, ccols), lambda i, cs_ref: (0, 0)),
                  pl.BlockSpec((3, r, ccols), lambda i, cs_ref: (0, 0, 0))],
        out_specs=pl.BlockSpec((None, r, ccols), lambda i, cs_ref: (cs_ref[0], 0, 0)))
    return _pcall(body, name="small_chip_sum", grid_spec=grid_spec,
                  out_shape=jax.ShapeDtypeStruct((2, r, ccols), F32), dims=("arbitrary",))(cs, pair, recv)


def _adamw(name, w, g, m, v, *, halves, comm=None):
    rows, cols = w.shape
    tr, tc = min(T_ELEM, rows), PACK_COLS
    c1 = 1.0 - ADAM_B1 ** ADAM_STEP
    c2 = 1.0 - ADAM_B2 ** ADAM_STEP

    def body(w_ref, g_ref, m_ref, v_ref, d_ref, mo_ref, vo_ref, go_ref):
        g_ = g_ref[...]
        m_ = ADAM_B1 * m_ref[...] + (1.0 - ADAM_B1) * g_
        v_ = ADAM_B2 * v_ref[...] + (1.0 - ADAM_B2) * (g_ * g_)
        mo_ref[...] = m_
        vo_ref[...] = v_
        go_ref[...] = g_
        d_ref[...] = (-ADAM_LR) * ((m_ / c1) / (jnp.sqrt(v_ / c2) + ADAM_EPS) + ADAM_WD * w_ref[...])

    spec = pl.BlockSpec((tr, tc), lambda r, j: (r, j))
    if halves == "rows":
        nrh = rows // 2 // tr
        g_spec = pl.BlockSpec((None, tr, tc), lambda r, j: (r // nrh, r % nrh, j))
    elif halves == "cols":
        nch = cols // 2 // tc
        g_spec = pl.BlockSpec((None, tr, tc), lambda r, j: (j // nch, r, j % nch))
    else:
        g_spec = spec
    return _pcall(body, name=name, grid=(rows // tr, cols // tc), in_specs=[spec, g_spec, spec, spec],
                  out_specs=[spec] * 4, out_shape=[jax.ShapeDtypeStruct((rows, cols), F32)] * 4,
                  dims=("arbitrary", "arbitrary"), comm=comm)(w, g, m, v)


def _pack(parts, rows):
    flat = jnp.concatenate([a.reshape(-1) for a in parts])
    return jnp.pad(flat, (0, rows * PACK_COLS - flat.shape[0])).reshape(rows, PACK_COLS)


def _unpack(buf, shapes):
    flat = buf.reshape(-1)
    out, off = [], 0
    for sh in shapes:
        size = math.prod(sh)
        out.append(flat[off:off + size].reshape(sh))
        off += size
    return out


def _pack_rows(shapes, multiple):
    total = sum(math.prod(sh) for sh in shapes)
    rows = -(-total // PACK_COLS)
    return -(-rows // multiple) * multiple


SMALL = ["norm1_g", "gm_v_g", "gm_v_b", "gm_ws", "gm_bs", "lru_conv_w", "lru_conv_b", "lru_wa", "lru_ba", "lru_wx",
         "lru_bx", "lru_lambda", "gm_out_g", "lru_out_g", "norm2_g", "ffn_conv_w", "ffn_conv_b", "final_g"]
BIG = ["w_in", "w_out", "ffn_w_up", "ffn_w_down"]
ORDER = ["norm1_g", "w_in", "gm_v_g", "gm_v_b", "gm_ws", "gm_bs", "lru_conv_w", "lru_conv_b", "lru_wa", "lru_ba",
         "lru_wx", "lru_bx", "lru_lambda", "gm_out_g", "lru_out_g", "w_out", "norm2_g", "ffn_w_up", "ffn_conv_w",
         "ffn_conv_b", "ffn_w_down", "final_g"]


def kernel(x, norm1_g, w_in, gm_v_g, gm_v_b, gm_ws, gm_bs, lru_conv_w, lru_conv_b, lru_wa, lru_ba, lru_wx, lru_bx, lru_lambda, gm_out_g, lru_out_g, w_out, norm2_g, ffn_w_up, ffn_conv_w, ffn_conv_b, ffn_w_down, final_g, loss_target, m_norm1_g, m_w_in, m_gm_v_g, m_gm_v_b, m_gm_ws, m_gm_bs, m_lru_conv_w, m_lru_conv_b, m_lru_wa, m_lru_ba, m_lru_wx, m_lru_bx, m_lru_lambda, m_gm_out_g, m_lru_out_g, m_w_out, m_norm2_g, m_ffn_w_up, m_ffn_conv_w, m_ffn_conv_b, m_ffn_w_down, m_final_g, v_norm1_g, v_w_in, v_gm_v_g, v_gm_v_b, v_gm_ws, v_gm_bs, v_lru_conv_w, v_lru_conv_b, v_lru_wa, v_lru_ba, v_lru_wx, v_lru_bx, v_lru_lambda, v_gm_out_g, v_lru_out_g, v_w_out, v_norm2_g, v_ffn_w_up, v_ffn_conv_w, v_ffn_conv_b, v_ffn_w_down, v_final_g):
    w = dict(norm1_g=norm1_g, w_in=w_in, gm_v_g=gm_v_g, gm_v_b=gm_v_b, gm_ws=gm_ws, gm_bs=gm_bs, lru_conv_w=lru_conv_w, lru_conv_b=lru_conv_b, lru_wa=lru_wa, lru_ba=lru_ba, lru_wx=lru_wx, lru_bx=lru_bx, lru_lambda=lru_lambda, gm_out_g=gm_out_g, lru_out_g=lru_out_g, w_out=w_out, norm2_g=norm2_g, ffn_w_up=ffn_w_up, ffn_conv_w=ffn_conv_w, ffn_conv_b=ffn_conv_b, ffn_w_down=ffn_w_down, final_g=final_g)
    m = dict(norm1_g=m_norm1_g, w_in=m_w_in, gm_v_g=m_gm_v_g, gm_v_b=m_gm_v_b, gm_ws=m_gm_ws, gm_bs=m_gm_bs, lru_conv_w=m_lru_conv_w, lru_conv_b=m_lru_conv_b, lru_wa=m_lru_wa, lru_ba=m_lru_ba, lru_wx=m_lru_wx, lru_bx=m_lru_bx, lru_lambda=m_lru_lambda, gm_out_g=m_gm_out_g, lru_out_g=m_lru_out_g, w_out=m_w_out, norm2_g=m_norm2_g, ffn_w_up=m_ffn_w_up, ffn_conv_w=m_ffn_conv_w, ffn_conv_b=m_ffn_conv_b, ffn_w_down=m_ffn_w_down, final_g=m_final_g)
    v = dict(norm1_g=v_norm1_g, w_in=v_w_in, gm_v_g=v_gm_v_g, gm_v_b=v_gm_v_b, gm_ws=v_gm_ws, gm_bs=v_gm_bs, lru_conv_w=v_lru_conv_w, lru_conv_b=v_lru_conv_b, lru_wa=v_lru_wa, lru_ba=v_lru_ba, lru_wx=v_lru_wx, lru_bx=v_lru_bx, lru_lambda=v_lru_lambda, gm_out_g=v_gm_out_g, lru_out_g=v_lru_out_g, w_out=v_w_out, norm2_g=v_norm2_g, ffn_w_up=v_ffn_w_up, ffn_conv_w=v_ffn_conv_w, ffn_conv_b=v_ffn_conv_b, ffn_w_down=v_ffn_w_down, final_g=v_final_g)

    mx, my, mc = _mesh_pos()
    shard = 2 * mx + my
    cs = jnp.stack([mc, shard]).astype(jnp.int32)

    xs = x[0]
    tgt = loss_target[0]
    s_len, d = xs.shape

    halves = lambda a: a.reshape((2, a.shape[0] // 2) + a.shape[1:])
    slot = {k: _to_slot("slot_" + k, halves(w[k][0]), cs, BF16) for k in BIG}
    win_b, wc_b, wfc_b = _all_gather([slot["w_in"],
                                      _to_slot("slot_lru_conv_w", w["lru_conv_w"][0].reshape(2, 4, -1), cs, F32),
                                      _to_slot("slot_ffn_conv_w", w["ffn_conv_w"][0].reshape(2, 12, -1), cs, F32)])
    win_g = win_b.reshape(N_CHIPS, d, -1)
    wc = wc_b.reshape(N_CHIPS, 4, -1).transpose(1, 0, 2).reshape(4, -1)
    wfc = wfc_b.reshape(N_CHIPS, 3, -1).transpose(1, 0, 2).reshape(3, -1)
    wout_b, wup_b, wdown_b = slot["w_out"], slot["ffn_w_up"], slot["ffn_w_down"]
    r_out, r_up, r_dn = wout_b.shape[2], wup_b.shape[2] // 2, wdown_b.shape[2] // 2

    tril = jnp.tril(jnp.ones((CHUNK, CHUNK), bool))
    wt32 = jnp.where(tril[None], w["gm_ws"][0], 0.0)
    wt = wt32.astype(BF16)
    wtt = wt32.transpose(0, 2, 1).astype(BF16)
    bst = jnp.repeat(w["gm_bs"][0].T, HEAD_DIM, axis=1)
    wa = w["lru_wa"][0].astype(BF16)
    wx = w["lru_wx"][0].astype(BF16)
    wat = w["lru_wa"][0].transpose(0, 2, 1).astype(BF16)
    wxt = w["lru_wx"][0].transpose(0, 2, 1).astype(BF16)
    ba = w["lru_ba"][0].reshape(1, -1)
    bx = w["lru_bx"][0].reshape(1, -1)
    gf = w["final_g"].reshape(1, -1)

    p, h1, wout_b, wup_b = _fwd_in_proj(
        xs, w["norm1_g"], win_g, comm=_job_gather([wout_b, wup_b], [(0, 0, r_out), (1, 0, r_up)], []))
    y, hs, wout_b, wup_b, wdown_b = _fwd_mixers(
        p, w["gm_v_g"], w["gm_v_b"], wt, bst, wc, w["lru_conv_b"], wa, ba, wx, bx, w["lru_lambda"],
        w["gm_out_g"], w["lru_out_g"],
        comm=_job_gather([wout_b, wup_b, wdown_b], [(1, r_up, r_up), (2, 0, r_dn)], [(0, 0, r_out), (1, 0, r_up)]))
    wout_g = wout_b.reshape(-1, d)
    x2, h2, wup_b, wdown_b = _fwd_out_proj(
        xs, y, wout_g, w["norm2_g"], comm=_job_gather([wup_b, wdown_b], [(1, r_dn, r_dn)], [(0, r_up, r_up), (1, 0, r_dn)]))
    wdown_b, = _comm_call("gather_tail", _job_gather([wdown_b], [], [(0, r_dn, r_dn)]))
    nps = wup_b.shape[3] // TN_UP
    wup_i = wup_b.reshape(2, 2, d, nps, TN_UP).transpose(1, 3, 2, 0, 4).reshape(2 * nps, d, 2 * TN_UP)
    wdown_g = wdown_b.reshape(-1, d)
    upb, cb, act, dx3, dx3b, loss_tile, dgf = _fwd_ffn(h2, wup_i, wfc, w["ffn_conv_b"], wdown_g, x2, gf, tgt)
    loss = lax.psum(loss_tile[0, 0], ("x", "y", "c"))

    adam = {}

    def adamw_big(name, gfull, hv, comm=None):
        r2 = lambda a: a.reshape(w[name].shape[1:])
        res = _adamw("adamw_" + name, r2(w[name]), gfull, r2(m[name]), r2(v[name]), halves=hv, comm=comm)
        adam[name] = res[:4]
        return res[4:]

    gd32, gdb = _bwd_weight("bwd_w_down", act[None], dx3b[None], a_planes=1, b_planes=1, shard_rows=True)
    dup, dwf, dx2, dx2b, dg2, rcv = _bwd_ffn(dx3b, wdown_g, upb, cb, wfc, wup_i, dx3, x2, w["norm2_g"],
                                             comm=_job_pair_swap([gdb]))
    pb_dn, own_dn = _pair_sum("pair_sum_3", gd32, rcv, cs)
    gu32, gub, got = _bwd_weight("bwd_w_up", h2[None], dup[None], a_planes=1, b_planes=1, shard_rows=False,
                                 interleaved=TN_UP, comm=_job_chip_exchange([pb_dn]))
    red_dn = _chip_sum("chip_sum_3", own_dn, got, cs)
    go32, gob, rcv, red_dn = _bwd_weight("bwd_w_out", y[None], dx2b[None], a_planes=1, b_planes=1, shard_rows=True,
                                         comm=_merge_jobs([_job_pair_swap([gub]), _job_halves_swap([red_dn])]))
    pb_up, own_up = _pair_sum("pair_sum_2", gu32, rcv, cs)
    adamw_big("ffn_w_down", red_dn, "cols")
    dy, rcv = _bwd_dy(dx2b, wout_g, comm=_job_pair_swap([gob]))
    pb_out, own_out = _pair_sum("pair_sum_1", go32, rcv, cs)
    (dp, dgv, dbv, dwt, dbst, dwc, dbc, dwa, dba, dwx, dbx, dsp, dggm, dglru, got_up, got_out) = _bwd_mixers(
        p, dy, hs, w["gm_v_g"], w["gm_v_b"], wt, wtt, bst, wc, w["lru_conv_b"], wa, wat, ba, wx, wxt, bx,
        w["lru_lambda"], w["gm_out_g"], w["lru_out_g"], comm=_job_chip_exchange([pb_up, pb_out]))
    red_up = _chip_sum("chip_sum_2", own_up, got_up, cs)
    red_out = _chip_sum("chip_sum_1", own_out, got_out, cs)
    gi32, gib, red_up, red_out = _bwd_weight("bwd_w_in", h1[None], dp[None], a_planes=1, b_planes=1, shard_rows=False,
                                             comm=_job_halves_swap([red_up, red_out]))
    rcv, = _comm_call("w_in_swap", _job_pair_swap([gib]))
    pb_in, own_in = _pair_sum("pair_sum_0", gi32, rcv, cs)
    grad_x, dg1, got_in = _bwd_in(dp, win_g, dx2, xs, w["norm1_g"], comm=_job_chip_exchange([pb_in]))
    red_in = _chip_sum("chip_sum_0", own_in, got_in, cs)

    dwfc = dwf[:, :, :3].transpose(2, 0, 1, 3).reshape(3, -1)
    dbfc = dwf[:, :, 3].reshape(1, -1)
    dlam = dsp * (-_sigmoid(-w["lru_lambda"]))
    small_grads = dict(
        norm1_g=dg1, gm_v_g=dgv, gm_v_b=dbv, gm_ws=jnp.where(tril[None], dwt, 0.0),
        gm_bs=dbst.reshape(CHUNK, HEADS, HEAD_DIM).sum(-1).T, lru_conv_w=dwc, lru_conv_b=dbc, lru_wa=dwa, lru_ba=dba,
        lru_wx=dwx, lru_bx=dbx, lru_lambda=dlam, gm_out_g=dggm, lru_out_g=dglru, norm2_g=dg2, ffn_conv_w=dwfc,
        ffn_conv_b=dbfc, final_g=dgf)
    full_shapes = [small_grads[k].shape for k in SMALL]
    rows_full = _pack_rows(full_shapes, 16)
    gpack = _pack([small_grads[k] for k in SMALL], rows_full).reshape(2, rows_full // 2, PACK_COLS)

    rcv, = _comm_call("small_swap", _job_pair_swap([gpack]))
    small_pair = _small_pair_sum(gpack, rcv, cs)
    got_small, = _comm_call("small_exchange", _job_chip_exchange([], [small_pair]))
    small_half = _small_chip_sum(small_pair, got_small, cs)
    red_in, small_full = _comm_call("tail_halves_swap", _job_halves_swap([red_in, small_half]))
    adamw_big("ffn_w_up", red_up, "rows")
    adamw_big("w_out", red_out, "cols")
    adamw_big("w_in", red_in, "rows")

    grads = {}
    for name, g in zip(SMALL, _unpack(small_full, full_shapes)):
        blk = w[name].shape[1:] if w[name].ndim > 1 else w[name].shape
        if name in ("lru_conv_w", "ffn_conv_w"):
            g = lax.dynamic_slice_in_dim(g, shard * blk[1], blk[1], axis=1)
        grads[name] = g.reshape(blk)

    delta, new_m, new_v = {}, {}, {}
    for name in BIG:
        delta[name], new_m[name], new_v[name], grads[name] = adam[name]
    blk_shapes = [grads[k].shape for k in SMALL]
    rows_blk = _pack_rows(blk_shapes, T_ELEM)
    packs = [_pack([src[k] for k in SMALL], rows_blk) for src in (w, grads, m, v)]
    outs = _adamw("adamw_small", *packs, halves=None)
    for dst, buf in zip((delta, new_m, new_v), outs):
        for name, a in zip(SMALL, _unpack(buf, blk_shapes)):
            dst[name] = a

    def shaped(dct):
        return [dct[k].reshape(w[k].shape) for k in ORDER]

    return (loss, grad_x[None], *shaped(grads), *shaped(delta), *shaped(new_m), *shaped(new_v))
```

```python
import functools
import math

import jax
import jax.numpy as jnp
from jax import lax
from jax.experimental import pallas as pl
from jax.experimental.pallas import tpu as pltpu

F32 = jnp.float32
BF16 = jnp.bfloat16
MESH = pl.DeviceIdType.MESH
ANY = pl.BlockSpec(memory_space=pltpu.HBM)

GM_W = 1024
LRU_W = 1024
CHUNK = 128
HEADS = 8
HEAD_DIM = 128
LRU_C = 8.0
RMS_EPS = 1e-6
LN_EPS = 1e-5
ADAM_LR = 0.001
ADAM_B1 = 0.9
ADAM_B2 = 0.999
ADAM_EPS = 1e-08
ADAM_WD = 0.01
ADAM_STEP = 10

N_CHIPS = 4
HALO = 8
PACK_COLS = 1024
VMEM_LIMIT = 56 * 1024 * 1024

TM_IN = 1024
TS_MIX = 256
TM_OUT = 512
TM_UP = 512
TN_UP = 512
T_EPI = 128
T_CHUNK = 16
TM_BW = 512
TW_M = 1024
TW_N = 1024
TW_K = 2048
T_ELEM = 256
LANES = 128


def _tile(dim, cap):
    t = min(cap, dim) // LANES * LANES
    while dim % t:
        t -= LANES
    return t

_GELU_K0 = 0.7978845608028654
_GELU_K1 = 0.044715


class _Job:
    def __init__(self, ins, out_shapes, aliases, n, copies):
        self.ins, self.out_shapes, self.aliases, self.n, self.copies = list(ins), list(out_shapes), dict(aliases), n, copies

    def make(self, cin, cout, send_sems, recv_sems):
        x, y, c = _mesh_pos()
        return [pltpu.make_async_remote_copy(src_ref=src, dst_ref=dst, send_sem=send_sems.at[k], recv_sem=recv_sems.at[k],
                                             device_id=dev, device_id_type=MESH)
                for k, (src, dst, dev) in enumerate(self.copies(cin, cout, x, y, c))]


def _merge_jobs(jobs):
    ins, outs, aliases, spans = [], [], {}, []
    for jb in jobs:
        spans.append((len(ins), len(jb.ins), len(outs), len(jb.out_shapes)))
        aliases.update({len(ins) + a: len(outs) + b for a, b in jb.aliases.items()})
        ins += jb.ins
        outs += jb.out_shapes

    def copies(cin, cout, x, y, c):
        out = []
        for jb, (i0, ni, o0, no) in zip(jobs, spans):
            out += jb.copies(cin[i0:i0 + ni], cout[o0:o0 + no], x, y, c)
        return out

    return _Job(ins, outs, aliases, sum(jb.n for jb in jobs), copies)


def _pcall(body, *, name, out_shape, grid=None, in_specs=None, out_specs=None, scratch_shapes=(),
           grid_spec=None, dims=None, aliases=None, comm=None):
    params = pltpu.CompilerParams(dimension_semantics=dims, vmem_limit_bytes=VMEM_LIMIT)
    kw = dict(name=name, compiler_params=params)
    if grid_spec is not None:
        if aliases:
            kw["input_output_aliases"] = aliases
        return pl.pallas_call(body, grid_spec=grid_spec, out_shape=out_shape, **kw)
    scratch_shapes = list(scratch_shapes)
    if comm is not None:
        out_shape = list(out_shape) if isinstance(out_shape, (list, tuple)) else [out_shape]
        out_specs = list(out_specs) if isinstance(out_specs, (list, tuple)) else [out_specs]
        n_in, n_out, n_scr = len(in_specs), len(out_shape), len(scratch_shapes)
        n_ci, n_co = len(comm.ins), len(comm.out_shapes)
        inner, steps = body, tuple(grid)

        def body(*refs):
            o0 = n_in + n_ci
            s0 = o0 + n_out + n_co
            cps = comm.make(refs[n_in:o0], refs[o0 + n_out:s0], refs[s0 + n_scr], refs[s0 + n_scr + 1])
            ids = [pl.program_id(k) for k in range(len(steps))]

            @pl.when(functools.reduce(jnp.logical_and, [i == 0 for i in ids]))
            def _():
                for cp in cps:
                    cp.start()

            inner(*refs[:n_in], *refs[o0:o0 + n_out], *refs[s0:s0 + n_scr])

            @pl.when(functools.reduce(jnp.logical_and, [i == n - 1 for i, n in zip(ids, steps)]))
            def _():
                for cp in cps:
                    cp.wait()

        in_specs = list(in_specs) + [ANY] * n_ci
        out_specs = out_specs + [ANY] * n_co
        out_shape = out_shape + comm.out_shapes
        scratch_shapes = scratch_shapes + [pltpu.SemaphoreType.DMA((comm.n,)), pltpu.SemaphoreType.DMA((comm.n,))]
        aliases = {**(aliases or {}), **{n_in + a: n_out + b for a, b in comm.aliases.items()}}
    if aliases:
        kw["input_output_aliases"] = aliases
    if grid is not None:
        kw["grid"] = grid
    call = pl.pallas_call(body, in_specs=in_specs, out_specs=out_specs, scratch_shapes=scratch_shapes,
                          out_shape=out_shape, **kw)
    if comm is None:
        return call
    return lambda *args: call(*args, *comm.ins)


def _comm_call(name, job):
    n_ci, n_co = len(job.ins), len(job.out_shapes)

    def body(*refs):
        cps = job.make(refs[:n_ci], refs[n_ci:n_ci + n_co], refs[n_ci + n_co], refs[n_ci + n_co + 1])
        for cp in cps:
            cp.start()
        for cp in cps:
            cp.wait()

    return pl.pallas_call(
        body, name=name, in_specs=[ANY] * n_ci, out_specs=[ANY] * n_co, out_shape=job.out_shapes,
        scratch_shapes=[pltpu.SemaphoreType.DMA((job.n,)), pltpu.SemaphoreType.DMA((job.n,))],
        input_output_aliases=job.aliases)(*job.ins)


def _gelu(x):
    t = jnp.tanh(_GELU_K0 * (x + _GELU_K1 * (x * x * x)))
    return 0.5 * x * (1.0 + t)


def _gelu_and_grad(x):
    x2 = x * x
    t = jnp.tanh(_GELU_K0 * (x + _GELU_K1 * (x2 * x)))
    g = 0.5 * x * (1.0 + t)
    dg = 0.5 * (1.0 + t) + 0.5 * x * (1.0 - t * t) * (_GELU_K0 * (1.0 + 3.0 * _GELU_K1 * x2))
    return g, dg


def _sigmoid(x):
    return 1.0 / (1.0 + jnp.exp(-x))


def _neg_expm1(x):
    series = -x * (1.0 + x * (0.5 + x * (1.0 / 6.0 + x * (1.0 / 24.0 + x * (1.0 / 120.0 + x * (1.0 / 720.0))))))
    return jnp.where(x > -0.1, series, 1.0 - jnp.exp(x))


def _softplus(z):
    return jnp.maximum(z, 0.0) + jnp.log(1.0 + jnp.exp(-jnp.abs(z)))


def _rowmean(x):
    return jnp.mean(x, axis=-1, keepdims=True)


def _colsum(x):
    return jnp.sum(x, axis=0, keepdims=True)


def _rms_stats(x):
    r = lax.rsqrt(_rowmean(x * x) + RMS_EPS)
    return r, x * r


def _rms_bwd(dy, n, r, g):
    dn = dy * g
    return r * (dn - n * _rowmean(dn * n)), dy * n


def _shift_prev(x, halo, d):
    cat = jnp.concatenate([halo, x], axis=0)
    return pltpu.roll(cat, d, 0)[HALO:, :]


def _prev_rows(cat):
    return cat[HALO:, :], pltpu.roll(cat, 1, 0)[HALO:, :], pltpu.roll(cat, 2, 0)[HALO:, :]


def _next_rows(cat):
    n = cat.shape[0]
    return cat[:n - HALO, :], pltpu.roll(cat, n - 1, 0)[:n - HALO, :], pltpu.roll(cat, n - 2, 0)[:n - HALO, :]


def _shift_next(x, halo, d):
    n = x.shape[0]
    cat = jnp.concatenate([x, halo], axis=0)
    return pltpu.roll(cat, n + HALO - d, 0)[:n, :]


def _dot(a, b):
    return jnp.dot(a, b, preferred_element_type=F32)


def _dot_nt(a, b):
    return lax.dot_general(a, b, (((1,), (1,)), ((), ())), preferred_element_type=F32)


def _dot_tn(a, b):
    return lax.dot_general(a, b, (((0,), (0,)), ((), ())), preferred_element_type=F32)


def _fwd_in_proj(x, g1, win_g, comm=None):
    s_len, d = x.shape
    nsh, _, ncol = win_g.shape
    tm = min(TM_IN, s_len)
    te = min(T_EPI, tm)

    def body(x_ref, g_ref, w_ref, p_ref, h_ref):
        @pl.when(pl.program_id(1) == 0)
        def _():
            g = g_ref[...]

            def chunk(q, carry):
                rows = pl.ds(pl.multiple_of(q * te, te), te)
                _, n = _rms_stats(x_ref[rows, :])
                h_ref[rows, :] = (n * g).astype(BF16)
                return carry

            lax.fori_loop(0, tm // te, chunk, 0)

        p_ref[...] = _dot(h_ref[...], w_ref[...])

    return _pcall(
        body, name="fwd_in_proj", grid=(s_len // tm, nsh),
        in_specs=[pl.BlockSpec((tm, d), lambda i, j: (i, 0)),
                  pl.BlockSpec((1, d), lambda i, j: (0, 0)),
                  pl.BlockSpec((None, d, ncol), lambda i, j: (j, 0, 0))],
        out_specs=[pl.BlockSpec((tm, ncol), lambda i, j: (i, j)),
                   pl.BlockSpec((tm, d), lambda i, j: (i, 0))],
        out_shape=[jax.ShapeDtypeStruct((s_len, nsh * ncol), F32), jax.ShapeDtypeStruct((s_len, d), BF16)],
        dims=("arbitrary", "arbitrary"), comm=comm)(x, g1, win_g)


def _gm_forward(z, gv, bv, wt_ref, bst_ref, vl_s, mix_s):
    ts = z.shape[0]
    ge = _gelu(z)
    u = ge[:, :GM_W]
    v = ge[:, GM_W:]
    vc = v - _rowmean(v)
    rs = lax.rsqrt(_rowmean(vc * vc) + LN_EPS)
    vh = vc * rs
    vl_s[...] = (vh * gv + bv).astype(BF16)
    for cc in range(ts // CHUNK):
        rows = slice(cc * CHUNK, (cc + 1) * CHUNK)
        for hh in range(HEADS):
            cols = slice(hh * HEAD_DIM, (hh + 1) * HEAD_DIM)
            mix_s[rows, cols] = _dot(wt_ref[hh], vl_s[rows, cols]) + bst_ref[:, cols]
    mixed = mix_s[...]
    return u, mixed, vh, rs, u * mixed


def _lru_gates(xl, halo, wc_ref, bc_ref, wa_ref, ba_ref, wx_ref, bx_ref, lam_ref, z_s):
    x1 = _shift_prev(xl, halo, 1)
    x2 = _shift_prev(xl, halo, 2)
    x3 = _shift_prev(xl, halo, 3)
    xr = bc_ref[...] + wc_ref[0:1, :] * x3 + wc_ref[1:2, :] * x2 + wc_ref[2:3, :] * x1 + wc_ref[3:4, :] * xl
    xrb = xr.astype(BF16)
    for hh in range(HEADS):
        cols = slice(hh * HEAD_DIM, (hh + 1) * HEAD_DIM)
        z_s[:, cols] = _dot(xrb[:, cols], wa_ref[hh])
        z_s[:, LRU_W + hh * HEAD_DIM:LRU_W + (hh + 1) * HEAD_DIM] = _dot(xrb[:, cols], wx_ref[hh])
    ra = _sigmoid(z_s[:, :LRU_W] + ba_ref[...])
    ri = _sigmoid(z_s[:, LRU_W:] + bx_ref[...])
    sp = _softplus(-lam_ref[...])
    la = (-LRU_C) * ra * sp
    a = jnp.exp(la)
    mult = jnp.sqrt(_neg_expm1(2.0 * la))
    return dict(x1=x1, x2=x2, x3=x3, xr=xr, xrb=xrb, ra=ra, ri=ri, sp=sp, a=a, mult=mult)


def _fwd_mixers(p, gv, bv, wt, bst, wc, bc, wa, ba, wx, bx, lam, ggm, glru, comm=None):
    s_len = p.shape[0]
    ts = min(TS_MIX, s_len)

    def body(pz_ref, pgl_ref, pxl_ref, gv_ref, bv_ref, wt_ref, bst_ref, wc_ref, bc_ref, wa_ref, ba_ref, wx_ref,
             bx_ref, lam_ref, ggm_ref, glru_ref, y_ref, hs_ref, tail_ref, h_ref, a_s, b_s, vl_s, mix_s, z_s):
        @pl.when(pl.program_id(0) == 0)
        def _():
            tail_ref[...] = jnp.zeros_like(tail_ref)
            h_ref[...] = jnp.zeros_like(h_ref)

        _, _, _, _, ygm = _gm_forward(pz_ref[...], gv_ref[...], bv_ref[...], wt_ref, bst_ref, vl_s, mix_s)
        _, ngm = _rms_stats(ygm)
        y_ref[:, :GM_W] = (ngm * ggm_ref[...]).astype(BF16)

        xl = pxl_ref[...]
        gts = _lru_gates(xl, tail_ref[...], wc_ref, bc_ref, wa_ref, ba_ref, wx_ref, bx_ref, lam_ref, z_s)
        tail_ref[...] = xl[ts - HALO:, :]
        a_s[...] = gts["a"]
        b_s[...] = gts["mult"] * (gts["ri"] * gts["xr"])

        def step(t, h):
            h = a_s[pl.ds(t, 1), :] * h + b_s[pl.ds(t, 1), :]
            hs_ref[pl.ds(t, 1), :] = h
            return h

        h_ref[...] = lax.fori_loop(0, ts, step, h_ref[...], unroll=8)
        yl = hs_ref[...] * _gelu(pgl_ref[...])
        _, nl = _rms_stats(yl)
        y_ref[:, GM_W:] = (nl * glru_ref[...]).astype(BF16)

    full = lambda shape: pl.BlockSpec(shape, lambda i: (0,) * len(shape))
    return _pcall(
        body, name="fwd_mixers", grid=(s_len // ts,),
        in_specs=[pl.BlockSpec((ts, 2 * GM_W), lambda i: (i, 0)),
                  pl.BlockSpec((ts, LRU_W), lambda i: (i, 2)),
                  pl.BlockSpec((ts, LRU_W), lambda i: (i, 3)),
                  full((1, GM_W)), full((1, GM_W)), full((HEADS, CHUNK, CHUNK)), full((CHUNK, GM_W)),
                  full((4, LRU_W)), full((1, LRU_W)), full((HEADS, HEAD_DIM, HEAD_DIM)), full((1, LRU_W)),
                  full((HEADS, HEAD_DIM, HEAD_DIM)), full((1, LRU_W)), full((1, LRU_W)), full((1, GM_W)),
                  full((1, LRU_W))],
        out_specs=[pl.BlockSpec((ts, GM_W + LRU_W), lambda i: (i, 0)), pl.BlockSpec((ts, LRU_W), lambda i: (i, 0))],
        out_shape=[jax.ShapeDtypeStruct((s_len, GM_W + LRU_W), BF16), jax.ShapeDtypeStruct((s_len, LRU_W), F32)],
        scratch_shapes=[pltpu.VMEM((HALO, LRU_W), F32), pltpu.VMEM((1, LRU_W), F32),
                        pltpu.VMEM((ts, LRU_W), F32), pltpu.VMEM((ts, LRU_W), F32),
                        pltpu.VMEM((ts, GM_W), BF16), pltpu.VMEM((ts, GM_W), F32), pltpu.VMEM((ts, 2 * LRU_W), F32)],
        dims=("arbitrary",), comm=comm)(p, p, p, gv, bv, wt, bst, wc, bc, wa, ba, wx, bx, lam, ggm, glru)


def _fwd_out_proj(x, y, wout_g, g2, comm=None):
    s_len, d = x.shape
    tm = min(TM_OUT, s_len)

    def body(x_ref, y_ref, w_ref, g_ref, x2_ref, h2_ref):
        x2 = x_ref[...] + _dot(y_ref[...], w_ref[...])
        x2_ref[...] = x2
        _, n = _rms_stats(x2)
        h2_ref[...] = (n * g_ref[...]).astype(BF16)

    return _pcall(
        body, name="fwd_out_proj", grid=(s_len // tm,),
        in_specs=[pl.BlockSpec((tm, d), lambda i: (i, 0)), pl.BlockSpec((tm, d), lambda i: (i, 0)),
                  pl.BlockSpec((d, d), lambda i: (0, 0)), pl.BlockSpec((1, d), lambda i: (0, 0))],
        out_specs=[pl.BlockSpec((tm, d), lambda i: (i, 0)), pl.BlockSpec((tm, d), lambda i: (i, 0))],
        out_shape=[jax.ShapeDtypeStruct((s_len, d), F32), jax.ShapeDtypeStruct((s_len, d), BF16)],
        dims=("arbitrary",), comm=comm)(x, y, wout_g, g2)


def _row_fetch(hbm_ref, buf_ref, sem, row0, rows):
    return pltpu.make_async_copy(hbm_ref.at[pl.ds(row0, rows), :], buf_ref, sem)


def _fwd_ffn(h2, wup_g, wfc, bfc, wdown_g, x2, gf, target):
    s_len, d = h2.shape
    nsh, _, ncol = wup_g.shape
    tn = TN_UP
    tn2 = 2 * tn
    f = nsh * ncol // 2
    nj = f // tn
    nps = ncol // tn
    tm = min(TM_UP, s_len)
    te = min(T_EPI, tm)
    tr = T_CHUNK

    def body(h_ref, wg_ref, wv_ref, wcg_ref, wcv_ref, bg_ref, bv_ref, wd_ref, g_ref, x2_hbm, t_hbm,
             upb_ref, cb_ref, act_ref, dx3_ref, dx3b_ref, loss_ref, dgf_ref, tail_ref, acc_ref, x2_buf, t_buf, up_ref,
             sems):
        i, j = pl.program_id(0), pl.program_id(1)
        row0 = pl.multiple_of(i * tm, tm)
        fetches = (_row_fetch(x2_hbm, x2_buf, sems.at[0], row0, tm), _row_fetch(t_hbm, t_buf, sems.at[1], row0, tm))

        @pl.when(jnp.logical_and(i == 0, j == 0))
        def _():
            tail_ref[...] = jnp.zeros_like(tail_ref)
            loss_ref[...] = jnp.zeros_like(loss_ref)
            dgf_ref[...] = jnp.zeros_like(dgf_ref)

        @pl.when(j == 0)
        def _():
            acc_ref[...] = jnp.zeros_like(acc_ref)
            for cp in fetches:
                cp.start()

        h = h_ref[...]
        up_ref[:, :tn] = _dot(h, wg_ref[...])
        up_ref[:, tn:] = _dot(h, wv_ref[...])

        planes = ((wcg_ref, bg_ref), (wcv_ref, bv_ref))
        for c in range(tn // LANES):
            cols = slice(c * LANES, (c + 1) * LANES)
            pcols = [slice(pln * tn + c * LANES, pln * tn + (c + 1) * LANES) for pln in range(2)]

            def conv(pln, u, u1, u2):
                wc_ref, b_ref = planes[pln]
                return b_ref[:, cols] + wc_ref[0:1, cols] * u2 + wc_ref[1:2, cols] * u1 + wc_ref[2:3, cols] * u

            def emit(rows, us):
                cs_ = [conv(pln, *us[pln]) for pln in range(2)]
                for pln in range(2):
                    upb_ref[rows, pcols[pln]] = us[pln][0].astype(BF16)
                    cb_ref[rows, pcols[pln]] = cs_[pln].astype(BF16)
                act_ref[rows, cols] = (_gelu(cs_[0]) * cs_[1]).astype(BF16)

            first = [_prev_rows(jnp.concatenate([tail_ref[j, :, pc], up_ref[0:tr, pc]], axis=0)) for pc in pcols]
            emit(slice(0, tr), first)

            for k in range(1, tm // tr):
                r0 = k * tr
                us = [_prev_rows(up_ref[r0 - HALO:r0 + tr, pc]) for pc in pcols]
                emit(slice(r0, r0 + tr), us)
        tail_ref[j] = up_ref[tm - HALO:tm, :]
        acc_ref[...] += _dot(act_ref[...], wd_ref[...])

        @pl.when(j == nj - 1)
        def _():
            for cp in fetches:
                cp.wait()
            g = g_ref[...]

            def chunk(k, carry):
                rows = pl.ds(pl.multiple_of(k * te, te), te)
                x3 = x2_buf[rows, :] + acc_ref[rows, :]
                r, n = _rms_stats(x3)
                err = n * g - t_buf[rows, :]
                loss_ref[...] += jnp.sum(err * err) * (0.5 / d)
                dx3, dgn = _rms_bwd(err * (1.0 / d), n, r, g)
                dgf_ref[...] += _colsum(dgn)
                dx3_ref[rows, :] = dx3
                dx3b_ref[rows, :] = dx3.astype(BF16)
                return carry

            lax.fori_loop(0, tm // te, chunk, 0)

    hbm = pl.BlockSpec(memory_space=pl.ANY)
    return _pcall(
        body, name="fwd_ffn", grid=(s_len // tm, nj),
        in_specs=[pl.BlockSpec((tm, d), lambda i, j: (i, 0)),
                  pl.BlockSpec((None, d, tn), lambda i, j: (j // nps, 0, j % nps)),
                  pl.BlockSpec((None, d, tn), lambda i, j: (nsh // 2 + j // nps, 0, j % nps)),
                  pl.BlockSpec((3, tn), lambda i, j: (0, j)), pl.BlockSpec((3, tn), lambda i, j: (0, nj + j)),
                  pl.BlockSpec((1, tn), lambda i, j: (0, j)), pl.BlockSpec((1, tn), lambda i, j: (0, nj + j)),
                  pl.BlockSpec((tn, d), lambda i, j: (j, 0)), pl.BlockSpec((1, d), lambda i, j: (0, 0)), hbm, hbm],
        out_specs=[pl.BlockSpec((tm, tn2), lambda i, j: (i, j)), pl.BlockSpec((tm, tn2), lambda i, j: (i, j)),
                   pl.BlockSpec((tm, tn), lambda i, j: (i, j)),
                   pl.BlockSpec((tm, d), lambda i, j: (i, 0)), pl.BlockSpec((tm, d), lambda i, j: (i, 0)),
                   pl.BlockSpec((8, 128), lambda i, j: (0, 0)), pl.BlockSpec((1, d), lambda i, j: (0, 0))],
        out_shape=[jax.ShapeDtypeStruct((s_len, 2 * f), BF16), jax.ShapeDtypeStruct((s_len, 2 * f), BF16),
                   jax.ShapeDtypeStruct((s_len, f), BF16),
                   jax.ShapeDtypeStruct((s_len, d), F32), jax.ShapeDtypeStruct((s_len, d), BF16),
                   jax.ShapeDtypeStruct((8, 128), F32), jax.ShapeDtypeStruct((1, d), F32)],
        scratch_shapes=[pltpu.VMEM((nj, HALO, tn2), F32), pltpu.VMEM((tm, d), F32), pltpu.VMEM((tm, d), F32),
                        pltpu.VMEM((tm, d), F32), pltpu.VMEM((tm, tn2), F32), pltpu.SemaphoreType.DMA((2,))],
        dims=("arbitrary", "arbitrary"))(h2, wup_g, wup_g, wfc, wfc, bfc, bfc, wdown_g, gf, x2, target)


def _bwd_ffn(dx3b, wdown_g, upb, cb, wfc, wup_g, dx3, x2, g2, comm=None):
    s_len, d = dx3b.shape
    nsh, _, ncol = wup_g.shape
    tn = TN_UP
    tn2 = 2 * tn
    f = nsh * ncol // 2
    nj = f // tn
    nps = ncol // tn
    tm = min(TM_UP, s_len)
    te = min(T_EPI, tm)
    nt = s_len // tm
    tr = T_CHUNK

    def body(dx_ref, w_ref, upb_ref, cb_ref, wcg_ref, wcv_ref, wug_ref, wuv_ref, g_ref, r_hbm, x_hbm,
             dup_ref, dwf_ref, dx2_ref, dx2b_ref, dg_ref, nxt_ref, acc_ref, r_buf, x_buf, da_s, dc_s, sems):
        i, j = pl.program_id(0), pl.program_id(1)
        ti = nt - 1 - i
        row0 = pl.multiple_of(ti * tm, tm)
        fetches = (_row_fetch(r_hbm, r_buf, sems.at[0], row0, tm), _row_fetch(x_hbm, x_buf, sems.at[1], row0, tm))

        @pl.when(jnp.logical_and(i == 0, j == 0))
        def _():
            dwf_ref[...] = jnp.zeros_like(dwf_ref)
            nxt_ref[...] = jnp.zeros_like(nxt_ref)
            dg_ref[...] = jnp.zeros_like(dg_ref)

        @pl.when(j == 0)
        def _():
            acc_ref[...] = jnp.zeros_like(acc_ref)
            for cp in fetches:
                cp.start()

        da_s[...] = _dot_nt(dx_ref[...], w_ref[...])
        planes = ((wcg_ref,), (wcv_ref,))
        for pln in range(2):
            dc_s[pln, tm:tm + HALO, :] = nxt_ref[pln, j]

        for c in range(tn // LANES):
            cols = slice(c * LANES, (c + 1) * LANES)
            pcols = [slice(pln * tn + c * LANES, pln * tn + (c + 1) * LANES) for pln in range(2)]
            for k in range(tm // tr):
                rows = slice(k * tr, (k + 1) * tr)
                dact = da_s[rows, cols]
                ge, gd = _gelu_and_grad(cb_ref[rows, pcols[0]].astype(F32))
                dc_s[0, rows, cols] = dact * cb_ref[rows, pcols[1]].astype(F32) * gd
                dc_s[1, rows, cols] = dact * ge

        for pln in range(2):
            nxt_ref[pln, j] = dc_s[pln, 0:HALO, :]

        def fold(v):
            out = v[0:8, :]
            for q in range(1, tr // 8):
                out = out + v[8 * q:8 * q + 8, :]
            return out

        for c in range(tn // LANES):
            cols = slice(c * LANES, (c + 1) * LANES)
            for pln in range(2):
                wc_ref = planes[pln][0]
                pc = slice(pln * tn + c * LANES, pln * tn + (c + 1) * LANES)
                sums = (jnp.zeros((8, LANES), F32),) * 4
                for k in range(tm // tr):
                    r0 = k * tr
                    dc, dc1, dc2 = _next_rows(dc_s[pln, r0:r0 + tr + HALO, cols])
                    dup = wc_ref[2:3, cols] * dc + wc_ref[1:2, cols] * dc1 + wc_ref[0:1, cols] * dc2
                    dup_ref[pln, r0:r0 + tr, cols] = dup.astype(BF16)
                    u = upb_ref[r0:r0 + tr, pc].astype(F32)
                    new = (fold(dc2 * u), fold(dc1 * u), fold(dc * u), fold(dc))
                    sums = tuple(a + b for a, b in zip(sums, new))
                for term in range(4):
                    dwf_ref[pln, j, term:term + 1, cols] += _colsum(sums[term])
        acc_ref[...] += _dot_nt(dup_ref[0], wug_ref[...]) + _dot_nt(dup_ref[1], wuv_ref[...])

        @pl.when(j == nj - 1)
        def _():
            for cp in fetches:
                cp.wait()
            g = g_ref[...]

            def chunk(k, carry):
                rows = pl.ds(pl.multiple_of(k * te, te), te)
                r, n = _rms_stats(x_buf[rows, :])
                dxn, dgn = _rms_bwd(acc_ref[rows, :], n, r, g)
                dg_ref[...] += _colsum(dgn)
                dx = r_buf[rows, :] + dxn
                dx2_ref[rows, :] = dx
                dx2b_ref[rows, :] = dx.astype(BF16)
                return carry

            lax.fori_loop(0, tm // te, chunk, 0)

    hbm = pl.BlockSpec(memory_space=pl.ANY)
    rev = lambda i: nt - 1 - i
    return _pcall(
        body, name="bwd_ffn", grid=(nt, nj),
        in_specs=[pl.BlockSpec((tm, d), lambda i, j: (rev(i), 0)),
                  pl.BlockSpec((tn, d), lambda i, j: (j, 0)),
                  pl.BlockSpec((tm, tn2), lambda i, j: (rev(i), j)),
                  pl.BlockSpec((tm, tn2), lambda i, j: (rev(i), j)),
                  pl.BlockSpec((3, tn), lambda i, j: (0, j)), pl.BlockSpec((3, tn), lambda i, j: (0, nj + j)),
                  pl.BlockSpec((None, d, tn), lambda i, j: (j // nps, 0, j % nps)),
                  pl.BlockSpec((None, d, tn), lambda i, j: (nsh // 2 + j // nps, 0, j % nps)),
                  pl.BlockSpec((1, d), lambda i, j: (0, 0)), hbm, hbm],
        out_specs=[pl.BlockSpec((2, tm, tn), lambda i, j: (0, rev(i), j)),
                   pl.BlockSpec((2, nj, 8, tn), lambda i, j: (0, 0, 0, 0)),
                   pl.BlockSpec((tm, d), lambda i, j: (rev(i), 0)), pl.BlockSpec((tm, d), lambda i, j: (rev(i), 0)),
                   pl.BlockSpec((1, d), lambda i, j: (0, 0))],
        out_shape=[jax.ShapeDtypeStruct((2, s_len, f), BF16), jax.ShapeDtypeStruct((2, nj, 8, tn), F32),
                   jax.ShapeDtypeStruct((s_len, d), F32), jax.ShapeDtypeStruct((s_len, d), BF16),
                   jax.ShapeDtypeStruct((1, d), F32)],
        scratch_shapes=[pltpu.VMEM((2, nj, HALO, tn), F32), pltpu.VMEM((tm, d), F32), pltpu.VMEM((tm, d), F32),
                        pltpu.VMEM((tm, d), F32), pltpu.VMEM((tm, tn), F32), pltpu.VMEM((2, tm + HALO, tn), F32),
                        pltpu.SemaphoreType.DMA((2,))],
        dims=("arbitrary", "arbitrary"), comm=comm)(dx3b, wdown_g, upb, cb, wfc, wfc, wup_g, wup_g, g2, dx3, x2)


def _bwd_in(dp, win_g, resid, x_in, g, comm=None):
    s_len, d = x_in.shape
    nsh, _, ncol = win_g.shape
    tm = min(TM_BW, s_len)
    te = min(T_EPI, tm)

    def body(dz_ref, w_ref, r_ref, x_ref, g_ref, dx_ref, dg_ref, acc_ref):
        i, k = pl.program_id(0), pl.program_id(1)

        @pl.when(jnp.logical_and(i == 0, k == 0))
        def _():
            dg_ref[...] = jnp.zeros_like(dg_ref)

        @pl.when(k == 0)
        def _():
            acc_ref[...] = jnp.zeros_like(acc_ref)

        acc_ref[...] += _dot_nt(dz_ref[...], w_ref[...])

        @pl.when(k == nsh - 1)
        def _():
            g = g_ref[...]

            def chunk(q, carry):
                rows = pl.ds(pl.multiple_of(q * te, te), te)
                r, n = _rms_stats(x_ref[rows, :])
                dxn, dgn = _rms_bwd(acc_ref[rows, :], n, r, g)
                dg_ref[...] += _colsum(dgn)
                dx_ref[rows, :] = r_ref[rows, :] + dxn
                return carry

            lax.fori_loop(0, tm // te, chunk, 0)

    return _pcall(
        body, name="bwd_in", grid=(s_len // tm, nsh),
        in_specs=[pl.BlockSpec((tm, ncol), lambda i, k: (i, k)),
                  pl.BlockSpec((None, d, ncol), lambda i, k: (k, 0, 0)),
                  pl.BlockSpec((tm, d), lambda i, k: (i, 0)), pl.BlockSpec((tm, d), lambda i, k: (i, 0)),
                  pl.BlockSpec((1, d), lambda i, k: (0, 0))],
        out_specs=[pl.BlockSpec((tm, d), lambda i, k: (i, 0)), pl.BlockSpec((1, d), lambda i, k: (0, 0))],
        out_shape=[jax.ShapeDtypeStruct((s_len, d), F32), jax.ShapeDtypeStruct((1, d), F32)],
        scratch_shapes=[pltpu.VMEM((tm, d), F32)],
        dims=("arbitrary", "arbitrary"), comm=comm)(dp, win_g, resid, x_in, g)


def _bwd_dy(dx2b, wout_g, comm=None):
    s_len, d = dx2b.shape
    tm = min(TM_OUT, s_len)

    def body(dx_ref, w_ref, dy_ref):
        dy_ref[...] = _dot_nt(dx_ref[...], w_ref[...])

    return _pcall(
        body, name="bwd_dy", grid=(s_len // tm,),
        in_specs=[pl.BlockSpec((tm, d), lambda i: (i, 0)), pl.BlockSpec((d, d), lambda i: (0, 0))],
        out_specs=pl.BlockSpec((tm, d), lambda i: (i, 0)),
        out_shape=jax.ShapeDtypeStruct((s_len, d), F32), dims=("arbitrary",), comm=comm)(dx2b, wout_g)


def _bwd_mixers(p, dy, hs, gv, bv, wt, wtt, bst, wc, bc, wa, wat, ba, wx, wxt, bx, lam, ggm, glru, comm=None):
    s_len = p.shape[0]
    ts = min(TS_MIX, s_len)
    nt = s_len // ts
    hb = ts // HALO

    def body(pz_ref, pgl_ref, pxl_ref, xh_ref, dy_ref, hs_ref, hh_ref, gv_ref, bv_ref, wt_ref, wtt_ref, bst_ref,
             wc_ref, bc_ref, wa_ref, wat_ref, ba_ref, wx_ref, wxt_ref, bx_ref, lam_ref, ggm_ref, glru_ref,
             dp_ref, dgv_ref, dbv_ref, dwt_ref, dbst_ref, dwc_ref, dbc_ref, dwa_ref, dba_ref, dwx_ref, dbx_ref,
             dsp_ref, dggm_ref, dglru_ref,
             carry_ref, nxt_ref, a_s, g_s, vl_s, mix_s, z_s, dm_s, dvl_s, dxr_s):
        i = pl.program_id(0)
        ti = nt - 1 - i

        @pl.when(i == 0)
        def _():
            for ref in (dgv_ref, dbv_ref, dwt_ref, dbst_ref, dwc_ref, dbc_ref, dwa_ref, dba_ref, dwx_ref, dbx_ref,
                        dsp_ref, dggm_ref, dglru_ref, carry_ref, nxt_ref):
                ref[...] = jnp.zeros_like(ref)

        z = pz_ref[...]
        u, mixed, vh, rs, ygm = _gm_forward(z, gv_ref[...], bv_ref[...], wt_ref, bst_ref, vl_s, mix_s)
        rg, ngm = _rms_stats(ygm)
        dygm, dgn = _rms_bwd(dy_ref[:, :GM_W], ngm, rg, ggm_ref[...])
        dggm_ref[...] += _colsum(dgn)
        du = dygm * mixed
        dmix = dygm * u
        dm_s[...] = dmix.astype(BF16)
        bsum = dmix[0:CHUNK, :]
        for cc in range(1, ts // CHUNK):
            bsum = bsum + dmix[cc * CHUNK:(cc + 1) * CHUNK, :]
        dbst_ref[...] += bsum
        for hh in range(HEADS):
            cols = slice(hh * HEAD_DIM, (hh + 1) * HEAD_DIM)
            dw = jnp.zeros((CHUNK, CHUNK), F32)
            for cc in range(ts // CHUNK):
                rows = slice(cc * CHUNK, (cc + 1) * CHUNK)
                dmb = dm_s[rows, cols]
                dw = dw + _dot_nt(dmb, vl_s[rows, cols])
                dvl_s[rows, cols] = _dot(wtt_ref[hh], dmb)
            dwt_ref[hh] += dw
        dvl = dvl_s[...]
        dgv_ref[...] += _colsum(dvl * vh)
        dbv_ref[...] += _colsum(dvl)
        dvh = dvl * gv_ref[...]
        dv = rs * (dvh - _rowmean(dvh) - vh * _rowmean(dvh * vh))
        _, gd = _gelu_and_grad(z)
        dp_ref[:, :GM_W] = (du * gd[:, :GM_W]).astype(BF16)
        dp_ref[:, GM_W:2 * GM_W] = (dv * gd[:, GM_W:]).astype(BF16)

        xl = pxl_ref[...]
        xhalo = jnp.where(ti == 0, 0.0, xh_ref[...])
        gts = _lru_gates(xl, xhalo, wc_ref, bc_ref, wa_ref, ba_ref, wx_ref, bx_ref, lam_ref, z_s)
        a, mult, ra, ri, xr, sp = gts["a"], gts["mult"], gts["ra"], gts["ri"], gts["xr"], gts["sp"]
        hs = hs_ref[...]
        hprev = _shift_prev(hs, jnp.where(ti == 0, 0.0, hh_ref[...]), 1)
        gl = pgl_ref[...]
        ggl, dggl = _gelu_and_grad(gl)
        yl = hs * ggl
        rl, nl = _rms_stats(yl)
        dyl, dgn = _rms_bwd(dy_ref[:, GM_W:], nl, rl, glru_ref[...])
        dglru_ref[...] += _colsum(dgn)
        dp_ref[:, 2 * GM_W:2 * GM_W + LRU_W] = (dyl * hs * dggl).astype(BF16)
        a_s[...] = a
        g_s[...] = dyl * ggl

        def step(k, carry):
            t = ts - 1 - k
            gt = g_s[pl.ds(t, 1), :] + carry
            g_s[pl.ds(t, 1), :] = gt
            return a_s[pl.ds(t, 1), :] * gt

        carry_ref[...] = lax.fori_loop(0, ts, step, carry_ref[...], unroll=8)
        gsc = g_s[...]
        da = gsc * hprev
        rix = ri * xr
        dmult = gsc * rix
        dri = gsc * mult * xr
        dxr = gsc * mult * ri
        dla = da * a - dmult * (a * a) / mult
        dsp_ref[...] += _colsum(dla * ra) * (-LRU_C)
        dza = (dla * sp) * (-LRU_C) * ra * (1.0 - ra)
        dzi = dri * ri * (1.0 - ri)
        dba_ref[...] += _colsum(dza)
        dbx_ref[...] += _colsum(dzi)
        dzab = dza.astype(BF16)
        dzib = dzi.astype(BF16)
        xrb = gts["xrb"]
        for hh in range(HEADS):
            cols = slice(hh * HEAD_DIM, (hh + 1) * HEAD_DIM)
            dwa_ref[hh] += _dot_tn(xrb[:, cols], dzab[:, cols])
            dwx_ref[hh] += _dot_tn(xrb[:, cols], dzib[:, cols])
            dxr_s[:, cols] = _dot(dzab[:, cols], wat_ref[hh]) + _dot(dzib[:, cols], wxt_ref[hh])
        dxr = dxr + dxr_s[...]
        dbc_ref[...] += _colsum(dxr)
        dwc_ref[0:1, :] += _colsum(dxr * gts["x3"])
        dwc_ref[1:2, :] += _colsum(dxr * gts["x2"])
        dwc_ref[2:3, :] += _colsum(dxr * gts["x1"])
        dwc_ref[3:4, :] += _colsum(dxr * xl)
        nxt = nxt_ref[...]
        nxt_ref[...] = dxr[:HALO, :]
        dxl = wc_ref[3:4, :] * dxr + wc_ref[2:3, :] * _shift_next(dxr, nxt, 1) \
            + wc_ref[1:2, :] * _shift_next(dxr, nxt, 2) + wc_ref[0:1, :] * _shift_next(dxr, nxt, 3)
        dp_ref[:, 2 * GM_W + LRU_W:] = dxl.astype(BF16)

    full = lambda shape: pl.BlockSpec(shape, lambda i: (0,) * len(shape))
    rev = lambda i: nt - 1 - i
    prev_blk = lambda i: jnp.maximum((nt - 1 - i) * hb - 1, 0)
    hhd = (HEADS, HEAD_DIM, HEAD_DIM)
    small_shapes = [(1, GM_W), (1, GM_W), (HEADS, CHUNK, CHUNK), (CHUNK, GM_W), (4, LRU_W), (1, LRU_W), hhd,
                    (1, LRU_W), hhd, (1, LRU_W), (1, LRU_W), (1, GM_W), (1, LRU_W)]
    return _pcall(
        body, name="bwd_mixers", grid=(nt,),
        in_specs=[pl.BlockSpec((ts, 2 * GM_W), lambda i: (rev(i), 0)),
                  pl.BlockSpec((ts, LRU_W), lambda i: (rev(i), 2)),
                  pl.BlockSpec((ts, LRU_W), lambda i: (rev(i), 3)),
                  pl.BlockSpec((HALO, LRU_W), lambda i: (prev_blk(i), 3)),
                  pl.BlockSpec((ts, GM_W + LRU_W), lambda i: (rev(i), 0)),
                  pl.BlockSpec((ts, LRU_W), lambda i: (rev(i), 0)),
                  pl.BlockSpec((HALO, LRU_W), lambda i: (prev_blk(i), 0)),
                  full((1, GM_W)), full((1, GM_W)), full((HEADS, CHUNK, CHUNK)), full((HEADS, CHUNK, CHUNK)),
                  full((CHUNK, GM_W)), full((4, LRU_W)), full((1, LRU_W)), full(hhd), full(hhd), full((1, LRU_W)),
                  full(hhd), full(hhd), full((1, LRU_W)), full((1, LRU_W)), full((1, GM_W)), full((1, LRU_W))],
        out_specs=[pl.BlockSpec((ts, 2 * GM_W + 2 * LRU_W), lambda i: (rev(i), 0))] + [full(s) for s in small_shapes],
        out_shape=[jax.ShapeDtypeStruct((s_len, 2 * GM_W + 2 * LRU_W), BF16)]
        + [jax.ShapeDtypeStruct(s, F32) for s in small_shapes],
        scratch_shapes=[pltpu.VMEM((1, LRU_W), F32), pltpu.VMEM((HALO, LRU_W), F32),
                        pltpu.VMEM((ts, LRU_W), F32), pltpu.VMEM((ts, LRU_W), F32),
                        pltpu.VMEM((ts, GM_W), BF16), pltpu.VMEM((ts, GM_W), F32), pltpu.VMEM((ts, 2 * LRU_W), F32),
                        pltpu.VMEM((ts, GM_W), BF16), pltpu.VMEM((ts, GM_W), F32), pltpu.VMEM((ts, LRU_W), F32)],
        dims=("arbitrary",), comm=comm)(p, p, p, p, dy, hs, hs, gv, bv, wt, wtt, bst, wc, bc, wa, wat, ba, wx, wxt, bx, lam,
                             ggm, glru)


def _bwd_weight(name, a, b, *, a_planes, b_planes, shard_rows, comm=None):
    _, s_len, ma = a.shape
    _, _, nb = b.shape
    m, n = a_planes * ma, b_planes * nb
    tk = min(TW_K, s_len)
    if shard_rows:
        rows, cols = m // N_CHIPS, n // 2
        tm, tn = _tile(rows, TW_M), _tile(cols, TW_N)
        out_idx = lambda i, j, k: (j * tn // cols, i * tm // rows, (i * tm % rows) // tm, (j * tn % cols) // tn)
    else:
        rows, cols = m // 2, n // N_CHIPS
        tm, tn = _tile(rows, TW_M), _tile(cols, TW_N)
        out_idx = lambda i, j, k: (i * tm // rows, j * tn // cols, (i * tm % rows) // tm, (j * tn % cols) // tn)
    nk = s_len // tk
    npa, npb = ma // tm, nb // tn

    def body(a_ref, b_ref, o_ref, ob_ref, acc_ref):
        k = pl.program_id(2)

        @pl.when(k == 0)
        def _():
            acc_ref[...] = jnp.zeros_like(acc_ref)

        acc_ref[...] += _dot_tn(a_ref[...], b_ref[...])

        @pl.when(k == nk - 1)
        def _():
            o_ref[...] = acc_ref[...]
            ob_ref[...] = acc_ref[...].astype(BF16)

    shape = (2, N_CHIPS, rows, cols)
    return _pcall(
        body, name=name, grid=(m // tm, n // tn, nk),
        in_specs=[pl.BlockSpec((None, tk, tm), lambda i, j, k: (i // npa, k, i % npa)),
                  pl.BlockSpec((None, tk, tn), lambda i, j, k: (j // npb, k, j % npb))],
        out_specs=[pl.BlockSpec((None, None, tm, tn), out_idx), pl.BlockSpec((None, None, tm, tn), out_idx)],
        out_shape=[jax.ShapeDtypeStruct(shape, F32), jax.ShapeDtypeStruct(shape, BF16)],
        scratch_shapes=[pltpu.VMEM((tm, tn), F32)],
        dims=("arbitrary", "arbitrary", "arbitrary"), comm=comm)(a, b)


def _mesh_pos():
    return lax.axis_index("x"), lax.axis_index("y"), lax.axis_index("c")


def _other_chips(x, y):
    return [(1 - x, y), (x, 1 - y), (1 - x, 1 - y)]


def _to_slot(name, a, cs, dtype):
    _, a_rows, b_cols = a.shape
    ta = min(T_ELEM, a_rows)

    def body(cs_ref, a_ref, o_ref):
        o_ref[...] = a_ref[...].astype(dtype)

    grid_spec = pltpu.PrefetchScalarGridSpec(
        num_scalar_prefetch=1, grid=(2, a_rows // ta),
        in_specs=[pl.BlockSpec((None, ta, b_cols), lambda h, r, cs_ref: (h, r, 0))],
        out_specs=pl.BlockSpec((None, None, ta, b_cols), lambda h, r, cs_ref: (cs_ref[1], h, r, 0)))
    return _pcall(body, name=name, grid_spec=grid_spec,
                  out_shape=jax.ShapeDtypeStruct((N_CHIPS,) + a.shape, dtype), dims=("arbitrary", "arbitrary"))(cs, a)


def _all_gather(bufs):
    n = len(bufs)

    def body(*refs):
        outs = refs[n:2 * n]
        send_sems, recv_sems = refs[2 * n:]
        x, y, c = _mesh_pos()
        s = 2 * x + y
        me, sib = (x, y, c), (x, y, 1 - c)
        chips = _other_chips(x, y)

        def rcopy(a, k, blk, to):
            return pltpu.make_async_remote_copy(src_ref=blk, dst_ref=blk, send_sem=send_sems.at[a * 6 + k],
                                                recv_sem=recv_sems.at[a * 6 + k], device_id=to, device_id_type=MESH)

        first = [rcopy(a, j, outs[a].at[s, c], (cx, cy, c)) for a in range(n) for j, (cx, cy) in enumerate(chips)]
        for cp in first:
            cp.start()
        passed = []
        for a in range(n):
            for j, (cx, cy) in enumerate(chips):
                blk = outs[a].at[2 * cx + cy, c]
                rcopy(a, j, blk, me).wait_recv()
                cp = rcopy(a, 3 + j, blk, sib)
                cp.start()
                passed.append(cp)
        for a in range(n):
            for j, (cx, cy) in enumerate(chips):
                rcopy(a, 3 + j, outs[a].at[2 * cx + cy, 1 - c], me).wait_recv()
        for cp in first + passed:
            cp.wait_send()

    return _pcall(
        body, name="all_gather_weights",
        in_specs=[ANY] * n, out_specs=[ANY] * n,
        out_shape=[jax.ShapeDtypeStruct(a.shape, a.dtype) for a in bufs],
        scratch_shapes=[pltpu.SemaphoreType.DMA((6 * n,)), pltpu.SemaphoreType.DMA((6 * n,))],
        aliases={a: a for a in range(n)})(*bufs)


def _same(a):
    return jax.ShapeDtypeStruct(a.shape, a.dtype)


def _job_gather(bufs, ici_parts, relay_parts):
    def copies(cin, cout, x, y, c):
        out = []
        for a, lo, n in ici_parts:
            blk = cout[a].at[2 * x + y, c, pl.ds(lo, n)]
            out += [(blk, blk, (cx, cy, c)) for cx, cy in _other_chips(x, y)]
        for a, lo, n in relay_parts:
            for cx, cy in _other_chips(x, y):
                blk = cout[a].at[2 * cx + cy, c, pl.ds(lo, n)]
                out.append((blk, blk, (x, y, 1 - c)))
        return out

    return _Job(bufs, [_same(b) for b in bufs], {a: a for a in range(len(bufs))},
                3 * (len(ici_parts) + len(relay_parts)), copies)


def _job_pair_swap(arrs):
    def copies(cin, cout, x, y, c):
        return [(cin[a].at[1 - c], cout[a], (x, y, 1 - c)) for a in range(len(arrs))]

    return _Job(arrs, [jax.ShapeDtypeStruct(a.shape[1:], a.dtype) for a in arrs], {}, len(arrs), copies)


def _job_chip_exchange(big, small=()):
    nb = len(big)
    arrs = list(big) + list(small)

    def copies(cin, cout, x, y, c):
        out = []
        for a in range(len(arrs)):
            for j, (cx, cy) in enumerate(_other_chips(x, y)):
                out.append((cin[a].at[2 * cx + cy] if a < nb else cin[a], cout[a].at[j], (cx, cy, c)))
        return out

    shapes = [a.shape[1:] for a in big] + [a.shape for a in small]
    return _Job(arrs, [jax.ShapeDtypeStruct((3,) + sh, a.dtype) for sh, a in zip(shapes, arrs)], {}, 3 * len(arrs), copies)


def _job_halves_swap(bufs):
    def copies(cin, cout, x, y, c):
        return [(cout[a].at[c], cout[a].at[c], (x, y, 1 - c)) for a in range(len(bufs))]

    return _Job(bufs, [_same(b) for b in bufs], {a: a for a in range(len(bufs))}, len(bufs), copies)


def _pair_sum(name, g32, recv, cs):
    _, nch, a_rows, b_cols = g32.shape
    ta = min(T_ELEM, a_rows)

    def body(cs_ref, g_ref, r_ref, pb_ref, own_ref):
        k = pl.program_id(1)
        v = g_ref[...] + r_ref[...].astype(F32)
        pb_ref[...] = v.astype(BF16)

        @pl.when(k == cs_ref[1])
        def _():
            own_ref[...] = v

    grid_spec = pltpu.PrefetchScalarGridSpec(
        num_scalar_prefetch=1, grid=(a_rows // ta, nch),
        in_specs=[pl.BlockSpec((None, None, ta, b_cols), lambda r, k, cs_ref: (cs_ref[0], k, r, 0)),
                  pl.BlockSpec((None, ta, b_cols), lambda r, k, cs_ref: (k, r, 0))],
        out_specs=[pl.BlockSpec((None, ta, b_cols), lambda r, k, cs_ref: (k, r, 0)),
                   pl.BlockSpec((ta, b_cols), lambda r, k, cs_ref: (r, 0))])
    return _pcall(
        body, name=name, grid_spec=grid_spec,
        out_shape=[jax.ShapeDtypeStruct((nch, a_rows, b_cols), BF16), jax.ShapeDtypeStruct((a_rows, b_cols), F32)],
        dims=("arbitrary", "arbitrary"))(cs, g32, recv)


def _small_pair_sum(mine, recv, cs):
    _, r, ccols = mine.shape

    def body(cs_ref, a_ref, b_ref, o_ref):
        o_ref[...] = a_ref[...] + b_ref[...]

    grid_spec = pltpu.PrefetchScalarGridSpec(
        num_scalar_prefetch=1, grid=(1,),
        in_specs=[pl.BlockSpec((None, r, ccols), lambda i, cs_ref: (cs_ref[0], 0, 0)),
                  pl.BlockSpec((r, ccols), lambda i, cs_ref: (0, 0))],
        out_specs=pl.BlockSpec((r, ccols), lambda i, cs_ref: (0, 0)))
    return _pcall(body, name="small_pair_sum", grid_spec=grid_spec,
                  out_shape=jax.ShapeDtypeStruct((r, ccols), F32), dims=("arbitrary",))(cs, mine, recv)


def _chip_sum(name, own, recv, cs):
    a_rows, b_cols = own.shape
    ta = min(T_ELEM, a_rows)

    def body(cs_ref, o_ref, r_ref, f_ref):
        f_ref[...] = ((o_ref[...] + r_ref[0].astype(F32)) + r_ref[1].astype(F32)) + r_ref[2].astype(F32)

    grid_spec = pltpu.PrefetchScalarGridSpec(
        num_scalar_prefetch=1, grid=(a_rows // ta,),
        in_specs=[pl.BlockSpec((ta, b_cols), lambda r, cs_ref: (r, 0)),
                  pl.BlockSpec((3, ta, b_cols), lambda r, cs_ref: (0, r, 0))],
        out_specs=pl.BlockSpec((None, ta, b_cols), lambda r, cs_ref: (cs_ref[0], r, 0)))
    return _pcall(body, name=name, grid_spec=grid_spec,
                  out_shape=jax.ShapeDtypeStruct((2, a_rows, b_cols), F32), dims=("arbitrary",))(cs, own, recv)


def _small_chip_sum(pair, recv, cs):
    r, ccols = pair.shape

    def body(cs_ref, p_ref, r_ref, o_ref):
        s = cs_ref[1]
        own = p_ref[...]
        total = None
        for k in range(N_CHIPS):
            flip = jnp.bitwise_xor(s, k)
            term = jnp.where(flip == 0, own, jnp.where(flip == 2, r_ref[0], jnp.where(flip == 1, r_ref[1], r_ref[2])))
            total = term if total is None else total + term
        o_ref[...] = total

    grid_spec = pltpu.PrefetchScalarGridSpec(
        num_scalar_prefetch=1, grid=(1,),
        in_specs=[pl.BlockSpec((r, ccols), lambda i, cs_ref: (0, 0)),
                  pl.BlockSpec((3, r, ccols), lambda i, cs_ref: (0, 0, 0))],
        out_specs=pl.BlockSpec((None, r, ccols), lambda i, cs_ref: (cs_ref[0], 0, 0)))
    return _pcall(body, name="small_chip_sum", grid_spec=grid_spec,
                  out_shape=jax.ShapeDtypeStruct((2, r, ccols), F32), dims=("arbitrary",))(cs, pair, recv)


def _adamw(name, w, g, m, v, *, halves, comm=None):
    rows, cols = w.shape
    tr, tc = min(T_ELEM, rows), PACK_COLS
    c1 = 1.0 - ADAM_B1 ** ADAM_STEP
    c2 = 1.0 - ADAM_B2 ** ADAM_STEP

    def body(w_ref, g_ref, m_ref, v_ref, d_ref, mo_ref, vo_ref, go_ref):
        g_ = g_ref[...]
        m_ = ADAM_B1 * m_ref[...] + (1.0 - ADAM_B1) * g_
        v_ = ADAM_B2 * v_ref[...] + (1.0 - ADAM_B2) * (g_ * g_)
        mo_ref[...] = m_
        vo_ref[...] = v_
        go_ref[...] = g_
        d_ref[...] = (-ADAM_LR) * ((m_ / c1) / (jnp.sqrt(v_ / c2) + ADAM_EPS) + ADAM_WD * w_ref[...])

    spec = pl.BlockSpec((tr, tc), lambda r, j: (r, j))
    if halves == "rows":
        nrh = rows // 2 // tr
        g_spec = pl.BlockSpec((None, tr, tc), lambda r, j: (r // nrh, r % nrh, j))
    elif halves == "cols":
        nch = cols // 2 // tc
        g_spec = pl.BlockSpec((None, tr, tc), lambda r, j: (j // nch, r, j % nch))
    else:
        g_spec = spec
    return _pcall(body, name=name, grid=(rows // tr, cols // tc), in_specs=[spec, g_spec, spec, spec],
                  out_specs=[spec] * 4, out_shape=[jax.ShapeDtypeStruct((rows, cols), F32)] * 4,
                  dims=("arbitrary", "arbitrary"), comm=comm)(w, g, m, v)


def _pack(parts, rows):
    flat = jnp.concatenate([a.reshape(-1) for a in parts])
    return jnp.pad(flat, (0, rows * PACK_COLS - flat.shape[0])).reshape(rows, PACK_COLS)


def _unpack(buf, shapes):
    flat = buf.reshape(-1)
    out, off = [], 0
    for sh in shapes:
        size = math.prod(sh)
        out.append(flat[off:off + size].reshape(sh))
        off += size
    return out


def _pack_rows(shapes, multiple):
    total = sum(math.prod(sh) for sh in shapes)
    rows = -(-total // PACK_COLS)
    return -(-rows // multiple) * multiple


SMALL = ["norm1_g", "gm_v_g", "gm_v_b", "gm_ws", "gm_bs", "lru_conv_w", "lru_conv_b", "lru_wa", "lru_ba", "lru_wx",
         "lru_bx", "lru_lambda", "gm_out_g", "lru_out_g", "norm2_g", "ffn_conv_w", "ffn_conv_b", "final_g"]
BIG = ["w_in", "w_out", "ffn_w_up", "ffn_w_down"]
ORDER = ["norm1_g", "w_in", "gm_v_g", "gm_v_b", "gm_ws", "gm_bs", "lru_conv_w", "lru_conv_b", "lru_wa", "lru_ba",
         "lru_wx", "lru_bx", "lru_lambda", "gm_out_g", "lru_out_g", "w_out", "norm2_g", "ffn_w_up", "ffn_conv_w",
         "ffn_conv_b", "ffn_w_down", "final_g"]


def kernel(x, norm1_g, w_in, gm_v_g, gm_v_b, gm_ws, gm_bs, lru_conv_w, lru_conv_b, lru_wa, lru_ba, lru_wx, lru_bx, lru_lambda, gm_out_g, lru_out_g, w_out, norm2_g, ffn_w_up, ffn_conv_w, ffn_conv_b, ffn_w_down, final_g, loss_target, m_norm1_g, m_w_in, m_gm_v_g, m_gm_v_b, m_gm_ws, m_gm_bs, m_lru_conv_w, m_lru_conv_b, m_lru_wa, m_lru_ba, m_lru_wx, m_lru_bx, m_lru_lambda, m_gm_out_g, m_lru_out_g, m_w_out, m_norm2_g, m_ffn_w_up, m_ffn_conv_w, m_ffn_conv_b, m_ffn_w_down, m_final_g, v_norm1_g, v_w_in, v_gm_v_g, v_gm_v_b, v_gm_ws, v_gm_bs, v_lru_conv_w, v_lru_conv_b, v_lru_wa, v_lru_ba, v_lru_wx, v_lru_bx, v_lru_lambda, v_gm_out_g, v_lru_out_g, v_w_out, v_norm2_g, v_ffn_w_up, v_ffn_conv_w, v_ffn_conv_b, v_ffn_w_down, v_final_g):
    w = dict(norm1_g=norm1_g, w_in=w_in, gm_v_g=gm_v_g, gm_v_b=gm_v_b, gm_ws=gm_ws, gm_bs=gm_bs, lru_conv_w=lru_conv_w, lru_conv_b=lru_conv_b, lru_wa=lru_wa, lru_ba=lru_ba, lru_wx=lru_wx, lru_bx=lru_bx, lru_lambda=lru_lambda, gm_out_g=gm_out_g, lru_out_g=lru_out_g, w_out=w_out, norm2_g=norm2_g, ffn_w_up=ffn_w_up, ffn_conv_w=ffn_conv_w, ffn_conv_b=ffn_conv_b, ffn_w_down=ffn_w_down, final_g=final_g)
    m = dict(norm1_g=m_norm1_g, w_in=m_w_in, gm_v_g=m_gm_v_g, gm_v_b=m_gm_v_b, gm_ws=m_gm_ws, gm_bs=m_gm_bs, lru_conv_w=m_lru_conv_w, lru_conv_b=m_lru_conv_b, lru_wa=m_lru_wa, lru_ba=m_lru_ba, lru_wx=m_lru_wx, lru_bx=m_lru_bx, lru_lambda=m_lru_lambda, gm_out_g=m_gm_out_g, lru_out_g=m_lru_out_g, w_out=m_w_out, norm2_g=m_norm2_g, ffn_w_up=m_ffn_w_up, ffn_conv_w=m_ffn_conv_w, ffn_conv_b=m_ffn_conv_b, ffn_w_down=m_ffn_w_down, final_g=m_final_g)
    v = dict(norm1_g=v_norm1_g, w_in=v_w_in, gm_v_g=v_gm_v_g, gm_v_b=v_gm_v_b, gm_ws=v_gm_ws, gm_bs=v_gm_bs, lru_conv_w=v_lru_conv_w, lru_conv_b=v_lru_conv_b, lru_wa=v_lru_wa, lru_ba=v_lru_ba, lru_wx=v_lru_wx, lru_bx=v_lru_bx, lru_lambda=v_lru_lambda, gm_out_g=v_gm_out_g, lru_out_g=v_lru_out_g, w_out=v_w_out, norm2_g=v_norm2_g, ffn_w_up=v_ffn_w_up, ffn_conv_w=v_ffn_conv_w, ffn_conv_b=v_ffn_conv_b, ffn_w_down=v_ffn_w_down, final_g=v_final_g)

    mx, my, mc = _mesh_pos()
    shard = 2 * mx + my
    cs = jnp.stack([mc, shard]).astype(jnp.int32)

    xs = x[0]
    tgt = loss_target[0]
    s_len, d = xs.shape

    halves = lambda a: a.reshape((2, a.shape[0] // 2) + a.shape[1:])
    slot = {k: _to_slot("slot_" + k, halves(w[k][0]), cs, BF16) for k in BIG}
    win_b, wc_b, wfc_b = _all_gather([slot["w_in"],
                                      _to_slot("slot_lru_conv_w", w["lru_conv_w"][0].reshape(2, 4, -1), cs, F32),
                                      _to_slot("slot_ffn_conv_w", w["ffn_conv_w"][0].reshape(2, 12, -1), cs, F32)])
    win_g = win_b.reshape(N_CHIPS, d, -1)
    wc = wc_b.reshape(N_CHIPS, 4, -1).transpose(1, 0, 2).reshape(4, -1)
    wfc = wfc_b.reshape(N_CHIPS, 3, -1).transpose(1, 0, 2).reshape(3, -1)
    wout_b, wup_b, wdown_b = slot["w_out"], slot["ffn_w_up"], slot["ffn_w_down"]
    r_out, r_up, r_dn = wout_b.shape[2], wup_b.shape[2] // 2, wdown_b.shape[2] // 2

    tril = jnp.tril(jnp.ones((CHUNK, CHUNK), bool))
    wt32 = jnp.where(tril[None], w["gm_ws"][0], 0.0)
    wt = wt32.astype(BF16)
    wtt = wt32.transpose(0, 2, 1).astype(BF16)
    bst = jnp.repeat(w["gm_bs"][0].T, HEAD_DIM, axis=1)
    wa = w["lru_wa"][0].astype(BF16)
    wx = w["lru_wx"][0].astype(BF16)
    wat = w["lru_wa"][0].transpose(0, 2, 1).astype(BF16)
    wxt = w["lru_wx"][0].transpose(0, 2, 1).astype(BF16)
    ba = w["lru_ba"][0].reshape(1, -1)
    bx = w["lru_bx"][0].reshape(1, -1)
    gf = w["final_g"].reshape(1, -1)

    p, h1, wout_b, wup_b = _fwd_in_proj(
        xs, w["norm1_g"], win_g, comm=_job_gather([wout_b, wup_b], [(0, 0, r_out), (1, 0, r_up)], []))
    y, hs, wout_b, wup_b, wdown_b = _fwd_mixers(
        p, w["gm_v_g"], w["gm_v_b"], wt, bst, wc, w["lru_conv_b"], wa, ba, wx, bx, w["lru_lambda"],
        w["gm_out_g"], w["lru_out_g"],
        comm=_job_gather([wout_b, wup_b, wdown_b], [(1, r_up, r_up), (2, 0, r_dn)], [(0, 0, r_out), (1, 0, r_up)]))
    wout_g = wout_b.reshape(-1, d)
    x2, h2, wup_b, wdown_b = _fwd_out_proj(
        xs, y, wout_g, w["norm2_g"], comm=_job_gather([wup_b, wdown_b], [(1, r_dn, r_dn)], [(0, r_up, r_up), (1, 0, r_dn)]))
    wdown_b, = _comm_call("gather_tail", _job_gather([wdown_b], [], [(0, r_dn, r_dn)]))
    wup_g = wup_b.reshape(N_CHIPS, d, -1)
    wdown_g = wdown_b.reshape(-1, d)
    upb, cb, act, dx3, dx3b, loss_tile, dgf = _fwd_ffn(h2, wup_g, wfc, w["ffn_conv_b"], wdown_g, x2, gf, tgt)
    loss = lax.psum(loss_tile[0, 0], ("x", "y", "c"))

    adam = {}

    def adamw_big(name, gfull, hv, comm=None):
        r2 = lambda a: a.reshape(w[name].shape[1:])
        res = _adamw("adamw_" + name, r2(w[name]), gfull, r2(m[name]), r2(v[name]), halves=hv, comm=comm)
        adam[name] = res[:4]
        return res[4:]

    gd32, gdb = _bwd_weight("bwd_w_down", act[None], dx3b[None], a_planes=1, b_planes=1, shard_rows=True)
    dup, dwf, dx2, dx2b, dg2, rcv = _bwd_ffn(dx3b, wdown_g, upb, cb, wfc, wup_g, dx3, x2, w["norm2_g"],
                                             comm=_job_pair_swap([gdb]))
    pb_dn, own_dn = _pair_sum("pair_sum_3", gd32, rcv, cs)
    gu32, gub, got = _bwd_weight("bwd_w_up", h2[None], dup, a_planes=1, b_planes=2, shard_rows=False,
                                 comm=_job_chip_exchange([pb_dn]))
    red_dn = _chip_sum("chip_sum_3", own_dn, got, cs)
    go32, gob, rcv, red_dn = _bwd_weight("bwd_w_out", y[None], dx2b[None], a_planes=1, b_planes=1, shard_rows=True,
                                         comm=_merge_jobs([_job_pair_swap([gub]), _job_halves_swap([red_dn])]))
    pb_up, own_up = _pair_sum("pair_sum_2", gu32, rcv, cs)
    adamw_big("ffn_w_down", red_dn, "cols")
    dy, rcv = _bwd_dy(dx2b, wout_g, comm=_job_pair_swap([gob]))
    pb_out, own_out = _pair_sum("pair_sum_1", go32, rcv, cs)
    (dp, dgv, dbv, dwt, dbst, dwc, dbc, dwa, dba, dwx, dbx, dsp, dggm, dglru, got_up, got_out) = _bwd_mixers(
        p, dy, hs, w["gm_v_g"], w["gm_v_b"], wt, wtt, bst, wc, w["lru_conv_b"], wa, wat, ba, wx, wxt, bx,
        w["lru_lambda"], w["gm_out_g"], w["lru_out_g"], comm=_job_chip_exchange([pb_up, pb_out]))
    red_up = _chip_sum("chip_sum_2", own_up, got_up, cs)
    red_out = _chip_sum("chip_sum_1", own_out, got_out, cs)
    gi32, gib, red_up, red_out = _bwd_weight("bwd_w_in", h1[None], dp[None], a_planes=1, b_planes=1, shard_rows=False,
                                             comm=_job_halves_swap([red_up, red_out]))
    rcv, = _comm_call("w_in_swap", _job_pair_swap([gib]))
    pb_in, own_in = _pair_sum("pair_sum_0", gi32, rcv, cs)
    grad_x, dg1, got_in = _bwd_in(dp, win_g, dx2, xs, w["norm1_g"], comm=_job_chip_exchange([pb_in]))
    red_in = _chip_sum("chip_sum_0", own_in, got_in, cs)

    dwfc = dwf[:, :, :3].transpose(2, 0, 1, 3).reshape(3, -1)
    dbfc = dwf[:, :, 3].reshape(1, -1)
    dlam = dsp * (-_sigmoid(-w["lru_lambda"]))
    small_grads = dict(
        norm1_g=dg1, gm_v_g=dgv, gm_v_b=dbv, gm_ws=jnp.where(tril[None], dwt, 0.0),
        gm_bs=dbst.reshape(CHUNK, HEADS, HEAD_DIM).sum(-1).T, lru_conv_w=dwc, lru_conv_b=dbc, lru_wa=dwa, lru_ba=dba,
        lru_wx=dwx, lru_bx=dbx, lru_lambda=dlam, gm_out_g=dggm, lru_out_g=dglru, norm2_g=dg2, ffn_conv_w=dwfc,
        ffn_conv_b=dbfc, final_g=dgf)
    full_shapes = [small_grads[k].shape for k in SMALL]
    rows_full = _pack_rows(full_shapes, 16)
    gpack = _pack([small_grads[k] for k in SMALL], rows_full).reshape(2, rows_full // 2, PACK_COLS)

    rcv, = _comm_call("small_swap", _job_pair_swap([gpack]))
    small_pair = _small_pair_sum(gpack, rcv, cs)
    got_small, = _comm_call("small_exchange", _job_chip_exchange([], [small_pair]))
    small_half = _small_chip_sum(small_pair, got_small, cs)
    red_in, small_full = _comm_call("tail_halves_swap", _job_halves_swap([red_in, small_half]))
    adamw_big("ffn_w_up", red_up, "rows")
    adamw_big("w_out", red_out, "cols")
    adamw_big("w_in", red_in, "rows")

    grads = {}
    for name, g in zip(SMALL, _unpack(small_full, full_shapes)):
        blk = w[name].shape[1:] if w[name].ndim > 1 else w[name].shape
        if name in ("lru_conv_w", "ffn_conv_w"):
            g = lax.dynamic_slice_in_dim(g, shard * blk[1], blk[1], axis=1)
        grads[name] = g.reshape(blk)

    delta, new_m, new_v = {}, {}, {}
    for name in BIG:
        delta[name], new_m[name], new_v[name], grads[name] = adam[name]
    blk_shapes = [grads[k].shape for k in SMALL]
    rows_blk = _pack_rows(blk_shapes, T_ELEM)
    packs = [_pack([src[k] for k in SMALL], rows_blk) for src in (w, grads, m, v)]
    outs = _adamw("adamw_small", *packs, halves=None)
    for dst, buf in zip((delta, new_m, new_v), outs):
        for name, a in zip(SMALL, _unpack(buf, blk_shapes)):
            dst[name] = a

    def shaped(dct):
        return [dct[k].reshape(w[k].shape) for k in ORDER]

    return (loss, grad_x[None], *shaped(grads), *shaped(delta), *shaped(new_m), *shaped(new_v))
```

```python
import functools
import math

import jax
import jax.numpy as jnp
from jax import lax
from jax.experimental import pallas as pl
from jax.experimental.pallas import tpu as pltpu

F32 = jnp.float32
BF16 = jnp.bfloat16
MESH = pl.DeviceIdType.MESH
ANY = pl.BlockSpec(memory_space=pltpu.HBM)

GM_W = 1024
LRU_W = 1024
CHUNK = 128
HEADS = 8
HEAD_DIM = 128
LRU_C = 8.0
RMS_EPS = 1e-6
LN_EPS = 1e-5
ADAM_LR = 0.001
ADAM_B1 = 0.9
ADAM_B2 = 0.999
ADAM_EPS = 1e-08
ADAM_WD = 0.01
ADAM_STEP = 10

N_CHIPS = 4
HALO = 8
PACK_COLS = 1024
VMEM_LIMIT = 56 * 1024 * 1024

TM_IN = 1024
TS_MIX = 256
TM_OUT = 512
TM_UP = 512
TN_UP = 512
T_EPI = 128
T_CHUNK = 16
TM_BW = 512
TW_M = 1024
TW_N = 1024
TW_K = 2048
T_ELEM = 256
LANES = 128


def _tile(dim, cap):
    t = min(cap, dim) // LANES * LANES
    while dim % t:
        t -= LANES
    return t

_GELU_K0 = 0.7978845608028654
_GELU_K1 = 0.044715


class _Job:
    def __init__(self, ins, out_shapes, aliases, n, copies):
        self.ins, self.out_shapes, self.aliases, self.n, self.copies = list(ins), list(out_shapes), dict(aliases), n, copies

    def make(self, cin, cout, send_sems, recv_sems):
        x, y, c = _mesh_pos()
        return [pltpu.make_async_remote_copy(src_ref=src, dst_ref=dst, send_sem=send_sems.at[k], recv_sem=recv_sems.at[k],
                                             device_id=dev, device_id_type=MESH)
                for k, (src, dst, dev) in enumerate(self.copies(cin, cout, x, y, c))]


def _merge_jobs(jobs):
    ins, outs, aliases, spans = [], [], {}, []
    for jb in jobs:
        spans.append((len(ins), len(jb.ins), len(outs), len(jb.out_shapes)))
        aliases.update({len(ins) + a: len(outs) + b for a, b in jb.aliases.items()})
        ins += jb.ins
        outs += jb.out_shapes

    def copies(cin, cout, x, y, c):
        out = []
        for jb, (i0, ni, o0, no) in zip(jobs, spans):
            out += jb.copies(cin[i0:i0 + ni], cout[o0:o0 + no], x, y, c)
        return out

    return _Job(ins, outs, aliases, sum(jb.n for jb in jobs), copies)


def _pcall(body, *, name, out_shape, grid=None, in_specs=None, out_specs=None, scratch_shapes=(),
           grid_spec=None, dims=None, aliases=None, comm=None):
    params = pltpu.CompilerParams(dimension_semantics=dims, vmem_limit_bytes=VMEM_LIMIT)
    kw = dict(name=name, compiler_params=params)
    if grid_spec is not None:
        if aliases:
            kw["input_output_aliases"] = aliases
        return pl.pallas_call(body, grid_spec=grid_spec, out_shape=out_shape, **kw)
    scratch_shapes = list(scratch_shapes)
    if comm is not None:
        out_shape = list(out_shape) if isinstance(out_shape, (list, tuple)) else [out_shape]
        out_specs = list(out_specs) if isinstance(out_specs, (list, tuple)) else [out_specs]
        n_in, n_out, n_scr = len(in_specs), len(out_shape), len(scratch_shapes)
        n_ci, n_co = len(comm.ins), len(comm.out_shapes)
        inner, steps = body, tuple(grid)

        def body(*refs):
            o0 = n_in + n_ci
            s0 = o0 + n_out + n_co
            cps = comm.make(refs[n_in:o0], refs[o0 + n_out:s0], refs[s0 + n_scr], refs[s0 + n_scr + 1])
            ids = [pl.program_id(k) for k in range(len(steps))]

            @pl.when(functools.reduce(jnp.logical_and, [i == 0 for i in ids]))
            def _():
                for cp in cps:
                    cp.start()

            inner(*refs[:n_in], *refs[o0:o0 + n_out], *refs[s0:s0 + n_scr])

            @pl.when(functools.reduce(jnp.logical_and, [i == n - 1 for i, n in zip(ids, steps)]))
            def _():
                for cp in cps:
                    cp.wait()

        in_specs = list(in_specs) + [ANY] * n_ci
        out_specs = out_specs + [ANY] * n_co
        out_shape = out_shape + comm.out_shapes
        scratch_shapes = scratch_shapes + [pltpu.SemaphoreType.DMA((comm.n,)), pltpu.SemaphoreType.DMA((comm.n,))]
        aliases = {**(aliases or {}), **{n_in + a: n_out + b for a, b in comm.aliases.items()}}
    if aliases:
        kw["input_output_aliases"] = aliases
    if grid is not None:
        kw["grid"] = grid
    call = pl.pallas_call(body, in_specs=in_specs, out_specs=out_specs, scratch_shapes=scratch_shapes,
                          out_shape=out_shape, **kw)
    if comm is None:
        return call
    return lambda *args: call(*args, *comm.ins)


def _comm_call(name, job):
    n_ci, n_co = len(job.ins), len(job.out_shapes)

    def body(*refs):
        cps = job.make(refs[:n_ci], refs[n_ci:n_ci + n_co], refs[n_ci + n_co], refs[n_ci + n_co + 1])
        for cp in cps:
            cp.start()
        for cp in cps:
            cp.wait()

    return pl.pallas_call(
        body, name=name, in_specs=[ANY] * n_ci, out_specs=[ANY] * n_co, out_shape=job.out_shapes,
        scratch_shapes=[pltpu.SemaphoreType.DMA((job.n,)), pltpu.SemaphoreType.DMA((job.n,))],
        input_output_aliases=job.aliases)(*job.ins)


def _gelu(x):
    t = jnp.tanh(_GELU_K0 * (x + _GELU_K1 * (x * x * x)))
    return 0.5 * x * (1.0 + t)


def _gelu_and_grad(x):
    x2 = x * x
    t = jnp.tanh(_GELU_K0 * (x + _GELU_K1 * (x2 * x)))
    g = 0.5 * x * (1.0 + t)
    dg = 0.5 * (1.0 + t) + 0.5 * x * (1.0 - t * t) * (_GELU_K0 * (1.0 + 3.0 * _GELU_K1 * x2))
    return g, dg


def _sigmoid(x):
    return 1.0 / (1.0 + jnp.exp(-x))


def _neg_expm1(x):
    series = -x * (1.0 + x * (0.5 + x * (1.0 / 6.0 + x * (1.0 / 24.0 + x * (1.0 / 120.0 + x * (1.0 / 720.0))))))
    return jnp.where(x > -0.1, series, 1.0 - jnp.exp(x))


def _softplus(z):
    return jnp.maximum(z, 0.0) + jnp.log(1.0 + jnp.exp(-jnp.abs(z)))


def _rowmean(x):
    return jnp.mean(x, axis=-1, keepdims=True)


def _colsum(x):
    return jnp.sum(x, axis=0, keepdims=True)


def _rms_stats(x):
    r = lax.rsqrt(_rowmean(x * x) + RMS_EPS)
    return r, x * r


def _rms_bwd(dy, n, r, g):
    dn = dy * g
    return r * (dn - n * _rowmean(dn * n)), dy * n


def _shift_prev(x, halo, d):
    cat = jnp.concatenate([halo, x], axis=0)
    return pltpu.roll(cat, d, 0)[HALO:, :]


def _prev_rows(cat):
    return cat[HALO:, :], pltpu.roll(cat, 1, 0)[HALO:, :], pltpu.roll(cat, 2, 0)[HALO:, :]


def _next_rows(cat):
    n = cat.shape[0]
    return cat[:n - HALO, :], pltpu.roll(cat, n - 1, 0)[:n - HALO, :], pltpu.roll(cat, n - 2, 0)[:n - HALO, :]


def _shift_next(x, halo, d):
    n = x.shape[0]
    cat = jnp.concatenate([x, halo], axis=0)
    return pltpu.roll(cat, n + HALO - d, 0)[:n, :]


def _dot(a, b):
    return jnp.dot(a, b, preferred_element_type=F32)


def _dot_nt(a, b):
    return lax.dot_general(a, b, (((1,), (1,)), ((), ())), preferred_element_type=F32)


def _dot_tn(a, b):
    return lax.dot_general(a, b, (((0,), (0,)), ((), ())), preferred_element_type=F32)


def _fwd_in_proj(x, g1, win_g, comm=None):
    s_len, d = x.shape
    nsh, _, ncol = win_g.shape
    tm = min(TM_IN, s_len)
    te = min(T_EPI, tm)

    def body(x_ref, g_ref, w_ref, p_ref, h_ref):
        @pl.when(pl.program_id(1) == 0)
        def _():
            g = g_ref[...]

            def chunk(q, carry):
                rows = pl.ds(pl.multiple_of(q * te, te), te)
                _, n = _rms_stats(x_ref[rows, :])
                h_ref[rows, :] = (n * g).astype(BF16)
                return carry

            lax.fori_loop(0, tm // te, chunk, 0)

        p_ref[...] = _dot(h_ref[...], w_ref[...])

    return _pcall(
        body, name="fwd_in_proj", grid=(s_len // tm, nsh),
        in_specs=[pl.BlockSpec((tm, d), lambda i, j: (i, 0)),
                  pl.BlockSpec((1, d), lambda i, j: (0, 0)),
                  pl.BlockSpec((None, d, ncol), lambda i, j: (j, 0, 0))],
        out_specs=[pl.BlockSpec((tm, ncol), lambda i, j: (i, j)),
                   pl.BlockSpec((tm, d), lambda i, j: (i, 0))],
        out_shape=[jax.ShapeDtypeStruct((s_len, nsh * ncol), F32), jax.ShapeDtypeStruct((s_len, d), BF16)],
        dims=("arbitrary", "arbitrary"), comm=comm)(x, g1, win_g)


def _gm_forward(z, gv, bv, wt_ref, bst_ref, vl_s, mix_s):
    ts = z.shape[0]
    ge = _gelu(z)
    u = ge[:, :GM_W]
    v = ge[:, GM_W:]
    vc = v - _rowmean(v)
    rs = lax.rsqrt(_rowmean(vc * vc) + LN_EPS)
    vh = vc * rs
    vl_s[...] = (vh * gv + bv).astype(BF16)
    for cc in range(ts // CHUNK):
        rows = slice(cc * CHUNK, (cc + 1) * CHUNK)
        for hh in range(HEADS):
            cols = slice(hh * HEAD_DIM, (hh + 1) * HEAD_DIM)
            mix_s[rows, cols] = _dot(wt_ref[hh], vl_s[rows, cols]) + bst_ref[:, cols]
    mixed = mix_s[...]
    return u, mixed, vh, rs, u * mixed


def _lru_gates(xl, halo, wc_ref, bc_ref, wa_ref, ba_ref, wx_ref, bx_ref, lam_ref, z_s):
    x1 = _shift_prev(xl, halo, 1)
    x2 = _shift_prev(xl, halo, 2)
    x3 = _shift_prev(xl, halo, 3)
    xr = bc_ref[...] + wc_ref[0:1, :] * x3 + wc_ref[1:2, :] * x2 + wc_ref[2:3, :] * x1 + wc_ref[3:4, :] * xl
    xrb = xr.astype(BF16)
    for hh in range(HEADS):
        cols = slice(hh * HEAD_DIM, (hh + 1) * HEAD_DIM)
        z_s[:, cols] = _dot(xrb[:, cols], wa_ref[hh])
        z_s[:, LRU_W + hh * HEAD_DIM:LRU_W + (hh + 1) * HEAD_DIM] = _dot(xrb[:, cols], wx_ref[hh])
    ra = _sigmoid(z_s[:, :LRU_W] + ba_ref[...])
    ri = _sigmoid(z_s[:, LRU_W:] + bx_ref[...])
    sp = _softplus(-lam_ref[...])
    la = (-LRU_C) * ra * sp
    a = jnp.exp(la)
    mult = jnp.sqrt(_neg_expm1(2.0 * la))
    return dict(x1=x1, x2=x2, x3=x3, xr=xr, xrb=xrb, ra=ra, ri=ri, sp=sp, a=a, mult=mult)


def _fwd_mixers(p, gv, bv, wt, bst, wc, bc, wa, ba, wx, bx, lam, ggm, glru, comm=None):
    s_len = p.shape[0]
    ts = min(TS_MIX, s_len)

    def body(pz_ref, pgl_ref, pxl_ref, gv_ref, bv_ref, wt_ref, bst_ref, wc_ref, bc_ref, wa_ref, ba_ref, wx_ref,
             bx_ref, lam_ref, ggm_ref, glru_ref, y_ref, hs_ref, tail_ref, h_ref, a_s, b_s, vl_s, mix_s, z_s):
        @pl.when(pl.program_id(0) == 0)
        def _():
            tail_ref[...] = jnp.zeros_like(tail_ref)
            h_ref[...] = jnp.zeros_like(h_ref)

        _, _, _, _, ygm = _gm_forward(pz_ref[...], gv_ref[...], bv_ref[...], wt_ref, bst_ref, vl_s, mix_s)
        _, ngm = _rms_stats(ygm)
        y_ref[:, :GM_W] = (ngm * ggm_ref[...]).astype(BF16)

        xl = pxl_ref[...]
        gts = _lru_gates(xl, tail_ref[...], wc_ref, bc_ref, wa_ref, ba_ref, wx_ref, bx_ref, lam_ref, z_s)
        tail_ref[...] = xl[ts - HALO:, :]
        a_s[...] = gts["a"]
        b_s[...] = gts["mult"] * (gts["ri"] * gts["xr"])

        def step(t, h):
            h = a_s[pl.ds(t, 1), :] * h + b_s[pl.ds(t, 1), :]
            hs_ref[pl.ds(t, 1), :] = h
            return h

        h_ref[...] = lax.fori_loop(0, ts, step, h_ref[...], unroll=8)
        yl = hs_ref[...] * _gelu(pgl_ref[...])
        _, nl = _rms_stats(yl)
        y_ref[:, GM_W:] = (nl * glru_ref[...]).astype(BF16)

    full = lambda shape: pl.BlockSpec(shape, lambda i: (0,) * len(shape))
    return _pcall(
        body, name="fwd_mixers", grid=(s_len // ts,),
        in_specs=[pl.BlockSpec((ts, 2 * GM_W), lambda i: (i, 0)),
                  pl.BlockSpec((ts, LRU_W), lambda i: (i, 2)),
                  pl.BlockSpec((ts, LRU_W), lambda i: (i, 3)),
                  full((1, GM_W)), full((1, GM_W)), full((HEADS, CHUNK, CHUNK)), full((CHUNK, GM_W)),
                  full((4, LRU_W)), full((1, LRU_W)), full((HEADS, HEAD_DIM, HEAD_DIM)), full((1, LRU_W)),
                  full((HEADS, HEAD_DIM, HEAD_DIM)), full((1, LRU_W)), full((1, LRU_W)), full((1, GM_W)),
                  full((1, LRU_W))],
        out_specs=[pl.BlockSpec((ts, GM_W + LRU_W), lambda i: (i, 0)), pl.BlockSpec((ts, LRU_W), lambda i: (i, 0))],
        out_shape=[jax.ShapeDtypeStruct((s_len, GM_W + LRU_W), BF16), jax.ShapeDtypeStruct((s_len, LRU_W), F32)],
        scratch_shapes=[pltpu.VMEM((HALO, LRU_W), F32), pltpu.VMEM((1, LRU_W), F32),
                        pltpu.VMEM((ts, LRU_W), F32), pltpu.VMEM((ts, LRU_W), F32),
                        pltpu.VMEM((ts, GM_W), BF16), pltpu.VMEM((ts, GM_W), F32), pltpu.VMEM((ts, 2 * LRU_W), F32)],
        dims=("arbitrary",), comm=comm)(p, p, p, gv, bv, wt, bst, wc, bc, wa, ba, wx, bx, lam, ggm, glru)


def _fwd_out_proj(x, y, wout_g, g2, comm=None):
    s_len, d = x.shape
    tm = min(TM_OUT, s_len)

    def body(x_ref, y_ref, w_ref, g_ref, x2_ref, h2_ref):
        x2 = x_ref[...] + _dot(y_ref[...], w_ref[...])
        x2_ref[...] = x2
        _, n = _rms_stats(x2)
        h2_ref[...] = (n * g_ref[...]).astype(BF16)

    return _pcall(
        body, name="fwd_out_proj", grid=(s_len // tm,),
        in_specs=[pl.BlockSpec((tm, d), lambda i: (i, 0)), pl.BlockSpec((tm, d), lambda i: (i, 0)),
                  pl.BlockSpec((d, d), lambda i: (0, 0)), pl.BlockSpec((1, d), lambda i: (0, 0))],
        out_specs=[pl.BlockSpec((tm, d), lambda i: (i, 0)), pl.BlockSpec((tm, d), lambda i: (i, 0))],
        out_shape=[jax.ShapeDtypeStruct((s_len, d), F32), jax.ShapeDtypeStruct((s_len, d), BF16)],
        dims=("arbitrary",), comm=comm)(x, y, wout_g, g2)


def _row_fetch(hbm_ref, buf_ref, sem, row0, rows):
    return pltpu.make_async_copy(hbm_ref.at[pl.ds(row0, rows), :], buf_ref, sem)


def _fwd_ffn(h2, wup_g, wfc, bfc, wdown_g, x2, gf, target):
    s_len, d = h2.shape
    nsh, _, ncol = wup_g.shape
    tn = TN_UP
    tn2 = 2 * tn
    f = nsh * ncol // 2
    nj = f // tn
    nps = ncol // tn
    tm = min(TM_UP, s_len)
    te = min(T_EPI, tm)
    tr = T_CHUNK

    def body(h_ref, wg_ref, wv_ref, wcg_ref, wcv_ref, bg_ref, bv_ref, wd_ref, g_ref, x2_hbm, t_hbm,
             upb_ref, cb_ref, act_ref, dx3_ref, dx3b_ref, loss_ref, dgf_ref, tail_ref, acc_ref, x2_buf, t_buf, up_ref,
             sems):
        i, j = pl.program_id(0), pl.program_id(1)
        row0 = pl.multiple_of(i * tm, tm)
        fetches = (_row_fetch(x2_hbm, x2_buf, sems.at[0], row0, tm), _row_fetch(t_hbm, t_buf, sems.at[1], row0, tm))

        @pl.when(jnp.logical_and(i == 0, j == 0))
        def _():
            tail_ref[...] = jnp.zeros_like(tail_ref)
            loss_ref[...] = jnp.zeros_like(loss_ref)
            dgf_ref[...] = jnp.zeros_like(dgf_ref)

        @pl.when(j == 0)
        def _():
            acc_ref[...] = jnp.zeros_like(acc_ref)
            for cp in fetches:
                cp.start()

        h = h_ref[...]
        up_ref[:, :tn] = _dot(h, wg_ref[...])
        up_ref[:, tn:] = _dot(h, wv_ref[...])

        planes = ((wcg_ref, bg_ref), (wcv_ref, bv_ref))
        for c in range(tn // LANES):
            cols = slice(c * LANES, (c + 1) * LANES)
            pcols = [slice(pln * tn + c * LANES, pln * tn + (c + 1) * LANES) for pln in range(2)]

            def conv(pln, u, u1, u2):
                wc_ref, b_ref = planes[pln]
                return b_ref[:, cols] + wc_ref[0:1, cols] * u2 + wc_ref[1:2, cols] * u1 + wc_ref[2:3, cols] * u

            def emit(rows, us):
                cs_ = [conv(pln, *us[pln]) for pln in range(2)]
                for pln in range(2):
                    upb_ref[rows, pcols[pln]] = us[pln][0].astype(BF16)
                    cb_ref[rows, pcols[pln]] = cs_[pln].astype(BF16)
                act_ref[rows, cols] = (_gelu(cs_[0]) * cs_[1]).astype(BF16)

            first = [_prev_rows(jnp.concatenate([tail_ref[j, :, pc], up_ref[0:tr, pc]], axis=0)) for pc in pcols]
            emit(slice(0, tr), first)

            for k in range(1, tm // tr):
                r0 = k * tr
                us = [_prev_rows(up_ref[r0 - HALO:r0 + tr, pc]) for pc in pcols]
                emit(slice(r0, r0 + tr), us)
        tail_ref[j] = up_ref[tm - HALO:tm, :]
        acc_ref[...] += _dot(act_ref[...], wd_ref[...])

        @pl.when(j == nj - 1)
        def _():
            for cp in fetches:
                cp.wait()
            g = g_ref[...]

            def chunk(k, carry):
                rows = pl.ds(pl.multiple_of(k * te, te), te)
                x3 = x2_buf[rows, :] + acc_ref[rows, :]
                r, n = _rms_stats(x3)
                err = n * g - t_buf[rows, :]
                loss_ref[...] += jnp.sum(err * err) * (0.5 / d)
                dx3, dgn = _rms_bwd(err * (1.0 / d), n, r, g)
                dgf_ref[...] += _colsum(dgn)
                dx3_ref[rows, :] = dx3
                dx3b_ref[rows, :] = dx3.astype(BF16)
                return carry

            lax.fori_loop(0, tm // te, chunk, 0)

    hbm = pl.BlockSpec(memory_space=pl.ANY)
    return _pcall(
        body, name="fwd_ffn", grid=(s_len // tm, nj),
        in_specs=[pl.BlockSpec((tm, d), lambda i, j: (i, 0)),
                  pl.BlockSpec((None, d, tn), lambda i, j: (j // nps, 0, j % nps)),
                  pl.BlockSpec((None, d, tn), lambda i, j: (nsh // 2 + j // nps, 0, j % nps)),
                  pl.BlockSpec((3, tn), lambda i, j: (0, j)), pl.BlockSpec((3, tn), lambda i, j: (0, nj + j)),
                  pl.BlockSpec((1, tn), lambda i, j: (0, j)), pl.BlockSpec((1, tn), lambda i, j: (0, nj + j)),
                  pl.BlockSpec((tn, d), lambda i, j: (j, 0)), pl.BlockSpec((1, d), lambda i, j: (0, 0)), hbm, hbm],
        out_specs=[pl.BlockSpec((tm, tn2), lambda i, j: (i, j)), pl.BlockSpec((tm, tn2), lambda i, j: (i, j)),
                   pl.BlockSpec((tm, tn), lambda i, j: (i, j)),
                   pl.BlockSpec((tm, d), lambda i, j: (i, 0)), pl.BlockSpec((tm, d), lambda i, j: (i, 0)),
                   pl.BlockSpec((8, 128), lambda i, j: (0, 0)), pl.BlockSpec((1, d), lambda i, j: (0, 0))],
        out_shape=[jax.ShapeDtypeStruct((s_len, 2 * f), BF16), jax.ShapeDtypeStruct((s_len, 2 * f), BF16),
                   jax.ShapeDtypeStruct((s_len, f), BF16),
                   jax.ShapeDtypeStruct((s_len, d), F32), jax.ShapeDtypeStruct((s_len, d), BF16),
                   jax.ShapeDtypeStruct((8, 128), F32), jax.ShapeDtypeStruct((1, d), F32)],
        scratch_shapes=[pltpu.VMEM((nj, HALO, tn2), F32), pltpu.VMEM((tm, d), F32), pltpu.VMEM((tm, d), F32),
                        pltpu.VMEM((tm, d), F32), pltpu.VMEM((tm, tn2), F32), pltpu.SemaphoreType.DMA((2,))],
        dims=("arbitrary", "arbitrary"))(h2, wup_g, wup_g, wfc, wfc, bfc, bfc, wdown_g, gf, x2, target)


def _bwd_ffn(dx3b, wdown_g, upb, cb, wfc, wup_g, dx3, x2, g2, comm=None):
    s_len, d = dx3b.shape
    nsh, _, ncol = wup_g.shape
    tn = TN_UP
    tn2 = 2 * tn
    f = nsh * ncol // 2
    nj = f // tn
    nps = ncol // tn
    tm = min(TM_UP, s_len)
    te = min(T_EPI, tm)
    nt = s_len // tm
    tr = T_CHUNK

    def body(dx_ref, w_ref, upb_ref, cb_ref, wcg_ref, wcv_ref, wug_ref, wuv_ref, g_ref, r_hbm, x_hbm,
             dup_ref, dwf_ref, dx2_ref, dx2b_ref, dg_ref, nxt_ref, acc_ref, r_buf, x_buf, da_s, dc_s, sems):
        i, j = pl.program_id(0), pl.program_id(1)
        ti = nt - 1 - i
        row0 = pl.multiple_of(ti * tm, tm)
        fetches = (_row_fetch(r_hbm, r_buf, sems.at[0], row0, tm), _row_fetch(x_hbm, x_buf, sems.at[1], row0, tm))

        @pl.when(jnp.logical_and(i == 0, j == 0))
        def _():
            dwf_ref[...] = jnp.zeros_like(dwf_ref)
            nxt_ref[...] = jnp.zeros_like(nxt_ref)
            dg_ref[...] = jnp.zeros_like(dg_ref)

        @pl.when(j == 0)
        def _():
            acc_ref[...] = jnp.zeros_like(acc_ref)
            for cp in fetches:
                cp.start()

        da_s[...] = _dot_nt(dx_ref[...], w_ref[...])
        planes = ((wcg_ref,), (wcv_ref,))
        for pln in range(2):
            dc_s[pln, tm:tm + HALO, :] = nxt_ref[pln, j]

        for c in range(tn // LANES):
            cols = slice(c * LANES, (c + 1) * LANES)
            pcols = [slice(pln * tn + c * LANES, pln * tn + (c + 1) * LANES) for pln in range(2)]
            for k in range(tm // tr):
                rows = slice(k * tr, (k + 1) * tr)
                dact = da_s[rows, cols]
                ge, gd = _gelu_and_grad(cb_ref[rows, pcols[0]].astype(F32))
                dc_s[0, rows, cols] = dact * cb_ref[rows, pcols[1]].astype(F32) * gd
                dc_s[1, rows, cols] = dact * ge

        for pln in range(2):
            nxt_ref[pln, j] = dc_s[pln, 0:HALO, :]

        def fold(v):
            out = v[0:8, :]
            for q in range(1, tr // 8):
                out = out + v[8 * q:8 * q + 8, :]
            return out

        for c in range(tn // LANES):
            cols = slice(c * LANES, (c + 1) * LANES)
            for pln in range(2):
                wc_ref = planes[pln][0]
                pc = slice(pln * tn + c * LANES, pln * tn + (c + 1) * LANES)
                sums = (jnp.zeros((8, LANES), F32),) * 4
                for k in range(tm // tr):
                    r0 = k * tr
                    dc, dc1, dc2 = _next_rows(dc_s[pln, r0:r0 + tr + HALO, cols])
                    dup = wc_ref[2:3, cols] * dc + wc_ref[1:2, cols] * dc1 + wc_ref[0:1, cols] * dc2
                    dup_ref[pln, r0:r0 + tr, cols] = dup.astype(BF16)
                    u = upb_ref[r0:r0 + tr, pc].astype(F32)
                    new = (fold(dc2 * u), fold(dc1 * u), fold(dc * u), fold(dc))
                    sums = tuple(a + b for a, b in zip(sums, new))
                for term in range(4):
                    dwf_ref[pln, j, term:term + 1, cols] += _colsum(sums[term])
        acc_ref[...] += _dot_nt(dup_ref[0], wug_ref[...]) + _dot_nt(dup_ref[1], wuv_ref[...])

        @pl.when(j == nj - 1)
        def _():
            for cp in fetches:
                cp.wait()
            g = g_ref[...]

            def chunk(k, carry):
                rows = pl.ds(pl.multiple_of(k * te, te), te)
                r, n = _rms_stats(x_buf[rows, :])
                dxn, dgn = _rms_bwd(acc_ref[rows, :], n, r, g)
                dg_ref[...] += _colsum(dgn)
                dx = r_buf[rows, :] + dxn
                dx2_ref[rows, :] = dx
                dx2b_ref[rows, :] = dx.astype(BF16)
                return carry

            lax.fori_loop(0, tm // te, chunk, 0)

    hbm = pl.BlockSpec(memory_space=pl.ANY)
    rev = lambda i: nt - 1 - i
    return _pcall(
        body, name="bwd_ffn", grid=(nt, nj),
        in_specs=[pl.BlockSpec((tm, d), lambda i, j: (rev(i), 0)),
                  pl.BlockSpec((tn, d), lambda i, j: (j, 0)),
                  pl.BlockSpec((tm, tn2), lambda i, j: (rev(i), j)),
                  pl.BlockSpec((tm, tn2), lambda i, j: (rev(i), j)),
                  pl.BlockSpec((3, tn), lambda i, j: (0, j)), pl.BlockSpec((3, tn), lambda i, j: (0, nj + j)),
                  pl.BlockSpec((None, d, tn), lambda i, j: (j // nps, 0, j % nps)),
                  pl.BlockSpec((None, d, tn), lambda i, j: (nsh // 2 + j // nps, 0, j % nps)),
                  pl.BlockSpec((1, d), lambda i, j: (0, 0)), hbm, hbm],
        out_specs=[pl.BlockSpec((2, tm, tn), lambda i, j: (0, rev(i), j)),
                   pl.BlockSpec((2, nj, 8, tn), lambda i, j: (0, 0, 0, 0)),
                   pl.BlockSpec((tm, d), lambda i, j: (rev(i), 0)), pl.BlockSpec((tm, d), lambda i, j: (rev(i), 0)),
                   pl.BlockSpec((1, d), lambda i, j: (0, 0))],
        out_shape=[jax.ShapeDtypeStruct((2, s_len, f), BF16), jax.ShapeDtypeStruct((2, nj, 8, tn), F32),
                   jax.ShapeDtypeStruct((s_len, d), F32), jax.ShapeDtypeStruct((s_len, d), BF16),
                   jax.ShapeDtypeStruct((1, d), F32)],
        scratch_shapes=[pltpu.VMEM((2, nj, HALO, tn), F32), pltpu.VMEM((tm, d), F32), pltpu.VMEM((tm, d), F32),
                        pltpu.VMEM((tm, d), F32), pltpu.VMEM((tm, tn), F32), pltpu.VMEM((2, tm + HALO, tn), F32),
                        pltpu.SemaphoreType.DMA((2,))],
        dims=("arbitrary", "arbitrary"), comm=comm)(dx3b, wdown_g, upb, cb, wfc, wfc, wup_g, wup_g, g2, dx3, x2)


def _bwd_in(dp, win_g, resid, x_in, g, comm=None):
    s_len, d = x_in.shape
    nsh, _, ncol = win_g.shape
    tm = min(TM_BW, s_len)
    te = min(T_EPI, tm)

    def body(dz_ref, w_ref, r_ref, x_ref, g_ref, dx_ref, dg_ref, acc_ref):
        i, k = pl.program_id(0), pl.program_id(1)

        @pl.when(jnp.logical_and(i == 0, k == 0))
        def _():
            dg_ref[...] = jnp.zeros_like(dg_ref)

        @pl.when(k == 0)
        def _():
            acc_ref[...] = jnp.zeros_like(acc_ref)

        acc_ref[...] += _dot_nt(dz_ref[...], w_ref[...])

        @pl.when(k == nsh - 1)
        def _():
            g = g_ref[...]

            def chunk(q, carry):
                rows = pl.ds(pl.multiple_of(q * te, te), te)
                r, n = _rms_stats(x_ref[rows, :])
                dxn, dgn = _rms_bwd(acc_ref[rows, :], n, r, g)
                dg_ref[...] += _colsum(dgn)
                dx_ref[rows, :] = r_ref[rows, :] + dxn
                return carry

            lax.fori_loop(0, tm // te, chunk, 0)

    return _pcall(
        body, name="bwd_in", grid=(s_len // tm, nsh),
        in_specs=[pl.BlockSpec((tm, ncol), lambda i, k: (i, k)),
                  pl.BlockSpec((None, d, ncol), lambda i, k: (k, 0, 0)),
                  pl.BlockSpec((tm, d), lambda i, k: (i, 0)), pl.BlockSpec((tm, d), lambda i, k: (i, 0)),
                  pl.BlockSpec((1, d), lambda i, k: (0, 0))],
        out_specs=[pl.BlockSpec((tm, d), lambda i, k: (i, 0)), pl.BlockSpec((1, d), lambda i, k: (0, 0))],
        out_shape=[jax.ShapeDtypeStruct((s_len, d), F32), jax.ShapeDtypeStruct((1, d), F32)],
        scratch_shapes=[pltpu.VMEM((tm, d), F32)],
        dims=("arbitrary", "arbitrary"), comm=comm)(dp, win_g, resid, x_in, g)


def _bwd_mixers(p, dx2b, wout_g, hs, gv, bv, wt, wtt, bst, wc, bc, wa, wat, ba, wx, wxt, bx, lam, ggm, glru, comm=None):
    s_len = p.shape[0]
    ts = min(TS_MIX, s_len)
    nt = s_len // ts
    hb = ts // HALO

    def body(pz_ref, pgl_ref, pxl_ref, xh_ref, dx_ref, wo_ref, hs_ref, hh_ref, gv_ref, bv_ref, wt_ref, wtt_ref, bst_ref,
             wc_ref, bc_ref, wa_ref, wat_ref, ba_ref, wx_ref, wxt_ref, bx_ref, lam_ref, ggm_ref, glru_ref,
             dp_ref, dgv_ref, dbv_ref, dwt_ref, dbst_ref, dwc_ref, dbc_ref, dwa_ref, dba_ref, dwx_ref, dbx_ref,
             dsp_ref, dggm_ref, dglru_ref,
             carry_ref, nxt_ref, a_s, g_s, vl_s, mix_s, z_s, dm_s, dvl_s, dxr_s, dy_ref):
        i = pl.program_id(0)
        ti = nt - 1 - i

        @pl.when(i == 0)
        def _():
            for ref in (dgv_ref, dbv_ref, dwt_ref, dbst_ref, dwc_ref, dbc_ref, dwa_ref, dba_ref, dwx_ref, dbx_ref,
                        dsp_ref, dggm_ref, dglru_ref, carry_ref, nxt_ref):
                ref[...] = jnp.zeros_like(ref)

        dy_ref[...] = _dot_nt(dx_ref[...], wo_ref[...])

        z = pz_ref[...]
        u, mixed, vh, rs, ygm = _gm_forward(z, gv_ref[...], bv_ref[...], wt_ref, bst_ref, vl_s, mix_s)
        rg, ngm = _rms_stats(ygm)
        dygm, dgn = _rms_bwd(dy_ref[:, :GM_W], ngm, rg, ggm_ref[...])
        dggm_ref[...] += _colsum(dgn)
        du = dygm * mixed
        dmix = dygm * u
        dm_s[...] = dmix.astype(BF16)
        bsum = dmix[0:CHUNK, :]
        for cc in range(1, ts // CHUNK):
            bsum = bsum + dmix[cc * CHUNK:(cc + 1) * CHUNK, :]
        dbst_ref[...] += bsum
        for hh in range(HEADS):
            cols = slice(hh * HEAD_DIM, (hh + 1) * HEAD_DIM)
            dw = jnp.zeros((CHUNK, CHUNK), F32)
            for cc in range(ts // CHUNK):
                rows = slice(cc * CHUNK, (cc + 1) * CHUNK)
                dmb = dm_s[rows, cols]
                dw = dw + _dot_nt(dmb, vl_s[rows, cols])
                dvl_s[rows, cols] = _dot(wtt_ref[hh], dmb)
            dwt_ref[hh] += dw
        dvl = dvl_s[...]
        dgv_ref[...] += _colsum(dvl * vh)
        dbv_ref[...] += _colsum(dvl)
        dvh = dvl * gv_ref[...]
        dv = rs * (dvh - _rowmean(dvh) - vh * _rowmean(dvh * vh))
        _, gd = _gelu_and_grad(z)
        dp_ref[:, :GM_W] = (du * gd[:, :GM_W]).astype(BF16)
        dp_ref[:, GM_W:2 * GM_W] = (dv * gd[:, GM_W:]).astype(BF16)

        xl = pxl_ref[...]
        xhalo = jnp.where(ti == 0, 0.0, xh_ref[...])
        gts = _lru_gates(xl, xhalo, wc_ref, bc_ref, wa_ref, ba_ref, wx_ref, bx_ref, lam_ref, z_s)
        a, mult, ra, ri, xr, sp = gts["a"], gts["mult"], gts["ra"], gts["ri"], gts["xr"], gts["sp"]
        hs = hs_ref[...]
        hprev = _shift_prev(hs, jnp.where(ti == 0, 0.0, hh_ref[...]), 1)
        gl = pgl_ref[...]
        ggl, dggl = _gelu_and_grad(gl)
        yl = hs * ggl
        rl, nl = _rms_stats(yl)
        dyl, dgn = _rms_bwd(dy_ref[:, GM_W:], nl, rl, glru_ref[...])
        dglru_ref[...] += _colsum(dgn)
        dp_ref[:, 2 * GM_W:2 * GM_W + LRU_W] = (dyl * hs * dggl).astype(BF16)
        a_s[...] = a
        g_s[...] = dyl * ggl

        def step(k, carry):
            t = ts - 1 - k
            gt = g_s[pl.ds(t, 1), :] + carry
            g_s[pl.ds(t, 1), :] = gt
            return a_s[pl.ds(t, 1), :] * gt

        carry_ref[...] = lax.fori_loop(0, ts, step, carry_ref[...], unroll=8)
        gsc = g_s[...]
        da = gsc * hprev
        rix = ri * xr
        dmult = gsc * rix
        dri = gsc * mult * xr
        dxr = gsc * mult * ri
        dla = da * a - dmult * (a * a) / mult
        dsp_ref[...] += _colsum(dla * ra) * (-LRU_C)
        dza = (dla * sp) * (-LRU_C) * ra * (1.0 - ra)
        dzi = dri * ri * (1.0 - ri)
        dba_ref[...] += _colsum(dza)
        dbx_ref[...] += _colsum(dzi)
        dzab = dza.astype(BF16)
        dzib = dzi.astype(BF16)
        xrb = gts["xrb"]
        for hh in range(HEADS):
            cols = slice(hh * HEAD_DIM, (hh + 1) * HEAD_DIM)
            dwa_ref[hh] += _dot_tn(xrb[:, cols], dzab[:, cols])
            dwx_ref[hh] += _dot_tn(xrb[:, cols], dzib[:, cols])
            dxr_s[:, cols] = _dot(dzab[:, cols], wat_ref[hh]) + _dot(dzib[:, cols], wxt_ref[hh])
        dxr = dxr + dxr_s[...]
        dbc_ref[...] += _colsum(dxr)
        dwc_ref[0:1, :] += _colsum(dxr * gts["x3"])
        dwc_ref[1:2, :] += _colsum(dxr * gts["x2"])
        dwc_ref[2:3, :] += _colsum(dxr * gts["x1"])
        dwc_ref[3:4, :] += _colsum(dxr * xl)
        nxt = nxt_ref[...]
        nxt_ref[...] = dxr[:HALO, :]
        dxl = wc_ref[3:4, :] * dxr + wc_ref[2:3, :] * _shift_next(dxr, nxt, 1) \
            + wc_ref[1:2, :] * _shift_next(dxr, nxt, 2) + wc_ref[0:1, :] * _shift_next(dxr, nxt, 3)
        dp_ref[:, 2 * GM_W + LRU_W:] = dxl.astype(BF16)

    full = lambda shape: pl.BlockSpec(shape, lambda i: (0,) * len(shape))
    rev = lambda i: nt - 1 - i
    prev_blk = lambda i: jnp.maximum((nt - 1 - i) * hb - 1, 0)
    hhd = (HEADS, HEAD_DIM, HEAD_DIM)
    small_shapes = [(1, GM_W), (1, GM_W), (HEADS, CHUNK, CHUNK), (CHUNK, GM_W), (4, LRU_W), (1, LRU_W), hhd,
                    (1, LRU_W), hhd, (1, LRU_W), (1, LRU_W), (1, GM_W), (1, LRU_W)]
    return _pcall(
        body, name="bwd_mixers", grid=(nt,),
        in_specs=[pl.BlockSpec((ts, 2 * GM_W), lambda i: (rev(i), 0)),
                  pl.BlockSpec((ts, LRU_W), lambda i: (rev(i), 2)),
                  pl.BlockSpec((ts, LRU_W), lambda i: (rev(i), 3)),
                  pl.BlockSpec((HALO, LRU_W), lambda i: (prev_blk(i), 3)),
                  pl.BlockSpec((ts, GM_W + LRU_W), lambda i: (rev(i), 0)),
                  full((GM_W + LRU_W, GM_W + LRU_W)),
                  pl.BlockSpec((ts, LRU_W), lambda i: (rev(i), 0)),
                  pl.BlockSpec((HALO, LRU_W), lambda i: (prev_blk(i), 0)),
                  full((1, GM_W)), full((1, GM_W)), full((HEADS, CHUNK, CHUNK)), full((HEADS, CHUNK, CHUNK)),
                  full((CHUNK, GM_W)), full((4, LRU_W)), full((1, LRU_W)), full(hhd), full(hhd), full((1, LRU_W)),
                  full(hhd), full(hhd), full((1, LRU_W)), full((1, LRU_W)), full((1, GM_W)), full((1, LRU_W))],
        out_specs=[pl.BlockSpec((ts, 2 * GM_W + 2 * LRU_W), lambda i: (rev(i), 0))] + [full(s) for s in small_shapes],
        out_shape=[jax.ShapeDtypeStruct((s_len, 2 * GM_W + 2 * LRU_W), BF16)]
        + [jax.ShapeDtypeStruct(s, F32) for s in small_shapes],
        scratch_shapes=[pltpu.VMEM((1, LRU_W), F32), pltpu.VMEM((HALO, LRU_W), F32),
                        pltpu.VMEM((ts, LRU_W), F32), pltpu.VMEM((ts, LRU_W), F32),
                        pltpu.VMEM((ts, GM_W), BF16), pltpu.VMEM((ts, GM_W), F32), pltpu.VMEM((ts, 2 * LRU_W), F32),
                        pltpu.VMEM((ts, GM_W), BF16), pltpu.VMEM((ts, GM_W), F32), pltpu.VMEM((ts, LRU_W), F32),
                        pltpu.VMEM((ts, GM_W + LRU_W), F32)],
        dims=("arbitrary",), comm=comm)(p, p, p, p, dx2b, wout_g, hs, hs, gv, bv, wt, wtt, bst, wc, bc, wa, wat, ba, wx, wxt,
                                        bx, lam, ggm, glru)


def _bwd_weight(name, a, b, *, a_planes, b_planes, shard_rows, comm=None):
    _, s_len, ma = a.shape
    _, _, nb = b.shape
    m, n = a_planes * ma, b_planes * nb
    tk = min(TW_K, s_len)
    if shard_rows:
        rows, cols = m // N_CHIPS, n // 2
        tm, tn = _tile(rows, TW_M), _tile(cols, TW_N)
        out_idx = lambda i, j, k: (j * tn // cols, i * tm // rows, (i * tm % rows) // tm, (j * tn % cols) // tn)
    else:
        rows, cols = m // 2, n // N_CHIPS
        tm, tn = _tile(rows, TW_M), _tile(cols, TW_N)
        out_idx = lambda i, j, k: (i * tm // rows, j * tn // cols, (i * tm % rows) // tm, (j * tn % cols) // tn)
    nk = s_len // tk
    npa, npb = ma // tm, nb // tn

    def body(a_ref, b_ref, o_ref, ob_ref, acc_ref):
        k = pl.program_id(2)

        @pl.when(k == 0)
        def _():
            acc_ref[...] = jnp.zeros_like(acc_ref)

        acc_ref[...] += _dot_tn(a_ref[...], b_ref[...])

        @pl.when(k == nk - 1)
        def _():
            o_ref[...] = acc_ref[...]
            ob_ref[...] = acc_ref[...].astype(BF16)

    shape = (2, N_CHIPS, rows, cols)
    return _pcall(
        body, name=name, grid=(m // tm, n // tn, nk),
        in_specs=[pl.BlockSpec((None, tk, tm), lambda i, j, k: (i // npa, k, i % npa)),
                  pl.BlockSpec((None, tk, tn), lambda i, j, k: (j // npb, k, j % npb))],
        out_specs=[pl.BlockSpec((None, None, tm, tn), out_idx), pl.BlockSpec((None, None, tm, tn), out_idx)],
        out_shape=[jax.ShapeDtypeStruct(shape, F32), jax.ShapeDtypeStruct(shape, BF16)],
        scratch_shapes=[pltpu.VMEM((tm, tn), F32)],
        dims=("arbitrary", "arbitrary", "arbitrary"), comm=comm)(a, b)


def _mesh_pos():
    return lax.axis_index("x"), lax.axis_index("y"), lax.axis_index("c")


def _other_chips(x, y):
    return [(1 - x, y), (x, 1 - y), (1 - x, 1 - y)]


def _to_slot(name, a, cs, dtype):
    _, a_rows, b_cols = a.shape
    ta = min(T_ELEM, a_rows)

    def body(cs_ref, a_ref, o_ref):
        o_ref[...] = a_ref[...].astype(dtype)

    grid_spec = pltpu.PrefetchScalarGridSpec(
        num_scalar_prefetch=1, grid=(2, a_rows // ta),
        in_specs=[pl.BlockSpec((None, ta, b_cols), lambda h, r, cs_ref: (h, r, 0))],
        out_specs=pl.BlockSpec((None, None, ta, b_cols), lambda h, r, cs_ref: (cs_ref[1], h, r, 0)))
    return _pcall(body, name=name, grid_spec=grid_spec,
                  out_shape=jax.ShapeDtypeStruct((N_CHIPS,) + a.shape, dtype), dims=("arbitrary", "arbitrary"))(cs, a)


def _all_gather(bufs):
    n = len(bufs)

    def body(*refs):
        outs = refs[n:2 * n]
        send_sems, recv_sems = refs[2 * n:]
        x, y, c = _mesh_pos()
        s = 2 * x + y
        me, sib = (x, y, c), (x, y, 1 - c)
        chips = _other_chips(x, y)

        def rcopy(a, k, blk, to):
            return pltpu.make_async_remote_copy(src_ref=blk, dst_ref=blk, send_sem=send_sems.at[a * 6 + k],
                                                recv_sem=recv_sems.at[a * 6 + k], device_id=to, device_id_type=MESH)

        first = [rcopy(a, j, outs[a].at[s, c], (cx, cy, c)) for a in range(n) for j, (cx, cy) in enumerate(chips)]
        for cp in first:
            cp.start()
        passed = []
        for a in range(n):
            for j, (cx, cy) in enumerate(chips):
                blk = outs[a].at[2 * cx + cy, c]
                rcopy(a, j, blk, me).wait_recv()
                cp = rcopy(a, 3 + j, blk, sib)
                cp.start()
                passed.append(cp)
        for a in range(n):
            for j, (cx, cy) in enumerate(chips):
                rcopy(a, 3 + j, outs[a].at[2 * cx + cy, 1 - c], me).wait_recv()
        for cp in first + passed:
            cp.wait_send()

    return _pcall(
        body, name="all_gather_weights",
        in_specs=[ANY] * n, out_specs=[ANY] * n,
        out_shape=[jax.ShapeDtypeStruct(a.shape, a.dtype) for a in bufs],
        scratch_shapes=[pltpu.SemaphoreType.DMA((6 * n,)), pltpu.SemaphoreType.DMA((6 * n,))],
        aliases={a: a for a in range(n)})(*bufs)


def _same(a):
    return jax.ShapeDtypeStruct(a.shape, a.dtype)


def _job_gather(bufs, ici_parts, relay_parts):
    def copies(cin, cout, x, y, c):
        out = []
        for a, lo, n in ici_parts:
            blk = cout[a].at[2 * x + y, c, pl.ds(lo, n)]
            out += [(blk, blk, (cx, cy, c)) for cx, cy in _other_chips(x, y)]
        for a, lo, n in relay_parts:
            for cx, cy in _other_chips(x, y):
                blk = cout[a].at[2 * cx + cy, c, pl.ds(lo, n)]
                out.append((blk, blk, (x, y, 1 - c)))
        return out

    return _Job(bufs, [_same(b) for b in bufs], {a: a for a in range(len(bufs))},
                3 * (len(ici_parts) + len(relay_parts)), copies)


def _job_pair_swap(arrs):
    def copies(cin, cout, x, y, c):
        return [(cin[a].at[1 - c], cout[a], (x, y, 1 - c)) for a in range(len(arrs))]

    return _Job(arrs, [jax.ShapeDtypeStruct(a.shape[1:], a.dtype) for a in arrs], {}, len(arrs), copies)


def _job_chip_exchange(big, small=()):
    nb = len(big)
    arrs = list(big) + list(small)

    def copies(cin, cout, x, y, c):
        out = []
        for a in range(len(arrs)):
            for j, (cx, cy) in enumerate(_other_chips(x, y)):
                out.append((cin[a].at[2 * cx + cy] if a < nb else cin[a], cout[a].at[j], (cx, cy, c)))
        return out

    shapes = [a.shape[1:] for a in big] + [a.shape for a in small]
    return _Job(arrs, [jax.ShapeDtypeStruct((3,) + sh, a.dtype) for sh, a in zip(shapes, arrs)], {}, 3 * len(arrs), copies)


def _job_halves_swap(bufs):
    def copies(cin, cout, x, y, c):
        return [(cout[a].at[c], cout[a].at[c], (x, y, 1 - c)) for a in range(len(bufs))]

    return _Job(bufs, [_same(b) for b in bufs], {a: a for a in range(len(bufs))}, len(bufs), copies)


def _pair_sum(name, g32, recv, cs):
    _, nch, a_rows, b_cols = g32.shape
    ta = min(T_ELEM, a_rows)

    def body(cs_ref, g_ref, r_ref, pb_ref, own_ref):
        k = pl.program_id(1)
        v = g_ref[...] + r_ref[...].astype(F32)
        pb_ref[...] = v.astype(BF16)

        @pl.when(k == cs_ref[1])
        def _():
            own_ref[...] = v

    grid_spec = pltpu.PrefetchScalarGridSpec(
        num_scalar_prefetch=1, grid=(a_rows // ta, nch),
        in_specs=[pl.BlockSpec((None, None, ta, b_cols), lambda r, k, cs_ref: (cs_ref[0], k, r, 0)),
                  pl.BlockSpec((None, ta, b_cols), lambda r, k, cs_ref: (k, r, 0))],
        out_specs=[pl.BlockSpec((None, ta, b_cols), lambda r, k, cs_ref: (k, r, 0)),
                   pl.BlockSpec((ta, b_cols), lambda r, k, cs_ref: (r, 0))])
    return _pcall(
        body, name=name, grid_spec=grid_spec,
        out_shape=[jax.ShapeDtypeStruct((nch, a_rows, b_cols), BF16), jax.ShapeDtypeStruct((a_rows, b_cols), F32)],
        dims=("arbitrary", "arbitrary"))(cs, g32, recv)


def _small_pair_sum(mine, recv, cs):
    _, r, ccols = mine.shape

    def body(cs_ref, a_ref, b_ref, o_ref):
        o_ref[...] = a_ref[...] + b_ref[...]

    grid_spec = pltpu.PrefetchScalarGridSpec(
        num_scalar_prefetch=1, grid=(1,),
        in_specs=[pl.BlockSpec((None, r, ccols), lambda i, cs_ref: (cs_ref[0], 0, 0)),
                  pl.BlockSpec((r, ccols), lambda i, cs_ref: (0, 0))],
        out_specs=pl.BlockSpec((r, ccols), lambda i, cs_ref: (0, 0)))
    return _pcall(body, name="small_pair_sum", grid_spec=grid_spec,
                  out_shape=jax.ShapeDtypeStruct((r, ccols), F32), dims=("arbitrary",))(cs, mine, recv)


def _chip_sum(name, own, recv, cs):
    a_rows, b_cols = own.shape
    ta = min(T_ELEM, a_rows)

    def body(cs_ref, o_ref, r_ref, f_ref):
        f_ref[...] = ((o_ref[...] + r_ref[0].astype(F32)) + r_ref[1].astype(F32)) + r_ref[2].astype(F32)

    grid_spec = pltpu.PrefetchScalarGridSpec(
        num_scalar_prefetch=1, grid=(a_rows // ta,),
        in_specs=[pl.BlockSpec((ta, b_cols), lambda r, cs_ref: (r, 0)),
                  pl.BlockSpec((3, ta, b_cols), lambda r, cs_ref: (0, r, 0))],
        out_specs=pl.BlockSpec((None, ta, b_cols), lambda r, cs_ref: (cs_ref[0], r, 0)))
    return _pcall(body, name=name, grid_spec=grid_spec,
                  out_shape=jax.ShapeDtypeStruct((2, a_rows, b_cols), F32), dims=("arbitrary",))(cs, own, recv)


def _small_chip_sum(pair, recv, cs):
    r, ccols = pair.shape

    def body(cs_ref, p_ref, r_ref, o_ref):
        s = cs_ref[1]
        own = p_ref[...]
        total = None
        for k in range(N_CHIPS):
            flip = jnp.bitwise_xor(s, k)
            term = jnp.where(flip == 0, own, jnp.where(flip == 2, r_ref[0], jnp.where(flip == 1, r_ref[1], r_ref[2])))
            total = term if total is None else total + term
        o_ref[...] = total

    grid_spec = pltpu.PrefetchScalarGridSpec(
        num_scalar_prefetch=1, grid=(1,),
        in_specs=[pl.BlockSpec((r, ccols), lambda i, cs_ref: (0, 0)),
                  pl.BlockSpec((3, r, ccols), lambda i, cs_ref: (0, 0, 0))],
        out_specs=pl.BlockSpec((None, r, ccols), lambda i, cs_ref: (cs_ref[0], 0, 0)))
    return _pcall(body, name="small_chip_sum", grid_spec=grid_spec,
                  out_shape=jax.ShapeDtypeStruct((2, r, ccols), F32), dims=("arbitrary",))(cs, pair, recv)


def _adamw(name, w, g, m, v, *, halves, comm=None):
    rows, cols = w.shape
    tr, tc = min(T_ELEM, rows), PACK_COLS
    c1 = 1.0 - ADAM_B1 ** ADAM_STEP
    c2 = 1.0 - ADAM_B2 ** ADAM_STEP

    def body(w_ref, g_ref, m_ref, v_ref, d_ref, mo_ref, vo_ref, go_ref):
        g_ = g_ref[...]
        m_ = ADAM_B1 * m_ref[...] + (1.0 - ADAM_B1) * g_
        v_ = ADAM_B2 * v_ref[...] + (1.0 - ADAM_B2) * (g_ * g_)
        mo_ref[...] = m_
        vo_ref[...] = v_
        go_ref[...] = g_
        d_ref[...] = (-ADAM_LR) * ((m_ / c1) / (jnp.sqrt(v_ / c2) + ADAM_EPS) + ADAM_WD * w_ref[...])

    spec = pl.BlockSpec((tr, tc), lambda r, j: (r, j))
    if halves == "rows":
        nrh = rows // 2 // tr
        g_spec = pl.BlockSpec((None, tr, tc), lambda r, j: (r // nrh, r % nrh, j))
    elif halves == "cols":
        nch = cols // 2 // tc
        g_spec = pl.BlockSpec((None, tr, tc), lambda r, j: (j // nch, r, j % nch))
    else:
        g_spec = spec
    return _pcall(body, name=name, grid=(rows // tr, cols // tc), in_specs=[spec, g_spec, spec, spec],
                  out_specs=[spec] * 4, out_shape=[jax.ShapeDtypeStruct((rows, cols), F32)] * 4,
                  dims=("arbitrary", "arbitrary"), comm=comm)(w, g, m, v)


def _pack(parts, rows):
    flat = jnp.concatenate([a.reshape(-1) for a in parts])
    return jnp.pad(flat, (0, rows * PACK_COLS - flat.shape[0])).reshape(rows, PACK_COLS)


def _unpack(buf, shapes):
    flat = buf.reshape(-1)
    out, off = [], 0
    for sh in shapes:
        size = math.prod(sh)
        out.append(flat[off:off + size].reshape(sh))
        off += size
    return out


def _pack_rows(shapes, multiple):
    total = sum(math.prod(sh) for sh in shapes)
    rows = -(-total // PACK_COLS)
    return -(-rows // multiple) * multiple


SMALL = ["norm1_g", "gm_v_g", "gm_v_b", "gm_ws", "gm_bs", "lru_conv_w", "lru_conv_b", "lru_wa", "lru_ba", "lru_wx",
         "lru_bx", "lru_lambda", "gm_out_g", "lru_out_g", "norm2_g", "ffn_conv_w", "ffn_conv_b", "final_g"]
BIG = ["w_in", "w_out", "ffn_w_up", "ffn_w_down"]
ORDER = ["norm1_g", "w_in", "gm_v_g", "gm_v_b", "gm_ws", "gm_bs", "lru_conv_w", "lru_conv_b", "lru_wa", "lru_ba",
         "lru_wx", "lru_bx", "lru_lambda", "gm_out_g", "lru_out_g", "w_out", "norm2_g", "ffn_w_up", "ffn_conv_w",
         "ffn_conv_b", "ffn_w_down", "final_g"]


def kernel(x, norm1_g, w_in, gm_v_g, gm_v_b, gm_ws, gm_bs, lru_conv_w, lru_conv_b, lru_wa, lru_ba, lru_wx, lru_bx, lru_lambda, gm_out_g, lru_out_g, w_out, norm2_g, ffn_w_up, ffn_conv_w, ffn_conv_b, ffn_w_down, final_g, loss_target, m_norm1_g, m_w_in, m_gm_v_g, m_gm_v_b, m_gm_ws, m_gm_bs, m_lru_conv_w, m_lru_conv_b, m_lru_wa, m_lru_ba, m_lru_wx, m_lru_bx, m_lru_lambda, m_gm_out_g, m_lru_out_g, m_w_out, m_norm2_g, m_ffn_w_up, m_ffn_conv_w, m_ffn_conv_b, m_ffn_w_down, m_final_g, v_norm1_g, v_w_in, v_gm_v_g, v_gm_v_b, v_gm_ws, v_gm_bs, v_lru_conv_w, v_lru_conv_b, v_lru_wa, v_lru_ba, v_lru_wx, v_lru_bx, v_lru_lambda, v_gm_out_g, v_lru_out_g, v_w_out, v_norm2_g, v_ffn_w_up, v_ffn_conv_w, v_ffn_conv_b, v_ffn_w_down, v_final_g):
    w = dict(norm1_g=norm1_g, w_in=w_in, gm_v_g=gm_v_g, gm_v_b=gm_v_b, gm_ws=gm_ws, gm_bs=gm_bs, lru_conv_w=lru_conv_w, lru_conv_b=lru_conv_b, lru_wa=lru_wa, lru_ba=lru_ba, lru_wx=lru_wx, lru_bx=lru_bx, lru_lambda=lru_lambda, gm_out_g=gm_out_g, lru_out_g=lru_out_g, w_out=w_out, norm2_g=norm2_g, ffn_w_up=ffn_w_up, ffn_conv_w=ffn_conv_w, ffn_conv_b=ffn_conv_b, ffn_w_down=ffn_w_down, final_g=final_g)
    m = dict(norm1_g=m_norm1_g, w_in=m_w_in, gm_v_g=m_gm_v_g, gm_v_b=m_gm_v_b, gm_ws=m_gm_ws, gm_bs=m_gm_bs, lru_conv_w=m_lru_conv_w, lru_conv_b=m_lru_conv_b, lru_wa=m_lru_wa, lru_ba=m_lru_ba, lru_wx=m_lru_wx, lru_bx=m_lru_bx, lru_lambda=m_lru_lambda, gm_out_g=m_gm_out_g, lru_out_g=m_lru_out_g, w_out=m_w_out, norm2_g=m_norm2_g, ffn_w_up=m_ffn_w_up, ffn_conv_w=m_ffn_conv_w, ffn_conv_b=m_ffn_conv_b, ffn_w_down=m_ffn_w_down, final_g=m_final_g)
    v = dict(norm1_g=v_norm1_g, w_in=v_w_in, gm_v_g=v_gm_v_g, gm_v_b=v_gm_v_b, gm_ws=v_gm_ws, gm_bs=v_gm_bs, lru_conv_w=v_lru_conv_w, lru_conv_b=v_lru_conv_b, lru_wa=v_lru_wa, lru_ba=v_lru_ba, lru_wx=v_lru_wx, lru_bx=v_lru_bx, lru_lambda=v_lru_lambda, gm_out_g=v_gm_out_g, lru_out_g=v_lru_out_g, w_out=v_w_out, norm2_g=v_norm2_g, ffn_w_up=v_ffn_w_up, ffn_conv_w=v_ffn_conv_w, ffn_conv_b=v_ffn_conv_b, ffn_w_down=v_ffn_w_down, final_g=v_final_g)

    mx, my, mc = _mesh_pos()
    shard = 2 * mx + my
    cs = jnp.stack([mc, shard]).astype(jnp.int32)

    xs = x[0]
    tgt = loss_target[0]
    s_len, d = xs.shape

    halves = lambda a: a.reshape((2, a.shape[0] // 2) + a.shape[1:])
    slot = {k: _to_slot("slot_" + k, halves(w[k][0]), cs, BF16) for k in BIG}
    win_b, wc_b, wfc_b = _all_gather([slot["w_in"],
                                      _to_slot("slot_lru_conv_w", w["lru_conv_w"][0].reshape(2, 4, -1), cs, F32),
                                      _to_slot("slot_ffn_conv_w", w["ffn_conv_w"][0].reshape(2, 12, -1), cs, F32)])
    win_g = win_b.reshape(N_CHIPS, d, -1)
    wc = wc_b.reshape(N_CHIPS, 4, -1).transpose(1, 0, 2).reshape(4, -1)
    wfc = wfc_b.reshape(N_CHIPS, 3, -1).transpose(1, 0, 2).reshape(3, -1)
    wout_b, wup_b, wdown_b = slot["w_out"], slot["ffn_w_up"], slot["ffn_w_down"]
    r_out, r_up, r_dn = wout_b.shape[2], wup_b.shape[2] // 2, wdown_b.shape[2] // 2

    tril = jnp.tril(jnp.ones((CHUNK, CHUNK), bool))
    wt32 = jnp.where(tril[None], w["gm_ws"][0], 0.0)
    wt = wt32.astype(BF16)
    wtt = wt32.transpose(0, 2, 1).astype(BF16)
    bst = jnp.repeat(w["gm_bs"][0].T, HEAD_DIM, axis=1)
    wa = w["lru_wa"][0].astype(BF16)
    wx = w["lru_wx"][0].astype(BF16)
    wat = w["lru_wa"][0].transpose(0, 2, 1).astype(BF16)
    wxt = w["lru_wx"][0].transpose(0, 2, 1).astype(BF16)
    ba = w["lru_ba"][0].reshape(1, -1)
    bx = w["lru_bx"][0].reshape(1, -1)
    gf = w["final_g"].reshape(1, -1)

    p, h1, wout_b, wup_b = _fwd_in_proj(
        xs, w["norm1_g"], win_g, comm=_job_gather([wout_b, wup_b], [(0, 0, r_out), (1, 0, r_up)], []))
    y, hs, wout_b, wup_b, wdown_b = _fwd_mixers(
        p, w["gm_v_g"], w["gm_v_b"], wt, bst, wc, w["lru_conv_b"], wa, ba, wx, bx, w["lru_lambda"],
        w["gm_out_g"], w["lru_out_g"],
        comm=_job_gather([wout_b, wup_b, wdown_b], [(1, r_up, r_up), (2, 0, r_dn)], [(0, 0, r_out), (1, 0, r_up)]))
    wout_g = wout_b.reshape(-1, d)
    x2, h2, wup_b, wdown_b = _fwd_out_proj(
        xs, y, wout_g, w["norm2_g"], comm=_job_gather([wup_b, wdown_b], [(1, r_dn, r_dn)], [(0, r_up, r_up), (1, 0, r_dn)]))
    wdown_b, = _comm_call("gather_tail", _job_gather([wdown_b], [], [(0, r_dn, r_dn)]))
    wup_g = wup_b.reshape(N_CHIPS, d, -1)
    wdown_g = wdown_b.reshape(-1, d)
    upb, cb, act, dx3, dx3b, loss_tile, dgf = _fwd_ffn(h2, wup_g, wfc, w["ffn_conv_b"], wdown_g, x2, gf, tgt)
    loss = lax.psum(loss_tile[0, 0], ("x", "y", "c"))

    adam = {}

    def adamw_big(name, gfull, hv, comm=None):
        r2 = lambda a: a.reshape(w[name].shape[1:])
        res = _adamw("adamw_" + name, r2(w[name]), gfull, r2(m[name]), r2(v[name]), halves=hv, comm=comm)
        adam[name] = res[:4]
        return res[4:]

    gd32, gdb = _bwd_weight("bwd_w_down", act[None], dx3b[None], a_planes=1, b_planes=1, shard_rows=True)
    dup, dwf, dx2, dx2b, dg2, rcv = _bwd_ffn(dx3b, wdown_g, upb, cb, wfc, wup_g, dx3, x2, w["norm2_g"],
                                             comm=_job_pair_swap([gdb]))
    pb_dn, own_dn = _pair_sum("pair_sum_3", gd32, rcv, cs)
    gu32, gub, got = _bwd_weight("bwd_w_up", h2[None], dup, a_planes=1, b_planes=2, shard_rows=False,
                                 comm=_job_chip_exchange([pb_dn]))
    red_dn = _chip_sum("chip_sum_3", own_dn, got, cs)
    go32, gob, rcv, red_dn = _bwd_weight("bwd_w_out", y[None], dx2b[None], a_planes=1, b_planes=1, shard_rows=True,
                                         comm=_merge_jobs([_job_pair_swap([gub]), _job_halves_swap([red_dn])]))
    pb_up, own_up = _pair_sum("pair_sum_2", gu32, rcv, cs)
    adamw_big("ffn_w_down", red_dn, "cols")
    (dp, dgv, dbv, dwt, dbst, dwc, dbc, dwa, dba, dwx, dbx, dsp, dggm, dglru, got_up, rcv) = _bwd_mixers(
        p, dx2b, wout_g, hs, w["gm_v_g"], w["gm_v_b"], wt, wtt, bst, wc, w["lru_conv_b"], wa, wat, ba, wx, wxt, bx,
        w["lru_lambda"], w["gm_out_g"], w["lru_out_g"],
        comm=_merge_jobs([_job_chip_exchange([pb_up]), _job_pair_swap([gob])]))
    pb_out, own_out = _pair_sum("pair_sum_1", go32, rcv, cs)
    red_up = _chip_sum("chip_sum_2", own_up, got_up, cs)
    gi32, gib, red_up, got_out = _bwd_weight("bwd_w_in", h1[None], dp[None], a_planes=1, b_planes=1, shard_rows=False,
                                             comm=_merge_jobs([_job_halves_swap([red_up]), _job_chip_exchange([pb_out])]))
    red_out = _chip_sum("chip_sum_1", own_out, got_out, cs)
    rcv, = _comm_call("w_in_swap", _job_pair_swap([gib]))
    pb_in, own_in = _pair_sum("pair_sum_0", gi32, rcv, cs)
    grad_x, dg1, got_in, red_out = _bwd_in(dp, win_g, dx2, xs, w["norm1_g"],
                                           comm=_merge_jobs([_job_chip_exchange([pb_in]), _job_halves_swap([red_out])]))
    red_in = _chip_sum("chip_sum_0", own_in, got_in, cs)

    dwfc = dwf[:, :, :3].transpose(2, 0, 1, 3).reshape(3, -1)
    dbfc = dwf[:, :, 3].reshape(1, -1)
    dlam = dsp * (-_sigmoid(-w["lru_lambda"]))
    small_grads = dict(
        norm1_g=dg1, gm_v_g=dgv, gm_v_b=dbv, gm_ws=jnp.where(tril[None], dwt, 0.0),
        gm_bs=dbst.reshape(CHUNK, HEADS, HEAD_DIM).sum(-1).T, lru_conv_w=dwc, lru_conv_b=dbc, lru_wa=dwa, lru_ba=dba,
        lru_wx=dwx, lru_bx=dbx, lru_lambda=dlam, gm_out_g=dggm, lru_out_g=dglru, norm2_g=dg2, ffn_conv_w=dwfc,
        ffn_conv_b=dbfc, final_g=dgf)
    full_shapes = [small_grads[k].shape for k in SMALL]
    rows_full = _pack_rows(full_shapes, 16)
    gpack = _pack([small_grads[k] for k in SMALL], rows_full).reshape(2, rows_full // 2, PACK_COLS)

    rcv, = _comm_call("small_swap", _job_pair_swap([gpack]))
    small_pair = _small_pair_sum(gpack, rcv, cs)
    got_small, = _comm_call("small_exchange", _job_chip_exchange([], [small_pair]))
    small_half = _small_chip_sum(small_pair, got_small, cs)
    red_in, small_full = _comm_call("tail_halves_swap", _job_halves_swap([red_in, small_half]))
    adamw_big("ffn_w_up", red_up, "rows")
    adamw_big("w_out", red_out, "cols")
    adamw_big("w_in", red_in, "rows")

    grads = {}
    for name, g in zip(SMALL, _unpack(small_full, full_shapes)):
        blk = w[name].shape[1:] if w[name].ndim > 1 else w[name].shape
        if name in ("lru_conv_w", "ffn_conv_w"):
            g = lax.dynamic_slice_in_dim(g, shard * blk[1], blk[1], axis=1)
        grads[name] = g.reshape(blk)

    delta, new_m, new_v = {}, {}, {}
    for name in BIG:
        delta[name], new_m[name], new_v[name], grads[name] = adam[name]
    blk_shapes = [grads[k].shape for k in SMALL]
    rows_blk = _pack_rows(blk_shapes, T_ELEM)
    packs = [_pack([src[k] for k in SMALL], rows_blk) for src in (w, grads, m, v)]
    outs = _adamw("adamw_small", *packs, halves=None)
    for dst, buf in zip((delta, new_m, new_v), outs):
        for name, a in zip(SMALL, _unpack(buf, blk_shapes)):
            dst[name] = a

    def shaped(dct):
        return [dct[k].reshape(w[k].shape) for k in ORDER]

    return (loss, grad_x[None], *shaped(grads), *shaped(delta), *shaped(new_m), *shaped(new_v))
```

```python
import functools
import math

import jax
import jax.numpy as jnp
from jax import lax
from jax.experimental import pallas as pl
from jax.experimental.pallas import tpu as pltpu

F32 = jnp.float32
BF16 = jnp.bfloat16
MESH = pl.DeviceIdType.MESH
ANY = pl.BlockSpec(memory_space=pltpu.HBM)

GM_W = 1024
LRU_W = 1024
CHUNK = 128
HEADS = 8
HEAD_DIM = 128
LRU_C = 8.0
RMS_EPS = 1e-6
LN_EPS = 1e-5
ADAM_LR = 0.001
ADAM_B1 = 0.9
ADAM_B2 = 0.999
ADAM_EPS = 1e-08
ADAM_WD = 0.01
ADAM_STEP = 10

N_CHIPS = 4
HALO = 8
PACK_COLS = 1024
VMEM_LIMIT = 56 * 1024 * 1024

TM_IN = 1024
TS_MIX = 256
TM_OUT = 512
TM_UP = 512
TN_UP = 512
T_EPI = 128
T_CHUNK = 16
TM_BW = 512
TW_M = 1024
TW_N = 1024
TW_K = 4096
T_ELEM = 256
LANES = 128


def _tile(dim, cap):
    t = min(cap, dim) // LANES * LANES
    while dim % t:
        t -= LANES
    return t

_GELU_K0 = 0.7978845608028654
_GELU_K1 = 0.044715


class _Job:
    def __init__(self, ins, out_shapes, aliases, n, copies):
        self.ins, self.out_shapes, self.aliases, self.n, self.copies = list(ins), list(out_shapes), dict(aliases), n, copies

    def make(self, cin, cout, send_sems, recv_sems):
        x, y, c = _mesh_pos()
        return [pltpu.make_async_remote_copy(src_ref=src, dst_ref=dst, send_sem=send_sems.at[k], recv_sem=recv_sems.at[k],
                                             device_id=dev, device_id_type=MESH)
                for k, (src, dst, dev) in enumerate(self.copies(cin, cout, x, y, c))]


def _merge_jobs(jobs):
    ins, outs, aliases, spans = [], [], {}, []
    for jb in jobs:
        spans.append((len(ins), len(jb.ins), len(outs), len(jb.out_shapes)))
        aliases.update({len(ins) + a: len(outs) + b for a, b in jb.aliases.items()})
        ins += jb.ins
        outs += jb.out_shapes

    def copies(cin, cout, x, y, c):
        out = []
        for jb, (i0, ni, o0, no) in zip(jobs, spans):
            out += jb.copies(cin[i0:i0 + ni], cout[o0:o0 + no], x, y, c)
        return out

    return _Job(ins, outs, aliases, sum(jb.n for jb in jobs), copies)


def _pcall(body, *, name, out_shape, grid=None, in_specs=None, out_specs=None, scratch_shapes=(),
           grid_spec=None, dims=None, aliases=None, comm=None):
    params = pltpu.CompilerParams(dimension_semantics=dims, vmem_limit_bytes=VMEM_LIMIT)
    kw = dict(name=name, compiler_params=params)
    if grid_spec is not None:
        if aliases:
            kw["input_output_aliases"] = aliases
        return pl.pallas_call(body, grid_spec=grid_spec, out_shape=out_shape, **kw)
    scratch_shapes = list(scratch_shapes)
    if comm is not None:
        out_shape = list(out_shape) if isinstance(out_shape, (list, tuple)) else [out_shape]
        out_specs = list(out_specs) if isinstance(out_specs, (list, tuple)) else [out_specs]
        n_in, n_out, n_scr = len(in_specs), len(out_shape), len(scratch_shapes)
        n_ci, n_co = len(comm.ins), len(comm.out_shapes)
        inner, steps = body, tuple(grid)

        def body(*refs):
            o0 = n_in + n_ci
            s0 = o0 + n_out + n_co
            cps = comm.make(refs[n_in:o0], refs[o0 + n_out:s0], refs[s0 + n_scr], refs[s0 + n_scr + 1])
            ids = [pl.program_id(k) for k in range(len(steps))]

            @pl.when(functools.reduce(jnp.logical_and, [i == 0 for i in ids]))
            def _():
                for cp in cps:
                    cp.start()

            inner(*refs[:n_in], *refs[o0:o0 + n_out], *refs[s0:s0 + n_scr])

            @pl.when(functools.reduce(jnp.logical_and, [i == n - 1 for i, n in zip(ids, steps)]))
            def _():
                for cp in cps:
                    cp.wait()

        in_specs = list(in_specs) + [ANY] * n_ci
        out_specs = out_specs + [ANY] * n_co
        out_shape = out_shape + comm.out_shapes
        scratch_shapes = scratch_shapes + [pltpu.SemaphoreType.DMA((comm.n,)), pltpu.SemaphoreType.DMA((comm.n,))]
        aliases = {**(aliases or {}), **{n_in + a: n_out + b for a, b in comm.aliases.items()}}
    if aliases:
        kw["input_output_aliases"] = aliases
    if grid is not None:
        kw["grid"] = grid
    call = pl.pallas_call(body, in_specs=in_specs, out_specs=out_specs, scratch_shapes=scratch_shapes,
                          out_shape=out_shape, **kw)
    if comm is None:
        return call
    return lambda *args: call(*args, *comm.ins)


def _comm_call(name, job):
    n_ci, n_co = len(job.ins), len(job.out_shapes)

    def body(*refs):
        cps = job.make(refs[:n_ci], refs[n_ci:n_ci + n_co], refs[n_ci + n_co], refs[n_ci + n_co + 1])
        for cp in cps:
            cp.start()
        for cp in cps:
            cp.wait()

    return pl.pallas_call(
        body, name=name, in_specs=[ANY] * n_ci, out_specs=[ANY] * n_co, out_shape=job.out_shapes,
        scratch_shapes=[pltpu.SemaphoreType.DMA((job.n,)), pltpu.SemaphoreType.DMA((job.n,))],
        input_output_aliases=job.aliases)(*job.ins)


def _gelu(x):
    t = jnp.tanh(_GELU_K0 * (x + _GELU_K1 * (x * x * x)))
    return 0.5 * x * (1.0 + t)


def _gelu_and_grad(x):
    x2 = x * x
    t = jnp.tanh(_GELU_K0 * (x + _GELU_K1 * (x2 * x)))
    g = 0.5 * x * (1.0 + t)
    dg = 0.5 * (1.0 + t) + 0.5 * x * (1.0 - t * t) * (_GELU_K0 * (1.0 + 3.0 * _GELU_K1 * x2))
    return g, dg


def _sigmoid(x):
    return 1.0 / (1.0 + jnp.exp(-x))


def _neg_expm1(x):
    series = -x * (1.0 + x * (0.5 + x * (1.0 / 6.0 + x * (1.0 / 24.0 + x * (1.0 / 120.0 + x * (1.0 / 720.0))))))
    return jnp.where(x > -0.1, series, 1.0 - jnp.exp(x))


def _softplus(z):
    return jnp.maximum(z, 0.0) + jnp.log(1.0 + jnp.exp(-jnp.abs(z)))


def _rowmean(x):
    return jnp.mean(x, axis=-1, keepdims=True)


def _colsum(x):
    return jnp.sum(x, axis=0, keepdims=True)


def _rms_stats(x):
    r = lax.rsqrt(_rowmean(x * x) + RMS_EPS)
    return r, x * r


def _rms_bwd(dy, n, r, g):
    dn = dy * g
    return r * (dn - n * _rowmean(dn * n)), dy * n


def _shift_prev(x, halo, d):
    cat = jnp.concatenate([halo, x], axis=0)
    return pltpu.roll(cat, d, 0)[HALO:, :]


def _prev_rows(cat):
    return cat[HALO:, :], pltpu.roll(cat, 1, 0)[HALO:, :], pltpu.roll(cat, 2, 0)[HALO:, :]


def _next_rows(cat):
    n = cat.shape[0]
    return cat[:n - HALO, :], pltpu.roll(cat, n - 1, 0)[:n - HALO, :], pltpu.roll(cat, n - 2, 0)[:n - HALO, :]


def _shift_next(x, halo, d):
    n = x.shape[0]
    cat = jnp.concatenate([x, halo], axis=0)
    return pltpu.roll(cat, n + HALO - d, 0)[:n, :]


def _dot(a, b):
    return jnp.dot(a, b, preferred_element_type=F32)


def _dot_nt(a, b):
    return lax.dot_general(a, b, (((1,), (1,)), ((), ())), preferred_element_type=F32)


def _dot_tn(a, b):
    return lax.dot_general(a, b, (((0,), (0,)), ((), ())), preferred_element_type=F32)


def _fwd_in_proj(x, g1, win_g, comm=None):
    s_len, d = x.shape
    nsh, _, ncol = win_g.shape
    tm = min(TM_IN, s_len)
    te = min(T_EPI, tm)

    def body(x_ref, g_ref, w_ref, p_ref, h_ref):
        @pl.when(pl.program_id(1) == 0)
        def _():
            g = g_ref[...]

            def chunk(q, carry):
                rows = pl.ds(pl.multiple_of(q * te, te), te)
                _, n = _rms_stats(x_ref[rows, :])
                h_ref[rows, :] = (n * g).astype(BF16)
                return carry

            lax.fori_loop(0, tm // te, chunk, 0)

        p_ref[...] = _dot(h_ref[...], w_ref[...])

    return _pcall(
        body, name="fwd_in_proj", grid=(s_len // tm, nsh),
        in_specs=[pl.BlockSpec((tm, d), lambda i, j: (i, 0)),
                  pl.BlockSpec((1, d), lambda i, j: (0, 0)),
                  pl.BlockSpec((None, d, ncol), lambda i, j: (j, 0, 0))],
        out_specs=[pl.BlockSpec((tm, ncol), lambda i, j: (i, j)),
                   pl.BlockSpec((tm, d), lambda i, j: (i, 0))],
        out_shape=[jax.ShapeDtypeStruct((s_len, nsh * ncol), F32), jax.ShapeDtypeStruct((s_len, d), BF16)],
        dims=("arbitrary", "arbitrary"), comm=comm)(x, g1, win_g)


def _gm_forward(z, gv, bv, wt_ref, bst_ref, vl_s, mix_s):
    ts = z.shape[0]
    ge = _gelu(z)
    u = ge[:, :GM_W]
    v = ge[:, GM_W:]
    vc = v - _rowmean(v)
    rs = lax.rsqrt(_rowmean(vc * vc) + LN_EPS)
    vh = vc * rs
    vl_s[...] = (vh * gv + bv).astype(BF16)
    for cc in range(ts // CHUNK):
        rows = slice(cc * CHUNK, (cc + 1) * CHUNK)
        for hh in range(HEADS):
            cols = slice(hh * HEAD_DIM, (hh + 1) * HEAD_DIM)
            mix_s[rows, cols] = _dot(wt_ref[hh], vl_s[rows, cols]) + bst_ref[:, cols]
    mixed = mix_s[...]
    return u, mixed, vh, rs, u * mixed


def _lru_gates(xl, halo, wc_ref, bc_ref, wa_ref, ba_ref, wx_ref, bx_ref, lam_ref, z_s):
    x1 = _shift_prev(xl, halo, 1)
    x2 = _shift_prev(xl, halo, 2)
    x3 = _shift_prev(xl, halo, 3)
    xr = bc_ref[...] + wc_ref[0:1, :] * x3 + wc_ref[1:2, :] * x2 + wc_ref[2:3, :] * x1 + wc_ref[3:4, :] * xl
    xrb = xr.astype(BF16)
    for hh in range(HEADS):
        cols = slice(hh * HEAD_DIM, (hh + 1) * HEAD_DIM)
        z_s[:, cols] = _dot(xrb[:, cols], wa_ref[hh])
        z_s[:, LRU_W + hh * HEAD_DIM:LRU_W + (hh + 1) * HEAD_DIM] = _dot(xrb[:, cols], wx_ref[hh])
    ra = _sigmoid(z_s[:, :LRU_W] + ba_ref[...])
    ri = _sigmoid(z_s[:, LRU_W:] + bx_ref[...])
    sp = _softplus(-lam_ref[...])
    la = (-LRU_C) * ra * sp
    a = jnp.exp(la)
    mult = jnp.sqrt(_neg_expm1(2.0 * la))
    return dict(x1=x1, x2=x2, x3=x3, xr=xr, xrb=xrb, ra=ra, ri=ri, sp=sp, a=a, mult=mult)


def _fwd_mixers(p, gv, bv, wt, bst, wc, bc, wa, ba, wx, bx, lam, ggm, glru, comm=None):
    s_len = p.shape[0]
    ts = min(TS_MIX, s_len)

    def body(pz_ref, pgl_ref, pxl_ref, gv_ref, bv_ref, wt_ref, bst_ref, wc_ref, bc_ref, wa_ref, ba_ref, wx_ref,
             bx_ref, lam_ref, ggm_ref, glru_ref, y_ref, hs_ref, tail_ref, h_ref, a_s, b_s, vl_s, mix_s, z_s):
        @pl.when(pl.program_id(0) == 0)
        def _():
            tail_ref[...] = jnp.zeros_like(tail_ref)
            h_ref[...] = jnp.zeros_like(h_ref)

        _, _, _, _, ygm = _gm_forward(pz_ref[...], gv_ref[...], bv_ref[...], wt_ref, bst_ref, vl_s, mix_s)
        _, ngm = _rms_stats(ygm)
        y_ref[:, :GM_W] = (ngm * ggm_ref[...]).astype(BF16)

        xl = pxl_ref[...]
        gts = _lru_gates(xl, tail_ref[...], wc_ref, bc_ref, wa_ref, ba_ref, wx_ref, bx_ref, lam_ref, z_s)
        tail_ref[...] = xl[ts - HALO:, :]
        a_s[...] = gts["a"]
        b_s[...] = gts["mult"] * (gts["ri"] * gts["xr"])

        def step(t, h):
            h = a_s[pl.ds(t, 1), :] * h + b_s[pl.ds(t, 1), :]
            hs_ref[pl.ds(t, 1), :] = h
            return h

        h_ref[...] = lax.fori_loop(0, ts, step, h_ref[...], unroll=8)
        yl = hs_ref[...] * _gelu(pgl_ref[...])
        _, nl = _rms_stats(yl)
        y_ref[:, GM_W:] = (nl * glru_ref[...]).astype(BF16)

    full = lambda shape: pl.BlockSpec(shape, lambda i: (0,) * len(shape))
    return _pcall(
        body, name="fwd_mixers", grid=(s_len // ts,),
        in_specs=[pl.BlockSpec((ts, 2 * GM_W), lambda i: (i, 0)),
                  pl.BlockSpec((ts, LRU_W), lambda i: (i, 2)),
                  pl.BlockSpec((ts, LRU_W), lambda i: (i, 3)),
                  full((1, GM_W)), full((1, GM_W)), full((HEADS, CHUNK, CHUNK)), full((CHUNK, GM_W)),
                  full((4, LRU_W)), full((1, LRU_W)), full((HEADS, HEAD_DIM, HEAD_DIM)), full((1, LRU_W)),
                  full((HEADS, HEAD_DIM, HEAD_DIM)), full((1, LRU_W)), full((1, LRU_W)), full((1, GM_W)),
                  full((1, LRU_W))],
        out_specs=[pl.BlockSpec((ts, GM_W + LRU_W), lambda i: (i, 0)), pl.BlockSpec((ts, LRU_W), lambda i: (i, 0))],
        out_shape=[jax.ShapeDtypeStruct((s_len, GM_W + LRU_W), BF16), jax.ShapeDtypeStruct((s_len, LRU_W), F32)],
        scratch_shapes=[pltpu.VMEM((HALO, LRU_W), F32), pltpu.VMEM((1, LRU_W), F32),
                        pltpu.VMEM((ts, LRU_W), F32), pltpu.VMEM((ts, LRU_W), F32),
                        pltpu.VMEM((ts, GM_W), BF16), pltpu.VMEM((ts, GM_W), F32), pltpu.VMEM((ts, 2 * LRU_W), F32)],
        dims=("arbitrary",), comm=comm)(p, p, p, gv, bv, wt, bst, wc, bc, wa, ba, wx, bx, lam, ggm, glru)


def _fwd_out_proj(x, y, wout_g, g2, comm=None):
    s_len, d = x.shape
    tm = min(TM_OUT, s_len)

    def body(x_ref, y_ref, w_ref, g_ref, x2_ref, h2_ref):
        x2 = x_ref[...] + _dot(y_ref[...], w_ref[...])
        x2_ref[...] = x2
        _, n = _rms_stats(x2)
        h2_ref[...] = (n * g_ref[...]).astype(BF16)

    return _pcall(
        body, name="fwd_out_proj", grid=(s_len // tm,),
        in_specs=[pl.BlockSpec((tm, d), lambda i: (i, 0)), pl.BlockSpec((tm, d), lambda i: (i, 0)),
                  pl.BlockSpec((d, d), lambda i: (0, 0)), pl.BlockSpec((1, d), lambda i: (0, 0))],
        out_specs=[pl.BlockSpec((tm, d), lambda i: (i, 0)), pl.BlockSpec((tm, d), lambda i: (i, 0))],
        out_shape=[jax.ShapeDtypeStruct((s_len, d), F32), jax.ShapeDtypeStruct((s_len, d), BF16)],
        dims=("arbitrary",), comm=comm)(x, y, wout_g, g2)


def _row_fetch(hbm_ref, buf_ref, sem, row0, rows):
    return pltpu.make_async_copy(hbm_ref.at[pl.ds(row0, rows), :], buf_ref, sem)


def _fwd_ffn(h2, wup_g, wfc, bfc, wdown_g, x2, gf, target):
    s_len, d = h2.shape
    nsh, _, ncol = wup_g.shape
    tn = TN_UP
    tn2 = 2 * tn
    f = nsh * ncol // 2
    nj = f // tn
    nps = ncol // tn
    tm = min(TM_UP, s_len)
    te = min(T_EPI, tm)
    tr = T_CHUNK

    def body(h_ref, wg_ref, wv_ref, wcg_ref, wcv_ref, bg_ref, bv_ref, wd_ref, g_ref, x2_hbm, t_hbm,
             upb_ref, cb_ref, act_ref, dx3_ref, dx3b_ref, loss_ref, dgf_ref, tail_ref, acc_ref, x2_buf, t_buf, up_ref,
             sems):
        i, j = pl.program_id(0), pl.program_id(1)
        row0 = pl.multiple_of(i * tm, tm)
        fetches = (_row_fetch(x2_hbm, x2_buf, sems.at[0], row0, tm), _row_fetch(t_hbm, t_buf, sems.at[1], row0, tm))

        @pl.when(jnp.logical_and(i == 0, j == 0))
        def _():
            tail_ref[...] = jnp.zeros_like(tail_ref)
            loss_ref[...] = jnp.zeros_like(loss_ref)
            dgf_ref[...] = jnp.zeros_like(dgf_ref)

        @pl.when(j == 0)
        def _():
            acc_ref[...] = jnp.zeros_like(acc_ref)
            for cp in fetches:
                cp.start()

        h = h_ref[...]
        up_ref[:, :tn] = _dot(h, wg_ref[...])
        up_ref[:, tn:] = _dot(h, wv_ref[...])

        planes = ((wcg_ref, bg_ref), (wcv_ref, bv_ref))
        for c in range(tn // LANES):
            cols = slice(c * LANES, (c + 1) * LANES)
            pcols = [slice(pln * tn + c * LANES, pln * tn + (c + 1) * LANES) for pln in range(2)]

            def conv(pln, u, u1, u2):
                wc_ref, b_ref = planes[pln]
                return b_ref[:, cols] + wc_ref[0:1, cols] * u2 + wc_ref[1:2, cols] * u1 + wc_ref[2:3, cols] * u

            def emit(rows, us):
                cs_ = [conv(pln, *us[pln]) for pln in range(2)]
                for pln in range(2):
                    upb_ref[rows, pcols[pln]] = us[pln][0].astype(BF16)
                    cb_ref[rows, pcols[pln]] = cs_[pln].astype(BF16)
                act_ref[rows, cols] = (_gelu(cs_[0]) * cs_[1]).astype(BF16)

            first = [_prev_rows(jnp.concatenate([tail_ref[j, :, pc], up_ref[0:tr, pc]], axis=0)) for pc in pcols]
            emit(slice(0, tr), first)

            for k in range(1, tm // tr):
                r0 = k * tr
                us = [_prev_rows(up_ref[r0 - HALO:r0 + tr, pc]) for pc in pcols]
                emit(slice(r0, r0 + tr), us)
        tail_ref[j] = up_ref[tm - HALO:tm, :]
        acc_ref[...] += _dot(act_ref[...], wd_ref[...])

        @pl.when(j == nj - 1)
        def _():
            for cp in fetches:
                cp.wait()
            g = g_ref[...]

            def chunk(k, carry):
                rows = pl.ds(pl.multiple_of(k * te, te), te)
                x3 = x2_buf[rows, :] + acc_ref[rows, :]
                r, n = _rms_stats(x3)
                err = n * g - t_buf[rows, :]
                loss_ref[...] += jnp.sum(err * err) * (0.5 / d)
                dx3, dgn = _rms_bwd(err * (1.0 / d), n, r, g)
                dgf_ref[...] += _colsum(dgn)
                dx3_ref[rows, :] = dx3
                dx3b_ref[rows, :] = dx3.astype(BF16)
                return carry

            lax.fori_loop(0, tm // te, chunk, 0)

    hbm = pl.BlockSpec(memory_space=pl.ANY)
    return _pcall(
        body, name="fwd_ffn", grid=(s_len // tm, nj),
        in_specs=[pl.BlockSpec((tm, d), lambda i, j: (i, 0)),
                  pl.BlockSpec((None, d, tn), lambda i, j: (j // nps, 0, j % nps)),
                  pl.BlockSpec((None, d, tn), lambda i, j: (nsh // 2 + j // nps, 0, j % nps)),
                  pl.BlockSpec((3, tn), lambda i, j: (0, j)), pl.BlockSpec((3, tn), lambda i, j: (0, nj + j)),
                  pl.BlockSpec((1, tn), lambda i, j: (0, j)), pl.BlockSpec((1, tn), lambda i, j: (0, nj + j)),
                  pl.BlockSpec((tn, d), lambda i, j: (j, 0)), pl.BlockSpec((1, d), lambda i, j: (0, 0)), hbm, hbm],
        out_specs=[pl.BlockSpec((tm, tn2), lambda i, j: (i, j)), pl.BlockSpec((tm, tn2), lambda i, j: (i, j)),
                   pl.BlockSpec((tm, tn), lambda i, j: (i, j)),
                   pl.BlockSpec((tm, d), lambda i, j: (i, 0)), pl.BlockSpec((tm, d), lambda i, j: (i, 0)),
                   pl.BlockSpec((8, 128), lambda i, j: (0, 0)), pl.BlockSpec((1, d), lambda i, j: (0, 0))],
        out_shape=[jax.ShapeDtypeStruct((s_len, 2 * f), BF16), jax.ShapeDtypeStruct((s_len, 2 * f), BF16),
                   jax.ShapeDtypeStruct((s_len, f), BF16),
                   jax.ShapeDtypeStruct((s_len, d), F32), jax.ShapeDtypeStruct((s_len, d), BF16),
                   jax.ShapeDtypeStruct((8, 128), F32), jax.ShapeDtypeStruct((1, d), F32)],
        scratch_shapes=[pltpu.VMEM((nj, HALO, tn2), F32), pltpu.VMEM((tm, d), F32), pltpu.VMEM((tm, d), F32),
                        pltpu.VMEM((tm, d), F32), pltpu.VMEM((tm, tn2), F32), pltpu.SemaphoreType.DMA((2,))],
        dims=("arbitrary", "arbitrary"))(h2, wup_g, wup_g, wfc, wfc, bfc, bfc, wdown_g, gf, x2, target)


def _bwd_ffn(dx3b, wdown_g, upb, cb, wfc, wup_g, dx3, x2, g2, comm=None):
    s_len, d = dx3b.shape
    nsh, _, ncol = wup_g.shape
    tn = TN_UP
    tn2 = 2 * tn
    f = nsh * ncol // 2
    nj = f // tn
    nps = ncol // tn
    tm = min(TM_UP, s_len)
    te = min(T_EPI, tm)
    nt = s_len // tm
    tr = T_CHUNK

    def body(dx_ref, w_ref, upb_ref, cb_ref, wcg_ref, wcv_ref, wug_ref, wuv_ref, g_ref, r_hbm, x_hbm,
             dup_ref, dwf_ref, dx2_ref, dx2b_ref, dg_ref, nxt_ref, acc_ref, r_buf, x_buf, da_s, dc_s, sems):
        i, j = pl.program_id(0), pl.program_id(1)
        ti = nt - 1 - i
        row0 = pl.multiple_of(ti * tm, tm)
        fetches = (_row_fetch(r_hbm, r_buf, sems.at[0], row0, tm), _row_fetch(x_hbm, x_buf, sems.at[1], row0, tm))

        @pl.when(jnp.logical_and(i == 0, j == 0))
        def _():
            dwf_ref[...] = jnp.zeros_like(dwf_ref)
            nxt_ref[...] = jnp.zeros_like(nxt_ref)
            dg_ref[...] = jnp.zeros_like(dg_ref)

        @pl.when(j == 0)
        def _():
            acc_ref[...] = jnp.zeros_like(acc_ref)
            for cp in fetches:
                cp.start()

        da_s[...] = _dot_nt(dx_ref[...], w_ref[...])
        planes = ((wcg_ref,), (wcv_ref,))
        for pln in range(2):
            dc_s[pln, tm:tm + HALO, :] = nxt_ref[pln, j]

        for c in range(tn // LANES):
            cols = slice(c * LANES, (c + 1) * LANES)
            pcols = [slice(pln * tn + c * LANES, pln * tn + (c + 1) * LANES) for pln in range(2)]
            for k in range(tm // tr):
                rows = slice(k * tr, (k + 1) * tr)
                dact = da_s[rows, cols]
                ge, gd = _gelu_and_grad(cb_ref[rows, pcols[0]].astype(F32))
                dc_s[0, rows, cols] = dact * cb_ref[rows, pcols[1]].astype(F32) * gd
                dc_s[1, rows, cols] = dact * ge

        for pln in range(2):
            nxt_ref[pln, j] = dc_s[pln, 0:HALO, :]

        def fold(v):
            out = v[0:8, :]
            for q in range(1, tr // 8):
                out = out + v[8 * q:8 * q + 8, :]
            return out

        for c in range(tn // LANES):
            cols = slice(c * LANES, (c + 1) * LANES)
            for pln in range(2):
                wc_ref = planes[pln][0]
                pc = slice(pln * tn + c * LANES, pln * tn + (c + 1) * LANES)
                sums = (jnp.zeros((8, LANES), F32),) * 4
                for k in range(tm // tr):
                    r0 = k * tr
                    dc, dc1, dc2 = _next_rows(dc_s[pln, r0:r0 + tr + HALO, cols])
                    dup = wc_ref[2:3, cols] * dc + wc_ref[1:2, cols] * dc1 + wc_ref[0:1, cols] * dc2
                    dup_ref[pln, r0:r0 + tr, cols] = dup.astype(BF16)
                    u = upb_ref[r0:r0 + tr, pc].astype(F32)
                    new = (fold(dc2 * u), fold(dc1 * u), fold(dc * u), fold(dc))
                    sums = tuple(a + b for a, b in zip(sums, new))
                for term in range(4):
                    dwf_ref[pln, j, term:term + 1, cols] += _colsum(sums[term])
        acc_ref[...] += _dot_nt(dup_ref[0], wug_ref[...]) + _dot_nt(dup_ref[1], wuv_ref[...])

        @pl.when(j == nj - 1)
        def _():
            for cp in fetches:
                cp.wait()
            g = g_ref[...]

            def chunk(k, carry):
                rows = pl.ds(pl.multiple_of(k * te, te), te)
                r, n = _rms_stats(x_buf[rows, :])
                dxn, dgn = _rms_bwd(acc_ref[rows, :], n, r, g)
                dg_ref[...] += _colsum(dgn)
                dx = r_buf[rows, :] + dxn
                dx2_ref[rows, :] = dx
                dx2b_ref[rows, :] = dx.astype(BF16)
                return carry

            lax.fori_loop(0, tm // te, chunk, 0)

    hbm = pl.BlockSpec(memory_space=pl.ANY)
    rev = lambda i: nt - 1 - i
    return _pcall(
        body, name="bwd_ffn", grid=(nt, nj),
        in_specs=[pl.BlockSpec((tm, d), lambda i, j: (rev(i), 0)),
                  pl.BlockSpec((tn, d), lambda i, j: (j, 0)),
                  pl.BlockSpec((tm, tn2), lambda i, j: (rev(i), j)),
                  pl.BlockSpec((tm, tn2), lambda i, j: (rev(i), j)),
                  pl.BlockSpec((3, tn), lambda i, j: (0, j)), pl.BlockSpec((3, tn), lambda i, j: (0, nj + j)),
                  pl.BlockSpec((None, d, tn), lambda i, j: (j // nps, 0, j % nps)),
                  pl.BlockSpec((None, d, tn), lambda i, j: (nsh // 2 + j // nps, 0, j % nps)),
                  pl.BlockSpec((1, d), lambda i, j: (0, 0)), hbm, hbm],
        out_specs=[pl.BlockSpec((2, tm, tn), lambda i, j: (0, rev(i), j)),
                   pl.BlockSpec((2, nj, 8, tn), lambda i, j: (0, 0, 0, 0)),
                   pl.BlockSpec((tm, d), lambda i, j: (rev(i), 0)), pl.BlockSpec((tm, d), lambda i, j: (rev(i), 0)),
                   pl.BlockSpec((1, d), lambda i, j: (0, 0))],
        out_shape=[jax.ShapeDtypeStruct((2, s_len, f), BF16), jax.ShapeDtypeStruct((2, nj, 8, tn), F32),
                   jax.ShapeDtypeStruct((s_len, d), F32), jax.ShapeDtypeStruct((s_len, d), BF16),
                   jax.ShapeDtypeStruct((1, d), F32)],
        scratch_shapes=[pltpu.VMEM((2, nj, HALO, tn), F32), pltpu.VMEM((tm, d), F32), pltpu.VMEM((tm, d), F32),
                        pltpu.VMEM((tm, d), F32), pltpu.VMEM((tm, tn), F32), pltpu.VMEM((2, tm + HALO, tn), F32),
                        pltpu.SemaphoreType.DMA((2,))],
        dims=("arbitrary", "arbitrary"), comm=comm)(dx3b, wdown_g, upb, cb, wfc, wfc, wup_g, wup_g, g2, dx3, x2)


def _bwd_in(dp, win_g, resid, x_in, g, comm=None):
    s_len, d = x_in.shape
    nsh, _, ncol = win_g.shape
    tm = min(TM_BW, s_len)
    te = min(T_EPI, tm)

    def body(dz_ref, w_ref, r_ref, x_ref, g_ref, dx_ref, dg_ref, acc_ref):
        i, k = pl.program_id(0), pl.program_id(1)

        @pl.when(jnp.logical_and(i == 0, k == 0))
        def _():
            dg_ref[...] = jnp.zeros_like(dg_ref)

        @pl.when(k == 0)
        def _():
            acc_ref[...] = jnp.zeros_like(acc_ref)

        acc_ref[...] += _dot_nt(dz_ref[...], w_ref[...])

        @pl.when(k == nsh - 1)
        def _():
            g = g_ref[...]

            def chunk(q, carry):
                rows = pl.ds(pl.multiple_of(q * te, te), te)
                r, n = _rms_stats(x_ref[rows, :])
                dxn, dgn = _rms_bwd(acc_ref[rows, :], n, r, g)
                dg_ref[...] += _colsum(dgn)
                dx_ref[rows, :] = r_ref[rows, :] + dxn
                return carry

            lax.fori_loop(0, tm // te, chunk, 0)

    return _pcall(
        body, name="bwd_in", grid=(s_len // tm, nsh),
        in_specs=[pl.BlockSpec((tm, ncol), lambda i, k: (i, k)),
                  pl.BlockSpec((None, d, ncol), lambda i, k: (k, 0, 0)),
                  pl.BlockSpec((tm, d), lambda i, k: (i, 0)), pl.BlockSpec((tm, d), lambda i, k: (i, 0)),
                  pl.BlockSpec((1, d), lambda i, k: (0, 0))],
        out_specs=[pl.BlockSpec((tm, d), lambda i, k: (i, 0)), pl.BlockSpec((1, d), lambda i, k: (0, 0))],
        out_shape=[jax.ShapeDtypeStruct((s_len, d), F32), jax.ShapeDtypeStruct((1, d), F32)],
        scratch_shapes=[pltpu.VMEM((tm, d), F32)],
        dims=("arbitrary", "arbitrary"), comm=comm)(dp, win_g, resid, x_in, g)


def _bwd_mixers(p, dx2b, wout_g, hs, gv, bv, wt, wtt, bst, wc, bc, wa, wat, ba, wx, wxt, bx, lam, ggm, glru, comm=None):
    s_len = p.shape[0]
    ts = min(TS_MIX, s_len)
    nt = s_len // ts
    hb = ts // HALO

    def body(pz_ref, pgl_ref, pxl_ref, xh_ref, dx_ref, wo_ref, hs_ref, hh_ref, gv_ref, bv_ref, wt_ref, wtt_ref, bst_ref,
             wc_ref, bc_ref, wa_ref, wat_ref, ba_ref, wx_ref, wxt_ref, bx_ref, lam_ref, ggm_ref, glru_ref,
             dp_ref, dgv_ref, dbv_ref, dwt_ref, dbst_ref, dwc_ref, dbc_ref, dwa_ref, dba_ref, dwx_ref, dbx_ref,
             dsp_ref, dggm_ref, dglru_ref,
             carry_ref, nxt_ref, a_s, g_s, vl_s, mix_s, z_s, dm_s, dvl_s, dxr_s, dy_ref):
        i = pl.program_id(0)
        ti = nt - 1 - i

        @pl.when(i == 0)
        def _():
            for ref in (dgv_ref, dbv_ref, dwt_ref, dbst_ref, dwc_ref, dbc_ref, dwa_ref, dba_ref, dwx_ref, dbx_ref,
                        dsp_ref, dggm_ref, dglru_ref, carry_ref, nxt_ref):
                ref[...] = jnp.zeros_like(ref)

        dy_ref[...] = _dot_nt(dx_ref[...], wo_ref[...])

        z = pz_ref[...]
        u, mixed, vh, rs, ygm = _gm_forward(z, gv_ref[...], bv_ref[...], wt_ref, bst_ref, vl_s, mix_s)
        rg, ngm = _rms_stats(ygm)
        dygm, dgn = _rms_bwd(dy_ref[:, :GM_W], ngm, rg, ggm_ref[...])
        dggm_ref[...] += _colsum(dgn)
        du = dygm * mixed
        dmix = dygm * u
        dm_s[...] = dmix.astype(BF16)
        bsum = dmix[0:CHUNK, :]
        for cc in range(1, ts // CHUNK):
            bsum = bsum + dmix[cc * CHUNK:(cc + 1) * CHUNK, :]
        dbst_ref[...] += bsum
        for hh in range(HEADS):
            cols = slice(hh * HEAD_DIM, (hh + 1) * HEAD_DIM)
            dw = jnp.zeros((CHUNK, CHUNK), F32)
            for cc in range(ts // CHUNK):
                rows = slice(cc * CHUNK, (cc + 1) * CHUNK)
                dmb = dm_s[rows, cols]
                dw = dw + _dot_nt(dmb, vl_s[rows, cols])
                dvl_s[rows, cols] = _dot(wtt_ref[hh], dmb)
            dwt_ref[hh] += dw
        dvl = dvl_s[...]
        dgv_ref[...] += _colsum(dvl * vh)
        dbv_ref[...] += _colsum(dvl)
        dvh = dvl * gv_ref[...]
        dv = rs * (dvh - _rowmean(dvh) - vh * _rowmean(dvh * vh))
        _, gd = _gelu_and_grad(z)
        dp_ref[:, :GM_W] = (du * gd[:, :GM_W]).astype(BF16)
        dp_ref[:, GM_W:2 * GM_W] = (dv * gd[:, GM_W:]).astype(BF16)

        xl = pxl_ref[...]
        xhalo = jnp.where(ti == 0, 0.0, xh_ref[...])
        gts = _lru_gates(xl, xhalo, wc_ref, bc_ref, wa_ref, ba_ref, wx_ref, bx_ref, lam_ref, z_s)
        a, mult, ra, ri, xr, sp = gts["a"], gts["mult"], gts["ra"], gts["ri"], gts["xr"], gts["sp"]
        hs = hs_ref[...]
        hprev = _shift_prev(hs, jnp.where(ti == 0, 0.0, hh_ref[...]), 1)
        gl = pgl_ref[...]
        ggl, dggl = _gelu_and_grad(gl)
        yl = hs * ggl
        rl, nl = _rms_stats(yl)
        dyl, dgn = _rms_bwd(dy_ref[:, GM_W:], nl, rl, glru_ref[...])
        dglru_ref[...] += _colsum(dgn)
        dp_ref[:, 2 * GM_W:2 * GM_W + LRU_W] = (dyl * hs * dggl).astype(BF16)
        a_s[...] = a
        g_s[...] = dyl * ggl

        def step(k, carry):
            t = ts - 1 - k
            gt = g_s[pl.ds(t, 1), :] + carry
            g_s[pl.ds(t, 1), :] = gt
            return a_s[pl.ds(t, 1), :] * gt

        carry_ref[...] = lax.fori_loop(0, ts, step, carry_ref[...], unroll=8)
        gsc = g_s[...]
        da = gsc * hprev
        rix = ri * xr
        dmult = gsc * rix
        dri = gsc * mult * xr
        dxr = gsc * mult * ri
        dla = da * a - dmult * (a * a) / mult
        dsp_ref[...] += _colsum(dla * ra) * (-LRU_C)
        dza = (dla * sp) * (-LRU_C) * ra * (1.0 - ra)
        dzi = dri * ri * (1.0 - ri)
        dba_ref[...] += _colsum(dza)
        dbx_ref[...] += _colsum(dzi)
        dzab = dza.astype(BF16)
        dzib = dzi.astype(BF16)
        xrb = gts["xrb"]
        for hh in range(HEADS):
            cols = slice(hh * HEAD_DIM, (hh + 1) * HEAD_DIM)
            dwa_ref[hh] += _dot_tn(xrb[:, cols], dzab[:, cols])
            dwx_ref[hh] += _dot_tn(xrb[:, cols], dzib[:, cols])
            dxr_s[:, cols] = _dot(dzab[:, cols], wat_ref[hh]) + _dot(dzib[:, cols], wxt_ref[hh])
        dxr = dxr + dxr_s[...]
        dbc_ref[...] += _colsum(dxr)
        dwc_ref[0:1, :] += _colsum(dxr * gts["x3"])
        dwc_ref[1:2, :] += _colsum(dxr * gts["x2"])
        dwc_ref[2:3, :] += _colsum(dxr * gts["x1"])
        dwc_ref[3:4, :] += _colsum(dxr * xl)
        nxt = nxt_ref[...]
        nxt_ref[...] = dxr[:HALO, :]
        dxl = wc_ref[3:4, :] * dxr + wc_ref[2:3, :] * _shift_next(dxr, nxt, 1) \
            + wc_ref[1:2, :] * _shift_next(dxr, nxt, 2) + wc_ref[0:1, :] * _shift_next(dxr, nxt, 3)
        dp_ref[:, 2 * GM_W + LRU_W:] = dxl.astype(BF16)

    full = lambda shape: pl.BlockSpec(shape, lambda i: (0,) * len(shape))
    rev = lambda i: nt - 1 - i
    prev_blk = lambda i: jnp.maximum((nt - 1 - i) * hb - 1, 0)
    hhd = (HEADS, HEAD_DIM, HEAD_DIM)
    small_shapes = [(1, GM_W), (1, GM_W), (HEADS, CHUNK, CHUNK), (CHUNK, GM_W), (4, LRU_W), (1, LRU_W), hhd,
                    (1, LRU_W), hhd, (1, LRU_W), (1, LRU_W), (1, GM_W), (1, LRU_W)]
    return _pcall(
        body, name="bwd_mixers", grid=(nt,),
        in_specs=[pl.BlockSpec((ts, 2 * GM_W), lambda i: (rev(i), 0)),
                  pl.BlockSpec((ts, LRU_W), lambda i: (rev(i), 2)),
                  pl.BlockSpec((ts, LRU_W), lambda i: (rev(i), 3)),
                  pl.BlockSpec((HALO, LRU_W), lambda i: (prev_blk(i), 3)),
                  pl.BlockSpec((ts, GM_W + LRU_W), lambda i: (rev(i), 0)),
                  full((GM_W + LRU_W, GM_W + LRU_W)),
                  pl.BlockSpec((ts, LRU_W), lambda i: (rev(i), 0)),
                  pl.BlockSpec((HALO, LRU_W), lambda i: (prev_blk(i), 0)),
                  full((1, GM_W)), full((1, GM_W)), full((HEADS, CHUNK, CHUNK)), full((HEADS, CHUNK, CHUNK)),
                  full((CHUNK, GM_W)), full((4, LRU_W)), full((1, LRU_W)), full(hhd), full(hhd), full((1, LRU_W)),
                  full(hhd), full(hhd), full((1, LRU_W)), full((1, LRU_W)), full((1, GM_W)), full((1, LRU_W))],
        out_specs=[pl.BlockSpec((ts, 2 * GM_W + 2 * LRU_W), lambda i: (rev(i), 0))] + [full(s) for s in small_shapes],
        out_shape=[jax.ShapeDtypeStruct((s_len, 2 * GM_W + 2 * LRU_W), BF16)]
        + [jax.ShapeDtypeStruct(s, F32) for s in small_shapes],
        scratch_shapes=[pltpu.VMEM((1, LRU_W), F32), pltpu.VMEM((HALO, LRU_W), F32),
                        pltpu.VMEM((ts, LRU_W), F32), pltpu.VMEM((ts, LRU_W), F32),
                        pltpu.VMEM((ts, GM_W), BF16), pltpu.VMEM((ts, GM_W), F32), pltpu.VMEM((ts, 2 * LRU_W), F32),
                        pltpu.VMEM((ts, GM_W), BF16), pltpu.VMEM((ts, GM_W), F32), pltpu.VMEM((ts, LRU_W), F32),
                        pltpu.VMEM((ts, GM_W + LRU_W), F32)],
        dims=("arbitrary",), comm=comm)(p, p, p, p, dx2b, wout_g, hs, hs, gv, bv, wt, wtt, bst, wc, bc, wa, wat, ba, wx, wxt,
                                        bx, lam, ggm, glru)


def _bwd_weight(name, a, b, *, a_planes, b_planes, shard_rows, comm=None):
    _, s_len, ma = a.shape
    _, _, nb = b.shape
    m, n = a_planes * ma, b_planes * nb
    tk = min(TW_K, s_len)
    if shard_rows:
        rows, cols = m // N_CHIPS, n // 2
        tm, tn = _tile(rows, TW_M), _tile(cols, TW_N)
        out_idx = lambda i, j, k: (j * tn // cols, i * tm // rows, (i * tm % rows) // tm, (j * tn % cols) // tn)
    else:
        rows, cols = m // 2, n // N_CHIPS
        tm, tn = _tile(rows, TW_M), _tile(cols, TW_N)
        out_idx = lambda i, j, k: (i * tm // rows, j * tn // cols, (i * tm % rows) // tm, (j * tn % cols) // tn)
    nk = s_len // tk
    npa, npb = ma // tm, nb // tn

    def body(a_ref, b_ref, o_ref, ob_ref, acc_ref):
        k = pl.program_id(2)

        @pl.when(k == 0)
        def _():
            acc_ref[...] = jnp.zeros_like(acc_ref)

        acc_ref[...] += _dot_tn(a_ref[...], b_ref[...])

        @pl.when(k == nk - 1)
        def _():
            o_ref[...] = acc_ref[...]
            ob_ref[...] = acc_ref[...].astype(BF16)

    shape = (2, N_CHIPS, rows, cols)
    return _pcall(
        body, name=name, grid=(m // tm, n // tn, nk),
        in_specs=[pl.BlockSpec((None, tk, tm), lambda i, j, k: (i // npa, k, i % npa)),
                  pl.BlockSpec((None, tk, tn), lambda i, j, k: (j // npb, k, j % npb))],
        out_specs=[pl.BlockSpec((None, None, tm, tn), out_idx), pl.BlockSpec((None, None, tm, tn), out_idx)],
        out_shape=[jax.ShapeDtypeStruct(shape, F32), jax.ShapeDtypeStruct(shape, BF16)],
        scratch_shapes=[pltpu.VMEM((tm, tn), F32)],
        dims=("arbitrary", "arbitrary", "arbitrary"), comm=comm)(a, b)


def _mesh_pos():
    return lax.axis_index("x"), lax.axis_index("y"), lax.axis_index("c")


def _other_chips(x, y):
    return [(1 - x, y), (x, 1 - y), (1 - x, 1 - y)]


def _to_slot(name, a, cs, dtype):
    _, a_rows, b_cols = a.shape
    ta = min(T_ELEM, a_rows)

    def body(cs_ref, a_ref, o_ref):
        o_ref[...] = a_ref[...].astype(dtype)

    grid_spec = pltpu.PrefetchScalarGridSpec(
        num_scalar_prefetch=1, grid=(2, a_rows // ta),
        in_specs=[pl.BlockSpec((None, ta, b_cols), lambda h, r, cs_ref: (h, r, 0))],
        out_specs=pl.BlockSpec((None, None, ta, b_cols), lambda h, r, cs_ref: (cs_ref[1], h, r, 0)))
    return _pcall(body, name=name, grid_spec=grid_spec,
                  out_shape=jax.ShapeDtypeStruct((N_CHIPS,) + a.shape, dtype), dims=("arbitrary", "arbitrary"))(cs, a)


def _all_gather(bufs):
    n = len(bufs)

    def body(*refs):
        outs = refs[n:2 * n]
        send_sems, recv_sems = refs[2 * n:]
        x, y, c = _mesh_pos()
        s = 2 * x + y
        me, sib = (x, y, c), (x, y, 1 - c)
        chips = _other_chips(x, y)

        def rcopy(a, k, blk, to):
            return pltpu.make_async_remote_copy(src_ref=blk, dst_ref=blk, send_sem=send_sems.at[a * 6 + k],
                                                recv_sem=recv_sems.at[a * 6 + k], device_id=to, device_id_type=MESH)

        first = [rcopy(a, j, outs[a].at[s, c], (cx, cy, c)) for a in range(n) for j, (cx, cy) in enumerate(chips)]
        for cp in first:
            cp.start()
        passed = []
        for a in range(n):
            for j, (cx, cy) in enumerate(chips):
                blk = outs[a].at[2 * cx + cy, c]
                rcopy(a, j, blk, me).wait_recv()
                cp = rcopy(a, 3 + j, blk, sib)
                cp.start()
                passed.append(cp)
        for a in range(n):
            for j, (cx, cy) in enumerate(chips):
                rcopy(a, 3 + j, outs[a].at[2 * cx + cy, 1 - c], me).wait_recv()
        for cp in first + passed:
            cp.wait_send()

    return _pcall(
        body, name="all_gather_weights",
        in_specs=[ANY] * n, out_specs=[ANY] * n,
        out_shape=[jax.ShapeDtypeStruct(a.shape, a.dtype) for a in bufs],
        scratch_shapes=[pltpu.SemaphoreType.DMA((6 * n,)), pltpu.SemaphoreType.DMA((6 * n,))],
        aliases={a: a for a in range(n)})(*bufs)


def _same(a):
    return jax.ShapeDtypeStruct(a.shape, a.dtype)


def _job_gather(bufs, ici_parts, relay_parts):
    def copies(cin, cout, x, y, c):
        out = []
        for a, lo, n in ici_parts:
            blk = cout[a].at[2 * x + y, c, pl.ds(lo, n)]
            out += [(blk, blk, (cx, cy, c)) for cx, cy in _other_chips(x, y)]
        for a, lo, n in relay_parts:
            for cx, cy in _other_chips(x, y):
                blk = cout[a].at[2 * cx + cy, c, pl.ds(lo, n)]
                out.append((blk, blk, (x, y, 1 - c)))
        return out

    return _Job(bufs, [_same(b) for b in bufs], {a: a for a in range(len(bufs))},
                3 * (len(ici_parts) + len(relay_parts)), copies)


def _job_pair_swap(arrs):
    def copies(cin, cout, x, y, c):
        return [(cin[a].at[1 - c], cout[a], (x, y, 1 - c)) for a in range(len(arrs))]

    return _Job(arrs, [jax.ShapeDtypeStruct(a.shape[1:], a.dtype) for a in arrs], {}, len(arrs), copies)


def _job_chip_exchange(big, small=()):
    nb = len(big)
    arrs = list(big) + list(small)

    def copies(cin, cout, x, y, c):
        out = []
        for a in range(len(arrs)):
            for j, (cx, cy) in enumerate(_other_chips(x, y)):
                out.append((cin[a].at[2 * cx + cy] if a < nb else cin[a], cout[a].at[j], (cx, cy, c)))
        return out

    shapes = [a.shape[1:] for a in big] + [a.shape for a in small]
    return _Job(arrs, [jax.ShapeDtypeStruct((3,) + sh, a.dtype) for sh, a in zip(shapes, arrs)], {}, 3 * len(arrs), copies)


def _job_halves_swap(bufs):
    def copies(cin, cout, x, y, c):
        return [(cout[a].at[c], cout[a].at[c], (x, y, 1 - c)) for a in range(len(bufs))]

    return _Job(bufs, [_same(b) for b in bufs], {a: a for a in range(len(bufs))}, len(bufs), copies)


def _pair_sum(name, g32, recv, cs):
    _, nch, a_rows, b_cols = g32.shape
    ta = min(T_ELEM, a_rows)

    def body(cs_ref, g_ref, r_ref, pb_ref, own_ref):
        k = pl.program_id(1)
        v = g_ref[...] + r_ref[...].astype(F32)
        pb_ref[...] = v.astype(BF16)

        @pl.when(k == cs_ref[1])
        def _():
            own_ref[...] = v

    grid_spec = pltpu.PrefetchScalarGridSpec(
        num_scalar_prefetch=1, grid=(a_rows // ta, nch),
        in_specs=[pl.BlockSpec((None, None, ta, b_cols), lambda r, k, cs_ref: (cs_ref[0], k, r, 0)),
                  pl.BlockSpec((None, ta, b_cols), lambda r, k, cs_ref: (k, r, 0))],
        out_specs=[pl.BlockSpec((None, ta, b_cols), lambda r, k, cs_ref: (k, r, 0)),
                   pl.BlockSpec((ta, b_cols), lambda r, k, cs_ref: (r, 0))])
    return _pcall(
        body, name=name, grid_spec=grid_spec,
        out_shape=[jax.ShapeDtypeStruct((nch, a_rows, b_cols), BF16), jax.ShapeDtypeStruct((a_rows, b_cols), F32)],
        dims=("arbitrary", "arbitrary"))(cs, g32, recv)


def _small_pair_sum(mine, recv, cs):
    _, r, ccols = mine.shape

    def body(cs_ref, a_ref, b_ref, o_ref):
        o_ref[...] = a_ref[...] + b_ref[...]

    grid_spec = pltpu.PrefetchScalarGridSpec(
        num_scalar_prefetch=1, grid=(1,),
        in_specs=[pl.BlockSpec((None, r, ccols), lambda i, cs_ref: (cs_ref[0], 0, 0)),
                  pl.BlockSpec((r, ccols), lambda i, cs_ref: (0, 0))],
        out_specs=pl.BlockSpec((r, ccols), lambda i, cs_ref: (0, 0)))
    return _pcall(body, name="small_pair_sum", grid_spec=grid_spec,
                  out_shape=jax.ShapeDtypeStruct((r, ccols), F32), dims=("arbitrary",))(cs, mine, recv)


def _chip_sum(name, own, recv, cs):
    a_rows, b_cols = own.shape
    ta = min(T_ELEM, a_rows)

    def body(cs_ref, o_ref, r_ref, f_ref):
        f_ref[...] = ((o_ref[...] + r_ref[0].astype(F32)) + r_ref[1].astype(F32)) + r_ref[2].astype(F32)

    grid_spec = pltpu.PrefetchScalarGridSpec(
        num_scalar_prefetch=1, grid=(a_rows // ta,),
        in_specs=[pl.BlockSpec((ta, b_cols), lambda r, cs_ref: (r, 0)),
                  pl.BlockSpec((3, ta, b_cols), lambda r, cs_ref: (0, r, 0))],
        out_specs=pl.BlockSpec((None, ta, b_cols), lambda r, cs_ref: (cs_ref[0], r, 0)))
    return _pcall(body, name=name, grid_spec=grid_spec,
                  out_shape=jax.ShapeDtypeStruct((2, a_rows, b_cols), F32), dims=("arbitrary",))(cs, own, recv)


def _small_chip_sum(pair, recv, cs):
    r, ccols = pair.shape

    def body(cs_ref, p_ref, r_ref, o_ref):
        s = cs_ref[1]
        own = p_ref[...]
        total = None
        for k in range(N_CHIPS):
            flip = jnp.bitwise_xor(s, k)
            term = jnp.where(flip == 0, own, jnp.where(flip == 2, r_ref[0], jnp.where(flip == 1, r_ref[1], r_ref[2])))
            total = term if total is None else total + term
        o_ref[...] = total

    grid_spec = pltpu.PrefetchScalarGridSpec(
        num_scalar_prefetch=1, grid=(1,),
        in_specs=[pl.BlockSpec((r, ccols), lambda i, cs_ref: (0, 0)),
                  pl.BlockSpec((3, r, ccols), lambda i, cs_ref: (0, 0, 0))],
        out_specs=pl.BlockSpec((None, r, ccols), lambda i, cs_ref: (cs_ref[0], 0, 0)))
    return _pcall(body, name="small_chip_sum", grid_spec=grid_spec,
                  out_shape=jax.ShapeDtypeStruct((2, r, ccols), F32), dims=("arbitrary",))(cs, pair, recv)


def _adamw(name, w, g, m, v, *, halves, comm=None):
    rows, cols = w.shape
    tr, tc = min(T_ELEM, rows), PACK_COLS
    c1 = 1.0 - ADAM_B1 ** ADAM_STEP
    c2 = 1.0 - ADAM_B2 ** ADAM_STEP

    def body(w_ref, g_ref, m_ref, v_ref, d_ref, mo_ref, vo_ref, go_ref):
        g_ = g_ref[...]
        m_ = ADAM_B1 * m_ref[...] + (1.0 - ADAM_B1) * g_
        v_ = ADAM_B2 * v_ref[...] + (1.0 - ADAM_B2) * (g_ * g_)
        mo_ref[...] = m_
        vo_ref[...] = v_
        go_ref[...] = g_
        d_ref[...] = (-ADAM_LR) * ((m_ / c1) / (jnp.sqrt(v_ / c2) + ADAM_EPS) + ADAM_WD * w_ref[...])

    spec = pl.BlockSpec((tr, tc), lambda r, j: (r, j))
    if halves == "rows":
        nrh = rows // 2 // tr
        g_spec = pl.BlockSpec((None, tr, tc), lambda r, j: (r // nrh, r % nrh, j))
    elif halves == "cols":
        nch = cols // 2 // tc
        g_spec = pl.BlockSpec((None, tr, tc), lambda r, j: (j // nch, r, j % nch))
    else:
        g_spec = spec
    return _pcall(body, name=name, grid=(rows // tr, cols // tc), in_specs=[spec, g_spec, spec, spec],
                  out_specs=[spec] * 4, out_shape=[jax.ShapeDtypeStruct((rows, cols), F32)] * 4,
                  dims=("arbitrary", "arbitrary"), comm=comm)(w, g, m, v)


def _pack(parts, rows):
    flat = jnp.concatenate([a.reshape(-1) for a in parts])
    return jnp.pad(flat, (0, rows * PACK_COLS - flat.shape[0])).reshape(rows, PACK_COLS)


def _unpack(buf, shapes):
    flat = buf.reshape(-1)
    out, off = [], 0
    for sh in shapes:
        size = math.prod(sh)
        out.append(flat[off:off + size].reshape(sh))
        off += size
    return out


def _pack_rows(shapes, multiple):
    total = sum(math.prod(sh) for sh in shapes)
    rows = -(-total // PACK_COLS)
    return -(-rows // multiple) * multiple


SMALL = ["norm1_g", "gm_v_g", "gm_v_b", "gm_ws", "gm_bs", "lru_conv_w", "lru_conv_b", "lru_wa", "lru_ba", "lru_wx",
         "lru_bx", "lru_lambda", "gm_out_g", "lru_out_g", "norm2_g", "ffn_conv_w", "ffn_conv_b", "final_g"]
BIG = ["w_in", "w_out", "ffn_w_up", "ffn_w_down"]
ORDER = ["norm1_g", "w_in", "gm_v_g", "gm_v_b", "gm_ws", "gm_bs", "lru_conv_w", "lru_conv_b", "lru_wa", "lru_ba",
         "lru_wx", "lru_bx", "lru_lambda", "gm_out_g", "lru_out_g", "w_out", "norm2_g", "ffn_w_up", "ffn_conv_w",
         "ffn_conv_b", "ffn_w_down", "final_g"]


def kernel(x, norm1_g, w_in, gm_v_g, gm_v_b, gm_ws, gm_bs, lru_conv_w, lru_conv_b, lru_wa, lru_ba, lru_wx, lru_bx, lru_lambda, gm_out_g, lru_out_g, w_out, norm2_g, ffn_w_up, ffn_conv_w, ffn_conv_b, ffn_w_down, final_g, loss_target, m_norm1_g, m_w_in, m_gm_v_g, m_gm_v_b, m_gm_ws, m_gm_bs, m_lru_conv_w, m_lru_conv_b, m_lru_wa, m_lru_ba, m_lru_wx, m_lru_bx, m_lru_lambda, m_gm_out_g, m_lru_out_g, m_w_out, m_norm2_g, m_ffn_w_up, m_ffn_conv_w, m_ffn_conv_b, m_ffn_w_down, m_final_g, v_norm1_g, v_w_in, v_gm_v_g, v_gm_v_b, v_gm_ws, v_gm_bs, v_lru_conv_w, v_lru_conv_b, v_lru_wa, v_lru_ba, v_lru_wx, v_lru_bx, v_lru_lambda, v_gm_out_g, v_lru_out_g, v_w_out, v_norm2_g, v_ffn_w_up, v_ffn_conv_w, v_ffn_conv_b, v_ffn_w_down, v_final_g):
    w = dict(norm1_g=norm1_g, w_in=w_in, gm_v_g=gm_v_g, gm_v_b=gm_v_b, gm_ws=gm_ws, gm_bs=gm_bs, lru_conv_w=lru_conv_w, lru_conv_b=lru_conv_b, lru_wa=lru_wa, lru_ba=lru_ba, lru_wx=lru_wx, lru_bx=lru_bx, lru_lambda=lru_lambda, gm_out_g=gm_out_g, lru_out_g=lru_out_g, w_out=w_out, norm2_g=norm2_g, ffn_w_up=ffn_w_up, ffn_conv_w=ffn_conv_w, ffn_conv_b=ffn_conv_b, ffn_w_down=ffn_w_down, final_g=final_g)
    m = dict(norm1_g=m_norm1_g, w_in=m_w_in, gm_v_g=m_gm_v_g, gm_v_b=m_gm_v_b, gm_ws=m_gm_ws, gm_bs=m_gm_bs, lru_conv_w=m_lru_conv_w, lru_conv_b=m_lru_conv_b, lru_wa=m_lru_wa, lru_ba=m_lru_ba, lru_wx=m_lru_wx, lru_bx=m_lru_bx, lru_lambda=m_lru_lambda, gm_out_g=m_gm_out_g, lru_out_g=m_lru_out_g, w_out=m_w_out, norm2_g=m_norm2_g, ffn_w_up=m_ffn_w_up, ffn_conv_w=m_ffn_conv_w, ffn_conv_b=m_ffn_conv_b, ffn_w_down=m_ffn_w_down, final_g=m_final_g)
    v = dict(norm1_g=v_norm1_g, w_in=v_w_in, gm_v_g=v_gm_v_g, gm_v_b=v_gm_v_b, gm_ws=v_gm_ws, gm_bs=v_gm_bs, lru_conv_w=v_lru_conv_w, lru_conv_b=v_lru_conv_b, lru_wa=v_lru_wa, lru_ba=v_lru_ba, lru_wx=v_lru_wx, lru_bx=v_lru_bx, lru_lambda=v_lru_lambda, gm_out_g=v_gm_out_g, lru_out_g=v_lru_out_g, w_out=v_w_out, norm2_g=v_norm2_g, ffn_w_up=v_ffn_w_up, ffn_conv_w=v_ffn_conv_w, ffn_conv_b=v_ffn_conv_b, ffn_w_down=v_ffn_w_down, final_g=v_final_g)

    mx, my, mc = _mesh_pos()
    shard = 2 * mx + my
    cs = jnp.stack([mc, shard]).astype(jnp.int32)

    xs = x[0]
    tgt = loss_target[0]
    s_len, d = xs.shape

    halves = lambda a: a.reshape((2, a.shape[0] // 2) + a.shape[1:])
    slot = {k: _to_slot("slot_" + k, halves(w[k][0]), cs, BF16) for k in BIG}
    win_b, wc_b, wfc_b = _all_gather([slot["w_in"],
                                      _to_slot("slot_lru_conv_w", w["lru_conv_w"][0].reshape(2, 4, -1), cs, F32),
                                      _to_slot("slot_ffn_conv_w", w["ffn_conv_w"][0].reshape(2, 12, -1), cs, F32)])
    win_g = win_b.reshape(N_CHIPS, d, -1)
    wc = wc_b.reshape(N_CHIPS, 4, -1).transpose(1, 0, 2).reshape(4, -1)
    wfc = wfc_b.reshape(N_CHIPS, 3, -1).transpose(1, 0, 2).reshape(3, -1)
    wout_b, wup_b, wdown_b = slot["w_out"], slot["ffn_w_up"], slot["ffn_w_down"]
    r_out, r_up, r_dn = wout_b.shape[2], wup_b.shape[2] // 2, wdown_b.shape[2] // 2

    tril = jnp.tril(jnp.ones((CHUNK, CHUNK), bool))
    wt32 = jnp.where(tril[None], w["gm_ws"][0], 0.0)
    wt = wt32.astype(BF16)
    wtt = wt32.transpose(0, 2, 1).astype(BF16)
    bst = jnp.repeat(w["gm_bs"][0].T, HEAD_DIM, axis=1)
    wa = w["lru_wa"][0].astype(BF16)
    wx = w["lru_wx"][0].astype(BF16)
    wat = w["lru_wa"][0].transpose(0, 2, 1).astype(BF16)
    wxt = w["lru_wx"][0].transpose(0, 2, 1).astype(BF16)
    ba = w["lru_ba"][0].reshape(1, -1)
    bx = w["lru_bx"][0].reshape(1, -1)
    gf = w["final_g"].reshape(1, -1)

    p, h1, wout_b, wup_b = _fwd_in_proj(
        xs, w["norm1_g"], win_g, comm=_job_gather([wout_b, wup_b], [(0, 0, r_out), (1, 0, r_up)], []))
    y, hs, wout_b, wup_b, wdown_b = _fwd_mixers(
        p, w["gm_v_g"], w["gm_v_b"], wt, bst, wc, w["lru_conv_b"], wa, ba, wx, bx, w["lru_lambda"],
        w["gm_out_g"], w["lru_out_g"],
        comm=_job_gather([wout_b, wup_b, wdown_b], [(1, r_up, r_up), (2, 0, r_dn)], [(0, 0, r_out), (1, 0, r_up)]))
    wout_g = wout_b.reshape(-1, d)
    x2, h2, wup_b, wdown_b = _fwd_out_proj(
        xs, y, wout_g, w["norm2_g"], comm=_job_gather([wup_b, wdown_b], [(1, r_dn, r_dn)], [(0, r_up, r_up), (1, 0, r_dn)]))
    wdown_b, = _comm_call("gather_tail", _job_gather([wdown_b], [], [(0, r_dn, r_dn)]))
    wup_g = wup_b.reshape(N_CHIPS, d, -1)
    wdown_g = wdown_b.reshape(-1, d)
    upb, cb, act, dx3, dx3b, loss_tile, dgf = _fwd_ffn(h2, wup_g, wfc, w["ffn_conv_b"], wdown_g, x2, gf, tgt)
    loss = lax.psum(loss_tile[0, 0], ("x", "y", "c"))

    adam = {}

    def adamw_big(name, gfull, hv, comm=None):
        r2 = lambda a: a.reshape(w[name].shape[1:])
        res = _adamw("adamw_" + name, r2(w[name]), gfull, r2(m[name]), r2(v[name]), halves=hv, comm=comm)
        adam[name] = res[:4]
        return res[4:]

    gd32, gdb = _bwd_weight("bwd_w_down", act[None], dx3b[None], a_planes=1, b_planes=1, shard_rows=True)
    dup, dwf, dx2, dx2b, dg2, rcv = _bwd_ffn(dx3b, wdown_g, upb, cb, wfc, wup_g, dx3, x2, w["norm2_g"],
                                             comm=_job_pair_swap([gdb]))
    pb_dn, own_dn = _pair_sum("pair_sum_3", gd32, rcv, cs)
    gu32, gub, got = _bwd_weight("bwd_w_up", h2[None], dup, a_planes=1, b_planes=2, shard_rows=False,
                                 comm=_job_chip_exchange([pb_dn]))
    red_dn = _chip_sum("chip_sum_3", own_dn, got, cs)
    go32, gob, rcv, red_dn = _bwd_weight("bwd_w_out", y[None], dx2b[None], a_planes=1, b_planes=1, shard_rows=True,
                                         comm=_merge_jobs([_job_pair_swap([gub]), _job_halves_swap([red_dn])]))
    pb_up, own_up = _pair_sum("pair_sum_2", gu32, rcv, cs)
    adamw_big("ffn_w_down", red_dn, "cols")
    (dp, dgv, dbv, dwt, dbst, dwc, dbc, dwa, dba, dwx, dbx, dsp, dggm, dglru, got_up, rcv) = _bwd_mixers(
        p, dx2b, wout_g, hs, w["gm_v_g"], w["gm_v_b"], wt, wtt, bst, wc, w["lru_conv_b"], wa, wat, ba, wx, wxt, bx,
        w["lru_lambda"], w["gm_out_g"], w["lru_out_g"],
        comm=_merge_jobs([_job_chip_exchange([pb_up]), _job_pair_swap([gob])]))
    pb_out, own_out = _pair_sum("pair_sum_1", go32, rcv, cs)
    red_up = _chip_sum("chip_sum_2", own_up, got_up, cs)
    gi32, gib, red_up, got_out = _bwd_weight("bwd_w_in", h1[None], dp[None], a_planes=1, b_planes=1, shard_rows=False,
                                             comm=_merge_jobs([_job_halves_swap([red_up]), _job_chip_exchange([pb_out])]))
    red_out = _chip_sum("chip_sum_1", own_out, got_out, cs)
    rcv, = _comm_call("w_in_swap", _job_pair_swap([gib]))
    pb_in, own_in = _pair_sum("pair_sum_0", gi32, rcv, cs)
    grad_x, dg1, got_in, red_out = _bwd_in(dp, win_g, dx2, xs, w["norm1_g"],
                                           comm=_merge_jobs([_job_chip_exchange([pb_in]), _job_halves_swap([red_out])]))
    red_in = _chip_sum("chip_sum_0", own_in, got_in, cs)

    dwfc = dwf[:, :, :3].transpose(2, 0, 1, 3).reshape(3, -1)
    dbfc = dwf[:, :, 3].reshape(1, -1)
    dlam = dsp * (-_sigmoid(-w["lru_lambda"]))
    small_grads = dict(
        norm1_g=dg1, gm_v_g=dgv, gm_v_b=dbv, gm_ws=jnp.where(tril[None], dwt, 0.0),
        gm_bs=dbst.reshape(CHUNK, HEADS, HEAD_DIM).sum(-1).T, lru_conv_w=dwc, lru_conv_b=dbc, lru_wa=dwa, lru_ba=dba,
        lru_wx=dwx, lru_bx=dbx, lru_lambda=dlam, gm_out_g=dggm, lru_out_g=dglru, norm2_g=dg2, ffn_conv_w=dwfc,
        ffn_conv_b=dbfc, final_g=dgf)
    full_shapes = [small_grads[k].shape for k in SMALL]
    rows_full = _pack_rows(full_shapes, 16)
    gpack = _pack([small_grads[k] for k in SMALL], rows_full).reshape(2, rows_full // 2, PACK_COLS)

    rcv, = _comm_call("small_swap", _job_pair_swap([gpack]))
    small_pair = _small_pair_sum(gpack, rcv, cs)
    got_small, = _comm_call("small_exchange", _job_chip_exchange([], [small_pair]))
    small_half = _small_chip_sum(small_pair, got_small, cs)
    red_in, small_full = _comm_call("tail_halves_swap", _job_halves_swap([red_in, small_half]))
    adamw_big("ffn_w_up", red_up, "rows")
    adamw_big("w_out", red_out, "cols")
    adamw_big("w_in", red_in, "rows")

    grads = {}
    for name, g in zip(SMALL, _unpack(small_full, full_shapes)):
        blk = w[name].shape[1:] if w[name].ndim > 1 else w[name].shape
        if name in ("lru_conv_w", "ffn_conv_w"):
            g = lax.dynamic_slice_in_dim(g, shard * blk[1], blk[1], axis=1)
        grads[name] = g.reshape(blk)

    delta, new_m, new_v = {}, {}, {}
    for name in BIG:
        delta[name], new_m[name], new_v[name], grads[name] = adam[name]
    blk_shapes = [grads[k].shape for k in SMALL]
    rows_blk = _pack_rows(blk_shapes, T_ELEM)
    packs = [_pack([src[k] for k in SMALL], rows_blk) for src in (w, grads, m, v)]
    outs = _adamw("adamw_small", *packs, halves=None)
    for dst, buf in zip((delta, new_m, new_v), outs):
        for name, a in zip(SMALL, _unpack(buf, blk_shapes)):
            dst[name] = a

    def shaped(dct):
        return [dct[k].reshape(w[k].shape) for k in ORDER]

    return (loss, grad_x[None], *shaped(grads), *shaped(delta), *shaped(new_m), *shaped(new_v))
```

```python
import functools
import math

import jax
import jax.numpy as jnp
from jax import lax
from jax.experimental import pallas as pl
from jax.experimental.pallas import tpu as pltpu

F32 = jnp.float32
BF16 = jnp.bfloat16
MESH = pl.DeviceIdType.MESH
ANY = pl.BlockSpec(memory_space=pltpu.HBM)

GM_W = 1024
LRU_W = 1024
CHUNK = 128
HEADS = 8
HEAD_DIM = 128
LRU_C = 8.0
RMS_EPS = 1e-6
LN_EPS = 1e-5
ADAM_LR = 0.001
ADAM_B1 = 0.9
ADAM_B2 = 0.999
ADAM_EPS = 1e-08
ADAM_WD = 0.01
ADAM_STEP = 10

N_CHIPS = 4
HALO = 8
PACK_COLS = 1024
VMEM_LIMIT = 56 * 1024 * 1024

TM_IN = 1024
TS_MIX = 256
TM_OUT = 512
TM_UP = 512
TN_UP = 512
T_EPI = 128
T_CHUNK = 16
TM_BW = 512
TW_M = 1024
TW_N = 1024
TW_K = 4096
T_ELEM = 256
LANES = 128


def _tile(dim, cap):
    t = min(cap, dim) // LANES * LANES
    while dim % t:
        t -= LANES
    return t

_GELU_K0 = 0.7978845608028654
_GELU_K1 = 0.044715


class _Job:
    def __init__(self, ins, out_shapes, aliases, n, copies):
        self.ins, self.out_shapes, self.aliases, self.n, self.copies = list(ins), list(out_shapes), dict(aliases), n, copies

    def make(self, cin, cout, send_sems, recv_sems):
        x, y, c = _mesh_pos()
        return [pltpu.make_async_remote_copy(src_ref=src, dst_ref=dst, send_sem=send_sems.at[k], recv_sem=recv_sems.at[k],
                                             device_id=dev, device_id_type=MESH)
                for k, (src, dst, dev) in enumerate(self.copies(cin, cout, x, y, c))]


def _merge_jobs(jobs):
    ins, outs, aliases, spans = [], [], {}, []
    for jb in jobs:
        spans.append((len(ins), len(jb.ins), len(outs), len(jb.out_shapes)))
        aliases.update({len(ins) + a: len(outs) + b for a, b in jb.aliases.items()})
        ins += jb.ins
        outs += jb.out_shapes

    def copies(cin, cout, x, y, c):
        out = []
        for jb, (i0, ni, o0, no) in zip(jobs, spans):
            out += jb.copies(cin[i0:i0 + ni], cout[o0:o0 + no], x, y, c)
        return out

    return _Job(ins, outs, aliases, sum(jb.n for jb in jobs), copies)


def _pcall(body, *, name, out_shape, grid=None, in_specs=None, out_specs=None, scratch_shapes=(),
           grid_spec=None, dims=None, aliases=None, comm=None):
    params = pltpu.CompilerParams(dimension_semantics=dims, vmem_limit_bytes=VMEM_LIMIT)
    kw = dict(name=name, compiler_params=params)
    if grid_spec is not None:
        if aliases:
            kw["input_output_aliases"] = aliases
        return pl.pallas_call(body, grid_spec=grid_spec, out_shape=out_shape, **kw)
    scratch_shapes = list(scratch_shapes)
    if comm is not None:
        out_shape = list(out_shape) if isinstance(out_shape, (list, tuple)) else [out_shape]
        out_specs = list(out_specs) if isinstance(out_specs, (list, tuple)) else [out_specs]
        n_in, n_out, n_scr = len(in_specs), len(out_shape), len(scratch_shapes)
        n_ci, n_co = len(comm.ins), len(comm.out_shapes)
        inner, steps = body, tuple(grid)

        def body(*refs):
            o0 = n_in + n_ci
            s0 = o0 + n_out + n_co
            cps = comm.make(refs[n_in:o0], refs[o0 + n_out:s0], refs[s0 + n_scr], refs[s0 + n_scr + 1])
            ids = [pl.program_id(k) for k in range(len(steps))]

            @pl.when(functools.reduce(jnp.logical_and, [i == 0 for i in ids]))
            def _():
                for cp in cps:
                    cp.start()

            inner(*refs[:n_in], *refs[o0:o0 + n_out], *refs[s0:s0 + n_scr])

            @pl.when(functools.reduce(jnp.logical_and, [i == n - 1 for i, n in zip(ids, steps)]))
            def _():
                for cp in cps:
                    cp.wait()

        in_specs = list(in_specs) + [ANY] * n_ci
        out_specs = out_specs + [ANY] * n_co
        out_shape = out_shape + comm.out_shapes
        scratch_shapes = scratch_shapes + [pltpu.SemaphoreType.DMA((comm.n,)), pltpu.SemaphoreType.DMA((comm.n,))]
        aliases = {**(aliases or {}), **{n_in + a: n_out + b for a, b in comm.aliases.items()}}
    if aliases:
        kw["input_output_aliases"] = aliases
    if grid is not None:
        kw["grid"] = grid
    call = pl.pallas_call(body, in_specs=in_specs, out_specs=out_specs, scratch_shapes=scratch_shapes,
                          out_shape=out_shape, **kw)
    if comm is None:
        return call
    return lambda *args: call(*args, *comm.ins)


def _comm_call(name, job):
    n_ci, n_co = len(job.ins), len(job.out_shapes)

    def body(*refs):
        cps = job.make(refs[:n_ci], refs[n_ci:n_ci + n_co], refs[n_ci + n_co], refs[n_ci + n_co + 1])
        for cp in cps:
            cp.start()
        for cp in cps:
            cp.wait()

    return pl.pallas_call(
        body, name=name, in_specs=[ANY] * n_ci, out_specs=[ANY] * n_co, out_shape=job.out_shapes,
        scratch_shapes=[pltpu.SemaphoreType.DMA((job.n,)), pltpu.SemaphoreType.DMA((job.n,))],
        input_output_aliases=job.aliases)(*job.ins)


def _gelu(x):
    t = jnp.tanh(_GELU_K0 * (x + _GELU_K1 * (x * x * x)))
    return 0.5 * x * (1.0 + t)


def _gelu_and_grad(x):
    x2 = x * x
    t = jnp.tanh(_GELU_K0 * (x + _GELU_K1 * (x2 * x)))
    g = 0.5 * x * (1.0 + t)
    dg = 0.5 * (1.0 + t) + 0.5 * x * (1.0 - t * t) * (_GELU_K0 * (1.0 + 3.0 * _GELU_K1 * x2))
    return g, dg


def _sigmoid(x):
    return 1.0 / (1.0 + jnp.exp(-x))


def _neg_expm1(x):
    series = -x * (1.0 + x * (0.5 + x * (1.0 / 6.0 + x * (1.0 / 24.0 + x * (1.0 / 120.0 + x * (1.0 / 720.0))))))
    return jnp.where(x > -0.1, series, 1.0 - jnp.exp(x))


def _softplus(z):
    return jnp.maximum(z, 0.0) + jnp.log(1.0 + jnp.exp(-jnp.abs(z)))


def _rowmean(x):
    return jnp.mean(x, axis=-1, keepdims=True)


def _colsum(x):
    return jnp.sum(x, axis=0, keepdims=True)


def _rms_stats(x):
    r = lax.rsqrt(_rowmean(x * x) + RMS_EPS)
    return r, x * r


def _rms_bwd(dy, n, r, g):
    dn = dy * g
    return r * (dn - n * _rowmean(dn * n)), dy * n


def _shift_prev(x, halo, d):
    cat = jnp.concatenate([halo, x], axis=0)
    return pltpu.roll(cat, d, 0)[HALO:, :]


def _prev_rows(cat):
    return cat[HALO:, :], pltpu.roll(cat, 1, 0)[HALO:, :], pltpu.roll(cat, 2, 0)[HALO:, :]


def _next_rows(cat):
    n = cat.shape[0]
    return cat[:n - HALO, :], pltpu.roll(cat, n - 1, 0)[:n - HALO, :], pltpu.roll(cat, n - 2, 0)[:n - HALO, :]


def _shift_next(x, halo, d):
    n = x.shape[0]
    cat = jnp.concatenate([x, halo], axis=0)
    return pltpu.roll(cat, n + HALO - d, 0)[:n, :]


def _dot(a, b):
    return jnp.dot(a, b, preferred_element_type=F32)


def _dot_nt(a, b):
    return lax.dot_general(a, b, (((1,), (1,)), ((), ())), preferred_element_type=F32)


def _dot_tn(a, b):
    return lax.dot_general(a, b, (((0,), (0,)), ((), ())), preferred_element_type=F32)


def _fwd_in_proj(x, g1, win_g, comm=None):
    s_len, d = x.shape
    nsh, _, ncol = win_g.shape
    tm = min(TM_IN, s_len)
    te = min(T_EPI, tm)

    def body(x_ref, g_ref, w_ref, p_ref, h_ref):
        @pl.when(pl.program_id(1) == 0)
        def _():
            g = g_ref[...]

            def chunk(q, carry):
                rows = pl.ds(pl.multiple_of(q * te, te), te)
                _, n = _rms_stats(x_ref[rows, :])
                h_ref[rows, :] = (n * g).astype(BF16)
                return carry

            lax.fori_loop(0, tm // te, chunk, 0)

        p_ref[...] = _dot(h_ref[...], w_ref[...])

    return _pcall(
        body, name="fwd_in_proj", grid=(s_len // tm, nsh),
        in_specs=[pl.BlockSpec((tm, d), lambda i, j: (i, 0)),
                  pl.BlockSpec((1, d), lambda i, j: (0, 0)),
                  pl.BlockSpec((None, d, ncol), lambda i, j: (j, 0, 0))],
        out_specs=[pl.BlockSpec((tm, ncol), lambda i, j: (i, j)),
                   pl.BlockSpec((tm, d), lambda i, j: (i, 0))],
        out_shape=[jax.ShapeDtypeStruct((s_len, nsh * ncol), F32), jax.ShapeDtypeStruct((s_len, d), BF16)],
        dims=("arbitrary", "arbitrary"), comm=comm)(x, g1, win_g)


def _gm_forward(z, gv, bv, wt_ref, bst_ref, vl_s, mix_s):
    ts = z.shape[0]
    ge = _gelu(z)
    u = ge[:, :GM_W]
    v = ge[:, GM_W:]
    vc = v - _rowmean(v)
    rs = lax.rsqrt(_rowmean(vc * vc) + LN_EPS)
    vh = vc * rs
    vl_s[...] = (vh * gv + bv).astype(BF16)
    for cc in range(ts // CHUNK):
        rows = slice(cc * CHUNK, (cc + 1) * CHUNK)
        for hh in range(HEADS):
            cols = slice(hh * HEAD_DIM, (hh + 1) * HEAD_DIM)
            mix_s[rows, cols] = _dot(wt_ref[hh], vl_s[rows, cols]) + bst_ref[:, cols]
    mixed = mix_s[...]
    return u, mixed, vh, rs, u * mixed


def _lru_gates(xl, halo, wc_ref, bc_ref, wa_ref, ba_ref, wx_ref, bx_ref, lam_ref, z_s):
    x1 = _shift_prev(xl, halo, 1)
    x2 = _shift_prev(xl, halo, 2)
    x3 = _shift_prev(xl, halo, 3)
    xr = bc_ref[...] + wc_ref[0:1, :] * x3 + wc_ref[1:2, :] * x2 + wc_ref[2:3, :] * x1 + wc_ref[3:4, :] * xl
    xrb = xr.astype(BF16)
    for hh in range(HEADS):
        cols = slice(hh * HEAD_DIM, (hh + 1) * HEAD_DIM)
        z_s[:, cols] = _dot(xrb[:, cols], wa_ref[hh])
        z_s[:, LRU_W + hh * HEAD_DIM:LRU_W + (hh + 1) * HEAD_DIM] = _dot(xrb[:, cols], wx_ref[hh])
    ra = _sigmoid(z_s[:, :LRU_W] + ba_ref[...])
    ri = _sigmoid(z_s[:, LRU_W:] + bx_ref[...])
    sp = _softplus(-lam_ref[...])
    la = (-LRU_C) * ra * sp
    a = jnp.exp(la)
    mult = jnp.sqrt(_neg_expm1(2.0 * la))
    return dict(x1=x1, x2=x2, x3=x3, xr=xr, xrb=xrb, ra=ra, ri=ri, sp=sp, a=a, mult=mult)


def _fwd_mixers(p, gv, bv, wt, bst, wc, bc, wa, ba, wx, bx, lam, ggm, glru, comm=None):
    s_len = p.shape[0]
    ts = min(TS_MIX, s_len)

    def body(pz_ref, pgl_ref, pxl_ref, gv_ref, bv_ref, wt_ref, bst_ref, wc_ref, bc_ref, wa_ref, ba_ref, wx_ref,
             bx_ref, lam_ref, ggm_ref, glru_ref, y_ref, hs_ref, tail_ref, h_ref, a_s, b_s, vl_s, mix_s, z_s):
        @pl.when(pl.program_id(0) == 0)
        def _():
            tail_ref[...] = jnp.zeros_like(tail_ref)
            h_ref[...] = jnp.zeros_like(h_ref)

        _, _, _, _, ygm = _gm_forward(pz_ref[...], gv_ref[...], bv_ref[...], wt_ref, bst_ref, vl_s, mix_s)
        _, ngm = _rms_stats(ygm)
        y_ref[:, :GM_W] = (ngm * ggm_ref[...]).astype(BF16)

        xl = pxl_ref[...]
        gts = _lru_gates(xl, tail_ref[...], wc_ref, bc_ref, wa_ref, ba_ref, wx_ref, bx_ref, lam_ref, z_s)
        tail_ref[...] = xl[ts - HALO:, :]
        a_s[...] = gts["a"]
        b_s[...] = gts["mult"] * (gts["ri"] * gts["xr"])

        def step(t, h):
            h = a_s[pl.ds(t, 1), :] * h + b_s[pl.ds(t, 1), :]
            hs_ref[pl.ds(t, 1), :] = h
            return h

        h_ref[...] = lax.fori_loop(0, ts, step, h_ref[...], unroll=8)
        yl = hs_ref[...] * _gelu(pgl_ref[...])
        _, nl = _rms_stats(yl)
        y_ref[:, GM_W:] = (nl * glru_ref[...]).astype(BF16)

    full = lambda shape: pl.BlockSpec(shape, lambda i: (0,) * len(shape))
    return _pcall(
        body, name="fwd_mixers", grid=(s_len // ts,),
        in_specs=[pl.BlockSpec((ts, 2 * GM_W), lambda i: (i, 0)),
                  pl.BlockSpec((ts, LRU_W), lambda i: (i, 2)),
                  pl.BlockSpec((ts, LRU_W), lambda i: (i, 3)),
                  full((1, GM_W)), full((1, GM_W)), full((HEADS, CHUNK, CHUNK)), full((CHUNK, GM_W)),
                  full((4, LRU_W)), full((1, LRU_W)), full((HEADS, HEAD_DIM, HEAD_DIM)), full((1, LRU_W)),
                  full((HEADS, HEAD_DIM, HEAD_DIM)), full((1, LRU_W)), full((1, LRU_W)), full((1, GM_W)),
                  full((1, LRU_W))],
        out_specs=[pl.BlockSpec((ts, GM_W + LRU_W), lambda i: (i, 0)), pl.BlockSpec((ts, LRU_W), lambda i: (i, 0))],
        out_shape=[jax.ShapeDtypeStruct((s_len, GM_W + LRU_W), BF16), jax.ShapeDtypeStruct((s_len, LRU_W), F32)],
        scratch_shapes=[pltpu.VMEM((HALO, LRU_W), F32), pltpu.VMEM((1, LRU_W), F32),
                        pltpu.VMEM((ts, LRU_W), F32), pltpu.VMEM((ts, LRU_W), F32),
                        pltpu.VMEM((ts, GM_W), BF16), pltpu.VMEM((ts, GM_W), F32), pltpu.VMEM((ts, 2 * LRU_W), F32)],
        dims=("arbitrary",), comm=comm)(p, p, p, gv, bv, wt, bst, wc, bc, wa, ba, wx, bx, lam, ggm, glru)


def _fwd_out_proj(x, y, wout_g, g2, comm=None):
    s_len, d = x.shape
    tm = min(TM_OUT, s_len)

    def body(x_ref, y_ref, w_ref, g_ref, x2_ref, h2_ref):
        x2 = x_ref[...] + _dot(y_ref[...], w_ref[...])
        x2_ref[...] = x2
        _, n = _rms_stats(x2)
        h2_ref[...] = (n * g_ref[...]).astype(BF16)

    return _pcall(
        body, name="fwd_out_proj", grid=(s_len // tm,),
        in_specs=[pl.BlockSpec((tm, d), lambda i: (i, 0)), pl.BlockSpec((tm, d), lambda i: (i, 0)),
                  pl.BlockSpec((d, d), lambda i: (0, 0)), pl.BlockSpec((1, d), lambda i: (0, 0))],
        out_specs=[pl.BlockSpec((tm, d), lambda i: (i, 0)), pl.BlockSpec((tm, d), lambda i: (i, 0))],
        out_shape=[jax.ShapeDtypeStruct((s_len, d), F32), jax.ShapeDtypeStruct((s_len, d), BF16)],
        dims=("arbitrary",), comm=comm)(x, y, wout_g, g2)


def _row_fetch(hbm_ref, buf_ref, sem, row0, rows):
    return pltpu.make_async_copy(hbm_ref.at[pl.ds(row0, rows), :], buf_ref, sem)


def _fwd_ffn(h2, wup_g, wfc, bfc, wdown_g, x2, gf, target):
    s_len, d = h2.shape
    nsh, _, ncol = wup_g.shape
    tn = TN_UP
    tn2 = 2 * tn
    f = nsh * ncol // 2
    nj = f // tn
    nps = ncol // tn
    tm = min(TM_UP, s_len)
    te = min(T_EPI, tm)
    tr = T_CHUNK

    steps = (s_len // tm) * nj

    def body(h_ref, wg_ref, wv_ref, wcg_ref, wcv_ref, bg_ref, bv_ref, g_ref, wd_hbm, x2_hbm, t_hbm,
             upb_ref, cb_ref, act_ref, dx3_ref, dx3b_ref, loss_ref, dgf_ref, tail_ref, acc_ref, x2_buf, t_buf, up_ref,
             sems, wd_buf, wd_sems):
        i, j = pl.program_id(0), pl.program_id(1)
        row0 = pl.multiple_of(i * tm, tm)
        fetches = (_row_fetch(x2_hbm, x2_buf, sems.at[0], row0, tm), _row_fetch(t_hbm, t_buf, sems.at[1], row0, tm))

        step = i * nj + j

        def wd_fetch(s):
            return _row_fetch(wd_hbm, wd_buf.at[s % 3], wd_sems.at[s % 3], pl.multiple_of((s % nj) * tn, tn), tn)

        @pl.when(step == 0)
        def _():
            wd_fetch(step).start()
            wd_fetch(step + 1).start()

        @pl.when(step + 2 < steps)
        def _():
            wd_fetch(step + 2).start()

        @pl.when(jnp.logical_and(i == 0, j == 0))
        def _():
            tail_ref[...] = jnp.zeros_like(tail_ref)
            loss_ref[...] = jnp.zeros_like(loss_ref)
            dgf_ref[...] = jnp.zeros_like(dgf_ref)

        @pl.when(j == 0)
        def _():
            acc_ref[...] = jnp.zeros_like(acc_ref)
            for cp in fetches:
                cp.start()

        h = h_ref[...]
        up_ref[:, :tn] = _dot(h, wg_ref[...])
        up_ref[:, tn:] = _dot(h, wv_ref[...])

        planes = ((wcg_ref, bg_ref), (wcv_ref, bv_ref))
        for c in range(tn // LANES):
            cols = slice(c * LANES, (c + 1) * LANES)
            pcols = [slice(pln * tn + c * LANES, pln * tn + (c + 1) * LANES) for pln in range(2)]

            def conv(pln, u, u1, u2):
                wc_ref, b_ref = planes[pln]
                return b_ref[:, cols] + wc_ref[0:1, cols] * u2 + wc_ref[1:2, cols] * u1 + wc_ref[2:3, cols] * u

            def emit(rows, us):
                cs_ = [conv(pln, *us[pln]) for pln in range(2)]
                for pln in range(2):
                    upb_ref[rows, pcols[pln]] = us[pln][0].astype(BF16)
                    cb_ref[rows, pcols[pln]] = cs_[pln].astype(BF16)
                act_ref[rows, cols] = (_gelu(cs_[0]) * cs_[1]).astype(BF16)

            first = [_prev_rows(jnp.concatenate([tail_ref[j, :, pc], up_ref[0:tr, pc]], axis=0)) for pc in pcols]
            emit(slice(0, tr), first)

            for k in range(1, tm // tr):
                r0 = k * tr
                us = [_prev_rows(up_ref[r0 - HALO:r0 + tr, pc]) for pc in pcols]
                emit(slice(r0, r0 + tr), us)
        tail_ref[j] = up_ref[tm - HALO:tm, :]
        wd_fetch(step).wait()
        acc_ref[...] += _dot(act_ref[...], wd_buf[step % 3])

        @pl.when(j == nj - 1)
        def _():
            for cp in fetches:
                cp.wait()
            g = g_ref[...]

            def chunk(k, carry):
                rows = pl.ds(pl.multiple_of(k * te, te), te)
                x3 = x2_buf[rows, :] + acc_ref[rows, :]
                r, n = _rms_stats(x3)
                err = n * g - t_buf[rows, :]
                loss_ref[...] += jnp.sum(err * err) * (0.5 / d)
                dx3, dgn = _rms_bwd(err * (1.0 / d), n, r, g)
                dgf_ref[...] += _colsum(dgn)
                dx3_ref[rows, :] = dx3
                dx3b_ref[rows, :] = dx3.astype(BF16)
                return carry

            lax.fori_loop(0, tm // te, chunk, 0)

    hbm = pl.BlockSpec(memory_space=pl.ANY)
    return _pcall(
        body, name="fwd_ffn", grid=(s_len // tm, nj),
        in_specs=[pl.BlockSpec((tm, d), lambda i, j: (i, 0)),
                  pl.BlockSpec((None, d, tn), lambda i, j: (j // nps, 0, j % nps)),
                  pl.BlockSpec((None, d, tn), lambda i, j: (nsh // 2 + j // nps, 0, j % nps)),
                  pl.BlockSpec((3, tn), lambda i, j: (0, j)), pl.BlockSpec((3, tn), lambda i, j: (0, nj + j)),
                  pl.BlockSpec((1, tn), lambda i, j: (0, j)), pl.BlockSpec((1, tn), lambda i, j: (0, nj + j)),
                  pl.BlockSpec((1, d), lambda i, j: (0, 0)), hbm, hbm, hbm],
        out_specs=[pl.BlockSpec((tm, tn2), lambda i, j: (i, j)), pl.BlockSpec((tm, tn2), lambda i, j: (i, j)),
                   pl.BlockSpec((tm, tn), lambda i, j: (i, j)),
                   pl.BlockSpec((tm, d), lambda i, j: (i, 0)), pl.BlockSpec((tm, d), lambda i, j: (i, 0)),
                   pl.BlockSpec((8, 128), lambda i, j: (0, 0)), pl.BlockSpec((1, d), lambda i, j: (0, 0))],
        out_shape=[jax.ShapeDtypeStruct((s_len, 2 * f), BF16), jax.ShapeDtypeStruct((s_len, 2 * f), BF16),
                   jax.ShapeDtypeStruct((s_len, f), BF16),
                   jax.ShapeDtypeStruct((s_len, d), F32), jax.ShapeDtypeStruct((s_len, d), BF16),
                   jax.ShapeDtypeStruct((8, 128), F32), jax.ShapeDtypeStruct((1, d), F32)],
        scratch_shapes=[pltpu.VMEM((nj, HALO, tn2), F32), pltpu.VMEM((tm, d), F32), pltpu.VMEM((tm, d), F32),
                        pltpu.VMEM((tm, d), F32), pltpu.VMEM((tm, tn2), F32), pltpu.SemaphoreType.DMA((2,)),
                        pltpu.VMEM((3, tn, d), BF16), pltpu.SemaphoreType.DMA((3,))],
        dims=("arbitrary", "arbitrary"))(h2, wup_g, wup_g, wfc, wfc, bfc, bfc, gf, wdown_g, x2, target)


def _bwd_ffn(dx3b, wdown_g, upb, cb, wfc, wup_g, dx3, x2, g2, comm=None):
    s_len, d = dx3b.shape
    nsh, _, ncol = wup_g.shape
    tn = TN_UP
    tn2 = 2 * tn
    f = nsh * ncol // 2
    nj = f // tn
    nps = ncol // tn
    tm = min(TM_UP, s_len)
    te = min(T_EPI, tm)
    nt = s_len // tm
    tr = T_CHUNK

    def body(dx_ref, w_ref, upb_ref, cb_ref, wcg_ref, wcv_ref, wug_ref, wuv_ref, g_ref, r_hbm, x_hbm,
             dup_ref, dwf_ref, dx2_ref, dx2b_ref, dg_ref, nxt_ref, acc_ref, r_buf, x_buf, da_s, dc_s, sems):
        i, j = pl.program_id(0), pl.program_id(1)
        ti = nt - 1 - i
        row0 = pl.multiple_of(ti * tm, tm)
        fetches = (_row_fetch(r_hbm, r_buf, sems.at[0], row0, tm), _row_fetch(x_hbm, x_buf, sems.at[1], row0, tm))

        @pl.when(jnp.logical_and(i == 0, j == 0))
        def _():
            dwf_ref[...] = jnp.zeros_like(dwf_ref)
            nxt_ref[...] = jnp.zeros_like(nxt_ref)
            dg_ref[...] = jnp.zeros_like(dg_ref)

        @pl.when(j == 0)
        def _():
            acc_ref[...] = jnp.zeros_like(acc_ref)
            for cp in fetches:
                cp.start()

        da_s[...] = _dot_nt(dx_ref[...], w_ref[...])
        planes = ((wcg_ref,), (wcv_ref,))
        for pln in range(2):
            dc_s[pln, tm:tm + HALO, :] = nxt_ref[pln, j]

        for c in range(tn // LANES):
            cols = slice(c * LANES, (c + 1) * LANES)
            pcols = [slice(pln * tn + c * LANES, pln * tn + (c + 1) * LANES) for pln in range(2)]
            for k in range(tm // tr):
                rows = slice(k * tr, (k + 1) * tr)
                dact = da_s[rows, cols]
                ge, gd = _gelu_and_grad(cb_ref[rows, pcols[0]].astype(F32))
                dc_s[0, rows, cols] = dact * cb_ref[rows, pcols[1]].astype(F32) * gd
                dc_s[1, rows, cols] = dact * ge

        for pln in range(2):
            nxt_ref[pln, j] = dc_s[pln, 0:HALO, :]

        def fold(v):
            out = v[0:8, :]
            for q in range(1, tr // 8):
                out = out + v[8 * q:8 * q + 8, :]
            return out

        for c in range(tn // LANES):
            cols = slice(c * LANES, (c + 1) * LANES)
            for pln in range(2):
                wc_ref = planes[pln][0]
                pc = slice(pln * tn + c * LANES, pln * tn + (c + 1) * LANES)
                sums = (jnp.zeros((8, LANES), F32),) * 4
                for k in range(tm // tr):
                    r0 = k * tr
                    dc, dc1, dc2 = _next_rows(dc_s[pln, r0:r0 + tr + HALO, cols])
                    dup = wc_ref[2:3, cols] * dc + wc_ref[1:2, cols] * dc1 + wc_ref[0:1, cols] * dc2
                    dup_ref[pln, r0:r0 + tr, cols] = dup.astype(BF16)
                    u = upb_ref[r0:r0 + tr, pc].astype(F32)
                    new = (fold(dc2 * u), fold(dc1 * u), fold(dc * u), fold(dc))
                    sums = tuple(a + b for a, b in zip(sums, new))
                for term in range(4):
                    dwf_ref[pln, j, term:term + 1, cols] += _colsum(sums[term])
        acc_ref[...] += _dot_nt(dup_ref[0], wug_ref[...]) + _dot_nt(dup_ref[1], wuv_ref[...])

        @pl.when(j == nj - 1)
        def _():
            for cp in fetches:
                cp.wait()
            g = g_ref[...]

            def chunk(k, carry):
                rows = pl.ds(pl.multiple_of(k * te, te), te)
                r, n = _rms_stats(x_buf[rows, :])
                dxn, dgn = _rms_bwd(acc_ref[rows, :], n, r, g)
                dg_ref[...] += _colsum(dgn)
                dx = r_buf[rows, :] + dxn
                dx2_ref[rows, :] = dx
                dx2b_ref[rows, :] = dx.astype(BF16)
                return carry

            lax.fori_loop(0, tm // te, chunk, 0)

    hbm = pl.BlockSpec(memory_space=pl.ANY)
    rev = lambda i: nt - 1 - i
    return _pcall(
        body, name="bwd_ffn", grid=(nt, nj),
        in_specs=[pl.BlockSpec((tm, d), lambda i, j: (rev(i), 0)),
                  pl.BlockSpec((tn, d), lambda i, j: (j, 0)),
                  pl.BlockSpec((tm, tn2), lambda i, j: (rev(i), j)),
                  pl.BlockSpec((tm, tn2), lambda i, j: (rev(i), j)),
                  pl.BlockSpec((3, tn), lambda i, j: (0, j)), pl.BlockSpec((3, tn), lambda i, j: (0, nj + j)),
                  pl.BlockSpec((None, d, tn), lambda i, j: (j // nps, 0, j % nps)),
                  pl.BlockSpec((None, d, tn), lambda i, j: (nsh // 2 + j // nps, 0, j % nps)),
                  pl.BlockSpec((1, d), lambda i, j: (0, 0)), hbm, hbm],
        out_specs=[pl.BlockSpec((2, tm, tn), lambda i, j: (0, rev(i), j)),
                   pl.BlockSpec((2, nj, 8, tn), lambda i, j: (0, 0, 0, 0)),
                   pl.BlockSpec((tm, d), lambda i, j: (rev(i), 0)), pl.BlockSpec((tm, d), lambda i, j: (rev(i), 0)),
                   pl.BlockSpec((1, d), lambda i, j: (0, 0))],
        out_shape=[jax.ShapeDtypeStruct((2, s_len, f), BF16), jax.ShapeDtypeStruct((2, nj, 8, tn), F32),
                   jax.ShapeDtypeStruct((s_len, d), F32), jax.ShapeDtypeStruct((s_len, d), BF16),
                   jax.ShapeDtypeStruct((1, d), F32)],
        scratch_shapes=[pltpu.VMEM((2, nj, HALO, tn), F32), pltpu.VMEM((tm, d), F32), pltpu.VMEM((tm, d), F32),
                        pltpu.VMEM((tm, d), F32), pltpu.VMEM((tm, tn), F32), pltpu.VMEM((2, tm + HALO, tn), F32),
                        pltpu.SemaphoreType.DMA((2,))],
        dims=("arbitrary", "arbitrary"), comm=comm)(dx3b, wdown_g, upb, cb, wfc, wfc, wup_g, wup_g, g2, dx3, x2)


def _bwd_in(dp, win_g, resid, x_in, g, comm=None):
    s_len, d = x_in.shape
    nsh, _, ncol = win_g.shape
    tm = min(TM_BW, s_len)
    te = min(T_EPI, tm)

    def body(dz_ref, w_ref, r_ref, x_ref, g_ref, dx_ref, dg_ref, acc_ref):
        i, k = pl.program_id(0), pl.program_id(1)

        @pl.when(jnp.logical_and(i == 0, k == 0))
        def _():
            dg_ref[...] = jnp.zeros_like(dg_ref)

        @pl.when(k == 0)
        def _():
            acc_ref[...] = jnp.zeros_like(acc_ref)

        acc_ref[...] += _dot_nt(dz_ref[...], w_ref[...])

        @pl.when(k == nsh - 1)
        def _():
            g = g_ref[...]

            def chunk(q, carry):
                rows = pl.ds(pl.multiple_of(q * te, te), te)
                r, n = _rms_stats(x_ref[rows, :])
                dxn, dgn = _rms_bwd(acc_ref[rows, :], n, r, g)
                dg_ref[...] += _colsum(dgn)
                dx_ref[rows, :] = r_ref[rows, :] + dxn
                return carry

            lax.fori_loop(0, tm // te, chunk, 0)

    return _pcall(
        body, name="bwd_in", grid=(s_len // tm, nsh),
        in_specs=[pl.BlockSpec((tm, ncol), lambda i, k: (i, k)),
                  pl.BlockSpec((None, d, ncol), lambda i, k: (k, 0, 0)),
                  pl.BlockSpec((tm, d), lambda i, k: (i, 0)), pl.BlockSpec((tm, d), lambda i, k: (i, 0)),
                  pl.BlockSpec((1, d), lambda i, k: (0, 0))],
        out_specs=[pl.BlockSpec((tm, d), lambda i, k: (i, 0)), pl.BlockSpec((1, d), lambda i, k: (0, 0))],
        out_shape=[jax.ShapeDtypeStruct((s_len, d), F32), jax.ShapeDtypeStruct((1, d), F32)],
        scratch_shapes=[pltpu.VMEM((tm, d), F32)],
        dims=("arbitrary", "arbitrary"), comm=comm)(dp, win_g, resid, x_in, g)


def _bwd_mixers(p, dx2b, wout_g, hs, gv, bv, wt, wtt, bst, wc, bc, wa, wat, ba, wx, wxt, bx, lam, ggm, glru, comm=None):
    s_len = p.shape[0]
    ts = min(TS_MIX, s_len)
    nt = s_len // ts
    hb = ts // HALO

    def body(pz_ref, pgl_ref, pxl_ref, xh_ref, dx_ref, wo_ref, hs_ref, hh_ref, gv_ref, bv_ref, wt_ref, wtt_ref, bst_ref,
             wc_ref, bc_ref, wa_ref, wat_ref, ba_ref, wx_ref, wxt_ref, bx_ref, lam_ref, ggm_ref, glru_ref,
             dp_ref, dgv_ref, dbv_ref, dwt_ref, dbst_ref, dwc_ref, dbc_ref, dwa_ref, dba_ref, dwx_ref, dbx_ref,
             dsp_ref, dggm_ref, dglru_ref,
             carry_ref, nxt_ref, a_s, g_s, vl_s, mix_s, z_s, dm_s, dvl_s, dxr_s, dy_ref):
        i = pl.program_id(0)
        ti = nt - 1 - i

        @pl.when(i == 0)
        def _():
            for ref in (dgv_ref, dbv_ref, dwt_ref, dbst_ref, dwc_ref, dbc_ref, dwa_ref, dba_ref, dwx_ref, dbx_ref,
                        dsp_ref, dggm_ref, dglru_ref, carry_ref, nxt_ref):
                ref[...] = jnp.zeros_like(ref)

        dy_ref[...] = _dot_nt(dx_ref[...], wo_ref[...])

        z = pz_ref[...]
        u, mixed, vh, rs, ygm = _gm_forward(z, gv_ref[...], bv_ref[...], wt_ref, bst_ref, vl_s, mix_s)
        rg, ngm = _rms_stats(ygm)
        dygm, dgn = _rms_bwd(dy_ref[:, :GM_W], ngm, rg, ggm_ref[...])
        dggm_ref[...] += _colsum(dgn)
        du = dygm * mixed
        dmix = dygm * u
        dm_s[...] = dmix.astype(BF16)
        bsum = dmix[0:CHUNK, :]
        for cc in range(1, ts // CHUNK):
            bsum = bsum + dmix[cc * CHUNK:(cc + 1) * CHUNK, :]
        dbst_ref[...] += bsum
        for hh in range(HEADS):
            cols = slice(hh * HEAD_DIM, (hh + 1) * HEAD_DIM)
            dw = jnp.zeros((CHUNK, CHUNK), F32)
            for cc in range(ts // CHUNK):
                rows = slice(cc * CHUNK, (cc + 1) * CHUNK)
                dmb = dm_s[rows, cols]
                dw = dw + _dot_nt(dmb, vl_s[rows, cols])
                dvl_s[rows, cols] = _dot(wtt_ref[hh], dmb)
            dwt_ref[hh] += dw
        dvl = dvl_s[...]
        dgv_ref[...] += _colsum(dvl * vh)
        dbv_ref[...] += _colsum(dvl)
        dvh = dvl * gv_ref[...]
        dv = rs * (dvh - _rowmean(dvh) - vh * _rowmean(dvh * vh))
        _, gd = _gelu_and_grad(z)
        dp_ref[:, :GM_W] = (du * gd[:, :GM_W]).astype(BF16)
        dp_ref[:, GM_W:2 * GM_W] = (dv * gd[:, GM_W:]).astype(BF16)

        xl = pxl_ref[...]
        xhalo = jnp.where(ti == 0, 0.0, xh_ref[...])
        gts = _lru_gates(xl, xhalo, wc_ref, bc_ref, wa_ref, ba_ref, wx_ref, bx_ref, lam_ref, z_s)
        a, mult, ra, ri, xr, sp = gts["a"], gts["mult"], gts["ra"], gts["ri"], gts["xr"], gts["sp"]
        hs = hs_ref[...]
        hprev = _shift_prev(hs, jnp.where(ti == 0, 0.0, hh_ref[...]), 1)
        gl = pgl_ref[...]
        ggl, dggl = _gelu_and_grad(gl)
        yl = hs * ggl
        rl, nl = _rms_stats(yl)
        dyl, dgn = _rms_bwd(dy_ref[:, GM_W:], nl, rl, glru_ref[...])
        dglru_ref[...] += _colsum(dgn)
        dp_ref[:, 2 * GM_W:2 * GM_W + LRU_W] = (dyl * hs * dggl).astype(BF16)
        a_s[...] = a
        g_s[...] = dyl * ggl

        def step(k, carry):
            t = ts - 1 - k
            gt = g_s[pl.ds(t, 1), :] + carry
            g_s[pl.ds(t, 1), :] = gt
            return a_s[pl.ds(t, 1), :] * gt

        carry_ref[...] = lax.fori_loop(0, ts, step, carry_ref[...], unroll=8)
        gsc = g_s[...]
        da = gsc * hprev
        rix = ri * xr
        dmult = gsc * rix
        dri = gsc * mult * xr
        dxr = gsc * mult * ri
        dla = da * a - dmult * (a * a) / mult
        dsp_ref[...] += _colsum(dla * ra) * (-LRU_C)
        dza = (dla * sp) * (-LRU_C) * ra * (1.0 - ra)
        dzi = dri * ri * (1.0 - ri)
        dba_ref[...] += _colsum(dza)
        dbx_ref[...] += _colsum(dzi)
        dzab = dza.astype(BF16)
        dzib = dzi.astype(BF16)
        xrb = gts["xrb"]
        for hh in range(HEADS):
            cols = slice(hh * HEAD_DIM, (hh + 1) * HEAD_DIM)
            dwa_ref[hh] += _dot_tn(xrb[:, cols], dzab[:, cols])
            dwx_ref[hh] += _dot_tn(xrb[:, cols], dzib[:, cols])
            dxr_s[:, cols] = _dot(dzab[:, cols], wat_ref[hh]) + _dot(dzib[:, cols], wxt_ref[hh])
        dxr = dxr + dxr_s[...]
        dbc_ref[...] += _colsum(dxr)
        dwc_ref[0:1, :] += _colsum(dxr * gts["x3"])
        dwc_ref[1:2, :] += _colsum(dxr * gts["x2"])
        dwc_ref[2:3, :] += _colsum(dxr * gts["x1"])
        dwc_ref[3:4, :] += _colsum(dxr * xl)
        nxt = nxt_ref[...]
        nxt_ref[...] = dxr[:HALO, :]
        dxl = wc_ref[3:4, :] * dxr + wc_ref[2:3, :] * _shift_next(dxr, nxt, 1) \
            + wc_ref[1:2, :] * _shift_next(dxr, nxt, 2) + wc_ref[0:1, :] * _shift_next(dxr, nxt, 3)
        dp_ref[:, 2 * GM_W + LRU_W:] = dxl.astype(BF16)

    full = lambda shape: pl.BlockSpec(shape, lambda i: (0,) * len(shape))
    rev = lambda i: nt - 1 - i
    prev_blk = lambda i: jnp.maximum((nt - 1 - i) * hb - 1, 0)
    hhd = (HEADS, HEAD_DIM, HEAD_DIM)
    small_shapes = [(1, GM_W), (1, GM_W), (HEADS, CHUNK, CHUNK), (CHUNK, GM_W), (4, LRU_W), (1, LRU_W), hhd,
                    (1, LRU_W), hhd, (1, LRU_W), (1, LRU_W), (1, GM_W), (1, LRU_W)]
    return _pcall(
        body, name="bwd_mixers", grid=(nt,),
        in_specs=[pl.BlockSpec((ts, 2 * GM_W), lambda i: (rev(i), 0)),
                  pl.BlockSpec((ts, LRU_W), lambda i: (rev(i), 2)),
                  pl.BlockSpec((ts, LRU_W), lambda i: (rev(i), 3)),
                  pl.BlockSpec((HALO, LRU_W), lambda i: (prev_blk(i), 3)),
                  pl.BlockSpec((ts, GM_W + LRU_W), lambda i: (rev(i), 0)),
                  full((GM_W + LRU_W, GM_W + LRU_W)),
                  pl.BlockSpec((ts, LRU_W), lambda i: (rev(i), 0)),
                  pl.BlockSpec((HALO, LRU_W), lambda i: (prev_blk(i), 0)),
                  full((1, GM_W)), full((1, GM_W)), full((HEADS, CHUNK, CHUNK)), full((HEADS, CHUNK, CHUNK)),
                  full((CHUNK, GM_W)), full((4, LRU_W)), full((1, LRU_W)), full(hhd), full(hhd), full((1, LRU_W)),
                  full(hhd), full(hhd), full((1, LRU_W)), full((1, LRU_W)), full((1, GM_W)), full((1, LRU_W))],
        out_specs=[pl.BlockSpec((ts, 2 * GM_W + 2 * LRU_W), lambda i: (rev(i), 0))] + [full(s) for s in small_shapes],
        out_shape=[jax.ShapeDtypeStruct((s_len, 2 * GM_W + 2 * LRU_W), BF16)]
        + [jax.ShapeDtypeStruct(s, F32) for s in small_shapes],
        scratch_shapes=[pltpu.VMEM((1, LRU_W), F32), pltpu.VMEM((HALO, LRU_W), F32),
                        pltpu.VMEM((ts, LRU_W), F32), pltpu.VMEM((ts, LRU_W), F32),
                        pltpu.VMEM((ts, GM_W), BF16), pltpu.VMEM((ts, GM_W), F32), pltpu.VMEM((ts, 2 * LRU_W), F32),
                        pltpu.VMEM((ts, GM_W), BF16), pltpu.VMEM((ts, GM_W), F32), pltpu.VMEM((ts, LRU_W), F32),
                        pltpu.VMEM((ts, GM_W + LRU_W), F32)],
        dims=("arbitrary",), comm=comm)(p, p, p, p, dx2b, wout_g, hs, hs, gv, bv, wt, wtt, bst, wc, bc, wa, wat, ba, wx, wxt,
                                        bx, lam, ggm, glru)


def _bwd_weight(name, a, b, *, a_planes, b_planes, shard_rows, comm=None):
    _, s_len, ma = a.shape
    _, _, nb = b.shape
    m, n = a_planes * ma, b_planes * nb
    tk = min(TW_K, s_len)
    if shard_rows:
        rows, cols = m // N_CHIPS, n // 2
        tm, tn = _tile(rows, TW_M), _tile(cols, TW_N)
        out_idx = lambda i, j, k: (j * tn // cols, i * tm // rows, (i * tm % rows) // tm, (j * tn % cols) // tn)
    else:
        rows, cols = m // 2, n // N_CHIPS
        tm, tn = _tile(rows, TW_M), _tile(cols, TW_N)
        out_idx = lambda i, j, k: (i * tm // rows, j * tn // cols, (i * tm % rows) // tm, (j * tn % cols) // tn)
    nk = s_len // tk
    npa, npb = ma // tm, nb // tn

    def body(a_ref, b_ref, o_ref, ob_ref, acc_ref):
        k = pl.program_id(2)

        @pl.when(k == 0)
        def _():
            acc_ref[...] = jnp.zeros_like(acc_ref)

        acc_ref[...] += _dot_tn(a_ref[...], b_ref[...])

        @pl.when(k == nk - 1)
        def _():
            o_ref[...] = acc_ref[...]
            ob_ref[...] = acc_ref[...].astype(BF16)

    shape = (2, N_CHIPS, rows, cols)
    return _pcall(
        body, name=name, grid=(m // tm, n // tn, nk),
        in_specs=[pl.BlockSpec((None, tk, tm), lambda i, j, k: (i // npa, k, i % npa)),
                  pl.BlockSpec((None, tk, tn), lambda i, j, k: (j // npb, k, j % npb))],
        out_specs=[pl.BlockSpec((None, None, tm, tn), out_idx), pl.BlockSpec((None, None, tm, tn), out_idx)],
        out_shape=[jax.ShapeDtypeStruct(shape, F32), jax.ShapeDtypeStruct(shape, BF16)],
        scratch_shapes=[pltpu.VMEM((tm, tn), F32)],
        dims=("arbitrary", "arbitrary", "arbitrary"), comm=comm)(a, b)


def _mesh_pos():
    return lax.axis_index("x"), lax.axis_index("y"), lax.axis_index("c")


def _other_chips(x, y):
    return [(1 - x, y), (x, 1 - y), (1 - x, 1 - y)]


def _to_slot(name, a, cs, dtype):
    _, a_rows, b_cols = a.shape
    ta = min(T_ELEM, a_rows)

    def body(cs_ref, a_ref, o_ref):
        o_ref[...] = a_ref[...].astype(dtype)

    grid_spec = pltpu.PrefetchScalarGridSpec(
        num_scalar_prefetch=1, grid=(2, a_rows // ta),
        in_specs=[pl.BlockSpec((None, ta, b_cols), lambda h, r, cs_ref: (h, r, 0))],
        out_specs=pl.BlockSpec((None, None, ta, b_cols), lambda h, r, cs_ref: (cs_ref[1], h, r, 0)))
    return _pcall(body, name=name, grid_spec=grid_spec,
                  out_shape=jax.ShapeDtypeStruct((N_CHIPS,) + a.shape, dtype), dims=("arbitrary", "arbitrary"))(cs, a)


def _all_gather(bufs):
    n = len(bufs)

    def body(*refs):
        outs = refs[n:2 * n]
        send_sems, recv_sems = refs[2 * n:]
        x, y, c = _mesh_pos()
        s = 2 * x + y
        me, sib = (x, y, c), (x, y, 1 - c)
        chips = _other_chips(x, y)

        def rcopy(a, k, blk, to):
            return pltpu.make_async_remote_copy(src_ref=blk, dst_ref=blk, send_sem=send_sems.at[a * 6 + k],
                                                recv_sem=recv_sems.at[a * 6 + k], device_id=to, device_id_type=MESH)

        first = [rcopy(a, j, outs[a].at[s, c], (cx, cy, c)) for a in range(n) for j, (cx, cy) in enumerate(chips)]
        for cp in first:
            cp.start()
        passed = []
        for a in range(n):
            for j, (cx, cy) in enumerate(chips):
                blk = outs[a].at[2 * cx + cy, c]
                rcopy(a, j, blk, me).wait_recv()
                cp = rcopy(a, 3 + j, blk, sib)
                cp.start()
                passed.append(cp)
        for a in range(n):
            for j, (cx, cy) in enumerate(chips):
                rcopy(a, 3 + j, outs[a].at[2 * cx + cy, 1 - c], me).wait_recv()
        for cp in first + passed:
            cp.wait_send()

    return _pcall(
        body, name="all_gather_weights",
        in_specs=[ANY] * n, out_specs=[ANY] * n,
        out_shape=[jax.ShapeDtypeStruct(a.shape, a.dtype) for a in bufs],
        scratch_shapes=[pltpu.SemaphoreType.DMA((6 * n,)), pltpu.SemaphoreType.DMA((6 * n,))],
        aliases={a: a for a in range(n)})(*bufs)


def _same(a):
    return jax.ShapeDtypeStruct(a.shape, a.dtype)


def _job_gather(bufs, ici_parts, relay_parts):
    def copies(cin, cout, x, y, c):
        out = []
        for a, lo, n in ici_parts:
            blk = cout[a].at[2 * x + y, c, pl.ds(lo, n)]
            out += [(blk, blk, (cx, cy, c)) for cx, cy in _other_chips(x, y)]
        for a, lo, n in relay_parts:
            for cx, cy in _other_chips(x, y):
                blk = cout[a].at[2 * cx + cy, c, pl.ds(lo, n)]
                out.append((blk, blk, (x, y, 1 - c)))
        return out

    return _Job(bufs, [_same(b) for b in bufs], {a: a for a in range(len(bufs))},
                3 * (len(ici_parts) + len(relay_parts)), copies)


def _job_pair_swap(arrs):
    def copies(cin, cout, x, y, c):
        return [(cin[a].at[1 - c], cout[a], (x, y, 1 - c)) for a in range(len(arrs))]

    return _Job(arrs, [jax.ShapeDtypeStruct(a.shape[1:], a.dtype) for a in arrs], {}, len(arrs), copies)


def _job_chip_exchange(big, small=()):
    nb = len(big)
    arrs = list(big) + list(small)

    def copies(cin, cout, x, y, c):
        out = []
        for a in range(len(arrs)):
            for j, (cx, cy) in enumerate(_other_chips(x, y)):
                out.append((cin[a].at[2 * cx + cy] if a < nb else cin[a], cout[a].at[j], (cx, cy, c)))
        return out

    shapes = [a.shape[1:] for a in big] + [a.shape for a in small]
    return _Job(arrs, [jax.ShapeDtypeStruct((3,) + sh, a.dtype) for sh, a in zip(shapes, arrs)], {}, 3 * len(arrs), copies)


def _job_halves_swap(bufs):
    def copies(cin, cout, x, y, c):
        return [(cout[a].at[c], cout[a].at[c], (x, y, 1 - c)) for a in range(len(bufs))]

    return _Job(bufs, [_same(b) for b in bufs], {a: a for a in range(len(bufs))}, len(bufs), copies)


def _pair_sum(name, g32, recv, cs):
    _, nch, a_rows, b_cols = g32.shape
    ta = min(T_ELEM, a_rows)

    def body(cs_ref, g_ref, r_ref, pb_ref, own_ref):
        k = pl.program_id(1)
        v = g_ref[...] + r_ref[...].astype(F32)
        pb_ref[...] = v.astype(BF16)

        @pl.when(k == cs_ref[1])
        def _():
            own_ref[...] = v

    grid_spec = pltpu.PrefetchScalarGridSpec(
        num_scalar_prefetch=1, grid=(a_rows // ta, nch),
        in_specs=[pl.BlockSpec((None, None, ta, b_cols), lambda r, k, cs_ref: (cs_ref[0], k, r, 0)),
                  pl.BlockSpec((None, ta, b_cols), lambda r, k, cs_ref: (k, r, 0))],
        out_specs=[pl.BlockSpec((None, ta, b_cols), lambda r, k, cs_ref: (k, r, 0)),
                   pl.BlockSpec((ta, b_cols), lambda r, k, cs_ref: (r, 0))])
    return _pcall(
        body, name=name, grid_spec=grid_spec,
        out_shape=[jax.ShapeDtypeStruct((nch, a_rows, b_cols), BF16), jax.ShapeDtypeStruct((a_rows, b_cols), F32)],
        dims=("arbitrary", "arbitrary"))(cs, g32, recv)


def _small_pair_sum(mine, recv, cs):
    _, r, ccols = mine.shape

    def body(cs_ref, a_ref, b_ref, o_ref):
        o_ref[...] = a_ref[...] + b_ref[...]

    grid_spec = pltpu.PrefetchScalarGridSpec(
        num_scalar_prefetch=1, grid=(1,),
        in_specs=[pl.BlockSpec((None, r, ccols), lambda i, cs_ref: (cs_ref[0], 0, 0)),
                  pl.BlockSpec((r, ccols), lambda i, cs_ref: (0, 0))],
        out_specs=pl.BlockSpec((r, ccols), lambda i, cs_ref: (0, 0)))
    return _pcall(body, name="small_pair_sum", grid_spec=grid_spec,
                  out_shape=jax.ShapeDtypeStruct((r, ccols), F32), dims=("arbitrary",))(cs, mine, recv)


def _chip_sum(name, own, recv, cs):
    a_rows, b_cols = own.shape
    ta = min(T_ELEM, a_rows)

    def body(cs_ref, o_ref, r_ref, f_ref):
        f_ref[...] = ((o_ref[...] + r_ref[0].astype(F32)) + r_ref[1].astype(F32)) + r_ref[2].astype(F32)

    grid_spec = pltpu.PrefetchScalarGridSpec(
        num_scalar_prefetch=1, grid=(a_rows // ta,),
        in_specs=[pl.BlockSpec((ta, b_cols), lambda r, cs_ref: (r, 0)),
                  pl.BlockSpec((3, ta, b_cols), lambda r, cs_ref: (0, r, 0))],
        out_specs=pl.BlockSpec((None, ta, b_cols), lambda r, cs_ref: (cs_ref[0], r, 0)))
    return _pcall(body, name=name, grid_spec=grid_spec,
                  out_shape=jax.ShapeDtypeStruct((2, a_rows, b_cols), F32), dims=("arbitrary",))(cs, own, recv)


def _small_chip_sum(pair, recv, cs):
    r, ccols = pair.shape

    def body(cs_ref, p_ref, r_ref, o_ref):
        s = cs_ref[1]
        own = p_ref[...]
        total = None
        for k in range(N_CHIPS):
            flip = jnp.bitwise_xor(s, k)
            term = jnp.where(flip == 0, own, jnp.where(flip == 2, r_ref[0], jnp.where(flip == 1, r_ref[1], r_ref[2])))
            total = term if total is None else total + term
        o_ref[...] = total

    grid_spec = pltpu.PrefetchScalarGridSpec(
        num_scalar_prefetch=1, grid=(1,),
        in_specs=[pl.BlockSpec((r, ccols), lambda i, cs_ref: (0, 0)),
                  pl.BlockSpec((3, r, ccols), lambda i, cs_ref: (0, 0, 0))],
        out_specs=pl.BlockSpec((None, r, ccols), lambda i, cs_ref: (cs_ref[0], 0, 0)))
    return _pcall(body, name="small_chip_sum", grid_spec=grid_spec,
                  out_shape=jax.ShapeDtypeStruct((2, r, ccols), F32), dims=("arbitrary",))(cs, pair, recv)


def _adamw(name, w, g, m, v, *, halves, comm=None):
    rows, cols = w.shape
    tr, tc = min(T_ELEM, rows), PACK_COLS
    c1 = 1.0 - ADAM_B1 ** ADAM_STEP
    c2 = 1.0 - ADAM_B2 ** ADAM_STEP

    def body(w_ref, g_ref, m_ref, v_ref, d_ref, mo_ref, vo_ref, go_ref):
        g_ = g_ref[...]
        m_ = ADAM_B1 * m_ref[...] + (1.0 - ADAM_B1) * g_
        v_ = ADAM_B2 * v_ref[...] + (1.0 - ADAM_B2) * (g_ * g_)
        mo_ref[...] = m_
        vo_ref[...] = v_
        go_ref[...] = g_
        d_ref[...] = (-ADAM_LR) * ((m_ / c1) / (jnp.sqrt(v_ / c2) + ADAM_EPS) + ADAM_WD * w_ref[...])

    spec = pl.BlockSpec((tr, tc), lambda r, j: (r, j))
    if halves == "rows":
        nrh = rows // 2 // tr
        g_spec = pl.BlockSpec((None, tr, tc), lambda r, j: (r // nrh, r % nrh, j))
    elif halves == "cols":
        nch = cols // 2 // tc
        g_spec = pl.BlockSpec((None, tr, tc), lambda r, j: (j // nch, r, j % nch))
    else:
        g_spec = spec
    return _pcall(body, name=name, grid=(rows // tr, cols // tc), in_specs=[spec, g_spec, spec, spec],
                  out_specs=[spec] * 4, out_shape=[jax.ShapeDtypeStruct((rows, cols), F32)] * 4,
                  dims=("arbitrary", "arbitrary"), comm=comm)(w, g, m, v)


def _pack(parts, rows):
    flat = jnp.concatenate([a.reshape(-1) for a in parts])
    return jnp.pad(flat, (0, rows * PACK_COLS - flat.shape[0])).reshape(rows, PACK_COLS)


def _unpack(buf, shapes):
    flat = buf.reshape(-1)
    out, off = [], 0
    for sh in shapes:
        size = math.prod(sh)
        out.append(flat[off:off + size].reshape(sh))
        off += size
    return out


def _pack_rows(shapes, multiple):
    total = sum(math.prod(sh) for sh in shapes)
    rows = -(-total // PACK_COLS)
    return -(-rows // multiple) * multiple


SMALL = ["norm1_g", "gm_v_g", "gm_v_b", "gm_ws", "gm_bs", "lru_conv_w", "lru_conv_b", "lru_wa", "lru_ba", "lru_wx",
         "lru_bx", "lru_lambda", "gm_out_g", "lru_out_g", "norm2_g", "ffn_conv_w", "ffn_conv_b", "final_g"]
BIG = ["w_in", "w_out", "ffn_w_up", "ffn_w_down"]
ORDER = ["norm1_g", "w_in", "gm_v_g", "gm_v_b", "gm_ws", "gm_bs", "lru_conv_w", "lru_conv_b", "lru_wa", "lru_ba",
         "lru_wx", "lru_bx", "lru_lambda", "gm_out_g", "lru_out_g", "w_out", "norm2_g", "ffn_w_up", "ffn_conv_w",
         "ffn_conv_b", "ffn_w_down", "final_g"]


def kernel(x, norm1_g, w_in, gm_v_g, gm_v_b, gm_ws, gm_bs, lru_conv_w, lru_conv_b, lru_wa, lru_ba, lru_wx, lru_bx, lru_lambda, gm_out_g, lru_out_g, w_out, norm2_g, ffn_w_up, ffn_conv_w, ffn_conv_b, ffn_w_down, final_g, loss_target, m_norm1_g, m_w_in, m_gm_v_g, m_gm_v_b, m_gm_ws, m_gm_bs, m_lru_conv_w, m_lru_conv_b, m_lru_wa, m_lru_ba, m_lru_wx, m_lru_bx, m_lru_lambda, m_gm_out_g, m_lru_out_g, m_w_out, m_norm2_g, m_ffn_w_up, m_ffn_conv_w, m_ffn_conv_b, m_ffn_w_down, m_final_g, v_norm1_g, v_w_in, v_gm_v_g, v_gm_v_b, v_gm_ws, v_gm_bs, v_lru_conv_w, v_lru_conv_b, v_lru_wa, v_lru_ba, v_lru_wx, v_lru_bx, v_lru_lambda, v_gm_out_g, v_lru_out_g, v_w_out, v_norm2_g, v_ffn_w_up, v_ffn_conv_w, v_ffn_conv_b, v_ffn_w_down, v_final_g):
    w = dict(norm1_g=norm1_g, w_in=w_in, gm_v_g=gm_v_g, gm_v_b=gm_v_b, gm_ws=gm_ws, gm_bs=gm_bs, lru_conv_w=lru_conv_w, lru_conv_b=lru_conv_b, lru_wa=lru_wa, lru_ba=lru_ba, lru_wx=lru_wx, lru_bx=lru_bx, lru_lambda=lru_lambda, gm_out_g=gm_out_g, lru_out_g=lru_out_g, w_out=w_out, norm2_g=norm2_g, ffn_w_up=ffn_w_up, ffn_conv_w=ffn_conv_w, ffn_conv_b=ffn_conv_b, ffn_w_down=ffn_w_down, final_g=final_g)
    m = dict(norm1_g=m_norm1_g, w_in=m_w_in, gm_v_g=m_gm_v_g, gm_v_b=m_gm_v_b, gm_ws=m_gm_ws, gm_bs=m_gm_bs, lru_conv_w=m_lru_conv_w, lru_conv_b=m_lru_conv_b, lru_wa=m_lru_wa, lru_ba=m_lru_ba, lru_wx=m_lru_wx, lru_bx=m_lru_bx, lru_lambda=m_lru_lambda, gm_out_g=m_gm_out_g, lru_out_g=m_lru_out_g, w_out=m_w_out, norm2_g=m_norm2_g, ffn_w_up=m_ffn_w_up, ffn_conv_w=m_ffn_conv_w, ffn_conv_b=m_ffn_conv_b, ffn_w_down=m_ffn_w_down, final_g=m_final_g)
    v = dict(norm1_g=v_norm1_g, w_in=v_w_in, gm_v_g=v_gm_v_g, gm_v_b=v_gm_v_b, gm_ws=v_gm_ws, gm_bs=v_gm_bs, lru_conv_w=v_lru_conv_w, lru_conv_b=v_lru_conv_b, lru_wa=v_lru_wa, lru_ba=v_lru_ba, lru_wx=v_lru_wx, lru_bx=v_lru_bx, lru_lambda=v_lru_lambda, gm_out_g=v_gm_out_g, lru_out_g=v_lru_out_g, w_out=v_w_out, norm2_g=v_norm2_g, ffn_w_up=v_ffn_w_up, ffn_conv_w=v_ffn_conv_w, ffn_conv_b=v_ffn_conv_b, ffn_w_down=v_ffn_w_down, final_g=v_final_g)

    mx, my, mc = _mesh_pos()
    shard = 2 * mx + my
    cs = jnp.stack([mc, shard]).astype(jnp.int32)

    xs = x[0]
    tgt = loss_target[0]
    s_len, d = xs.shape

    halves = lambda a: a.reshape((2, a.shape[0] // 2) + a.shape[1:])
    slot = {k: _to_slot("slot_" + k, halves(w[k][0]), cs, BF16) for k in BIG}
    win_b, wc_b, wfc_b = _all_gather([slot["w_in"],
                                      _to_slot("slot_lru_conv_w", w["lru_conv_w"][0].reshape(2, 4, -1), cs, F32),
                                      _to_slot("slot_ffn_conv_w", w["ffn_conv_w"][0].reshape(2, 12, -1), cs, F32)])
    win_g = win_b.reshape(N_CHIPS, d, -1)
    wc = wc_b.reshape(N_CHIPS, 4, -1).transpose(1, 0, 2).reshape(4, -1)
    wfc = wfc_b.reshape(N_CHIPS, 3, -1).transpose(1, 0, 2).reshape(3, -1)
    wout_b, wup_b, wdown_b = slot["w_out"], slot["ffn_w_up"], slot["ffn_w_down"]
    r_out, r_up, r_dn = wout_b.shape[2], wup_b.shape[2] // 2, wdown_b.shape[2] // 2

    tril = jnp.tril(jnp.ones((CHUNK, CHUNK), bool))
    wt32 = jnp.where(tril[None], w["gm_ws"][0], 0.0)
    wt = wt32.astype(BF16)
    wtt = wt32.transpose(0, 2, 1).astype(BF16)
    bst = jnp.repeat(w["gm_bs"][0].T, HEAD_DIM, axis=1)
    wa = w["lru_wa"][0].astype(BF16)
    wx = w["lru_wx"][0].astype(BF16)
    wat = w["lru_wa"][0].transpose(0, 2, 1).astype(BF16)
    wxt = w["lru_wx"][0].transpose(0, 2, 1).astype(BF16)
    ba = w["lru_ba"][0].reshape(1, -1)
    bx = w["lru_bx"][0].reshape(1, -1)
    gf = w["final_g"].reshape(1, -1)

    p, h1, wout_b, wup_b = _fwd_in_proj(
        xs, w["norm1_g"], win_g, comm=_job_gather([wout_b, wup_b], [(0, 0, r_out), (1, 0, r_up)], []))
    y, hs, wout_b, wup_b, wdown_b = _fwd_mixers(
        p, w["gm_v_g"], w["gm_v_b"], wt, bst, wc, w["lru_conv_b"], wa, ba, wx, bx, w["lru_lambda"],
        w["gm_out_g"], w["lru_out_g"],
        comm=_job_gather([wout_b, wup_b, wdown_b], [(1, r_up, r_up), (2, 0, r_dn)], [(0, 0, r_out), (1, 0, r_up)]))
    wout_g = wout_b.reshape(-1, d)
    x2, h2, wup_b, wdown_b = _fwd_out_proj(
        xs, y, wout_g, w["norm2_g"], comm=_job_gather([wup_b, wdown_b], [(1, r_dn, r_dn)], [(0, r_up, r_up), (1, 0, r_dn)]))
    wdown_b, = _comm_call("gather_tail", _job_gather([wdown_b], [], [(0, r_dn, r_dn)]))
    wup_g = wup_b.reshape(N_CHIPS, d, -1)
    wdown_g = wdown_b.reshape(-1, d)
    upb, cb, act, dx3, dx3b, loss_tile, dgf = _fwd_ffn(h2, wup_g, wfc, w["ffn_conv_b"], wdown_g, x2, gf, tgt)
    loss = lax.psum(loss_tile[0, 0], ("x", "y", "c"))

    adam = {}

    def adamw_big(name, gfull, hv, comm=None):
        r2 = lambda a: a.reshape(w[name].shape[1:])
        res = _adamw("adamw_" + name, r2(w[name]), gfull, r2(m[name]), r2(v[name]), halves=hv, comm=comm)
        adam[name] = res[:4]
        return res[4:]

    gd32, gdb = _bwd_weight("bwd_w_down", act[None], dx3b[None], a_planes=1, b_planes=1, shard_rows=True)
    dup, dwf, dx2, dx2b, dg2, rcv = _bwd_ffn(dx3b, wdown_g, upb, cb, wfc, wup_g, dx3, x2, w["norm2_g"],
                                             comm=_job_pair_swap([gdb]))
    pb_dn, own_dn = _pair_sum("pair_sum_3", gd32, rcv, cs)
    gu32, gub, got = _bwd_weight("bwd_w_up", h2[None], dup, a_planes=1, b_planes=2, shard_rows=False,
                                 comm=_job_chip_exchange([pb_dn]))
    red_dn = _chip_sum("chip_sum_3", own_dn, got, cs)
    go32, gob, rcv, red_dn = _bwd_weight("bwd_w_out", y[None], dx2b[None], a_planes=1, b_planes=1, shard_rows=True,
                                         comm=_merge_jobs([_job_pair_swap([gub]), _job_halves_swap([red_dn])]))
    pb_up, own_up = _pair_sum("pair_sum_2", gu32, rcv, cs)
    adamw_big("ffn_w_down", red_dn, "cols")
    (dp, dgv, dbv, dwt, dbst, dwc, dbc, dwa, dba, dwx, dbx, dsp, dggm, dglru, got_up, rcv) = _bwd_mixers(
        p, dx2b, wout_g, hs, w["gm_v_g"], w["gm_v_b"], wt, wtt, bst, wc, w["lru_conv_b"], wa, wat, ba, wx, wxt, bx,
        w["lru_lambda"], w["gm_out_g"], w["lru_out_g"],
        comm=_merge_jobs([_job_chip_exchange([pb_up]), _job_pair_swap([gob])]))
    pb_out, own_out = _pair_sum("pair_sum_1", go32, rcv, cs)
    red_up = _chip_sum("chip_sum_2", own_up, got_up, cs)
    gi32, gib, red_up, got_out = _bwd_weight("bwd_w_in", h1[None], dp[None], a_planes=1, b_planes=1, shard_rows=False,
                                             comm=_merge_jobs([_job_halves_swap([red_up]), _job_chip_exchange([pb_out])]))
    red_out = _chip_sum("chip_sum_1", own_out, got_out, cs)
    rcv, = _comm_call("w_in_swap", _job_pair_swap([gib]))
    pb_in, own_in = _pair_sum("pair_sum_0", gi32, rcv, cs)
    grad_x, dg1, got_in, red_out = _bwd_in(dp, win_g, dx2, xs, w["norm1_g"],
                                           comm=_merge_jobs([_job_chip_exchange([pb_in]), _job_halves_swap([red_out])]))
    red_in = _chip_sum("chip_sum_0", own_in, got_in, cs)

    dwfc = dwf[:, :, :3].transpose(2, 0, 1, 3).reshape(3, -1)
    dbfc = dwf[:, :, 3].reshape(1, -1)
    dlam = dsp * (-_sigmoid(-w["lru_lambda"]))
    small_grads = dict(
        norm1_g=dg1, gm_v_g=dgv, gm_v_b=dbv, gm_ws=jnp.where(tril[None], dwt, 0.0),
        gm_bs=dbst.reshape(CHUNK, HEADS, HEAD_DIM).sum(-1).T, lru_conv_w=dwc, lru_conv_b=dbc, lru_wa=dwa, lru_ba=dba,
        lru_wx=dwx, lru_bx=dbx, lru_lambda=dlam, gm_out_g=dggm, lru_out_g=dglru, norm2_g=dg2, ffn_conv_w=dwfc,
        ffn_conv_b=dbfc, final_g=dgf)
    full_shapes = [small_grads[k].shape for k in SMALL]
    rows_full = _pack_rows(full_shapes, 16)
    gpack = _pack([small_grads[k] for k in SMALL], rows_full).reshape(2, rows_full // 2, PACK_COLS)

    rcv, = _comm_call("small_swap", _job_pair_swap([gpack]))
    small_pair = _small_pair_sum(gpack, rcv, cs)
    got_small, = _comm_call("small_exchange", _job_chip_exchange([], [small_pair]))
    small_half = _small_chip_sum(small_pair, got_small, cs)
    red_in, small_full = _comm_call("tail_halves_swap", _job_halves_swap([red_in, small_half]))
    adamw_big("ffn_w_up", red_up, "rows")
    adamw_big("w_out", red_out, "cols")
    adamw_big("w_in", red_in, "rows")

    grads = {}
    for name, g in zip(SMALL, _unpack(small_full, full_shapes)):
        blk = w[name].shape[1:] if w[name].ndim > 1 else w[name].shape
        if name in ("lru_conv_w", "ffn_conv_w"):
            g = lax.dynamic_slice_in_dim(g, shard * blk[1], blk[1], axis=1)
        grads[name] = g.reshape(blk)

    delta, new_m, new_v = {}, {}, {}
    for name in BIG:
        delta[name], new_m[name], new_v[name], grads[name] = adam[name]
    blk_shapes = [grads[k].shape for k in SMALL]
    rows_blk = _pack_rows(blk_shapes, T_ELEM)
    packs = [_pack([src[k] for k in SMALL], rows_blk) for src in (w, grads, m, v)]
    outs = _adamw("adamw_small", *packs, halves=None)
    for dst, buf in zip((delta, new_m, new_v), outs):
        for name, a in zip(SMALL, _unpack(buf, blk_shapes)):
            dst[name] = a

    def shaped(dct):
        return [dct[k].reshape(w[k].shape) for k in ORDER]

    return (loss, grad_x[None], *shaped(grads), *shaped(delta), *shaped(new_m), *shaped(new_v))
```

```python
import functools
import math

import jax
import jax.numpy as jnp
from jax import lax
from jax.experimental import pallas as pl
from jax.experimental.pallas import tpu as pltpu

F32 = jnp.float32
BF16 = jnp.bfloat16
MESH = pl.DeviceIdType.MESH
ANY = pl.BlockSpec(memory_space=pltpu.HBM)

GM_W = 1024
LRU_W = 1024
CHUNK = 128
HEADS = 8
HEAD_DIM = 128
LRU_C = 8.0
RMS_EPS = 1e-6
LN_EPS = 1e-5
ADAM_LR = 0.001
ADAM_B1 = 0.9
ADAM_B2 = 0.999
ADAM_EPS = 1e-08
ADAM_WD = 0.01
ADAM_STEP = 10

N_CHIPS = 4
HALO = 8
PACK_COLS = 1024
VMEM_LIMIT = 56 * 1024 * 1024

TM_IN = 512
TS_MIX = 256
TM_OUT = 512
TM_UP = 512
TN_UP = 512
T_EPI = 128
T_CHUNK = 16
TM_BW = 512
TW_M = 1024
TW_N = 1024
TW_K = 4096
T_ELEM = 256
LANES = 128


def _tile(dim, cap):
    t = min(cap, dim) // LANES * LANES
    while dim % t:
        t -= LANES
    return t

_GELU_K0 = 0.7978845608028654
_GELU_K1 = 0.044715


class _Job:
    def __init__(self, ins, out_shapes, aliases, n, copies):
        self.ins, self.out_shapes, self.aliases, self.n, self.copies = list(ins), list(out_shapes), dict(aliases), n, copies

    def make(self, cin, cout, send_sems, recv_sems):
        x, y, c = _mesh_pos()
        return [pltpu.make_async_remote_copy(src_ref=src, dst_ref=dst, send_sem=send_sems.at[k], recv_sem=recv_sems.at[k],
                                             device_id=dev, device_id_type=MESH)
                for k, (src, dst, dev) in enumerate(self.copies(cin, cout, x, y, c))]


def _merge_jobs(jobs):
    ins, outs, aliases, spans = [], [], {}, []
    for jb in jobs:
        spans.append((len(ins), len(jb.ins), len(outs), len(jb.out_shapes)))
        aliases.update({len(ins) + a: len(outs) + b for a, b in jb.aliases.items()})
        ins += jb.ins
        outs += jb.out_shapes

    def copies(cin, cout, x, y, c):
        out = []
        for jb, (i0, ni, o0, no) in zip(jobs, spans):
            out += jb.copies(cin[i0:i0 + ni], cout[o0:o0 + no], x, y, c)
        return out

    return _Job(ins, outs, aliases, sum(jb.n for jb in jobs), copies)


def _pcall(body, *, name, out_shape, grid=None, in_specs=None, out_specs=None, scratch_shapes=(),
           grid_spec=None, dims=None, aliases=None, comm=None):
    params = pltpu.CompilerParams(dimension_semantics=dims, vmem_limit_bytes=VMEM_LIMIT)
    kw = dict(name=name, compiler_params=params)
    if grid_spec is not None:
        if aliases:
            kw["input_output_aliases"] = aliases
        return pl.pallas_call(body, grid_spec=grid_spec, out_shape=out_shape, **kw)
    scratch_shapes = list(scratch_shapes)
    if comm is not None:
        out_shape = list(out_shape) if isinstance(out_shape, (list, tuple)) else [out_shape]
        out_specs = list(out_specs) if isinstance(out_specs, (list, tuple)) else [out_specs]
        n_in, n_out, n_scr = len(in_specs), len(out_shape), len(scratch_shapes)
        n_ci, n_co = len(comm.ins), len(comm.out_shapes)
        inner, steps = body, tuple(grid)

        def body(*refs):
            o0 = n_in + n_ci
            s0 = o0 + n_out + n_co
            cps = comm.make(refs[n_in:o0], refs[o0 + n_out:s0], refs[s0 + n_scr], refs[s0 + n_scr + 1])
            ids = [pl.program_id(k) for k in range(len(steps))]

            @pl.when(functools.reduce(jnp.logical_and, [i == 0 for i in ids]))
            def _():
                for cp in cps:
                    cp.start()

            inner(*refs[:n_in], *refs[o0:o0 + n_out], *refs[s0:s0 + n_scr])

            @pl.when(functools.reduce(jnp.logical_and, [i == n - 1 for i, n in zip(ids, steps)]))
            def _():
                for cp in cps:
                    cp.wait()

        in_specs = list(in_specs) + [ANY] * n_ci
        out_specs = out_specs + [ANY] * n_co
        out_shape = out_shape + comm.out_shapes
        scratch_shapes = scratch_shapes + [pltpu.SemaphoreType.DMA((comm.n,)), pltpu.SemaphoreType.DMA((comm.n,))]
        aliases = {**(aliases or {}), **{n_in + a: n_out + b for a, b in comm.aliases.items()}}
    if aliases:
        kw["input_output_aliases"] = aliases
    if grid is not None:
        kw["grid"] = grid
    call = pl.pallas_call(body, in_specs=in_specs, out_specs=out_specs, scratch_shapes=scratch_shapes,
                          out_shape=out_shape, **kw)
    if comm is None:
        return call
    return lambda *args: call(*args, *comm.ins)


def _comm_call(name, job):
    n_ci, n_co = len(job.ins), len(job.out_shapes)

    def body(*refs):
        cps = job.make(refs[:n_ci], refs[n_ci:n_ci + n_co], refs[n_ci + n_co], refs[n_ci + n_co + 1])
        for cp in cps:
            cp.start()
        for cp in cps:
            cp.wait()

    return pl.pallas_call(
        body, name=name, in_specs=[ANY] * n_ci, out_specs=[ANY] * n_co, out_shape=job.out_shapes,
        scratch_shapes=[pltpu.SemaphoreType.DMA((job.n,)), pltpu.SemaphoreType.DMA((job.n,))],
        input_output_aliases=job.aliases)(*job.ins)


def _gelu(x):
    t = jnp.tanh(_GELU_K0 * (x + _GELU_K1 * (x * x * x)))
    return 0.5 * x * (1.0 + t)


def _gelu_and_grad(x):
    x2 = x * x
    t = jnp.tanh(_GELU_K0 * (x + _GELU_K1 * (x2 * x)))
    g = 0.5 * x * (1.0 + t)
    dg = 0.5 * (1.0 + t) + 0.5 * x * (1.0 - t * t) * (_GELU_K0 * (1.0 + 3.0 * _GELU_K1 * x2))
    return g, dg


def _sigmoid(x):
    return 1.0 / (1.0 + jnp.exp(-x))


def _neg_expm1(x):
    series = -x * (1.0 + x * (0.5 + x * (1.0 / 6.0 + x * (1.0 / 24.0 + x * (1.0 / 120.0 + x * (1.0 / 720.0))))))
    return jnp.where(x > -0.1, series, 1.0 - jnp.exp(x))


def _softplus(z):
    return jnp.maximum(z, 0.0) + jnp.log(1.0 + jnp.exp(-jnp.abs(z)))


def _rowmean(x):
    return jnp.mean(x, axis=-1, keepdims=True)


def _colsum(x):
    return jnp.sum(x, axis=0, keepdims=True)


def _rms_stats(x):
    r = lax.rsqrt(_rowmean(x * x) + RMS_EPS)
    return r, x * r


def _rms_bwd(dy, n, r, g):
    dn = dy * g
    return r * (dn - n * _rowmean(dn * n)), dy * n


def _shift_prev(x, halo, d):
    cat = jnp.concatenate([halo, x], axis=0)
    return pltpu.roll(cat, d, 0)[HALO:, :]


def _prev_rows(cat):
    return cat[HALO:, :], pltpu.roll(cat, 1, 0)[HALO:, :], pltpu.roll(cat, 2, 0)[HALO:, :]


def _next_rows(cat):
    n = cat.shape[0]
    return cat[:n - HALO, :], pltpu.roll(cat, n - 1, 0)[:n - HALO, :], pltpu.roll(cat, n - 2, 0)[:n - HALO, :]


def _shift_next(x, halo, d):
    n = x.shape[0]
    cat = jnp.concatenate([x, halo], axis=0)
    return pltpu.roll(cat, n + HALO - d, 0)[:n, :]


def _dot(a, b):
    return jnp.dot(a, b, preferred_element_type=F32)


def _dot_nt(a, b):
    return lax.dot_general(a, b, (((1,), (1,)), ((), ())), preferred_element_type=F32)


def _dot_tn(a, b):
    return lax.dot_general(a, b, (((0,), (0,)), ((), ())), preferred_element_type=F32)


def _fwd_in_proj(x, g1, win_g, comm=None):
    s_len, d = x.shape
    nsh, _, ncol = win_g.shape
    tm = min(TM_IN, s_len)
    te = min(T_EPI, tm)

    def body(x_ref, g_ref, w_ref, p_ref, h_ref):
        g = g_ref[...]

        def chunk(q, carry):
            rows = pl.ds(pl.multiple_of(q * te, te), te)
            _, n = _rms_stats(x_ref[rows, :])
            h_ref[rows, :] = (n * g).astype(BF16)
            return carry

        lax.fori_loop(0, tm // te, chunk, 0)
        h = h_ref[...]
        for k in range(nsh):
            p_ref[:, k * ncol:(k + 1) * ncol] = _dot(h, w_ref[k])

    return _pcall(
        body, name="fwd_in_proj", grid=(s_len // tm,),
        in_specs=[pl.BlockSpec((tm, d), lambda i: (i, 0)),
                  pl.BlockSpec((1, d), lambda i: (0, 0)),
                  pl.BlockSpec((nsh, d, ncol), lambda i: (0, 0, 0), pipeline_mode=pl.Buffered(1))],
        out_specs=[pl.BlockSpec((tm, nsh * ncol), lambda i: (i, 0)),
                   pl.BlockSpec((tm, d), lambda i: (i, 0))],
        out_shape=[jax.ShapeDtypeStruct((s_len, nsh * ncol), F32), jax.ShapeDtypeStruct((s_len, d), BF16)],
        dims=("arbitrary",), comm=comm)(x, g1, win_g)


def _gm_forward(z, gv, bv, wt_ref, bst_ref, vl_s, mix_s):
    ts = z.shape[0]
    ge = _gelu(z)
    u = ge[:, :GM_W]
    v = ge[:, GM_W:]
    vc = v - _rowmean(v)
    rs = lax.rsqrt(_rowmean(vc * vc) + LN_EPS)
    vh = vc * rs
    vl_s[...] = (vh * gv + bv).astype(BF16)
    for cc in range(ts // CHUNK):
        rows = slice(cc * CHUNK, (cc + 1) * CHUNK)
        for hh in range(HEADS):
            cols = slice(hh * HEAD_DIM, (hh + 1) * HEAD_DIM)
            mix_s[rows, cols] = _dot(wt_ref[hh], vl_s[rows, cols]) + bst_ref[:, cols]
    mixed = mix_s[...]
    return u, mixed, vh, rs, u * mixed


def _lru_gates(xl, halo, wc_ref, bc_ref, wa_ref, ba_ref, wx_ref, bx_ref, lam_ref, z_s):
    x1 = _shift_prev(xl, halo, 1)
    x2 = _shift_prev(xl, halo, 2)
    x3 = _shift_prev(xl, halo, 3)
    xr = bc_ref[...] + wc_ref[0:1, :] * x3 + wc_ref[1:2, :] * x2 + wc_ref[2:3, :] * x1 + wc_ref[3:4, :] * xl
    xrb = xr.astype(BF16)
    for hh in range(HEADS):
        cols = slice(hh * HEAD_DIM, (hh + 1) * HEAD_DIM)
        z_s[:, cols] = _dot(xrb[:, cols], wa_ref[hh])
        z_s[:, LRU_W + hh * HEAD_DIM:LRU_W + (hh + 1) * HEAD_DIM] = _dot(xrb[:, cols], wx_ref[hh])
    ra = _sigmoid(z_s[:, :LRU_W] + ba_ref[...])
    ri = _sigmoid(z_s[:, LRU_W:] + bx_ref[...])
    sp = _softplus(-lam_ref[...])
    la = (-LRU_C) * ra * sp
    a = jnp.exp(la)
    mult = jnp.sqrt(_neg_expm1(2.0 * la))
    return dict(x1=x1, x2=x2, x3=x3, xr=xr, xrb=xrb, ra=ra, ri=ri, sp=sp, a=a, mult=mult)


def _fwd_mixers(p, gv, bv, wt, bst, wc, bc, wa, ba, wx, bx, lam, ggm, glru, comm=None):
    s_len = p.shape[0]
    ts = min(TS_MIX, s_len)

    def body(pz_ref, pgl_ref, pxl_ref, gv_ref, bv_ref, wt_ref, bst_ref, wc_ref, bc_ref, wa_ref, ba_ref, wx_ref,
             bx_ref, lam_ref, ggm_ref, glru_ref, y_ref, hs_ref, tail_ref, h_ref, a_s, b_s, vl_s, mix_s, z_s):
        @pl.when(pl.program_id(0) == 0)
        def _():
            tail_ref[...] = jnp.zeros_like(tail_ref)
            h_ref[...] = jnp.zeros_like(h_ref)

        _, _, _, _, ygm = _gm_forward(pz_ref[...], gv_ref[...], bv_ref[...], wt_ref, bst_ref, vl_s, mix_s)
        _, ngm = _rms_stats(ygm)
        y_ref[:, :GM_W] = (ngm * ggm_ref[...]).astype(BF16)

        xl = pxl_ref[...]
        gts = _lru_gates(xl, tail_ref[...], wc_ref, bc_ref, wa_ref, ba_ref, wx_ref, bx_ref, lam_ref, z_s)
        tail_ref[...] = xl[ts - HALO:, :]
        a_s[...] = gts["a"]
        b_s[...] = gts["mult"] * (gts["ri"] * gts["xr"])

        def step(t, h):
            h = a_s[pl.ds(t, 1), :] * h + b_s[pl.ds(t, 1), :]
            hs_ref[pl.ds(t, 1), :] = h
            return h

        h_ref[...] = lax.fori_loop(0, ts, step, h_ref[...], unroll=8)
        yl = hs_ref[...] * _gelu(pgl_ref[...])
        _, nl = _rms_stats(yl)
        y_ref[:, GM_W:] = (nl * glru_ref[...]).astype(BF16)

    full = lambda shape: pl.BlockSpec(shape, lambda i: (0,) * len(shape))
    return _pcall(
        body, name="fwd_mixers", grid=(s_len // ts,),
        in_specs=[pl.BlockSpec((ts, 2 * GM_W), lambda i: (i, 0)),
                  pl.BlockSpec((ts, LRU_W), lambda i: (i, 2)),
                  pl.BlockSpec((ts, LRU_W), lambda i: (i, 3)),
                  full((1, GM_W)), full((1, GM_W)), full((HEADS, CHUNK, CHUNK)), full((CHUNK, GM_W)),
                  full((4, LRU_W)), full((1, LRU_W)), full((HEADS, HEAD_DIM, HEAD_DIM)), full((1, LRU_W)),
                  full((HEADS, HEAD_DIM, HEAD_DIM)), full((1, LRU_W)), full((1, LRU_W)), full((1, GM_W)),
                  full((1, LRU_W))],
        out_specs=[pl.BlockSpec((ts, GM_W + LRU_W), lambda i: (i, 0)), pl.BlockSpec((ts, LRU_W), lambda i: (i, 0))],
        out_shape=[jax.ShapeDtypeStruct((s_len, GM_W + LRU_W), BF16), jax.ShapeDtypeStruct((s_len, LRU_W), F32)],
        scratch_shapes=[pltpu.VMEM((HALO, LRU_W), F32), pltpu.VMEM((1, LRU_W), F32),
                        pltpu.VMEM((ts, LRU_W), F32), pltpu.VMEM((ts, LRU_W), F32),
                        pltpu.VMEM((ts, GM_W), BF16), pltpu.VMEM((ts, GM_W), F32), pltpu.VMEM((ts, 2 * LRU_W), F32)],
        dims=("arbitrary",), comm=comm)(p, p, p, gv, bv, wt, bst, wc, bc, wa, ba, wx, bx, lam, ggm, glru)


def _fwd_out_proj(x, y, wout_g, g2, comm=None):
    s_len, d = x.shape
    tm = min(TM_OUT, s_len)

    def body(x_ref, y_ref, w_ref, g_ref, x2_ref, h2_ref):
        x2 = x_ref[...] + _dot(y_ref[...], w_ref[...])
        x2_ref[...] = x2
        _, n = _rms_stats(x2)
        h2_ref[...] = (n * g_ref[...]).astype(BF16)

    return _pcall(
        body, name="fwd_out_proj", grid=(s_len // tm,),
        in_specs=[pl.BlockSpec((tm, d), lambda i: (i, 0)), pl.BlockSpec((tm, d), lambda i: (i, 0)),
                  pl.BlockSpec((d, d), lambda i: (0, 0)), pl.BlockSpec((1, d), lambda i: (0, 0))],
        out_specs=[pl.BlockSpec((tm, d), lambda i: (i, 0)), pl.BlockSpec((tm, d), lambda i: (i, 0))],
        out_shape=[jax.ShapeDtypeStruct((s_len, d), F32), jax.ShapeDtypeStruct((s_len, d), BF16)],
        dims=("arbitrary",), comm=comm)(x, y, wout_g, g2)


def _row_fetch(hbm_ref, buf_ref, sem, row0, rows):
    return pltpu.make_async_copy(hbm_ref.at[pl.ds(row0, rows), :], buf_ref, sem)


def _fwd_ffn(h2, wup_g, wfc, bfc, wdown_g, x2, gf, target):
    s_len, d = h2.shape
    nsh, _, ncol = wup_g.shape
    tn = TN_UP
    tn2 = 2 * tn
    f = nsh * ncol // 2
    nj = f // tn
    nps = ncol // tn
    tm = min(TM_UP, s_len)
    te = min(T_EPI, tm)
    tr = T_CHUNK

    def body(h_ref, wg_ref, wv_ref, wcg_ref, wcv_ref, bg_ref, bv_ref, wd_ref, g_ref, x2_hbm, t_hbm,
             upb_ref, cb_ref, act_ref, dx3_ref, dx3b_ref, loss_ref, dgf_ref, tail_ref, acc_ref, x2_buf, t_buf, up_ref,
             sems):
        i, j = pl.program_id(0), pl.program_id(1)
        row0 = pl.multiple_of(i * tm, tm)
        fetches = (_row_fetch(x2_hbm, x2_buf, sems.at[0], row0, tm), _row_fetch(t_hbm, t_buf, sems.at[1], row0, tm))

        @pl.when(jnp.logical_and(i == 0, j == 0))
        def _():
            tail_ref[...] = jnp.zeros_like(tail_ref)
            loss_ref[...] = jnp.zeros_like(loss_ref)
            dgf_ref[...] = jnp.zeros_like(dgf_ref)

        @pl.when(j == 0)
        def _():
            acc_ref[...] = jnp.zeros_like(acc_ref)
            for cp in fetches:
                cp.start()

        h = h_ref[...]
        up_ref[:, :tn] = _dot(h, wg_ref[...])
        up_ref[:, tn:] = _dot(h, wv_ref[...])

        planes = ((wcg_ref, bg_ref), (wcv_ref, bv_ref))
        for c in range(tn // LANES):
            cols = slice(c * LANES, (c + 1) * LANES)
            pcols = [slice(pln * tn + c * LANES, pln * tn + (c + 1) * LANES) for pln in range(2)]

            def conv(pln, u, u1, u2):
                wc_ref, b_ref = planes[pln]
                return b_ref[:, cols] + wc_ref[0:1, cols] * u2 + wc_ref[1:2, cols] * u1 + wc_ref[2:3, cols] * u

            def emit(rows, us):
                cs_ = [conv(pln, *us[pln]) for pln in range(2)]
                for pln in range(2):
                    upb_ref[rows, pcols[pln]] = us[pln][0].astype(BF16)
                    cb_ref[rows, pcols[pln]] = cs_[pln].astype(BF16)
                act_ref[rows, cols] = (_gelu(cs_[0]) * cs_[1]).astype(BF16)

            first = [_prev_rows(jnp.concatenate([tail_ref[j, :, pc], up_ref[0:tr, pc]], axis=0)) for pc in pcols]
            emit(slice(0, tr), first)

            for k in range(1, tm // tr):
                r0 = k * tr
                us = [_prev_rows(up_ref[r0 - HALO:r0 + tr, pc]) for pc in pcols]
                emit(slice(r0, r0 + tr), us)
        tail_ref[j] = up_ref[tm - HALO:tm, :]
        acc_ref[...] += _dot(act_ref[...], wd_ref[...])

        @pl.when(j == nj - 1)
        def _():
            for cp in fetches:
                cp.wait()
            g = g_ref[...]

            def chunk(k, carry):
                rows = pl.ds(pl.multiple_of(k * te, te), te)
                x3 = x2_buf[rows, :] + acc_ref[rows, :]
                r, n = _rms_stats(x3)
                err = n * g - t_buf[rows, :]
                loss_ref[...] += jnp.sum(err * err) * (0.5 / d)
                dx3, dgn = _rms_bwd(err * (1.0 / d), n, r, g)
                dgf_ref[...] += _colsum(dgn)
                dx3_ref[rows, :] = dx3
                dx3b_ref[rows, :] = dx3.astype(BF16)
                return carry

            lax.fori_loop(0, tm // te, chunk, 0)

    hbm = pl.BlockSpec(memory_space=pl.ANY)
    return _pcall(
        body, name="fwd_ffn", grid=(s_len // tm, nj),
        in_specs=[pl.BlockSpec((tm, d), lambda i, j: (i, 0)),
                  pl.BlockSpec((None, d, tn), lambda i, j: (j // nps, 0, j % nps)),
                  pl.BlockSpec((None, d, tn), lambda i, j: (nsh // 2 + j // nps, 0, j % nps)),
                  pl.BlockSpec((3, tn), lambda i, j: (0, j)), pl.BlockSpec((3, tn), lambda i, j: (0, nj + j)),
                  pl.BlockSpec((1, tn), lambda i, j: (0, j)), pl.BlockSpec((1, tn), lambda i, j: (0, nj + j)),
                  pl.BlockSpec((tn, d), lambda i, j: (j, 0)), pl.BlockSpec((1, d), lambda i, j: (0, 0)), hbm, hbm],
        out_specs=[pl.BlockSpec((tm, tn2), lambda i, j: (i, j)), pl.BlockSpec((tm, tn2), lambda i, j: (i, j)),
                   pl.BlockSpec((tm, tn), lambda i, j: (i, j)),
                   pl.BlockSpec((tm, d), lambda i, j: (i, 0)), pl.BlockSpec((tm, d), lambda i, j: (i, 0)),
                   pl.BlockSpec((8, 128), lambda i, j: (0, 0)), pl.BlockSpec((1, d), lambda i, j: (0, 0))],
        out_shape=[jax.ShapeDtypeStruct((s_len, 2 * f), BF16), jax.ShapeDtypeStruct((s_len, 2 * f), BF16),
                   jax.ShapeDtypeStruct((s_len, f), BF16),
                   jax.ShapeDtypeStruct((s_len, d), F32), jax.ShapeDtypeStruct((s_len, d), BF16),
                   jax.ShapeDtypeStruct((8, 128), F32), jax.ShapeDtypeStruct((1, d), F32)],
        scratch_shapes=[pltpu.VMEM((nj, HALO, tn2), F32), pltpu.VMEM((tm, d), F32), pltpu.VMEM((tm, d), F32),
                        pltpu.VMEM((tm, d), F32), pltpu.VMEM((tm, tn2), F32), pltpu.SemaphoreType.DMA((2,))],
        dims=("arbitrary", "arbitrary"))(h2, wup_g, wup_g, wfc, wfc, bfc, bfc, wdown_g, gf, x2, target)


def _bwd_ffn(dx3b, wdown_g, upb, cb, wfc, wup_g, dx3, x2, g2, comm=None):
    s_len, d = dx3b.shape
    nsh, _, ncol = wup_g.shape
    tn = TN_UP
    tn2 = 2 * tn
    f = nsh * ncol // 2
    nj = f // tn
    nps = ncol // tn
    tm = min(TM_UP, s_len)
    te = min(T_EPI, tm)
    nt = s_len // tm
    tr = T_CHUNK

    def body(dx_ref, w_ref, upb_ref, cb_ref, wcg_ref, wcv_ref, wug_ref, wuv_ref, g_ref, r_hbm, x_hbm,
             dup_ref, dwf_ref, dx2_ref, dx2b_ref, dg_ref, nxt_ref, acc_ref, r_buf, x_buf, da_s, dc_s, sems):
        i, j = pl.program_id(0), pl.program_id(1)
        ti = nt - 1 - i
        row0 = pl.multiple_of(ti * tm, tm)
        fetches = (_row_fetch(r_hbm, r_buf, sems.at[0], row0, tm), _row_fetch(x_hbm, x_buf, sems.at[1], row0, tm))

        @pl.when(jnp.logical_and(i == 0, j == 0))
        def _():
            dwf_ref[...] = jnp.zeros_like(dwf_ref)
            nxt_ref[...] = jnp.zeros_like(nxt_ref)
            dg_ref[...] = jnp.zeros_like(dg_ref)

        @pl.when(j == 0)
        def _():
            acc_ref[...] = jnp.zeros_like(acc_ref)
            for cp in fetches:
                cp.start()

        da_s[...] = _dot_nt(dx_ref[...], w_ref[...])
        planes = ((wcg_ref,), (wcv_ref,))
        for pln in range(2):
            dc_s[pln, tm:tm + HALO, :] = nxt_ref[pln, j]

        for c in range(tn // LANES):
            cols = slice(c * LANES, (c + 1) * LANES)
            pcols = [slice(pln * tn + c * LANES, pln * tn + (c + 1) * LANES) for pln in range(2)]
            for k in range(tm // tr):
                rows = slice(k * tr, (k + 1) * tr)
                dact = da_s[rows, cols]
                ge, gd = _gelu_and_grad(cb_ref[rows, pcols[0]].astype(F32))
                dc_s[0, rows, cols] = dact * cb_ref[rows, pcols[1]].astype(F32) * gd
                dc_s[1, rows, cols] = dact * ge

        for pln in range(2):
            nxt_ref[pln, j] = dc_s[pln, 0:HALO, :]

        def fold(v):
            out = v[0:8, :]
            for q in range(1, tr // 8):
                out = out + v[8 * q:8 * q + 8, :]
            return out

        for c in range(tn // LANES):
            cols = slice(c * LANES, (c + 1) * LANES)
            for pln in range(2):
                wc_ref = planes[pln][0]
                pc = slice(pln * tn + c * LANES, pln * tn + (c + 1) * LANES)
                sums = (jnp.zeros((8, LANES), F32),) * 4
                for k in range(tm // tr):
                    r0 = k * tr
                    dc, dc1, dc2 = _next_rows(dc_s[pln, r0:r0 + tr + HALO, cols])
                    dup = wc_ref[2:3, cols] * dc + wc_ref[1:2, cols] * dc1 + wc_ref[0:1, cols] * dc2
                    dup_ref[pln, r0:r0 + tr, cols] = dup.astype(BF16)
                    u = upb_ref[r0:r0 + tr, pc].astype(F32)
                    new = (fold(dc2 * u), fold(dc1 * u), fold(dc * u), fold(dc))
                    sums = tuple(a + b for a, b in zip(sums, new))
                for term in range(4):
                    dwf_ref[pln, j, term:term + 1, cols] += _colsum(sums[term])
        acc_ref[...] += _dot_nt(dup_ref[0], wug_ref[...]) + _dot_nt(dup_ref[1], wuv_ref[...])

        @pl.when(j == nj - 1)
        def _():
            for cp in fetches:
                cp.wait()
            g = g_ref[...]

            def chunk(k, carry):
                rows = pl.ds(pl.multiple_of(k * te, te), te)
                r, n = _rms_stats(x_buf[rows, :])
                dxn, dgn = _rms_bwd(acc_ref[rows, :], n, r, g)
                dg_ref[...] += _colsum(dgn)
                dx = r_buf[rows, :] + dxn
                dx2_ref[rows, :] = dx
                dx2b_ref[rows, :] = dx.astype(BF16)
                return carry

            lax.fori_loop(0, tm // te, chunk, 0)

    hbm = pl.BlockSpec(memory_space=pl.ANY)
    rev = lambda i: nt - 1 - i
    return _pcall(
        body, name="bwd_ffn", grid=(nt, nj),
        in_specs=[pl.BlockSpec((tm, d), lambda i, j: (rev(i), 0)),
                  pl.BlockSpec((tn, d), lambda i, j: (j, 0)),
                  pl.BlockSpec((tm, tn2), lambda i, j: (rev(i), j)),
                  pl.BlockSpec((tm, tn2), lambda i, j: (rev(i), j)),
                  pl.BlockSpec((3, tn), lambda i, j: (0, j)), pl.BlockSpec((3, tn), lambda i, j: (0, nj + j)),
                  pl.BlockSpec((None, d, tn), lambda i, j: (j // nps, 0, j % nps)),
                  pl.BlockSpec((None, d, tn), lambda i, j: (nsh // 2 + j // nps, 0, j % nps)),
                  pl.BlockSpec((1, d), lambda i, j: (0, 0)), hbm, hbm],
        out_specs=[pl.BlockSpec((2, tm, tn), lambda i, j: (0, rev(i), j)),
                   pl.BlockSpec((2, nj, 8, tn), lambda i, j: (0, 0, 0, 0)),
                   pl.BlockSpec((tm, d), lambda i, j: (rev(i), 0)), pl.BlockSpec((tm, d), lambda i, j: (rev(i), 0)),
                   pl.BlockSpec((1, d), lambda i, j: (0, 0))],
        out_shape=[jax.ShapeDtypeStruct((2, s_len, f), BF16), jax.ShapeDtypeStruct((2, nj, 8, tn), F32),
                   jax.ShapeDtypeStruct((s_len, d), F32), jax.ShapeDtypeStruct((s_len, d), BF16),
                   jax.ShapeDtypeStruct((1, d), F32)],
        scratch_shapes=[pltpu.VMEM((2, nj, HALO, tn), F32), pltpu.VMEM((tm, d), F32), pltpu.VMEM((tm, d), F32),
                        pltpu.VMEM((tm, d), F32), pltpu.VMEM((tm, tn), F32), pltpu.VMEM((2, tm + HALO, tn), F32),
                        pltpu.SemaphoreType.DMA((2,))],
        dims=("arbitrary", "arbitrary"), comm=comm)(dx3b, wdown_g, upb, cb, wfc, wfc, wup_g, wup_g, g2, dx3, x2)


def _bwd_in(dp, win_g, resid, x_in, g, comm=None):
    s_len, d = x_in.shape
    nsh, _, ncol = win_g.shape
    tm = min(TM_BW, s_len)
    te = min(T_EPI, tm)

    def body(dz_ref, w_ref, r_ref, x_ref, g_ref, dx_ref, dg_ref, acc_ref):
        i, k = pl.program_id(0), pl.program_id(1)

        @pl.when(jnp.logical_and(i == 0, k == 0))
        def _():
            dg_ref[...] = jnp.zeros_like(dg_ref)

        @pl.when(k == 0)
        def _():
            acc_ref[...] = jnp.zeros_like(acc_ref)

        acc_ref[...] += _dot_nt(dz_ref[...], w_ref[...])

        @pl.when(k == nsh - 1)
        def _():
            g = g_ref[...]

            def chunk(q, carry):
                rows = pl.ds(pl.multiple_of(q * te, te), te)
                r, n = _rms_stats(x_ref[rows, :])
                dxn, dgn = _rms_bwd(acc_ref[rows, :], n, r, g)
                dg_ref[...] += _colsum(dgn)
                dx_ref[rows, :] = r_ref[rows, :] + dxn
                return carry

            lax.fori_loop(0, tm // te, chunk, 0)

    return _pcall(
        body, name="bwd_in", grid=(s_len // tm, nsh),
        in_specs=[pl.BlockSpec((tm, ncol), lambda i, k: (i, k)),
                  pl.BlockSpec((None, d, ncol), lambda i, k: (k, 0, 0)),
                  pl.BlockSpec((tm, d), lambda i, k: (i, 0)), pl.BlockSpec((tm, d), lambda i, k: (i, 0)),
                  pl.BlockSpec((1, d), lambda i, k: (0, 0))],
        out_specs=[pl.BlockSpec((tm, d), lambda i, k: (i, 0)), pl.BlockSpec((1, d), lambda i, k: (0, 0))],
        out_shape=[jax.ShapeDtypeStruct((s_len, d), F32), jax.ShapeDtypeStruct((1, d), F32)],
        scratch_shapes=[pltpu.VMEM((tm, d), F32)],
        dims=("arbitrary", "arbitrary"), comm=comm)(dp, win_g, resid, x_in, g)


def _bwd_mixers(p, dx2b, wout_g, hs, gv, bv, wt, wtt, bst, wc, bc, wa, wat, ba, wx, wxt, bx, lam, ggm, glru, comm=None):
    s_len = p.shape[0]
    ts = min(TS_MIX, s_len)
    nt = s_len // ts
    hb = ts // HALO

    def body(pz_ref, pgl_ref, pxl_ref, xh_ref, dx_ref, wo_ref, hs_ref, hh_ref, gv_ref, bv_ref, wt_ref, wtt_ref, bst_ref,
             wc_ref, bc_ref, wa_ref, wat_ref, ba_ref, wx_ref, wxt_ref, bx_ref, lam_ref, ggm_ref, glru_ref,
             dp_ref, dgv_ref, dbv_ref, dwt_ref, dbst_ref, dwc_ref, dbc_ref, dwa_ref, dba_ref, dwx_ref, dbx_ref,
             dsp_ref, dggm_ref, dglru_ref,
             carry_ref, nxt_ref, a_s, g_s, vl_s, mix_s, z_s, dm_s, dvl_s, dxr_s, dy_ref):
        i = pl.program_id(0)
        ti = nt - 1 - i

        @pl.when(i == 0)
        def _():
            for ref in (dgv_ref, dbv_ref, dwt_ref, dbst_ref, dwc_ref, dbc_ref, dwa_ref, dba_ref, dwx_ref, dbx_ref,
                        dsp_ref, dggm_ref, dglru_ref, carry_ref, nxt_ref):
                ref[...] = jnp.zeros_like(ref)

        dy_ref[...] = _dot_nt(dx_ref[...], wo_ref[...])

        z = pz_ref[...]
        u, mixed, vh, rs, ygm = _gm_forward(z, gv_ref[...], bv_ref[...], wt_ref, bst_ref, vl_s, mix_s)
        rg, ngm = _rms_stats(ygm)
        dygm, dgn = _rms_bwd(dy_ref[:, :GM_W], ngm, rg, ggm_ref[...])
        dggm_ref[...] += _colsum(dgn)
        du = dygm * mixed
        dmix = dygm * u
        dm_s[...] = dmix.astype(BF16)
        bsum = dmix[0:CHUNK, :]
        for cc in range(1, ts // CHUNK):
            bsum = bsum + dmix[cc * CHUNK:(cc + 1) * CHUNK, :]
        dbst_ref[...] += bsum
        for hh in range(HEADS):
            cols = slice(hh * HEAD_DIM, (hh + 1) * HEAD_DIM)
            dw = jnp.zeros((CHUNK, CHUNK), F32)
            for cc in range(ts // CHUNK):
                rows = slice(cc * CHUNK, (cc + 1) * CHUNK)
                dmb = dm_s[rows, cols]
                dw = dw + _dot_nt(dmb, vl_s[rows, cols])
                dvl_s[rows, cols] = _dot(wtt_ref[hh], dmb)
            dwt_ref[hh] += dw
        dvl = dvl_s[...]
        dgv_ref[...] += _colsum(dvl * vh)
        dbv_ref[...] += _colsum(dvl)
        dvh = dvl * gv_ref[...]
        dv = rs * (dvh - _rowmean(dvh) - vh * _rowmean(dvh * vh))
        _, gd = _gelu_and_grad(z)
        dp_ref[:, :GM_W] = (du * gd[:, :GM_W]).astype(BF16)
        dp_ref[:, GM_W:2 * GM_W] = (dv * gd[:, GM_W:]).astype(BF16)

        xl = pxl_ref[...]
        xhalo = jnp.where(ti == 0, 0.0, xh_ref[...])
        gts = _lru_gates(xl, xhalo, wc_ref, bc_ref, wa_ref, ba_ref, wx_ref, bx_ref, lam_ref, z_s)
        a, mult, ra, ri, xr, sp = gts["a"], gts["mult"], gts["ra"], gts["ri"], gts["xr"], gts["sp"]
        hs = hs_ref[...]
        hprev = _shift_prev(hs, jnp.where(ti == 0, 0.0, hh_ref[...]), 1)
        gl = pgl_ref[...]
        ggl, dggl = _gelu_and_grad(gl)
        yl = hs * ggl
        rl, nl = _rms_stats(yl)
        dyl, dgn = _rms_bwd(dy_ref[:, GM_W:], nl, rl, glru_ref[...])
        dglru_ref[...] += _colsum(dgn)
        dp_ref[:, 2 * GM_W:2 * GM_W + LRU_W] = (dyl * hs * dggl).astype(BF16)
        a_s[...] = a
        g_s[...] = dyl * ggl

        def step(k, carry):
            t = ts - 1 - k
            gt = g_s[pl.ds(t, 1), :] + carry
            g_s[pl.ds(t, 1), :] = gt
            return a_s[pl.ds(t, 1), :] * gt

        carry_ref[...] = lax.fori_loop(0, ts, step, carry_ref[...], unroll=8)
        gsc = g_s[...]
        da = gsc * hprev
        rix = ri * xr
        dmult = gsc * rix
        dri = gsc * mult * xr
        dxr = gsc * mult * ri
        dla = da * a - dmult * (a * a) / mult
        dsp_ref[...] += _colsum(dla * ra) * (-LRU_C)
        dza = (dla * sp) * (-LRU_C) * ra * (1.0 - ra)
        dzi = dri * ri * (1.0 - ri)
        dba_ref[...] += _colsum(dza)
        dbx_ref[...] += _colsum(dzi)
        dzab = dza.astype(BF16)
        dzib = dzi.astype(BF16)
        xrb = gts["xrb"]
        for hh in range(HEADS):
            cols = slice(hh * HEAD_DIM, (hh + 1) * HEAD_DIM)
            dwa_ref[hh] += _dot_tn(xrb[:, cols], dzab[:, cols])
            dwx_ref[hh] += _dot_tn(xrb[:, cols], dzib[:, cols])
            dxr_s[:, cols] = _dot(dzab[:, cols], wat_ref[hh]) + _dot(dzib[:, cols], wxt_ref[hh])
        dxr = dxr + dxr_s[...]
        dbc_ref[...] += _colsum(dxr)
        dwc_ref[0:1, :] += _colsum(dxr * gts["x3"])
        dwc_ref[1:2, :] += _colsum(dxr * gts["x2"])
        dwc_ref[2:3, :] += _colsum(dxr * gts["x1"])
        dwc_ref[3:4, :] += _colsum(dxr * xl)
        nxt = nxt_ref[...]
        nxt_ref[...] = dxr[:HALO, :]
        dxl = wc_ref[3:4, :] * dxr + wc_ref[2:3, :] * _shift_next(dxr, nxt, 1) \
            + wc_ref[1:2, :] * _shift_next(dxr, nxt, 2) + wc_ref[0:1, :] * _shift_next(dxr, nxt, 3)
        dp_ref[:, 2 * GM_W + LRU_W:] = dxl.astype(BF16)

    full = lambda shape: pl.BlockSpec(shape, lambda i: (0,) * len(shape))
    rev = lambda i: nt - 1 - i
    prev_blk = lambda i: jnp.maximum((nt - 1 - i) * hb - 1, 0)
    hhd = (HEADS, HEAD_DIM, HEAD_DIM)
    small_shapes = [(1, GM_W), (1, GM_W), (HEADS, CHUNK, CHUNK), (CHUNK, GM_W), (4, LRU_W), (1, LRU_W), hhd,
                    (1, LRU_W), hhd, (1, LRU_W), (1, LRU_W), (1, GM_W), (1, LRU_W)]
    return _pcall(
        body, name="bwd_mixers", grid=(nt,),
        in_specs=[pl.BlockSpec((ts, 2 * GM_W), lambda i: (rev(i), 0)),
                  pl.BlockSpec((ts, LRU_W), lambda i: (rev(i), 2)),
                  pl.BlockSpec((ts, LRU_W), lambda i: (rev(i), 3)),
                  pl.BlockSpec((HALO, LRU_W), lambda i: (prev_blk(i), 3)),
                  pl.BlockSpec((ts, GM_W + LRU_W), lambda i: (rev(i), 0)),
                  full((GM_W + LRU_W, GM_W + LRU_W)),
                  pl.BlockSpec((ts, LRU_W), lambda i: (rev(i), 0)),
                  pl.BlockSpec((HALO, LRU_W), lambda i: (prev_blk(i), 0)),
                  full((1, GM_W)), full((1, GM_W)), full((HEADS, CHUNK, CHUNK)), full((HEADS, CHUNK, CHUNK)),
                  full((CHUNK, GM_W)), full((4, LRU_W)), full((1, LRU_W)), full(hhd), full(hhd), full((1, LRU_W)),
                  full(hhd), full(hhd), full((1, LRU_W)), full((1, LRU_W)), full((1, GM_W)), full((1, LRU_W))],
        out_specs=[pl.BlockSpec((ts, 2 * GM_W + 2 * LRU_W), lambda i: (rev(i), 0))] + [full(s) for s in small_shapes],
        out_shape=[jax.ShapeDtypeStruct((s_len, 2 * GM_W + 2 * LRU_W), BF16)]
        + [jax.ShapeDtypeStruct(s, F32) for s in small_shapes],
        scratch_shapes=[pltpu.VMEM((1, LRU_W), F32), pltpu.VMEM((HALO, LRU_W), F32),
                        pltpu.VMEM((ts, LRU_W), F32), pltpu.VMEM((ts, LRU_W), F32),
                        pltpu.VMEM((ts, GM_W), BF16), pltpu.VMEM((ts, GM_W), F32), pltpu.VMEM((ts, 2 * LRU_W), F32),
                        pltpu.VMEM((ts, GM_W), BF16), pltpu.VMEM((ts, GM_W), F32), pltpu.VMEM((ts, LRU_W), F32),
                        pltpu.VMEM((ts, GM_W + LRU_W), F32)],
        dims=("arbitrary",), comm=comm)(p, p, p, p, dx2b, wout_g, hs, hs, gv, bv, wt, wtt, bst, wc, bc, wa, wat, ba, wx, wxt,
                                        bx, lam, ggm, glru)


def _bwd_weight(name, a, b, *, a_planes, b_planes, shard_rows, comm=None):
    _, s_len, ma = a.shape
    _, _, nb = b.shape
    m, n = a_planes * ma, b_planes * nb
    tk = min(TW_K, s_len)
    if shard_rows:
        rows, cols = m // N_CHIPS, n // 2
        tm, tn = _tile(rows, TW_M), _tile(cols, TW_N)
        out_idx = lambda i, j, k: (j * tn // cols, i * tm // rows, (i * tm % rows) // tm, (j * tn % cols) // tn)
    else:
        rows, cols = m // 2, n // N_CHIPS
        tm, tn = _tile(rows, TW_M), _tile(cols, TW_N)
        out_idx = lambda i, j, k: (i * tm // rows, j * tn // cols, (i * tm % rows) // tm, (j * tn % cols) // tn)
    nk = s_len // tk
    npa, npb = ma // tm, nb // tn

    def body(a_ref, b_ref, o_ref, ob_ref, acc_ref):
        k = pl.program_id(2)

        @pl.when(k == 0)
        def _():
            acc_ref[...] = jnp.zeros_like(acc_ref)

        acc_ref[...] += _dot_tn(a_ref[...], b_ref[...])

        @pl.when(k == nk - 1)
        def _():
            o_ref[...] = acc_ref[...]
            ob_ref[...] = acc_ref[...].astype(BF16)

    shape = (2, N_CHIPS, rows, cols)
    return _pcall(
        body, name=name, grid=(m // tm, n // tn, nk),
        in_specs=[pl.BlockSpec((None, tk, tm), lambda i, j, k: (i // npa, k, i % npa)),
                  pl.BlockSpec((None, tk, tn), lambda i, j, k: (j // npb, k, j % npb))],
        out_specs=[pl.BlockSpec((None, None, tm, tn), out_idx), pl.BlockSpec((None, None, tm, tn), out_idx)],
        out_shape=[jax.ShapeDtypeStruct(shape, F32), jax.ShapeDtypeStruct(shape, BF16)],
        scratch_shapes=[pltpu.VMEM((tm, tn), F32)],
        dims=("arbitrary", "arbitrary", "arbitrary"), comm=comm)(a, b)


def _mesh_pos():
    return lax.axis_index("x"), lax.axis_index("y"), lax.axis_index("c")


def _other_chips(x, y):
    return [(1 - x, y), (x, 1 - y), (1 - x, 1 - y)]


def _to_slot(name, a, cs, dtype):
    _, a_rows, b_cols = a.shape
    ta = min(T_ELEM, a_rows)

    def body(cs_ref, a_ref, o_ref):
        o_ref[...] = a_ref[...].astype(dtype)

    grid_spec = pltpu.PrefetchScalarGridSpec(
        num_scalar_prefetch=1, grid=(2, a_rows // ta),
        in_specs=[pl.BlockSpec((None, ta, b_cols), lambda h, r, cs_ref: (h, r, 0))],
        out_specs=pl.BlockSpec((None, None, ta, b_cols), lambda h, r, cs_ref: (cs_ref[1], h, r, 0)))
    return _pcall(body, name=name, grid_spec=grid_spec,
                  out_shape=jax.ShapeDtypeStruct((N_CHIPS,) + a.shape, dtype), dims=("arbitrary", "arbitrary"))(cs, a)


def _all_gather(bufs):
    n = len(bufs)

    def body(*refs):
        outs = refs[n:2 * n]
        send_sems, recv_sems = refs[2 * n:]
        x, y, c = _mesh_pos()
        s = 2 * x + y
        me, sib = (x, y, c), (x, y, 1 - c)
        chips = _other_chips(x, y)

        def rcopy(a, k, blk, to):
            return pltpu.make_async_remote_copy(src_ref=blk, dst_ref=blk, send_sem=send_sems.at[a * 6 + k],
                                                recv_sem=recv_sems.at[a * 6 + k], device_id=to, device_id_type=MESH)

        first = [rcopy(a, j, outs[a].at[s, c], (cx, cy, c)) for a in range(n) for j, (cx, cy) in enumerate(chips)]
        for cp in first:
            cp.start()
        passed = []
        for a in range(n):
            for j, (cx, cy) in enumerate(chips):
                blk = outs[a].at[2 * cx + cy, c]
                rcopy(a, j, blk, me).wait_recv()
                cp = rcopy(a, 3 + j, blk, sib)
                cp.start()
                passed.append(cp)
        for a in range(n):
            for j, (cx, cy) in enumerate(chips):
                rcopy(a, 3 + j, outs[a].at[2 * cx + cy, 1 - c], me).wait_recv()
        for cp in first + passed:
            cp.wait_send()

    return _pcall(
        body, name="all_gather_weights",
        in_specs=[ANY] * n, out_specs=[ANY] * n,
        out_shape=[jax.ShapeDtypeStruct(a.shape, a.dtype) for a in bufs],
        scratch_shapes=[pltpu.SemaphoreType.DMA((6 * n,)), pltpu.SemaphoreType.DMA((6 * n,))],
        aliases={a: a for a in range(n)})(*bufs)


def _same(a):
    return jax.ShapeDtypeStruct(a.shape, a.dtype)


def _job_gather(bufs, ici_parts, relay_parts):
    def copies(cin, cout, x, y, c):
        out = []
        for a, lo, n in ici_parts:
            blk = cout[a].at[2 * x + y, c, pl.ds(lo, n)]
            out += [(blk, blk, (cx, cy, c)) for cx, cy in _other_chips(x, y)]
        for a, lo, n in relay_parts:
            for cx, cy in _other_chips(x, y):
                blk = cout[a].at[2 * cx + cy, c, pl.ds(lo, n)]
                out.append((blk, blk, (x, y, 1 - c)))
        return out

    return _Job(bufs, [_same(b) for b in bufs], {a: a for a in range(len(bufs))},
                3 * (len(ici_parts) + len(relay_parts)), copies)


def _job_pair_swap(arrs):
    def copies(cin, cout, x, y, c):
        return [(cin[a].at[1 - c], cout[a], (x, y, 1 - c)) for a in range(len(arrs))]

    return _Job(arrs, [jax.ShapeDtypeStruct(a.shape[1:], a.dtype) for a in arrs], {}, len(arrs), copies)


def _job_chip_exchange(big, small=()):
    nb = len(big)
    arrs = list(big) + list(small)

    def copies(cin, cout, x, y, c):
        out = []
        for a in range(len(arrs)):
            for j, (cx, cy) in enumerate(_other_chips(x, y)):
                out.append((cin[a].at[2 * cx + cy] if a < nb else cin[a], cout[a].at[j], (cx, cy, c)))
        return out

    shapes = [a.shape[1:] for a in big] + [a.shape for a in small]
    return _Job(arrs, [jax.ShapeDtypeStruct((3,) + sh, a.dtype) for sh, a in zip(shapes, arrs)], {}, 3 * len(arrs), copies)


def _job_halves_swap(bufs):
    def copies(cin, cout, x, y, c):
        return [(cout[a].at[c], cout[a].at[c], (x, y, 1 - c)) for a in range(len(bufs))]

    return _Job(bufs, [_same(b) for b in bufs], {a: a for a in range(len(bufs))}, len(bufs), copies)


def _pair_sum(name, g32, recv, cs):
    _, nch, a_rows, b_cols = g32.shape
    ta = min(T_ELEM, a_rows)

    def body(cs_ref, g_ref, r_ref, pb_ref, own_ref):
        k = pl.program_id(1)
        v = g_ref[...] + r_ref[...].astype(F32)
        pb_ref[...] = v.astype(BF16)

        @pl.when(k == cs_ref[1])
        def _():
            own_ref[...] = v

    grid_spec = pltpu.PrefetchScalarGridSpec(
        num_scalar_prefetch=1, grid=(a_rows // ta, nch),
        in_specs=[pl.BlockSpec((None, None, ta, b_cols), lambda r, k, cs_ref: (cs_ref[0], k, r, 0)),
                  pl.BlockSpec((None, ta, b_cols), lambda r, k, cs_ref: (k, r, 0))],
        out_specs=[pl.BlockSpec((None, ta, b_cols), lambda r, k, cs_ref: (k, r, 0)),
                   pl.BlockSpec((ta, b_cols), lambda r, k, cs_ref: (r, 0))])
    return _pcall(
        body, name=name, grid_spec=grid_spec,
        out_shape=[jax.ShapeDtypeStruct((nch, a_rows, b_cols), BF16), jax.ShapeDtypeStruct((a_rows, b_cols), F32)],
        dims=("arbitrary", "arbitrary"))(cs, g32, recv)


def _small_pair_sum(mine, recv, cs):
    _, r, ccols = mine.shape

    def body(cs_ref, a_ref, b_ref, o_ref):
        o_ref[...] = a_ref[...] + b_ref[...]

    grid_spec = pltpu.PrefetchScalarGridSpec(
        num_scalar_prefetch=1, grid=(1,),
        in_specs=[pl.BlockSpec((None, r, ccols), lambda i, cs_ref: (cs_ref[0], 0, 0)),
                  pl.BlockSpec((r, ccols), lambda i, cs_ref: (0, 0))],
        out_specs=pl.BlockSpec((r, ccols), lambda i, cs_ref: (0, 0)))
    return _pcall(body, name="small_pair_sum", grid_spec=grid_spec,
                  out_shape=jax.ShapeDtypeStruct((r, ccols), F32), dims=("arbitrary",))(cs, mine, recv)


def _chip_sum(name, own, recv, cs):
    a_rows, b_cols = own.shape
    ta = min(T_ELEM, a_rows)

    def body(cs_ref, o_ref, r_ref, f_ref):
        f_ref[...] = ((o_ref[...] + r_ref[0].astype(F32)) + r_ref[1].astype(F32)) + r_ref[2].astype(F32)

    grid_spec = pltpu.PrefetchScalarGridSpec(
        num_scalar_prefetch=1, grid=(a_rows // ta,),
        in_specs=[pl.BlockSpec((ta, b_cols), lambda r, cs_ref: (r, 0)),
                  pl.BlockSpec((3, ta, b_cols), lambda r, cs_ref: (0, r, 0))],
        out_specs=pl.BlockSpec((None, ta, b_cols), lambda r, cs_ref: (cs_ref[0], r, 0)))
    return _pcall(body, name=name, grid_spec=grid_spec,
                  out_shape=jax.ShapeDtypeStruct((2, a_rows, b_cols), F32), dims=("arbitrary",))(cs, own, recv)


def _small_chip_sum(pair, recv, cs):
    r, ccols = pair.shape

    def body(cs_ref, p_ref, r_ref, o_ref):
        s = cs_ref[1]
        own = p_ref[...]
        total = None
        for k in range(N_CHIPS):
            flip = jnp.bitwise_xor(s, k)
            term = jnp.where(flip == 0, own, jnp.where(flip == 2, r_ref[0], jnp.where(flip == 1, r_ref[1], r_ref[2])))
            total = term if total is None else total + term
        o_ref[...] = total

    grid_spec = pltpu.PrefetchScalarGridSpec(
        num_scalar_prefetch=1, grid=(1,),
        in_specs=[pl.BlockSpec((r, ccols), lambda i, cs_ref: (0, 0)),
                  pl.BlockSpec((3, r, ccols), lambda i, cs_ref: (0, 0, 0))],
        out_specs=pl.BlockSpec((None, r, ccols), lambda i, cs_ref: (cs_ref[0], 0, 0)))
    return _pcall(body, name="small_chip_sum", grid_spec=grid_spec,
                  out_shape=jax.ShapeDtypeStruct((2, r, ccols), F32), dims=("arbitrary",))(cs, pair, recv)


def _adamw(name, w, g, m, v, *, halves, comm=None):
    rows, cols = w.shape
    tr, tc = min(T_ELEM, rows), PACK_COLS
    c1 = 1.0 - ADAM_B1 ** ADAM_STEP
    c2 = 1.0 - ADAM_B2 ** ADAM_STEP

    def body(w_ref, g_ref, m_ref, v_ref, d_ref, mo_ref, vo_ref, go_ref):
        g_ = g_ref[...]
        m_ = ADAM_B1 * m_ref[...] + (1.0 - ADAM_B1) * g_
        v_ = ADAM_B2 * v_ref[...] + (1.0 - ADAM_B2) * (g_ * g_)
        mo_ref[...] = m_
        vo_ref[...] = v_
        go_ref[...] = g_
        d_ref[...] = (-ADAM_LR) * ((m_ / c1) / (jnp.sqrt(v_ / c2) + ADAM_EPS) + ADAM_WD * w_ref[...])

    spec = pl.BlockSpec((tr, tc), lambda r, j: (r, j))
    if halves == "rows":
        nrh = rows // 2 // tr
        g_spec = pl.BlockSpec((None, tr, tc), lambda r, j: (r // nrh, r % nrh, j))
    elif halves == "cols":
        nch = cols // 2 // tc
        g_spec = pl.BlockSpec((None, tr, tc), lambda r, j: (j // nch, r, j % nch))
    else:
        g_spec = spec
    return _pcall(body, name=name, grid=(rows // tr, cols // tc), in_specs=[spec, g_spec, spec, spec],
                  out_specs=[spec] * 4, out_shape=[jax.ShapeDtypeStruct((rows, cols), F32)] * 4,
                  dims=("arbitrary", "arbitrary"), comm=comm)(w, g, m, v)


def _pack(parts, rows):
    flat = jnp.concatenate([a.reshape(-1) for a in parts])
    return jnp.pad(flat, (0, rows * PACK_COLS - flat.shape[0])).reshape(rows, PACK_COLS)


def _unpack(buf, shapes):
    flat = buf.reshape(-1)
    out, off = [], 0
    for sh in shapes:
        size = math.prod(sh)
        out.append(flat[off:off + size].reshape(sh))
        off += size
    return out


def _pack_rows(shapes, multiple):
    total = sum(math.prod(sh) for sh in shapes)
    rows = -(-total // PACK_COLS)
    return -(-rows // multiple) * multiple


SMALL = ["norm1_g", "gm_v_g", "gm_v_b", "gm_ws", "gm_bs", "lru_conv_w", "lru_conv_b", "lru_wa", "lru_ba", "lru_wx",
         "lru_bx", "lru_lambda", "gm_out_g", "lru_out_g", "norm2_g", "ffn_conv_w", "ffn_conv_b", "final_g"]
BIG = ["w_in", "w_out", "ffn_w_up", "ffn_w_down"]
ORDER = ["norm1_g", "w_in", "gm_v_g", "gm_v_b", "gm_ws", "gm_bs", "lru_conv_w", "lru_conv_b", "lru_wa", "lru_ba",
         "lru_wx", "lru_bx", "lru_lambda", "gm_out_g", "lru_out_g", "w_out", "norm2_g", "ffn_w_up", "ffn_conv_w",
         "ffn_conv_b", "ffn_w_down", "final_g"]


def kernel(x, norm1_g, w_in, gm_v_g, gm_v_b, gm_ws, gm_bs, lru_conv_w, lru_conv_b, lru_wa, lru_ba, lru_wx, lru_bx, lru_lambda, gm_out_g, lru_out_g, w_out, norm2_g, ffn_w_up, ffn_conv_w, ffn_conv_b, ffn_w_down, final_g, loss_target, m_norm1_g, m_w_in, m_gm_v_g, m_gm_v_b, m_gm_ws, m_gm_bs, m_lru_conv_w, m_lru_conv_b, m_lru_wa, m_lru_ba, m_lru_wx, m_lru_bx, m_lru_lambda, m_gm_out_g, m_lru_out_g, m_w_out, m_norm2_g, m_ffn_w_up, m_ffn_conv_w, m_ffn_conv_b, m_ffn_w_down, m_final_g, v_norm1_g, v_w_in, v_gm_v_g, v_gm_v_b, v_gm_ws, v_gm_bs, v_lru_conv_w, v_lru_conv_b, v_lru_wa, v_lru_ba, v_lru_wx, v_lru_bx, v_lru_lambda, v_gm_out_g, v_lru_out_g, v_w_out, v_norm2_g, v_ffn_w_up, v_ffn_conv_w, v_ffn_conv_b, v_ffn_w_down, v_final_g):
    w = dict(norm1_g=norm1_g, w_in=w_in, gm_v_g=gm_v_g, gm_v_b=gm_v_b, gm_ws=gm_ws, gm_bs=gm_bs, lru_conv_w=lru_conv_w, lru_conv_b=lru_conv_b, lru_wa=lru_wa, lru_ba=lru_ba, lru_wx=lru_wx, lru_bx=lru_bx, lru_lambda=lru_lambda, gm_out_g=gm_out_g, lru_out_g=lru_out_g, w_out=w_out, norm2_g=norm2_g, ffn_w_up=ffn_w_up, ffn_conv_w=ffn_conv_w, ffn_conv_b=ffn_conv_b, ffn_w_down=ffn_w_down, final_g=final_g)
    m = dict(norm1_g=m_norm1_g, w_in=m_w_in, gm_v_g=m_gm_v_g, gm_v_b=m_gm_v_b, gm_ws=m_gm_ws, gm_bs=m_gm_bs, lru_conv_w=m_lru_conv_w, lru_conv_b=m_lru_conv_b, lru_wa=m_lru_wa, lru_ba=m_lru_ba, lru_wx=m_lru_wx, lru_bx=m_lru_bx, lru_lambda=m_lru_lambda, gm_out_g=m_gm_out_g, lru_out_g=m_lru_out_g, w_out=m_w_out, norm2_g=m_norm2_g, ffn_w_up=m_ffn_w_up, ffn_conv_w=m_ffn_conv_w, ffn_conv_b=m_ffn_conv_b, ffn_w_down=m_ffn_w_down, final_g=m_final_g)
    v = dict(norm1_g=v_norm1_g, w_in=v_w_in, gm_v_g=v_gm_v_g, gm_v_b=v_gm_v_b, gm_ws=v_gm_ws, gm_bs=v_gm_bs, lru_conv_w=v_lru_conv_w, lru_conv_b=v_lru_conv_b, lru_wa=v_lru_wa, lru_ba=v_lru_ba, lru_wx=v_lru_wx, lru_bx=v_lru_bx, lru_lambda=v_lru_lambda, gm_out_g=v_gm_out_g, lru_out_g=v_lru_out_g, w_out=v_w_out, norm2_g=v_norm2_g, ffn_w_up=v_ffn_w_up, ffn_conv_w=v_ffn_conv_w, ffn_conv_b=v_ffn_conv_b, ffn_w_down=v_ffn_w_down, final_g=v_final_g)

    mx, my, mc = _mesh_pos()
    shard = 2 * mx + my
    cs = jnp.stack([mc, shard]).astype(jnp.int32)

    xs = x[0]
    tgt = loss_target[0]
    s_len, d = xs.shape

    halves = lambda a: a.reshape((2, a.shape[0] // 2) + a.shape[1:])
    slot = {k: _to_slot("slot_" + k, halves(w[k][0]), cs, BF16) for k in BIG}
    win_b, wc_b, wfc_b = _all_gather([slot["w_in"],
                                      _to_slot("slot_lru_conv_w", w["lru_conv_w"][0].reshape(2, 4, -1), cs, F32),
                                      _to_slot("slot_ffn_conv_w", w["ffn_conv_w"][0].reshape(2, 12, -1), cs, F32)])
    win_g = win_b.reshape(N_CHIPS, d, -1)
    wc = wc_b.reshape(N_CHIPS, 4, -1).transpose(1, 0, 2).reshape(4, -1)
    wfc = wfc_b.reshape(N_CHIPS, 3, -1).transpose(1, 0, 2).reshape(3, -1)
    wout_b, wup_b, wdown_b = slot["w_out"], slot["ffn_w_up"], slot["ffn_w_down"]
    r_out, r_up, r_dn = wout_b.shape[2], wup_b.shape[2] // 2, wdown_b.shape[2] // 2

    tril = jnp.tril(jnp.ones((CHUNK, CHUNK), bool))
    wt32 = jnp.where(tril[None], w["gm_ws"][0], 0.0)
    wt = wt32.astype(BF16)
    wtt = wt32.transpose(0, 2, 1).astype(BF16)
    bst = jnp.repeat(w["gm_bs"][0].T, HEAD_DIM, axis=1)
    wa = w["lru_wa"][0].astype(BF16)
    wx = w["lru_wx"][0].astype(BF16)
    wat = w["lru_wa"][0].transpose(0, 2, 1).astype(BF16)
    wxt = w["lru_wx"][0].transpose(0, 2, 1).astype(BF16)
    ba = w["lru_ba"][0].reshape(1, -1)
    bx = w["lru_bx"][0].reshape(1, -1)
    gf = w["final_g"].reshape(1, -1)

    p, h1, wout_b, wup_b = _fwd_in_proj(
        xs, w["norm1_g"], win_g, comm=_job_gather([wout_b, wup_b], [(0, 0, r_out), (1, 0, r_up)], []))
    y, hs, wout_b, wup_b, wdown_b = _fwd_mixers(
        p, w["gm_v_g"], w["gm_v_b"], wt, bst, wc, w["lru_conv_b"], wa, ba, wx, bx, w["lru_lambda"],
        w["gm_out_g"], w["lru_out_g"],
        comm=_job_gather([wout_b, wup_b, wdown_b], [(1, r_up, r_up), (2, 0, r_dn)], [(0, 0, r_out), (1, 0, r_up)]))
    wout_g = wout_b.reshape(-1, d)
    x2, h2, wup_b, wdown_b = _fwd_out_proj(
        xs, y, wout_g, w["norm2_g"], comm=_job_gather([wup_b, wdown_b], [(1, r_dn, r_dn)], [(0, r_up, r_up), (1, 0, r_dn)]))
    wdown_b, = _comm_call("gather_tail", _job_gather([wdown_b], [], [(0, r_dn, r_dn)]))
    wup_g = wup_b.reshape(N_CHIPS, d, -1)
    wdown_g = wdown_b.reshape(-1, d)
    upb, cb, act, dx3, dx3b, loss_tile, dgf = _fwd_ffn(h2, wup_g, wfc, w["ffn_conv_b"], wdown_g, x2, gf, tgt)
    loss = lax.psum(loss_tile[0, 0], ("x", "y", "c"))

    adam = {}

    def adamw_big(name, gfull, hv, comm=None):
        r2 = lambda a: a.reshape(w[name].shape[1:])
        res = _adamw("adamw_" + name, r2(w[name]), gfull, r2(m[name]), r2(v[name]), halves=hv, comm=comm)
        adam[name] = res[:4]
        return res[4:]

    gd32, gdb = _bwd_weight("bwd_w_down", act[None], dx3b[None], a_planes=1, b_planes=1, shard_rows=True)
    dup, dwf, dx2, dx2b, dg2, rcv = _bwd_ffn(dx3b, wdown_g, upb, cb, wfc, wup_g, dx3, x2, w["norm2_g"],
                                             comm=_job_pair_swap([gdb]))
    pb_dn, own_dn = _pair_sum("pair_sum_3", gd32, rcv, cs)
    gu32, gub, got = _bwd_weight("bwd_w_up", h2[None], dup, a_planes=1, b_planes=2, shard_rows=False,
                                 comm=_job_chip_exchange([pb_dn]))
    red_dn = _chip_sum("chip_sum_3", own_dn, got, cs)
    go32, gob, rcv, red_dn = _bwd_weight("bwd_w_out", y[None], dx2b[None], a_planes=1, b_planes=1, shard_rows=True,
                                         comm=_merge_jobs([_job_pair_swap([gub]), _job_halves_swap([red_dn])]))
    pb_up, own_up = _pair_sum("pair_sum_2", gu32, rcv, cs)
    adamw_big("ffn_w_down", red_dn, "cols")
    (dp, dgv, dbv, dwt, dbst, dwc, dbc, dwa, dba, dwx, dbx, dsp, dggm, dglru, got_up, rcv) = _bwd_mixers(
        p, dx2b, wout_g, hs, w["gm_v_g"], w["gm_v_b"], wt, wtt, bst, wc, w["lru_conv_b"], wa, wat, ba, wx, wxt, bx,
        w["lru_lambda"], w["gm_out_g"], w["lru_out_g"],
        comm=_merge_jobs([_job_chip_exchange([pb_up]), _job_pair_swap([gob])]))
    pb_out, own_out = _pair_sum("pair_sum_1", go32, rcv, cs)
    red_up = _chip_sum("chip_sum_2", own_up, got_up, cs)
    gi32, gib, red_up, got_out = _bwd_weight("bwd_w_in", h1[None], dp[None], a_planes=1, b_planes=1, shard_rows=False,
                                             comm=_merge_jobs([_job_halves_swap([red_up]), _job_chip_exchange([pb_out])]))
    red_out = _chip_sum("chip_sum_1", own_out, got_out, cs)
    rcv, = _comm_call("w_in_swap", _job_pair_swap([gib]))
    pb_in, own_in = _pair_sum("pair_sum_0", gi32, rcv, cs)
    grad_x, dg1, got_in, red_out = _bwd_in(dp, win_g, dx2, xs, w["norm1_g"],
                                           comm=_merge_jobs([_job_chip_exchange([pb_in]), _job_halves_swap([red_out])]))
    red_in = _chip_sum("chip_sum_0", own_in, got_in, cs)

    dwfc = dwf[:, :, :3].transpose(2, 0, 1, 3).reshape(3, -1)
    dbfc = dwf[:, :, 3].reshape(1, -1)
    dlam = dsp * (-_sigmoid(-w["lru_lambda"]))
    small_grads = dict(
        norm1_g=dg1, gm_v_g=dgv, gm_v_b=dbv, gm_ws=jnp.where(tril[None], dwt, 0.0),
        gm_bs=dbst.reshape(CHUNK, HEADS, HEAD_DIM).sum(-1).T, lru_conv_w=dwc, lru_conv_b=dbc, lru_wa=dwa, lru_ba=dba,
        lru_wx=dwx, lru_bx=dbx, lru_lambda=dlam, gm_out_g=dggm, lru_out_g=dglru, norm2_g=dg2, ffn_conv_w=dwfc,
        ffn_conv_b=dbfc, final_g=dgf)
    full_shapes = [small_grads[k].shape for k in SMALL]
    rows_full = _pack_rows(full_shapes, 16)
    gpack = _pack([small_grads[k] for k in SMALL], rows_full).reshape(2, rows_full // 2, PACK_COLS)

    rcv, = _comm_call("small_swap", _job_pair_swap([gpack]))
    small_pair = _small_pair_sum(gpack, rcv, cs)
    got_small, = _comm_call("small_exchange", _job_chip_exchange([], [small_pair]))
    small_half = _small_chip_sum(small_pair, got_small, cs)
    red_in, small_full = _comm_call("tail_halves_swap", _job_halves_swap([red_in, small_half]))
    adamw_big("ffn_w_up", red_up, "rows")
    adamw_big("w_out", red_out, "cols")
    adamw_big("w_in", red_in, "rows")

    grads = {}
    for name, g in zip(SMALL, _unpack(small_full, full_shapes)):
        blk = w[name].shape[1:] if w[name].ndim > 1 else w[name].shape
        if name in ("lru_conv_w", "ffn_conv_w"):
            g = lax.dynamic_slice_in_dim(g, shard * blk[1], blk[1], axis=1)
        grads[name] = g.reshape(blk)

    delta, new_m, new_v = {}, {}, {}
    for name in BIG:
        delta[name], new_m[name], new_v[name], grads[name] = adam[name]
    blk_shapes = [grads[k].shape for k in SMALL]
    rows_blk = _pack_rows(blk_shapes, T_ELEM)
    packs = [_pack([src[k] for k in SMALL], rows_blk) for src in (w, grads, m, v)]
    outs = _adamw("adamw_small", *packs, halves=None)
    for dst, buf in zip((delta, new_m, new_v), outs):
        for name, a in zip(SMALL, _unpack(buf, blk_shapes)):
            dst[name] = a

    def shaped(dct):
        return [dct[k].reshape(w[k].shape) for k in ORDER]

    return (loss, grad_x[None], *shaped(grads), *shaped(delta), *shaped(new_m), *shaped(new_v))
```
